```python
import math
import jax, jax.numpy as jnp
from jax import lax
import numpy as np

D_MODEL = 1024
BATCH = 8
SEQ = 2048
DEPTH = 1

CHUNK = 64
QBLOCK = 128

ATT_HEADS = 8
ATT_HEAD_DIM = 64
ATT_V_DIM = 2 * ATT_HEAD_DIM
ATT_WIDTH = ATT_HEADS * ATT_V_DIM

SSM_HEADS = 16
SSM_HEAD_DIM = 64
SSM_WIDTH = SSM_HEADS * SSM_HEAD_DIM
SSM_GROUPS = 2
SSM_STATE = 128
SSM_CONV = 4
SSM_HEADS_PER_GROUP = SSM_HEADS // SSM_GROUPS

MIX_WIDTH = ATT_WIDTH + SSM_WIDTH

FFN_DIM = 2816
FFN_CONV = 3

REL_BUCKETS = 32
REL_MAX_DIST = 128

NORM_EPS = 1e-6
SUBLN_EPS = 1e-5
SSM_NORM_EPS = 1e-5

Q_COLS = ATT_HEADS * 2 * ATT_HEAD_DIM
K_COLS = ATT_HEADS * 2 * ATT_HEAD_DIM
V_COLS = ATT_WIDTH
Z_COLS = SSM_WIDTH
XBC_COLS = SSM_WIDTH + 2 * SSM_GROUPS * SSM_STATE
DT_COLS = SSM_HEADS
IN_COLS = Q_COLS + K_COLS + V_COLS + Z_COLS + XBC_COLS + DT_COLS
IN_SPLITS = (Q_COLS, Q_COLS + K_COLS, Q_COLS + K_COLS + V_COLS,
             Q_COLS + K_COLS + V_COLS + Z_COLS,
             Q_COLS + K_COLS + V_COLS + Z_COLS + XBC_COLS)

kernel_name = "hymba_diffattn_ssd_convffn_block"


def rms_norm(x, g, eps=NORM_EPS):
    x32 = x.astype(jnp.float32)
    y = x32 * lax.rsqrt(jnp.mean(x32 * x32, axis=-1, keepdims=True) + eps)
    return (y * g.astype(jnp.float32)).astype(x.dtype)


def causal_depthwise_conv(x, w, b):
    k = w.shape[0]
    out = lax.conv_general_dilated(
        x, w[:, None, :].astype(x.dtype), window_strides=(1,),
        padding=[(k - 1, 0)], dimension_numbers=('NWC', 'WIO', 'NWC'),
        feature_group_count=x.shape[-1])
    return out + b.astype(x.dtype)


def t5_bucket(rel):
    nb = REL_BUCKETS // 2
    max_exact = nb // 2
    bucket = jnp.where(rel > 0, nb, 0)
    n = jnp.abs(rel)
    nf = jnp.maximum(n, 1).astype(jnp.float32)
    large = max_exact + (jnp.log(nf / max_exact) / math.log(REL_MAX_DIST / max_exact)
                         * (nb - max_exact)).astype(jnp.int32)
    large = jnp.minimum(large, nb - 1)
    return bucket + jnp.where(n < max_exact, n, large)


def differential_attention(q, k, v, rel_table, lam, sub_g, lam_init):
    b, seq_len = q.shape[0], q.shape[1]
    scale = ATT_HEAD_DIM ** -0.5
    outs = []
    for blk in range(seq_len // QBLOCK):
        q0 = blk * QBLOCK
        kv_len = q0 + QBLOCK
        qb = q[:, q0:kv_len]
        kb = k[:, :kv_len]
        vb = v[:, :kv_len]
        q_pos = jnp.arange(q0, kv_len)
        k_pos = jnp.arange(kv_len)
        bias = jnp.transpose(rel_table[t5_bucket(k_pos[None, :] - q_pos[:, None])], (2, 0, 1))
        s = jnp.einsum('bqhmd,bkhmd->bhmqk', qb, kb).astype(jnp.float32) * scale
        s = s + bias[None, :, None].astype(jnp.float32)
        allowed = (k_pos[None, :] // CHUNK) <= (q_pos[:, None] // CHUNK)
        s = jnp.where(allowed, s, -1e30)
        p = jax.nn.softmax(s, axis=-1)
        a = p[:, :, 0] - lam * p[:, :, 1]
        outs.append(jnp.einsum('bhqk,bkhe->bqhe', a.astype(v.dtype), vb))
    o = jnp.concatenate(outs, axis=1)
    o = rms_norm(o, sub_g, SUBLN_EPS) * (1.0 - lam_init)
    return o.reshape(b, seq_len, ATT_WIDTH)


def ssd_mixer(z, xbc, dt_raw, conv_w, conv_b, dt_bias, a_log, d_skip, norm_g):
    b, seq_len, _ = xbc.shape
    G, R, P, N = SSM_GROUPS, SSM_HEADS_PER_GROUP, SSM_HEAD_DIM, SSM_STATE
    nc = seq_len // CHUNK
    f32 = jnp.float32
    xbc = jax.nn.silu(causal_depthwise_conv(xbc, conv_w, conv_b))
    xs, bm, cm = jnp.split(xbc, (SSM_WIDTH, SSM_WIDTH + G * N), axis=-1)
    xs = xs.astype(f32).reshape(b, nc, CHUNK, G, R, P)
    bm = bm.astype(f32).reshape(b, nc, CHUNK, G, N)
    cm = cm.astype(f32).reshape(b, nc, CHUNK, G, N)
    dt = jax.nn.softplus(dt_raw.astype(f32) + dt_bias.astype(f32)).reshape(b, nc, CHUNK, G, R)
    a = -jnp.exp(a_log.astype(f32)).reshape(G, R) * dt
    xdt = xs * dt[..., None]
    a_cum = jnp.cumsum(a, axis=2)
    seg = a_cum[:, :, :, None] - a_cum[:, :, None, :]
    tri = jnp.tril(jnp.ones((CHUNK, CHUNK), dtype=bool))[:, :, None, None]
    decay = jnp.exp(jnp.where(tri, seg, -jnp.inf))
    cb = jnp.einsum('bclgn,bcsgn->bclsg', cm, bm)
    y_diag = jnp.einsum('bclsg,bclsgr,bcsgrp->bclgrp', cb, decay, xdt)
    decay_to_end = jnp.exp(a_cum[:, :, -1:] - a_cum)
    chunk_states = jnp.einsum('bclgn,bclgr,bclgrp->bcgrpn', bm, decay_to_end, xdt)
    chunk_decay = jnp.exp(a_cum[:, :, -1])

    def step(h, inp):
        s_c, d_c = inp
        return d_c[..., None, None] * h + s_c, h

    h0 = jnp.zeros((b, G, R, P, N), f32)
    _, prev = lax.scan(step, h0, (jnp.moveaxis(chunk_states, 1, 0), jnp.moveaxis(chunk_decay, 1, 0)))
    prev = jnp.moveaxis(prev, 0, 1)
    y_off = jnp.einsum('bclgn,bcgrpn,bclgr->bclgrp', cm, prev, jnp.exp(a_cum))
    y = y_diag + y_off + xs * d_skip.astype(f32).reshape(G, R)[..., None]
    y = y.reshape(b, seq_len, SSM_WIDTH)
    g = (y * jax.nn.silu(z.astype(f32))).reshape(b, seq_len, G, SSM_WIDTH // G)
    g = g * lax.rsqrt(jnp.mean(g * g, axis=-1, keepdims=True) + SSM_NORM_EPS)
    return (g.reshape(b, seq_len, SSM_WIDTH) * norm_g.astype(f32)).astype(z.dtype)


def conv_ffn(h, w_up, conv_w, conv_b, w_down):
    u = causal_depthwise_conv(h @ w_up, conv_w, conv_b)
    gate, val = jnp.split(u, 2, axis=-1)
    return (jax.nn.silu(gate) * val) @ w_down


def setup_inputs(seed: int = 0) -> dict:
    key = jax.random.key(seed)
    ks = jax.random.split(key, 24)
    f32 = jnp.float32
    nrm = lambda k, shape, s: jax.random.normal(k, shape, f32) * s
    dt0 = jnp.exp(jax.random.uniform(ks[10], (DEPTH, SSM_HEADS), f32, math.log(1e-3), math.log(1e-1)))
    return {
        "x": nrm(ks[0], (BATCH, SEQ, D_MODEL), 1.0),
        "rel_bias_table": nrm(ks[1], (REL_BUCKETS, ATT_HEADS), 0.1),
        "attn_norm_g": 1.0 + nrm(ks[2], (DEPTH, D_MODEL), 0.02),
        "w_in": nrm(ks[3], (DEPTH, D_MODEL, IN_COLS), D_MODEL ** -0.5),
        "lambda_q1": nrm(ks[4], (DEPTH, ATT_HEAD_DIM), 0.1),
        "lambda_k1": nrm(ks[5], (DEPTH, ATT_HEAD_DIM), 0.1),
        "lambda_q2": nrm(ks[6], (DEPTH, ATT_HEAD_DIM), 0.1),
        "lambda_k2": nrm(ks[7], (DEPTH, ATT_HEAD_DIM), 0.1),
        "attn_subln_g": 1.0 + nrm(ks[8], (DEPTH, ATT_V_DIM), 0.02),
        "ssm_conv_w": nrm(ks[9], (DEPTH, SSM_CONV, XBC_COLS), SSM_CONV ** -0.5),
        "ssm_conv_b": nrm(ks[11], (DEPTH, XBC_COLS), 0.02),
        "ssm_dt_bias": dt0 + jnp.log(-jnp.expm1(-dt0)),
        "ssm_a_log": jnp.log(jax.random.uniform(ks[12], (DEPTH, SSM_HEADS), f32, 1.0, 16.0)),
        "ssm_d": 1.0 + nrm(ks[13], (DEPTH, SSM_HEADS), 0.1),
        "ssm_norm_g": 1.0 + nrm(ks[14], (DEPTH, SSM_WIDTH), 0.02),
        "w_out": nrm(ks[15], (DEPTH, MIX_WIDTH, D_MODEL), MIX_WIDTH ** -0.5),
        "ffn_norm_g": 1.0 + nrm(ks[16], (DEPTH, D_MODEL), 0.02),
        "ffn_w_up": nrm(ks[17], (DEPTH, D_MODEL, 2 * FFN_DIM), D_MODEL ** -0.5),
        "ffn_conv_w": nrm(ks[18], (DEPTH, FFN_CONV, 2 * FFN_DIM), FFN_CONV ** -0.5),
        "ffn_conv_b": nrm(ks[19], (DEPTH, 2 * FFN_DIM), 0.02),
        "ffn_w_down": nrm(ks[20], (DEPTH, FFN_DIM, D_MODEL), FFN_DIM ** -0.5),
        "final_norm_g": 1.0 + nrm(ks[21], (D_MODEL,), 0.02),
    }


def reference(x, rel_bias_table, attn_norm_g, w_in, lambda_q1, lambda_k1, lambda_q2, lambda_k2,
              attn_subln_g, ssm_conv_w, ssm_conv_b, ssm_dt_bias, ssm_a_log, ssm_d, ssm_norm_g,
              w_out, ffn_norm_g, ffn_w_up, ffn_conv_w, ffn_conv_b, ffn_w_down, final_norm_g):
    b, seq_len, _ = x.shape
    for i in range(DEPTH):
        lam_init = 0.8 - 0.6 * math.exp(-0.3 * i)
        h = rms_norm(x, attn_norm_g[i])
        proj = h @ w_in[i]
        q, k, v, z, xbc, dt_raw = jnp.split(proj, IN_SPLITS, axis=-1)
        q = q.reshape(b, seq_len, ATT_HEADS, 2, ATT_HEAD_DIM)
        k = k.reshape(b, seq_len, ATT_HEADS, 2, ATT_HEAD_DIM)
        v = v.reshape(b, seq_len, ATT_HEADS, ATT_V_DIM)
        lam = (jnp.exp(jnp.sum(lambda_q1[i].astype(jnp.float32) * lambda_k1[i].astype(jnp.float32)))
               - jnp.exp(jnp.sum(lambda_q2[i].astype(jnp.float32) * lambda_k2[i].astype(jnp.float32)))
               + lam_init)
        att_out = differential_attention(q, k, v, rel_bias_table, lam, attn_subln_g[i], lam_init)
        ssm_out = ssd_mixer(z, xbc, dt_raw, ssm_conv_w[i], ssm_conv_b[i], ssm_dt_bias[i],
                            ssm_a_log[i], ssm_d[i], ssm_norm_g[i])
        x = x + jnp.concatenate([att_out, ssm_out], axis=-1) @ w_out[i]
        h = rms_norm(x, ffn_norm_g[i])
        x = x + conv_ffn(h, ffn_w_up[i], ffn_conv_w[i], ffn_conv_b[i], ffn_w_down[i])
    return rms_norm(x, final_norm_g)
```

```python
import functools
import math

import jax
import jax.numpy as jnp
from jax import lax
from jax.experimental import pallas as pl
from jax.experimental.pallas import tpu as pltpu

F32 = jnp.float32
BF16 = jnp.bfloat16

D_MODEL = 1024
CHUNK = 64
ATT_HEADS = 8
ATT_HEAD_DIM = 64
ATT_V_DIM = 2 * ATT_HEAD_DIM
ATT_WIDTH = ATT_HEADS * ATT_V_DIM
SSM_HEADS = 16
SSM_HEAD_DIM = 64
SSM_WIDTH = SSM_HEADS * SSM_HEAD_DIM
SSM_GROUPS = 2
SSM_STATE = 128
SSM_CONV = 4
SSM_HEADS_PER_GROUP = SSM_HEADS // SSM_GROUPS
SSM_GROUP_WIDTH = SSM_WIDTH // SSM_GROUPS
FFN_DIM = 2816
FFN_CONV = 3
REL_BUCKETS = 32
REL_MAX_DIST = 128
NORM_EPS = 1e-6
SUBLN_EPS = 1e-5
SSM_NORM_EPS = 1e-5
BC_COLS = 2 * SSM_GROUPS * SSM_STATE
MAIN_COLS = 3 * ATT_WIDTH + SSM_WIDTH + SSM_WIDTH + BC_COLS
DT_COLS = SSM_HEADS

LANES = 128
SUBLANES = 8
VMEM_LIMIT = 56 * 1024 * 1024

PROJ_TM = 512
PROJ_CN = 512
ATT_QB = 256
ATT_KB = 256
SSD_Q = 256
FFN_TM = 512
FFN_CW = 256
HALO = SUBLANES


def _resident(shape):
    nd = len(shape)
    return pl.BlockSpec(shape, lambda *_: (0,) * nd, pipeline_mode=pl.Buffered(1))


def _sigmoid(x):
    return 1.0 / (1.0 + jnp.exp(-x))


def _inproj_body(x_ref, g_ref, w_ref, wdt_ref, proj_ref, dt_ref):
    x = x_ref[...]
    h = x * lax.rsqrt(jnp.mean(x * x, axis=-1, keepdims=True) + NORM_EPS) * g_ref[...]
    h = h.astype(BF16)
    for j in range(MAIN_COLS // PROJ_CN):
        sl = slice(j * PROJ_CN, (j + 1) * PROJ_CN)
        proj_ref[:, sl] = jnp.dot(h, w_ref[:, sl], preferred_element_type=F32).astype(BF16)
    dt_ref[...] = jnp.dot(h, wdt_ref[...], preferred_element_type=F32)


def _in_proj(x2, g, w_main, w_dt):
    t = x2.shape[0]
    return pl.pallas_call(
        _inproj_body,
        grid=(t // PROJ_TM,),
        in_specs=[
            pl.BlockSpec((PROJ_TM, D_MODEL), lambda i: (i, 0)),
            _resident((1, D_MODEL)),
            _resident((D_MODEL, MAIN_COLS)),
            _resident((D_MODEL, LANES)),
        ],
        out_specs=[
            pl.BlockSpec((PROJ_TM, MAIN_COLS), lambda i: (i, 0)),
            pl.BlockSpec((PROJ_TM, LANES), lambda i: (i, 0)),
        ],
        out_shape=[
            jax.ShapeDtypeStruct((t, MAIN_COLS), BF16),
            jax.ShapeDtypeStruct((t, LANES), F32),
        ],
        compiler_params=pltpu.CompilerParams(
            dimension_semantics=("arbitrary",), vmem_limit_bytes=VMEM_LIMIT),
        name="in_proj",
    )(x2, g, w_main, w_dt)


FAR_BUCKET = REL_BUCKETS // 2 - 1


def _t5_bucket(rel):
    nb = REL_BUCKETS // 2
    max_exact = nb // 2
    bucket = jnp.where(rel > 0, nb, 0)
    n = jnp.abs(rel)
    nf = jnp.maximum(n, 1).astype(F32)
    large = max_exact + (jnp.log(nf / max_exact) / math.log(REL_MAX_DIST / max_exact)
                         * (nb - max_exact)).astype(jnp.int32)
    large = jnp.minimum(large, nb - 1)
    return bucket + jnp.where(n < max_exact, n, large)


def _bias_body(tbl_ref, idx_ref, out_ref):
    h = pl.program_id(0)
    idx = idx_ref[0]
    base = tbl_ref[FAR_BUCKET, h]
    acc = jnp.zeros(idx.shape, F32)
    for b in range(REL_BUCKETS):
        acc = jnp.where(idx == b, tbl_ref[b, h] - base, acc)
    out_ref[0, 0] = acc


def _bias_tiles(rel_table):
    qi = jnp.arange(ATT_QB, dtype=jnp.int32)[:, None]
    kj = jnp.arange(ATT_KB, dtype=jnp.int32)[None, :]
    idx = jnp.stack([_t5_bucket(kj - qi), _t5_bucket(kj - qi - ATT_KB)])
    return pl.pallas_call(
        _bias_body,
        grid=(ATT_HEADS, 2),
        in_specs=[
            pl.BlockSpec(memory_space=pltpu.SMEM),
            pl.BlockSpec((1, ATT_QB, ATT_KB), lambda h, d: (d, 0, 0)),
        ],
        out_specs=pl.BlockSpec((1, 1, ATT_QB, ATT_KB), lambda h, d: (h, d, 0, 0)),
        out_shape=jax.ShapeDtypeStruct((ATT_HEADS, 2, ATT_QB, ATT_KB), F32),
        name="rel_bias_tiles",
    )(rel_table.astype(F32), idx)


def _attn_body(lam_ref, q_ref, k_ref, v_ref, bias_ref, subg_ref, o_ref, m_ref, l_ref, acc_ref,
               *, lam_init):
    i = pl.program_id(2)
    qb, kb = ATT_QB, ATT_KB
    q = q_ref[...] * (ATT_HEAD_DIM ** -0.5)
    lane = lax.broadcasted_iota(jnp.int32, q.shape, 1)
    zero = jnp.zeros_like(q)
    qs = jnp.concatenate([jnp.where(lane < ATT_HEAD_DIM, q, zero),
                          jnp.where(lane >= ATT_HEAD_DIM, q, zero)], axis=0)

    m_ref[...] = jnp.full(m_ref.shape, -jnp.inf, F32)
    l_ref[...] = jnp.zeros(l_ref.shape, F32)
    acc_ref[...] = jnp.zeros(acc_ref.shape, F32)

    def flash_step(j, adjust):
        start = pl.multiple_of(j * kb, kb)
        kj = k_ref[pl.ds(start, kb), :]
        vj = v_ref[pl.ds(start, kb), :]
        s = lax.dot_general(qs, kj, (((1,), (1,)), ((), ())), preferred_element_type=F32)
        s = adjust(s)
        m_old = m_ref[...]
        m_new = jnp.maximum(m_old, jnp.max(s, axis=-1, keepdims=True))
        alpha = jnp.exp(m_old - m_new)
        p = jnp.exp(s - jnp.concatenate([m_new] * (kb // LANES), axis=1))
        l_ref[...] = alpha * l_ref[...] + jnp.sum(p, axis=-1, keepdims=True)
        acc_ref[...] = alpha * acc_ref[...] + jnp.dot(p.astype(BF16), vj, preferred_element_type=F32)
        m_ref[...] = m_new

    def far_block(j, carry):
        flash_step(j, lambda s: s)
        return carry

    lax.fori_loop(0, i - 1, far_block, 0)

    @pl.when(i >= 1)
    def _():
        b = bias_ref[0, 1]
        flash_step(i - 1, lambda s: s + jnp.concatenate([b, b], axis=0))

    def diag_adjust(s):
        b = bias_ref[0, 0]
        row = lax.broadcasted_iota(jnp.int32, (qb, kb), 0)
        col = lax.broadcasted_iota(jnp.int32, (qb, kb), 1)
        allowed = (col // CHUNK) <= (row // CHUNK)
        allowed = jnp.concatenate([allowed, allowed], axis=0)
        return jnp.where(allowed, s + jnp.concatenate([b, b], axis=0), -1e30)

    flash_step(i, diag_adjust)

    o = acc_ref[...] / l_ref[...]
    o = o[:qb] - lam_ref[0] * o[qb:]
    o = o * lax.rsqrt(jnp.mean(o * o, axis=-1, keepdims=True) + SUBLN_EPS) * subg_ref[...]
    o_ref[...] = (o * (1.0 - lam_init)).astype(BF16)


def _attention(proj, lam, bias, sub_g, batch, seq, lam_init):
    nq = seq // ATT_QB
    t = batch * seq
    kcol = ATT_WIDTH // LANES
    vcol = 2 * ATT_WIDTH // LANES
    return pl.pallas_call(
        functools.partial(_attn_body, lam_init=lam_init),
        grid=(batch, ATT_HEADS, nq),
        in_specs=[
            pl.BlockSpec(memory_space=pltpu.SMEM),
            pl.BlockSpec((ATT_QB, LANES), lambda b, h, i: (b * nq + i, h)),
            pl.BlockSpec((seq, LANES), lambda b, h, i: (b, kcol + h)),
            pl.BlockSpec((seq, LANES), lambda b, h, i: (b, vcol + h)),
            pl.BlockSpec((1, 2, ATT_QB, ATT_KB), lambda b, h, i: (h, 0, 0, 0)),
            pl.BlockSpec((1, ATT_V_DIM), lambda b, h, i: (0, 0)),
        ],
        out_specs=pl.BlockSpec((ATT_QB, LANES), lambda b, h, i: (b * nq + i, h)),
        out_shape=jax.ShapeDtypeStruct((t, ATT_WIDTH), BF16),
        scratch_shapes=[pltpu.VMEM((2 * ATT_QB, LANES), F32)] * 3,
        compiler_params=pltpu.CompilerParams(
            dimension_semantics=("arbitrary", "arbitrary", "arbitrary"),
            vmem_limit_bytes=VMEM_LIMIT),
        name="diff_attn",
    )(lam, proj, proj, proj, bias, sub_g)


def _split3(x):
    x1 = x.astype(BF16)
    r1 = x - x1.astype(F32)
    x2 = r1.astype(BF16)
    x3 = (r1 - x2.astype(F32)).astype(BF16)
    return x1, x2, x3


def _ssd_body(z_ref, xs_ref, bc_ref, dt_ref, cwx_ref, cbx_ref, cwbc_ref, cbbc_ref, dtb_ref,
              alog_ref, dexp_ref, ng_ref, e3_ref, o_ref, extx_ref, extbc_ref, state_ref):
    c = pl.program_id(1)
    q = SSD_Q

    @pl.when(c == 0)
    def _():
        extx_ref[0:HALO, :] = jnp.zeros((HALO, SSM_WIDTH), F32)
        extbc_ref[0:HALO, :] = jnp.zeros((HALO, BC_COLS), F32)
        state_ref[...] = jnp.zeros(state_ref.shape, F32)

    def conv_silu(raw_ref, ext_ref, w_ref, b_ref):
        ext_ref[HALO:HALO + q, :] = raw_ref[...].astype(F32)
        acc = b_ref[...] + w_ref[SSM_CONV - 1:SSM_CONV, :] * ext_ref[HALO:HALO + q, :]
        for k in range(SSM_CONV - 1):
            off = HALO - (SSM_CONV - 1) + k
            acc = acc + w_ref[k:k + 1, :] * ext_ref[off:off + q, :]
        ext_ref[0:HALO, :] = ext_ref[q:q + HALO, :]
        return acc * _sigmoid(acc)

    xs = conv_silu(xs_ref, extx_ref, cwx_ref, cbx_ref)
    bc = conv_silu(bc_ref, extbc_ref, cwbc_ref, cbbc_ref)

    dtr = dt_ref[...] + dtb_ref[...]
    dt = jnp.maximum(dtr, 0.0) + jnp.log(1.0 + jnp.exp(-jnp.abs(dtr)))
    a = -jnp.exp(alog_ref[...]) * dt

    row = lax.broadcasted_iota(jnp.int32, (q, q), 0)
    col = lax.broadcasted_iota(jnp.int32, (q, q), 1)
    causal = col <= row
    tril = jnp.where(causal, 1.0, 0.0).astype(BF16)
    acum = sum(jnp.dot(tril, t, preferred_element_type=F32) for t in _split3(a))

    lane = lax.broadcasted_iota(jnp.int32, (q, LANES), 1)

    def expand(x):
        parts = [jnp.where(lane < SSM_HEADS, t.astype(F32), 0.0) for t in _split3(x)]
        packed = parts[0] + pltpu.roll(parts[1], SSM_HEADS, 1) + pltpu.roll(parts[2], 2 * SSM_HEADS, 1)
        return jnp.dot(packed.astype(BF16), e3_ref[...], preferred_element_type=F32)

    dt_exp = expand(dt)
    acum_exp = expand(acum)
    ea_exp = jnp.exp(acum_exp)
    last_exp = acum_exp[q - 1:q, :]
    dte_exp = jnp.exp(last_exp - acum_exp)
    ea_last = ea_exp[q - 1:q, :]

    xdt = xs * dt_exp
    xdt_b = xdt.astype(BF16)
    acum_t = acum.T
    pair_lane = lax.broadcasted_iota(jnp.int32, (q, LANES), 1)

    y_groups = []
    for g in range(SSM_GROUPS):
        gsl = slice(g * SSM_GROUP_WIDTH, (g + 1) * SSM_GROUP_WIDTH)
        bg = bc[:, g * SSM_STATE:(g + 1) * SSM_STATE]
        cg = bc[:, (SSM_GROUPS + g) * SSM_STATE:(SSM_GROUPS + g + 1) * SSM_STATE]
        bg_b = bg.astype(BF16)
        cg_b = cg.astype(BF16)
        cb = lax.dot_general(cg_b, bg_b, (((1,), (1,)), ((), ())), preferred_element_type=F32)

        def masked(h):
            seg = acum[:, h:h + 1] - acum_t[h:h + 1, :]
            return (cb * jnp.exp(jnp.where(causal, seg, -jnp.inf))).astype(BF16)

        pairs = []
        for j in range(SSM_HEADS_PER_GROUP // 2):
            h0 = g * SSM_HEADS_PER_GROUP + 2 * j
            xp = xdt_b[:, h0 * SSM_HEAD_DIM:(h0 + 2) * SSM_HEAD_DIM]
            r0 = jnp.dot(masked(h0), xp, preferred_element_type=F32)
            r1 = jnp.dot(masked(h0 + 1), xp, preferred_element_type=F32)
            pairs.append(jnp.where(pair_lane < SSM_HEAD_DIM, r0, r1))
        y_diag = jnp.concatenate(pairs, axis=1)

        st = state_ref[g]
        y_off = jnp.dot(cg_b, st.astype(BF16), preferred_element_type=F32) * ea_exp[:, gsl]
        w = (xdt[:, gsl] * dte_exp[:, gsl]).astype(BF16)
        s_new = jnp.dot(bg.T.astype(BF16), w, preferred_element_type=F32)
        state_ref[g] = st * ea_last[:, gsl] + s_new

        y = y_diag + y_off + xs[:, gsl] * dexp_ref[:, gsl]
        zf = z_ref[:, gsl].astype(F32)
        gated = y * (zf * _sigmoid(zf))
        gated = gated * lax.rsqrt(jnp.mean(gated * gated, axis=-1, keepdims=True) + SSM_NORM_EPS)
        y_groups.append(gated * ng_ref[:, gsl])
    o_ref[...] = jnp.concatenate(y_groups, axis=1).astype(BF16)


def _ssd(proj, dt_raw, conv_w, conv_b, dt_bias, a_log, d_skip, norm_g, batch, seq):
    nc = seq // SSD_Q
    t = batch * seq
    zcol = 3 * ATT_WIDTH // SSM_WIDTH
    xcol = zcol + 1
    bccol = (3 * ATT_WIDTH + 2 * SSM_WIDTH) // BC_COLS
    pad = LANES - SSM_HEADS
    dtb = jnp.pad(dt_bias.astype(F32), (0, pad)).reshape(1, LANES)
    alog = jnp.pad(a_log.astype(F32), (0, pad)).reshape(1, LANES)
    dexp = jnp.repeat(d_skip.astype(F32), SSM_HEAD_DIM).reshape(1, SSM_WIDTH)
    r = jnp.arange(LANES)[:, None]
    hcol = (jnp.arange(SSM_WIDTH) // SSM_HEAD_DIM)[None, :]
    e3 = jnp.where((r % SSM_HEADS == hcol) & (r < 3 * SSM_HEADS), 1.0, 0.0).astype(BF16)
    cw = conv_w.astype(F32)
    cb = conv_b.astype(F32).reshape(1, -1)
    row_spec = lambda width, colblk: pl.BlockSpec((SSD_Q, width), lambda b, c: (b * nc + c, colblk))
    return pl.pallas_call(
        _ssd_body,
        grid=(batch, nc),
        in_specs=[
            row_spec(SSM_WIDTH, zcol),
            row_spec(SSM_WIDTH, xcol),
            row_spec(BC_COLS, bccol),
            row_spec(LANES, 0),
            _resident((SSM_CONV, SSM_WIDTH)),
            _resident((1, SSM_WIDTH)),
            _resident((SSM_CONV, BC_COLS)),
            _resident((1, BC_COLS)),
            _resident((1, LANES)),
            _resident((1, LANES)),
            _resident((1, SSM_WIDTH)),
            _resident((1, SSM_WIDTH)),
            _resident((LANES, SSM_WIDTH)),
        ],
        out_specs=pl.BlockSpec((SSD_Q, SSM_WIDTH), lambda b, c: (b * nc + c, 0)),
        out_shape=jax.ShapeDtypeStruct((t, SSM_WIDTH), BF16),
        scratch_shapes=[
            pltpu.VMEM((HALO + SSD_Q, SSM_WIDTH), F32),
            pltpu.VMEM((HALO + SSD_Q, BC_COLS), F32),
            pltpu.VMEM((SSM_GROUPS, SSM_STATE, SSM_GROUP_WIDTH), F32),
        ],
        compiler_params=pltpu.CompilerParams(
            dimension_semantics=("arbitrary", "arbitrary"), vmem_limit_bytes=VMEM_LIMIT),
        name="ssd_mixer",
    )(proj, proj, proj, dt_raw, cw[:, :SSM_WIDTH], cb[:, :SSM_WIDTH], cw[:, SSM_WIDTH:],
      cb[:, SSM_WIDTH:], dtb, alog, dexp, norm_g.astype(F32).reshape(1, -1), e3)


N_UBUF = 4


def _ffn_body(x_ref, att_ref, ssm_ref, wo_ref, g2_ref, wup_ref, cw_ref, cb_ref, wdn_ref, gf_ref,
              o_ref, x1_ref, h2_ref, act_ref, uext_ref, carry_ref, *, tiles_per_seq):
    i = pl.program_id(0)
    tm, cw = FFN_TM, FFN_CW

    @pl.when(i % tiles_per_seq == 0)
    def _():
        carry_ref[...] = jnp.zeros(carry_ref.shape, F32)

    x1 = (x_ref[...]
          + jnp.dot(att_ref[...], wo_ref[0:ATT_WIDTH, :], preferred_element_type=F32)
          + jnp.dot(ssm_ref[...], wo_ref[ATT_WIDTH:, :], preferred_element_type=F32))
    x1_ref[...] = x1
    h2_ref[...] = (x1 * lax.rsqrt(jnp.mean(x1 * x1, axis=-1, keepdims=True) + NORM_EPS)
                   * g2_ref[...]).astype(BF16)

    def up_conv(slot, col):
        buf = uext_ref.at[slot % N_UBUF]
        buf[0:HALO, :] = carry_ref[slot]
        buf[HALO:HALO + tm, :] = jnp.dot(h2_ref[...], wup_ref[:, col:col + cw],
                                         preferred_element_type=F32)
        acc = cb_ref[:, col:col + cw] + cw_ref[FFN_CONV - 1:FFN_CONV, col:col + cw] * buf[HALO:HALO + tm, :]
        for k in range(FFN_CONV - 1):
            off = HALO - (FFN_CONV - 1) + k
            acc = acc + cw_ref[k:k + 1, col:col + cw] * buf[off:off + tm, :]
        carry_ref[slot] = buf[tm:tm + HALO, :]
        return acc

    for j in range(FFN_DIM // cw):
        gate = up_conv(2 * j, j * cw)
        val = up_conv(2 * j + 1, FFN_DIM + j * cw)
        act_ref[:, j * cw:(j + 1) * cw] = (gate * _sigmoid(gate) * val).astype(BF16)

    x2 = x1_ref[...] + jnp.dot(act_ref[...], wdn_ref[...], preferred_element_type=F32)
    o_ref[...] = x2 * lax.rsqrt(jnp.mean(x2 * x2, axis=-1, keepdims=True) + NORM_EPS) * gf_ref[...]


def _mix_ffn(x2, att, ssm, w_out, g2, w_up, conv_w, conv_b, w_down, gf, seq):
    t = x2.shape[0]
    n_slots = 2 * (FFN_DIM // FFN_CW)
    tok = lambda width: pl.BlockSpec((FFN_TM, width), lambda i: (i, 0))
    return pl.pallas_call(
        functools.partial(_ffn_body, tiles_per_seq=seq // FFN_TM),
        grid=(t // FFN_TM,),
        in_specs=[
            tok(D_MODEL), tok(ATT_WIDTH), tok(SSM_WIDTH),
            _resident((ATT_WIDTH + SSM_WIDTH, D_MODEL)),
            _resident((1, D_MODEL)),
            _resident((D_MODEL, 2 * FFN_DIM)),
            _resident((FFN_CONV, 2 * FFN_DIM)),
            _resident((1, 2 * FFN_DIM)),
            _resident((FFN_DIM, D_MODEL)),
            _resident((1, D_MODEL)),
        ],
        out_specs=tok(D_MODEL),
        out_shape=jax.ShapeDtypeStruct((t, D_MODEL), F32),
        scratch_shapes=[
            pltpu.VMEM((FFN_TM, D_MODEL), F32),
            pltpu.VMEM((FFN_TM, D_MODEL), BF16),
            pltpu.VMEM((FFN_TM, FFN_DIM), BF16),
            pltpu.VMEM((N_UBUF, HALO + FFN_TM, FFN_CW), F32),
            pltpu.VMEM((n_slots, HALO, FFN_CW), F32),
        ],
        compiler_params=pltpu.CompilerParams(
            dimension_semantics=("arbitrary",), vmem_limit_bytes=VMEM_LIMIT),
        name="mix_ffn",
    )(x2, att, ssm, w_out, g2, w_up, conv_w, conv_b, w_down, gf)


def kernel(x, rel_bias_table, attn_norm_g, w_in, lambda_q1, lambda_k1, lambda_q2, lambda_k2,
           attn_subln_g, ssm_conv_w, ssm_conv_b, ssm_dt_bias, ssm_a_log, ssm_d, ssm_norm_g,
           w_out, ffn_norm_g, ffn_w_up, ffn_conv_w, ffn_conv_b, ffn_w_down, final_norm_g):
    batch, seq, _ = x.shape
    depth = w_in.shape[0]
    assert seq % max(ATT_QB, SSD_Q, FFN_TM) == 0 and (batch * seq) % PROJ_TM == 0
    x2 = x.reshape(batch * seq, D_MODEL)
    row = lambda v: v.astype(F32).reshape(1, -1)
    bias = _bias_tiles(rel_bias_table)
    for i in range(depth):
        lam_init = 0.8 - 0.6 * math.exp(-0.3 * i)
        lam = (jnp.exp(jnp.sum(lambda_q1[i].astype(F32) * lambda_k1[i].astype(F32)))
               - jnp.exp(jnp.sum(lambda_q2[i].astype(F32) * lambda_k2[i].astype(F32)))
               + lam_init).reshape(1)
        w_main = w_in[i, :, :MAIN_COLS].astype(BF16)
        w_dt = jnp.pad(w_in[i, :, MAIN_COLS:], ((0, 0), (0, LANES - DT_COLS))).astype(BF16)
        proj, dt_raw = _in_proj(x2, row(attn_norm_g[i]), w_main, w_dt)
        att = _attention(proj, lam, bias, row(attn_subln_g[i]), batch, seq, lam_init)
        ssm = _ssd(proj, dt_raw, ssm_conv_w[i], ssm_conv_b[i], ssm_dt_bias[i], ssm_a_log[i],
                   ssm_d[i], ssm_norm_g[i], batch, seq)
        assert depth == 1
        x2 = _mix_ffn(x2, att, ssm, w_out[i].astype(BF16), row(ffn_norm_g[i]),
                      ffn_w_up[i].astype(BF16), ffn_conv_w[i].astype(F32), row(ffn_conv_b[i]),
                      ffn_w_down[i].astype(BF16), row(final_norm_g), seq)
    return x2.reshape(batch, seq, D_MODEL)
```

```python
import functools
import math

import jax
import jax.numpy as jnp
from jax import lax
from jax.experimental import pallas as pl
from jax.experimental.pallas import tpu as pltpu

F32 = jnp.float32
BF16 = jnp.bfloat16

D_MODEL = 1024
CHUNK = 64
ATT_HEADS = 8
ATT_HEAD_DIM = 64
ATT_V_DIM = 2 * ATT_HEAD_DIM
ATT_WIDTH = ATT_HEADS * ATT_V_DIM
SSM_HEADS = 16
SSM_HEAD_DIM = 64
SSM_WIDTH = SSM_HEADS * SSM_HEAD_DIM
SSM_GROUPS = 2
SSM_STATE = 128
SSM_CONV = 4
SSM_HEADS_PER_GROUP = SSM_HEADS // SSM_GROUPS
SSM_GROUP_WIDTH = SSM_WIDTH // SSM_GROUPS
FFN_DIM = 2816
FFN_CONV = 3
REL_BUCKETS = 32
REL_MAX_DIST = 128
NORM_EPS = 1e-6
SUBLN_EPS = 1e-5
SSM_NORM_EPS = 1e-5
BC_COLS = 2 * SSM_GROUPS * SSM_STATE
MAIN_COLS = 3 * ATT_WIDTH + SSM_WIDTH + SSM_WIDTH + BC_COLS
DT_COLS = SSM_HEADS
LOG2E = math.log2(math.e)

LANES = 128
SUBLANES = 8
VMEM_LIMIT = 56 * 1024 * 1024

PROJ_TM = 512
PROJ_CN = 512
ATT_QB = 256
ATT_KB = 256
SSD_Q = 256
FFN_TM = 512
FFN_CW = 256
HALO = SUBLANES


def _resident(shape):
    nd = len(shape)
    return pl.BlockSpec(shape, lambda *_: (0,) * nd, pipeline_mode=pl.Buffered(1))


def _sigmoid(x):
    return 1.0 / (1.0 + jnp.exp(-x))


def _inproj_body(x_ref, g_ref, w_ref, wdt_ref, proj_ref, dt_ref):
    x = x_ref[...]
    h = x * lax.rsqrt(jnp.mean(x * x, axis=-1, keepdims=True) + NORM_EPS) * g_ref[...]
    h = h.astype(BF16)
    for j in range(MAIN_COLS // PROJ_CN):
        sl = slice(j * PROJ_CN, (j + 1) * PROJ_CN)
        proj_ref[:, sl] = jnp.dot(h, w_ref[:, sl], preferred_element_type=F32).astype(BF16)
    dt_ref[...] = jnp.dot(h, wdt_ref[...], preferred_element_type=F32)


def _in_proj(x2, g, w_main, w_dt):
    t = x2.shape[0]
    return pl.pallas_call(
        _inproj_body,
        grid=(t // PROJ_TM,),
        in_specs=[
            pl.BlockSpec((PROJ_TM, D_MODEL), lambda i: (i, 0)),
            _resident((1, D_MODEL)),
            _resident((D_MODEL, MAIN_COLS)),
            _resident((D_MODEL, LANES)),
        ],
        out_specs=[
            pl.BlockSpec((PROJ_TM, MAIN_COLS), lambda i: (i, 0)),
            pl.BlockSpec((PROJ_TM, LANES), lambda i: (i, 0)),
        ],
        out_shape=[
            jax.ShapeDtypeStruct((t, MAIN_COLS), BF16),
            jax.ShapeDtypeStruct((t, LANES), F32),
        ],
        compiler_params=pltpu.CompilerParams(
            dimension_semantics=("arbitrary",), vmem_limit_bytes=VMEM_LIMIT),
        name="in_proj",
    )(x2, g, w_main, w_dt)


FAR_BUCKET = REL_BUCKETS // 2 - 1


def _t5_bucket(rel):
    nb = REL_BUCKETS // 2
    max_exact = nb // 2
    bucket = jnp.where(rel > 0, nb, 0)
    n = jnp.abs(rel)
    nf = jnp.maximum(n, 1).astype(F32)
    large = max_exact + (jnp.log(nf / max_exact) / math.log(REL_MAX_DIST / max_exact)
                         * (nb - max_exact)).astype(jnp.int32)
    large = jnp.minimum(large, nb - 1)
    return bucket + jnp.where(n < max_exact, n, large)


def _bias_body(tbl_ref, idx_ref, out_ref):
    h = pl.program_id(0)
    idx = idx_ref[0]
    base = tbl_ref[FAR_BUCKET, h]
    acc = jnp.zeros(idx.shape, F32)
    for b in range(REL_BUCKETS):
        acc = jnp.where(idx == b, (tbl_ref[b, h] - base) * LOG2E, acc)
    out_ref[0, 0] = acc


def _bias_tiles(rel_table):
    qi = jnp.arange(ATT_QB, dtype=jnp.int32)[:, None]
    kj = jnp.arange(ATT_KB, dtype=jnp.int32)[None, :]
    idx = jnp.stack([_t5_bucket(kj - qi), _t5_bucket(kj - qi - ATT_KB)])
    return pl.pallas_call(
        _bias_body,
        grid=(ATT_HEADS, 2),
        in_specs=[
            pl.BlockSpec(memory_space=pltpu.SMEM),
            pl.BlockSpec((1, ATT_QB, ATT_KB), lambda h, d: (d, 0, 0)),
        ],
        out_specs=pl.BlockSpec((1, 1, ATT_QB, ATT_KB), lambda h, d: (h, d, 0, 0)),
        out_shape=jax.ShapeDtypeStruct((ATT_HEADS, 2, ATT_QB, ATT_KB), F32),
        name="rel_bias_tiles",
    )(rel_table.astype(F32), idx)


def _attn_body(lam_ref, q_ref, k_ref, v_ref, bias_ref, subg_ref, o_ref, v1_ref, *, lam_init):
    qb, kb = ATT_QB, ATT_KB
    seq = q_ref.shape[0]
    v1_ref[:, 0:LANES] = v_ref[...]
    v1_ref[:, LANES:2 * LANES] = jnp.ones((seq, LANES), BF16)

    lane = lax.broadcasted_iota(jnp.int32, (qb, LANES), 1)
    row = lax.broadcasted_iota(jnp.int32, (qb, kb), 0)
    col = lax.broadcasted_iota(jnp.int32, (qb, kb), 1)
    allowed = (col // CHUNK) <= (row // CHUNK)
    allowed = jnp.concatenate([allowed, allowed], axis=0)
    b_diag = jnp.concatenate([bias_ref[0, 0]] * 2, axis=0)
    b_prev = jnp.concatenate([bias_ref[0, 1]] * 2, axis=0)
    lam = lam_ref[0]

    for i in range(seq // qb):
        kvl = (i + 1) * kb
        q = q_ref[i * qb:(i + 1) * qb, :]
        zero = jnp.zeros_like(q)
        qs = jnp.concatenate([jnp.where(lane < ATT_HEAD_DIM, q, zero),
                              jnp.where(lane >= ATT_HEAD_DIM, q, zero)], axis=0)
        s = lax.dot_general(qs, k_ref[0:kvl, :], (((1,), (1,)), ((), ())),
                            preferred_element_type=F32)
        blocks = [s[:, j * kb:(j + 1) * kb] for j in range(i + 1)]
        blocks[i] = jnp.where(allowed, blocks[i] + b_diag, -1e30)
        if i >= 1:
            blocks[i - 1] = blocks[i - 1] + b_prev
        m = jnp.max(functools.reduce(jnp.maximum, blocks), axis=-1, keepdims=True)
        p = jnp.concatenate([jnp.exp2(blk - m).astype(BF16) for blk in blocks], axis=1)
        acc = jnp.dot(p, v1_ref[0:kvl, :], preferred_element_type=F32)
        o = acc[:, 0:LANES] / acc[:, LANES:2 * LANES]
        o = o[:qb] - lam * o[qb:]
        o = o * lax.rsqrt(jnp.mean(o * o, axis=-1, keepdims=True) + SUBLN_EPS) * subg_ref[...]
        o_ref[i * qb:(i + 1) * qb, :] = (o * (1.0 - lam_init)).astype(BF16)


def _attention(proj, lam, bias, sub_g, batch, seq, lam_init):
    t = batch * seq
    kcol = ATT_WIDTH // LANES
    vcol = 2 * ATT_WIDTH // LANES
    return pl.pallas_call(
        functools.partial(_attn_body, lam_init=lam_init),
        grid=(batch, ATT_HEADS),
        in_specs=[
            pl.BlockSpec(memory_space=pltpu.SMEM),
            pl.BlockSpec((seq, LANES), lambda b, h: (b, h)),
            pl.BlockSpec((seq, LANES), lambda b, h: (b, kcol + h)),
            pl.BlockSpec((seq, LANES), lambda b, h: (b, vcol + h)),
            pl.BlockSpec((1, 2, ATT_QB, ATT_KB), lambda b, h: (h, 0, 0, 0)),
            pl.BlockSpec((1, ATT_V_DIM), lambda b, h: (0, 0)),
        ],
        out_specs=pl.BlockSpec((seq, LANES), lambda b, h: (b, h)),
        out_shape=jax.ShapeDtypeStruct((t, ATT_WIDTH), BF16),
        scratch_shapes=[pltpu.VMEM((seq, 2 * LANES), BF16)],
        compiler_params=pltpu.CompilerParams(
            dimension_semantics=("arbitrary", "arbitrary"), vmem_limit_bytes=VMEM_LIMIT),
        name="diff_attn",
    )(lam, proj, proj, proj, bias, sub_g)


def _split3(x):
    x1 = x.astype(BF16)
    r1 = x - x1.astype(F32)
    x2 = r1.astype(BF16)
    x3 = (r1 - x2.astype(F32)).astype(BF16)
    return x1, x2, x3


def _ssd_body(z_ref, xs_ref, bc_ref, dt_ref, cwx_ref, cbx_ref, cwbc_ref, cbbc_ref, dtb_ref,
              alog_ref, dexp_ref, ng_ref, e3_ref, o_ref, extx_ref, extbc_ref, state_ref):
    c = pl.program_id(1)
    q = SSD_Q

    @pl.when(c == 0)
    def _():
        extx_ref[0:HALO, :] = jnp.zeros((HALO, SSM_WIDTH), F32)
        extbc_ref[0:HALO, :] = jnp.zeros((HALO, BC_COLS), F32)
        state_ref[...] = jnp.zeros(state_ref.shape, F32)

    def conv_silu(raw_ref, ext_ref, w_ref, b_ref):
        ext_ref[HALO:HALO + q, :] = raw_ref[...].astype(F32)
        acc = b_ref[...] + w_ref[SSM_CONV - 1:SSM_CONV, :] * ext_ref[HALO:HALO + q, :]
        for k in range(SSM_CONV - 1):
            off = HALO - (SSM_CONV - 1) + k
            acc = acc + w_ref[k:k + 1, :] * ext_ref[off:off + q, :]
        ext_ref[0:HALO, :] = ext_ref[q:q + HALO, :]
        return acc * _sigmoid(acc)

    xs = conv_silu(xs_ref, extx_ref, cwx_ref, cbx_ref)
    bc = conv_silu(bc_ref, extbc_ref, cwbc_ref, cbbc_ref)

    dtr = dt_ref[...] + dtb_ref[...]
    dt = jnp.maximum(dtr, 0.0) + jnp.log(1.0 + jnp.exp(-jnp.abs(dtr)))
    a = -jnp.exp(alog_ref[...]) * dt

    row = lax.broadcasted_iota(jnp.int32, (q, q), 0)
    col = lax.broadcasted_iota(jnp.int32, (q, q), 1)
    causal = col <= row
    tril = jnp.where(causal, 1.0, 0.0).astype(BF16)
    acum = sum(jnp.dot(tril, t, preferred_element_type=F32) for t in _split3(a))

    lane = lax.broadcasted_iota(jnp.int32, (q, LANES), 1)

    def expand(x):
        parts = [jnp.where(lane < SSM_HEADS, t.astype(F32), 0.0) for t in _split3(x)]
        packed = parts[0] + pltpu.roll(parts[1], SSM_HEADS, 1) + pltpu.roll(parts[2], 2 * SSM_HEADS, 1)
        return jnp.dot(packed.astype(BF16), e3_ref[...], preferred_element_type=F32)

    dt_exp = expand(dt)
    acum_exp = expand(acum)
    ea_exp = jnp.exp(acum_exp)
    last_exp = acum_exp[q - 1:q, :]
    dte_exp = jnp.exp(last_exp - acum_exp)
    ea_last = ea_exp[q - 1:q, :]

    xdt = xs * dt_exp
    xdt_b = xdt.astype(BF16)
    acum_t = acum.T
    pair_lane = lax.broadcasted_iota(jnp.int32, (q, LANES), 1)

    y_groups = []
    for g in range(SSM_GROUPS):
        gsl = slice(g * SSM_GROUP_WIDTH, (g + 1) * SSM_GROUP_WIDTH)
        bg = bc[:, g * SSM_STATE:(g + 1) * SSM_STATE]
        cg = bc[:, (SSM_GROUPS + g) * SSM_STATE:(SSM_GROUPS + g + 1) * SSM_STATE]
        bg_b = bg.astype(BF16)
        cg_b = cg.astype(BF16)
        cb = lax.dot_general(cg_b, bg_b, (((1,), (1,)), ((), ())), preferred_element_type=F32)

        def masked(h):
            seg = acum[:, h:h + 1] - acum_t[h:h + 1, :]
            return (cb * jnp.exp(jnp.where(causal, seg, -jnp.inf))).astype(BF16)

        pairs = []
        for j in range(SSM_HEADS_PER_GROUP // 2):
            h0 = g * SSM_HEADS_PER_GROUP + 2 * j
            xp = xdt_b[:, h0 * SSM_HEAD_DIM:(h0 + 2) * SSM_HEAD_DIM]
            r0 = jnp.dot(masked(h0), xp, preferred_element_type=F32)
            r1 = jnp.dot(masked(h0 + 1), xp, preferred_element_type=F32)
            pairs.append(jnp.where(pair_lane < SSM_HEAD_DIM, r0, r1))
        y_diag = jnp.concatenate(pairs, axis=1)

        st = state_ref[g]
        y_off = jnp.dot(cg_b, st.astype(BF16), preferred_element_type=F32) * ea_exp[:, gsl]
        w = (xdt[:, gsl] * dte_exp[:, gsl]).astype(BF16)
        s_new = jnp.dot(bg.T.astype(BF16), w, preferred_element_type=F32)
        state_ref[g] = st * ea_last[:, gsl] + s_new

        y = y_diag + y_off + xs[:, gsl] * dexp_ref[:, gsl]
        zf = z_ref[:, gsl].astype(F32)
        gated = y * (zf * _sigmoid(zf))
        gated = gated * lax.rsqrt(jnp.mean(gated * gated, axis=-1, keepdims=True) + SSM_NORM_EPS)
        y_groups.append(gated * ng_ref[:, gsl])
    o_ref[...] = jnp.concatenate(y_groups, axis=1).astype(BF16)


def _ssd(proj, dt_raw, conv_w, conv_b, dt_bias, a_log, d_skip, norm_g, batch, seq):
    nc = seq // SSD_Q
    t = batch * seq
    zcol = 3 * ATT_WIDTH // SSM_WIDTH
    xcol = zcol + 1
    bccol = (3 * ATT_WIDTH + 2 * SSM_WIDTH) // BC_COLS
    pad = LANES - SSM_HEADS
    dtb = jnp.pad(dt_bias.astype(F32), (0, pad)).reshape(1, LANES)
    alog = jnp.pad(a_log.astype(F32), (0, pad)).reshape(1, LANES)
    dexp = jnp.repeat(d_skip.astype(F32), SSM_HEAD_DIM).reshape(1, SSM_WIDTH)
    r = jnp.arange(LANES)[:, None]
    hcol = (jnp.arange(SSM_WIDTH) // SSM_HEAD_DIM)[None, :]
    e3 = jnp.where((r % SSM_HEADS == hcol) & (r < 3 * SSM_HEADS), 1.0, 0.0).astype(BF16)
    cw = conv_w.astype(F32)
    cb = conv_b.astype(F32).reshape(1, -1)
    row_spec = lambda width, colblk: pl.BlockSpec((SSD_Q, width), lambda b, c: (b * nc + c, colblk))
    return pl.pallas_call(
        _ssd_body,
        grid=(batch, nc),
        in_specs=[
            row_spec(SSM_WIDTH, zcol),
            row_spec(SSM_WIDTH, xcol),
            row_spec(BC_COLS, bccol),
            row_spec(LANES, 0),
            _resident((SSM_CONV, SSM_WIDTH)),
            _resident((1, SSM_WIDTH)),
            _resident((SSM_CONV, BC_COLS)),
            _resident((1, BC_COLS)),
            _resident((1, LANES)),
            _resident((1, LANES)),
            _resident((1, SSM_WIDTH)),
            _resident((1, SSM_WIDTH)),
            _resident((LANES, SSM_WIDTH)),
        ],
        out_specs=pl.BlockSpec((SSD_Q, SSM_WIDTH), lambda b, c: (b * nc + c, 0)),
        out_shape=jax.ShapeDtypeStruct((t, SSM_WIDTH), BF16),
        scratch_shapes=[
            pltpu.VMEM((HALO + SSD_Q, SSM_WIDTH), F32),
            pltpu.VMEM((HALO + SSD_Q, BC_COLS), F32),
            pltpu.VMEM((SSM_GROUPS, SSM_STATE, SSM_GROUP_WIDTH), F32),
        ],
        compiler_params=pltpu.CompilerParams(
            dimension_semantics=("arbitrary", "arbitrary"), vmem_limit_bytes=VMEM_LIMIT),
        name="ssd_mixer",
    )(proj, proj, proj, dt_raw, cw[:, :SSM_WIDTH], cb[:, :SSM_WIDTH], cw[:, SSM_WIDTH:],
      cb[:, SSM_WIDTH:], dtb, alog, dexp, norm_g.astype(F32).reshape(1, -1), e3)


N_UBUF = 4


def _ffn_body(x_ref, att_ref, ssm_ref, wo_ref, g2_ref, wup_ref, cw_ref, cb_ref, wdn_ref, gf_ref,
              o_ref, x1_ref, h2_ref, act_ref, uext_ref, carry_ref, *, tiles_per_seq):
    i = pl.program_id(0)
    tm, cw = FFN_TM, FFN_CW

    @pl.when(i % tiles_per_seq == 0)
    def _():
        carry_ref[...] = jnp.zeros(carry_ref.shape, F32)

    x1 = (x_ref[...]
          + jnp.dot(att_ref[...], wo_ref[0:ATT_WIDTH, :], preferred_element_type=F32)
          + jnp.dot(ssm_ref[...], wo_ref[ATT_WIDTH:, :], preferred_element_type=F32))
    x1_ref[...] = x1
    h2_ref[...] = (x1 * lax.rsqrt(jnp.mean(x1 * x1, axis=-1, keepdims=True) + NORM_EPS)
                   * g2_ref[...]).astype(BF16)

    def up_conv(slot, col):
        buf = uext_ref.at[slot % N_UBUF]
        buf[0:HALO, :] = carry_ref[slot]
        buf[HALO:HALO + tm, :] = jnp.dot(h2_ref[...], wup_ref[:, col:col + cw],
                                         preferred_element_type=F32)
        acc = cb_ref[:, col:col + cw] + cw_ref[FFN_CONV - 1:FFN_CONV, col:col + cw] * buf[HALO:HALO + tm, :]
        for k in range(FFN_CONV - 1):
            off = HALO - (FFN_CONV - 1) + k
            acc = acc + cw_ref[k:k + 1, col:col + cw] * buf[off:off + tm, :]
        carry_ref[slot] = buf[tm:tm + HALO, :]
        return acc

    for j in range(FFN_DIM // cw):
        gate = up_conv(2 * j, j * cw)
        val = up_conv(2 * j + 1, FFN_DIM + j * cw)
        act_ref[:, j * cw:(j + 1) * cw] = (gate * _sigmoid(gate) * val).astype(BF16)

    x2 = x1_ref[...] + jnp.dot(act_ref[...], wdn_ref[...], preferred_element_type=F32)
    o_ref[...] = x2 * lax.rsqrt(jnp.mean(x2 * x2, axis=-1, keepdims=True) + NORM_EPS) * gf_ref[...]


def _mix_ffn(x2, att, ssm, w_out, g2, w_up, conv_w, conv_b, w_down, gf, seq):
    t = x2.shape[0]
    n_slots = 2 * (FFN_DIM // FFN_CW)
    tok = lambda width: pl.BlockSpec((FFN_TM, width), lambda i: (i, 0))
    return pl.pallas_call(
        functools.partial(_ffn_body, tiles_per_seq=seq // FFN_TM),
        grid=(t // FFN_TM,),
        in_specs=[
            tok(D_MODEL), tok(ATT_WIDTH), tok(SSM_WIDTH),
            _resident((ATT_WIDTH + SSM_WIDTH, D_MODEL)),
            _resident((1, D_MODEL)),
            _resident((D_MODEL, 2 * FFN_DIM)),
            _resident((FFN_CONV, 2 * FFN_DIM)),
            _resident((1, 2 * FFN_DIM)),
            _resident((FFN_DIM, D_MODEL)),
            _resident((1, D_MODEL)),
        ],
        out_specs=tok(D_MODEL),
        out_shape=jax.ShapeDtypeStruct((t, D_MODEL), F32),
        scratch_shapes=[
            pltpu.VMEM((FFN_TM, D_MODEL), F32),
            pltpu.VMEM((FFN_TM, D_MODEL), BF16),
            pltpu.VMEM((FFN_TM, FFN_DIM), BF16),
            pltpu.VMEM((N_UBUF, HALO + FFN_TM, FFN_CW), F32),
            pltpu.VMEM((n_slots, HALO, FFN_CW), F32),
        ],
        compiler_params=pltpu.CompilerParams(
            dimension_semantics=("arbitrary",), vmem_limit_bytes=VMEM_LIMIT),
        name="mix_ffn",
    )(x2, att, ssm, w_out, g2, w_up, conv_w, conv_b, w_down, gf)


def kernel(x, rel_bias_table, attn_norm_g, w_in, lambda_q1, lambda_k1, lambda_q2, lambda_k2,
           attn_subln_g, ssm_conv_w, ssm_conv_b, ssm_dt_bias, ssm_a_log, ssm_d, ssm_norm_g,
           w_out, ffn_norm_g, ffn_w_up, ffn_conv_w, ffn_conv_b, ffn_w_down, final_norm_g):
    batch, seq, _ = x.shape
    depth = w_in.shape[0]
    assert seq % max(ATT_QB, SSD_Q, FFN_TM) == 0 and (batch * seq) % PROJ_TM == 0
    x2 = x.reshape(batch * seq, D_MODEL)
    row = lambda v: v.astype(F32).reshape(1, -1)
    bias = _bias_tiles(rel_bias_table)
    for i in range(depth):
        lam_init = 0.8 - 0.6 * math.exp(-0.3 * i)
        lam = (jnp.exp(jnp.sum(lambda_q1[i].astype(F32) * lambda_k1[i].astype(F32)))
               - jnp.exp(jnp.sum(lambda_q2[i].astype(F32) * lambda_k2[i].astype(F32)))
               + lam_init).reshape(1)
        col_scale = jnp.where(jnp.arange(MAIN_COLS) < ATT_WIDTH, LOG2E * ATT_HEAD_DIM ** -0.5, 1.0)
        w_main = (w_in[i, :, :MAIN_COLS] * col_scale.astype(F32)).astype(BF16)
        w_dt = jnp.pad(w_in[i, :, MAIN_COLS:], ((0, 0), (0, LANES - DT_COLS))).astype(BF16)
        proj, dt_raw = _in_proj(x2, row(attn_norm_g[i]), w_main, w_dt)
        att = _attention(proj, lam, bias, row(attn_subln_g[i]), batch, seq, lam_init)
        ssm = _ssd(proj, dt_raw, ssm_conv_w[i], ssm_conv_b[i], ssm_dt_bias[i], ssm_a_log[i],
                   ssm_d[i], ssm_norm_g[i], batch, seq)
        assert depth == 1
        x2 = _mix_ffn(x2, att, ssm, w_out[i].astype(BF16), row(ffn_norm_g[i]),
                      ffn_w_up[i].astype(BF16), ffn_conv_w[i].astype(F32), row(ffn_conv_b[i]),
                      ffn_w_down[i].astype(BF16), row(final_norm_g), seq)
    return x2.reshape(batch, seq, D_MODEL)
```

```python
import functools
import math

import jax
import jax.numpy as jnp
from jax import lax
from jax.experimental import pallas as pl
from jax.experimental.pallas import tpu as pltpu

F32 = jnp.float32
BF16 = jnp.bfloat16

D_MODEL = 1024
CHUNK = 64
ATT_HEADS = 8
ATT_HEAD_DIM = 64
ATT_V_DIM = 2 * ATT_HEAD_DIM
ATT_WIDTH = ATT_HEADS * ATT_V_DIM
SSM_HEADS = 16
SSM_HEAD_DIM = 64
SSM_WIDTH = SSM_HEADS * SSM_HEAD_DIM
SSM_GROUPS = 2
SSM_STATE = 128
SSM_CONV = 4
SSM_HEADS_PER_GROUP = SSM_HEADS // SSM_GROUPS
SSM_GROUP_WIDTH = SSM_WIDTH // SSM_GROUPS
FFN_DIM = 2816
FFN_CONV = 3
REL_BUCKETS = 32
REL_MAX_DIST = 128
NORM_EPS = 1e-6
SUBLN_EPS = 1e-5
SSM_NORM_EPS = 1e-5
BC_COLS = 2 * SSM_GROUPS * SSM_STATE
MAIN_COLS = 3 * ATT_WIDTH + SSM_WIDTH + SSM_WIDTH + BC_COLS
DT_COLS = SSM_HEADS
LOG2E = math.log2(math.e)

LANES = 128
SUBLANES = 8
VMEM_LIMIT = 56 * 1024 * 1024

PROJ_TM = 512
PROJ_CN = 512
ATT_QB = 256
ATT_KB = 256
SSD_Q = 256
FFN_TM = 512
FFN_CW = 256
HALO = SUBLANES


def _resident(shape):
    nd = len(shape)
    return pl.BlockSpec(shape, lambda *_: (0,) * nd, pipeline_mode=pl.Buffered(1))


def _sigmoid(x):
    return 1.0 / (1.0 + jnp.exp(-x))


def _inproj_body(x_ref, g_ref, w_ref, wdt_ref, proj_ref, dt_ref):
    x = x_ref[...]
    h = x * lax.rsqrt(jnp.mean(x * x, axis=-1, keepdims=True) + NORM_EPS) * g_ref[...]
    h = h.astype(BF16)
    for j in range(MAIN_COLS // PROJ_CN):
        sl = slice(j * PROJ_CN, (j + 1) * PROJ_CN)
        proj_ref[:, sl] = jnp.dot(h, w_ref[:, sl], preferred_element_type=F32).astype(BF16)
    dt_ref[...] = jnp.dot(h, wdt_ref[...], preferred_element_type=F32)


def _in_proj(x2, g, w_main, w_dt):
    t = x2.shape[0]
    return pl.pallas_call(
        _inproj_body,
        grid=(t // PROJ_TM,),
        in_specs=[
            pl.BlockSpec((PROJ_TM, D_MODEL), lambda i: (i, 0)),
            _resident((1, D_MODEL)),
            _resident((D_MODEL, MAIN_COLS)),
            _resident((D_MODEL, LANES)),
        ],
        out_specs=[
            pl.BlockSpec((PROJ_TM, MAIN_COLS), lambda i: (i, 0)),
            pl.BlockSpec((PROJ_TM, LANES), lambda i: (i, 0)),
        ],
        out_shape=[
            jax.ShapeDtypeStruct((t, MAIN_COLS), BF16),
            jax.ShapeDtypeStruct((t, LANES), F32),
        ],
        compiler_params=pltpu.CompilerParams(
            dimension_semantics=("arbitrary",), vmem_limit_bytes=VMEM_LIMIT),
        name="in_proj",
    )(x2, g, w_main, w_dt)


FAR_BUCKET = REL_BUCKETS // 2 - 1


def _t5_bucket(rel):
    nb = REL_BUCKETS // 2
    max_exact = nb // 2
    bucket = jnp.where(rel > 0, nb, 0)
    n = jnp.abs(rel)
    nf = jnp.maximum(n, 1).astype(F32)
    large = max_exact + (jnp.log(nf / max_exact) / math.log(REL_MAX_DIST / max_exact)
                         * (nb - max_exact)).astype(jnp.int32)
    large = jnp.minimum(large, nb - 1)
    return bucket + jnp.where(n < max_exact, n, large)


def _bias_body(tbl_ref, idx_ref, out_ref):
    h = pl.program_id(0)
    idx = idx_ref[0]
    base = tbl_ref[FAR_BUCKET, h]
    acc = jnp.zeros(idx.shape, F32)
    for b in range(REL_BUCKETS):
        acc = jnp.where(idx == b, (tbl_ref[b, h] - base) * LOG2E, acc)
    out_ref[0, 0] = acc


def _bias_tiles(rel_table):
    qi = jnp.arange(ATT_QB, dtype=jnp.int32)[:, None]
    kj = jnp.arange(ATT_KB, dtype=jnp.int32)[None, :]
    idx = jnp.stack([_t5_bucket(kj - qi), _t5_bucket(kj - qi - ATT_KB)])
    return pl.pallas_call(
        _bias_body,
        grid=(ATT_HEADS, 2),
        in_specs=[
            pl.BlockSpec(memory_space=pltpu.SMEM),
            pl.BlockSpec((1, ATT_QB, ATT_KB), lambda h, d: (d, 0, 0)),
        ],
        out_specs=pl.BlockSpec((1, 1, ATT_QB, ATT_KB), lambda h, d: (h, d, 0, 0)),
        out_shape=jax.ShapeDtypeStruct((ATT_HEADS, 2, ATT_QB, ATT_KB), F32),
        name="rel_bias_tiles",
    )(rel_table.astype(F32), idx)


def _attn_body(lam_ref, q_ref, k_ref, v_ref, bias_ref, subg_ref, o_ref, v1_ref, *, lam_init):
    qb, kb = ATT_QB, ATT_KB
    seq = q_ref.shape[0]
    v1_ref[:, 0:LANES] = v_ref[...]
    v1_ref[:, LANES:2 * LANES] = jnp.ones((seq, LANES), BF16)

    lane = lax.broadcasted_iota(jnp.int32, (qb, LANES), 1)
    row = lax.broadcasted_iota(jnp.int32, (qb, kb), 0)
    col = lax.broadcasted_iota(jnp.int32, (qb, kb), 1)
    allowed = (col // CHUNK) <= (row // CHUNK)
    allowed = jnp.concatenate([allowed, allowed], axis=0)
    b_diag = jnp.concatenate([bias_ref[0, 0]] * 2, axis=0)
    b_prev = jnp.concatenate([bias_ref[0, 1]] * 2, axis=0)
    lam = lam_ref[0]

    for i in reversed(range(seq // qb)):
        kvl = (i + 1) * kb
        q = q_ref[i * qb:(i + 1) * qb, :]
        zero = jnp.zeros_like(q)
        qs = jnp.concatenate([jnp.where(lane < ATT_HEAD_DIM, q, zero),
                              jnp.where(lane >= ATT_HEAD_DIM, q, zero)], axis=0)
        s = lax.dot_general(qs, k_ref[0:kvl, :], (((1,), (1,)), ((), ())),
                            preferred_element_type=F32)
        blocks = [s[:, j * kb:(j + 1) * kb] for j in range(i + 1)]
        blocks[i] = jnp.where(allowed, blocks[i] + b_diag, -1e30)
        if i >= 1:
            blocks[i - 1] = blocks[i - 1] + b_prev
        m = jnp.max(functools.reduce(jnp.maximum, blocks), axis=-1, keepdims=True)
        p = jnp.concatenate([jnp.exp2(blk - m).astype(BF16) for blk in blocks], axis=1)
        acc = jnp.dot(p, v1_ref[0:kvl, :], preferred_element_type=F32)
        o = acc[:, 0:LANES] / acc[:, LANES:2 * LANES]
        o = o[:qb] - lam * o[qb:]
        o = o * lax.rsqrt(jnp.mean(o * o, axis=-1, keepdims=True) + SUBLN_EPS) * subg_ref[...]
        o_ref[i * qb:(i + 1) * qb, :] = (o * (1.0 - lam_init)).astype(BF16)


def _attention(proj, lam, bias, sub_g, batch, seq, lam_init):
    t = batch * seq
    kcol = ATT_WIDTH // LANES
    vcol = 2 * ATT_WIDTH // LANES
    return pl.pallas_call(
        functools.partial(_attn_body, lam_init=lam_init),
        grid=(batch, ATT_HEADS),
        in_specs=[
            pl.BlockSpec(memory_space=pltpu.SMEM),
            pl.BlockSpec((seq, LANES), lambda b, h: (b, h)),
            pl.BlockSpec((seq, LANES), lambda b, h: (b, kcol + h)),
            pl.BlockSpec((seq, LANES), lambda b, h: (b, vcol + h)),
            pl.BlockSpec((1, 2, ATT_QB, ATT_KB), lambda b, h: (h, 0, 0, 0)),
            pl.BlockSpec((1, ATT_V_DIM), lambda b, h: (0, 0)),
        ],
        out_specs=pl.BlockSpec((seq, LANES), lambda b, h: (b, h)),
        out_shape=jax.ShapeDtypeStruct((t, ATT_WIDTH), BF16),
        scratch_shapes=[pltpu.VMEM((seq, 2 * LANES), BF16)],
        compiler_params=pltpu.CompilerParams(
            dimension_semantics=("arbitrary", "arbitrary"), vmem_limit_bytes=VMEM_LIMIT),
        name="diff_attn",
    )(lam, proj, proj, proj, bias, sub_g)


def _split3(x):
    x1 = x.astype(BF16)
    r1 = x - x1.astype(F32)
    x2 = r1.astype(BF16)
    x3 = (r1 - x2.astype(F32)).astype(BF16)
    return x1, x2, x3


def _ssd_body(z_ref, xs_ref, bc_ref, dt_ref, cwx_ref, cbx_ref, cwbc_ref, cbbc_ref, dtb_ref,
              alog_ref, dexp_ref, ng_ref, e3_ref, shift_ref, tril_ref, o_ref, halox_ref, halobc_ref,
              state_ref):
    c = pl.program_id(1)
    q = SSD_Q

    @pl.when(c == 0)
    def _():
        halox_ref[...] = jnp.zeros(halox_ref.shape, F32)
        halobc_ref[...] = jnp.zeros(halobc_ref.shape, F32)
        state_ref[...] = jnp.zeros(state_ref.shape, F32)

    def conv_silu(raw_ref, halo_ref, w_ref, b_ref):
        xb = raw_ref[...]
        x = xb.astype(F32)
        shifted = jnp.dot(shift_ref[...], xb, preferred_element_type=F32)
        acc = b_ref[...] + w_ref[SSM_CONV - 1:SSM_CONV, :] * x
        head = jnp.zeros((HALO, x.shape[1]), F32)
        for j in range(1, SSM_CONV):
            wj = w_ref[SSM_CONV - 1 - j:SSM_CONV - j, :]
            acc = acc + wj * shifted[(j - 1) * q:j * q]
            head = head + wj * halo_ref[HALO - j:2 * HALO - j, :]
        acc = jnp.concatenate([acc[0:HALO] + head, acc[HALO:]], axis=0)
        halo_ref[0:HALO, :] = x[q - HALO:q]
        return acc * _sigmoid(acc)

    xs = conv_silu(xs_ref, halox_ref, cwx_ref, cbx_ref)
    bc = conv_silu(bc_ref, halobc_ref, cwbc_ref, cbbc_ref)

    dtr = dt_ref[...] + dtb_ref[...]
    dt = jnp.maximum(dtr, 0.0) + jnp.log(1.0 + jnp.exp(-jnp.abs(dtr)))
    a = (-LOG2E * jnp.exp(alog_ref[...])) * dt

    row = lax.broadcasted_iota(jnp.int32, (q, q), 0)
    col = lax.broadcasted_iota(jnp.int32, (q, q), 1)
    causal = col <= row
    tril = tril_ref[...]
    acum = sum(jnp.dot(tril, t, preferred_element_type=F32) for t in _split3(a))

    lane = lax.broadcasted_iota(jnp.int32, (q, LANES), 1)

    def expand(x):
        parts = [jnp.where(lane < SSM_HEADS, t.astype(F32), 0.0) for t in _split3(x)]
        packed = parts[0] + pltpu.roll(parts[1], SSM_HEADS, 1) + pltpu.roll(parts[2], 2 * SSM_HEADS, 1)
        return jnp.dot(packed.astype(BF16), e3_ref[...], preferred_element_type=F32)

    dt_exp = expand(dt)
    acum_exp = expand(acum)
    ea_exp = jnp.exp2(acum_exp)
    last_exp = acum_exp[q - 1:q, :]
    dte_exp = jnp.exp2(last_exp - acum_exp)
    ea_last = ea_exp[q - 1:q, :]

    xdt = xs * dt_exp
    xdt_b = xdt.astype(BF16)
    acum_t = acum.T
    pair_lane = lax.broadcasted_iota(jnp.int32, (q, LANES), 1)

    y_groups = []
    for g in range(SSM_GROUPS):
        gsl = slice(g * SSM_GROUP_WIDTH, (g + 1) * SSM_GROUP_WIDTH)
        bg = bc[:, g * SSM_STATE:(g + 1) * SSM_STATE]
        cg = bc[:, (SSM_GROUPS + g) * SSM_STATE:(SSM_GROUPS + g + 1) * SSM_STATE]
        bg_b = bg.astype(BF16)
        cg_b = cg.astype(BF16)
        cb = lax.dot_general(cg_b, bg_b, (((1,), (1,)), ((), ())), preferred_element_type=F32)

        def masked(h):
            seg = acum[:, h:h + 1] - acum_t[h:h + 1, :]
            return (cb * jnp.exp2(jnp.where(causal, seg, -jnp.inf))).astype(BF16)

        pairs = []
        for j in range(SSM_HEADS_PER_GROUP // 2):
            h0 = g * SSM_HEADS_PER_GROUP + 2 * j
            xp = xdt_b[:, h0 * SSM_HEAD_DIM:(h0 + 2) * SSM_HEAD_DIM]
            r0 = jnp.dot(masked(h0), xp, preferred_element_type=F32)
            r1 = jnp.dot(masked(h0 + 1), xp, preferred_element_type=F32)
            pairs.append(jnp.where(pair_lane < SSM_HEAD_DIM, r0, r1))
        y_diag = jnp.concatenate(pairs, axis=1)

        st = state_ref[g]
        y_off = jnp.dot(cg_b, st.astype(BF16), preferred_element_type=F32) * ea_exp[:, gsl]
        w = (xdt[:, gsl] * dte_exp[:, gsl]).astype(BF16)
        s_new = jnp.dot(bg.T.astype(BF16), w, preferred_element_type=F32)
        state_ref[g] = st * ea_last[:, gsl] + s_new

        y = y_diag + y_off + xs[:, gsl] * dexp_ref[:, gsl]
        zf = z_ref[:, gsl].astype(F32)
        gated = y * (zf * _sigmoid(zf))
        gated = gated * lax.rsqrt(jnp.mean(gated * gated, axis=-1, keepdims=True) + SSM_NORM_EPS)
        y_groups.append(gated * ng_ref[:, gsl])
    o_ref[...] = jnp.concatenate(y_groups, axis=1).astype(BF16)


def _ssd(proj, dt_raw, conv_w, conv_b, dt_bias, a_log, d_skip, norm_g, batch, seq):
    nc = seq // SSD_Q
    t = batch * seq
    zcol = 3 * ATT_WIDTH // SSM_WIDTH
    xcol = zcol + 1
    bccol = (3 * ATT_WIDTH + 2 * SSM_WIDTH) // BC_COLS
    pad = LANES - SSM_HEADS
    dtb = jnp.pad(dt_bias.astype(F32), (0, pad)).reshape(1, LANES)
    alog = jnp.pad(a_log.astype(F32), (0, pad)).reshape(1, LANES)
    dexp = jnp.repeat(d_skip.astype(F32), SSM_HEAD_DIM).reshape(1, SSM_WIDTH)
    r = jnp.arange(LANES)[:, None]
    hcol = (jnp.arange(SSM_WIDTH) // SSM_HEAD_DIM)[None, :]
    e3 = jnp.where((r % SSM_HEADS == hcol) & (r < 3 * SSM_HEADS), 1.0, 0.0).astype(BF16)
    cw = conv_w.astype(F32)
    cb = conv_b.astype(F32).reshape(1, -1)
    tq = jnp.arange(SSD_Q)
    delta = tq[:, None] - tq[None, :]
    shift = jnp.concatenate([jnp.where(delta == j, 1.0, 0.0) for j in range(1, SSM_CONV)]).astype(BF16)
    tril = jnp.where(delta >= 0, 1.0, 0.0).astype(BF16)
    row_spec = lambda width, colblk: pl.BlockSpec((SSD_Q, width), lambda b, c: (b * nc + c, colblk))
    return pl.pallas_call(
        _ssd_body,
        grid=(batch, nc),
        in_specs=[
            row_spec(SSM_WIDTH, zcol),
            row_spec(SSM_WIDTH, xcol),
            row_spec(BC_COLS, bccol),
            row_spec(LANES, 0),
            _resident((SSM_CONV, SSM_WIDTH)),
            _resident((1, SSM_WIDTH)),
            _resident((SSM_CONV, BC_COLS)),
            _resident((1, BC_COLS)),
            _resident((1, LANES)),
            _resident((1, LANES)),
            _resident((1, SSM_WIDTH)),
            _resident((1, SSM_WIDTH)),
            _resident((LANES, SSM_WIDTH)),
            _resident(((SSM_CONV - 1) * SSD_Q, SSD_Q)),
            _resident((SSD_Q, SSD_Q)),
        ],
        out_specs=pl.BlockSpec((SSD_Q, SSM_WIDTH), lambda b, c: (b * nc + c, 0)),
        out_shape=jax.ShapeDtypeStruct((t, SSM_WIDTH), BF16),
        scratch_shapes=[
            pltpu.VMEM((2 * HALO, SSM_WIDTH), F32),
            pltpu.VMEM((2 * HALO, BC_COLS), F32),
            pltpu.VMEM((SSM_GROUPS, SSM_STATE, SSM_GROUP_WIDTH), F32),
        ],
        compiler_params=pltpu.CompilerParams(
            dimension_semantics=("arbitrary", "arbitrary"), vmem_limit_bytes=VMEM_LIMIT),
        name="ssd_mixer",
    )(proj, proj, proj, dt_raw, cw[:, :SSM_WIDTH], cb[:, :SSM_WIDTH], cw[:, SSM_WIDTH:],
      cb[:, SSM_WIDTH:], dtb, alog, dexp, norm_g.astype(F32).reshape(1, -1), e3, shift, tril)


N_UBUF = 4


def _ffn_body(x_ref, att_ref, ssm_ref, wo_ref, g2_ref, wup_ref, cw_ref, cb_ref, wdn_ref, gf_ref,
              o_ref, x1_ref, h2_ref, act_ref, uext_ref, carry_ref, *, tiles_per_seq):
    i = pl.program_id(0)
    tm, cw = FFN_TM, FFN_CW

    @pl.when(i % tiles_per_seq == 0)
    def _():
        carry_ref[...] = jnp.zeros(carry_ref.shape, F32)

    x1 = (x_ref[...]
          + jnp.dot(att_ref[...], wo_ref[0:ATT_WIDTH, :], preferred_element_type=F32)
          + jnp.dot(ssm_ref[...], wo_ref[ATT_WIDTH:, :], preferred_element_type=F32))
    x1_ref[...] = x1
    h2_ref[...] = (x1 * lax.rsqrt(jnp.mean(x1 * x1, axis=-1, keepdims=True) + NORM_EPS)
                   * g2_ref[...]).astype(BF16)

    def up_conv(slot, col):
        buf = uext_ref.at[slot % N_UBUF]
        buf[0:HALO, :] = carry_ref[slot]
        buf[HALO:HALO + tm, :] = jnp.dot(h2_ref[...], wup_ref[:, col:col + cw],
                                         preferred_element_type=F32)
        acc = cb_ref[:, col:col + cw] + cw_ref[FFN_CONV - 1:FFN_CONV, col:col + cw] * buf[HALO:HALO + tm, :]
        for k in range(FFN_CONV - 1):
            off = HALO - (FFN_CONV - 1) + k
            acc = acc + cw_ref[k:k + 1, col:col + cw] * buf[off:off + tm, :]
        carry_ref[slot] = buf[tm:tm + HALO, :]
        return acc

    for j in range(FFN_DIM // cw):
        gate = up_conv(2 * j, j * cw)
        val = up_conv(2 * j + 1, FFN_DIM + j * cw)
        act_ref[:, j * cw:(j + 1) * cw] = (gate * _sigmoid(gate) * val).astype(BF16)

    x2 = x1_ref[...] + jnp.dot(act_ref[...], wdn_ref[...], preferred_element_type=F32)
    o_ref[...] = x2 * lax.rsqrt(jnp.mean(x2 * x2, axis=-1, keepdims=True) + NORM_EPS) * gf_ref[...]


def _mix_ffn(x2, att, ssm, w_out, g2, w_up, conv_w, conv_b, w_down, gf, seq):
    t = x2.shape[0]
    n_slots = 2 * (FFN_DIM // FFN_CW)
    tok = lambda width: pl.BlockSpec((FFN_TM, width), lambda i: (i, 0))
    return pl.pallas_call(
        functools.partial(_ffn_body, tiles_per_seq=seq // FFN_TM),
        grid=(t // FFN_TM,),
        in_specs=[
            tok(D_MODEL), tok(ATT_WIDTH), tok(SSM_WIDTH),
            _resident((ATT_WIDTH + SSM_WIDTH, D_MODEL)),
            _resident((1, D_MODEL)),
            _resident((D_MODEL, 2 * FFN_DIM)),
            _resident((FFN_CONV, 2 * FFN_DIM)),
            _resident((1, 2 * FFN_DIM)),
            _resident((FFN_DIM, D_MODEL)),
            _resident((1, D_MODEL)),
        ],
        out_specs=tok(D_MODEL),
        out_shape=jax.ShapeDtypeStruct((t, D_MODEL), F32),
        scratch_shapes=[
            pltpu.VMEM((FFN_TM, D_MODEL), F32),
            pltpu.VMEM((FFN_TM, D_MODEL), BF16),
            pltpu.VMEM((FFN_TM, FFN_DIM), BF16),
            pltpu.VMEM((N_UBUF, HALO + FFN_TM, FFN_CW), F32),
            pltpu.VMEM((n_slots, HALO, FFN_CW), F32),
        ],
        compiler_params=pltpu.CompilerParams(
            dimension_semantics=("arbitrary",), vmem_limit_bytes=VMEM_LIMIT),
        name="mix_ffn",
    )(x2, att, ssm, w_out, g2, w_up, conv_w, conv_b, w_down, gf)


def kernel(x, rel_bias_table, attn_norm_g, w_in, lambda_q1, lambda_k1, lambda_q2, lambda_k2,
           attn_subln_g, ssm_conv_w, ssm_conv_b, ssm_dt_bias, ssm_a_log, ssm_d, ssm_norm_g,
           w_out, ffn_norm_g, ffn_w_up, ffn_conv_w, ffn_conv_b, ffn_w_down, final_norm_g):
    batch, seq, _ = x.shape
    depth = w_in.shape[0]
    assert seq % max(ATT_QB, SSD_Q, FFN_TM) == 0 and (batch * seq) % PROJ_TM == 0
    x2 = x.reshape(batch * seq, D_MODEL)
    row = lambda v: v.astype(F32).reshape(1, -1)
    bias = _bias_tiles(rel_bias_table)
    for i in range(depth):
        lam_init = 0.8 - 0.6 * math.exp(-0.3 * i)
        lam = (jnp.exp(jnp.sum(lambda_q1[i].astype(F32) * lambda_k1[i].astype(F32)))
               - jnp.exp(jnp.sum(lambda_q2[i].astype(F32) * lambda_k2[i].astype(F32)))
               + lam_init).reshape(1)
        col_scale = jnp.where(jnp.arange(MAIN_COLS) < ATT_WIDTH, LOG2E * ATT_HEAD_DIM ** -0.5, 1.0)
        w_main = (w_in[i, :, :MAIN_COLS] * col_scale.astype(F32)).astype(BF16)
        w_dt = jnp.pad(w_in[i, :, MAIN_COLS:], ((0, 0), (0, LANES - DT_COLS))).astype(BF16)
        proj, dt_raw = _in_proj(x2, row(attn_norm_g[i]), w_main, w_dt)
        att = _attention(proj, lam, bias, row(attn_subln_g[i]), batch, seq, lam_init)
        ssm = _ssd(proj, dt_raw, ssm_conv_w[i], ssm_conv_b[i], ssm_dt_bias[i], ssm_a_log[i],
                   ssm_d[i], ssm_norm_g[i], batch, seq)
        assert depth == 1
        x2 = _mix_ffn(x2, att, ssm, w_out[i].astype(BF16), row(ffn_norm_g[i]),
                      ffn_w_up[i].astype(BF16), ffn_conv_w[i].astype(F32), row(ffn_conv_b[i]),
                      ffn_w_down[i].astype(BF16), row(final_norm_g), seq)
    return x2.reshape(batch, seq, D_MODEL)
```

```python
import functools
import math

import jax
import jax.numpy as jnp
from jax import lax
from jax.experimental import pallas as pl
from jax.experimental.pallas import tpu as pltpu

F32 = jnp.float32
BF16 = jnp.bfloat16

D_MODEL = 1024
CHUNK = 64
ATT_HEADS = 8
ATT_HEAD_DIM = 64
ATT_V_DIM = 2 * ATT_HEAD_DIM
ATT_WIDTH = ATT_HEADS * ATT_V_DIM
SSM_HEADS = 16
SSM_HEAD_DIM = 64
SSM_WIDTH = SSM_HEADS * SSM_HEAD_DIM
SSM_GROUPS = 2
SSM_STATE = 128
SSM_CONV = 4
SSM_HEADS_PER_GROUP = SSM_HEADS // SSM_GROUPS
SSM_GROUP_WIDTH = SSM_WIDTH // SSM_GROUPS
FFN_DIM = 2816
FFN_CONV = 3
REL_BUCKETS = 32
REL_MAX_DIST = 128
NORM_EPS = 1e-6
SUBLN_EPS = 1e-5
SSM_NORM_EPS = 1e-5
BC_COLS = 2 * SSM_GROUPS * SSM_STATE
MAIN_COLS = 3 * ATT_WIDTH + SSM_WIDTH + SSM_WIDTH + BC_COLS
DT_COLS = SSM_HEADS
LOG2E = math.log2(math.e)

LANES = 128
SUBLANES = 8
VMEM_LIMIT = 56 * 1024 * 1024

PROJ_TM = 512
PROJ_CN = 512
ATT_QB = 256
ATT_KB = 256
SSD_Q = 256
FFN_TM = 512
FFN_CW = 256
HALO = SUBLANES


def _resident(shape):
    nd = len(shape)
    return pl.BlockSpec(shape, lambda *_: (0,) * nd, pipeline_mode=pl.Buffered(1))


def _sigmoid(x):
    return 1.0 / (1.0 + jnp.exp(-x))


def _inproj_body(x_ref, g_ref, w_ref, wdt_ref, proj_ref, dt_ref):
    x = x_ref[...]
    h = x * lax.rsqrt(jnp.mean(x * x, axis=-1, keepdims=True) + NORM_EPS) * g_ref[...]
    h = h.astype(BF16)
    for j in range(MAIN_COLS // PROJ_CN):
        sl = slice(j * PROJ_CN, (j + 1) * PROJ_CN)
        proj_ref[:, sl] = jnp.dot(h, w_ref[:, sl], preferred_element_type=F32).astype(BF16)
    dt_ref[...] = jnp.dot(h, wdt_ref[...], preferred_element_type=F32)


def _in_proj(x2, g, w_main, w_dt):
    t = x2.shape[0]
    return pl.pallas_call(
        _inproj_body,
        grid=(t // PROJ_TM,),
        in_specs=[
            pl.BlockSpec((PROJ_TM, D_MODEL), lambda i: (i, 0)),
            _resident((1, D_MODEL)),
            _resident((D_MODEL, MAIN_COLS)),
            _resident((D_MODEL, LANES)),
        ],
        out_specs=[
            pl.BlockSpec((PROJ_TM, MAIN_COLS), lambda i: (i, 0)),
            pl.BlockSpec((PROJ_TM, LANES), lambda i: (i, 0)),
        ],
        out_shape=[
            jax.ShapeDtypeStruct((t, MAIN_COLS), BF16),
            jax.ShapeDtypeStruct((t, LANES), F32),
        ],
        compiler_params=pltpu.CompilerParams(
            dimension_semantics=("arbitrary",), vmem_limit_bytes=VMEM_LIMIT),
        name="in_proj",
    )(x2, g, w_main, w_dt)


FAR_BUCKET = REL_BUCKETS // 2 - 1
BIAS_SPAN = 4 * ATT_KB


def _t5_bucket(rel):
    nb = REL_BUCKETS // 2
    max_exact = nb // 2
    bucket = jnp.where(rel > 0, nb, 0)
    n = jnp.abs(rel)
    nf = jnp.maximum(n, 1).astype(F32)
    large = max_exact + (jnp.log(nf / max_exact) / math.log(REL_MAX_DIST / max_exact)
                         * (nb - max_exact)).astype(jnp.int32)
    large = jnp.minimum(large, nb - 1)
    return bucket + jnp.where(n < max_exact, n, large)


def _bias_bucket_row():
    rel = jnp.arange(BIAS_SPAN, dtype=jnp.int32) - 2 * ATT_KB
    return jnp.broadcast_to(_t5_bucket(rel)[None, :], (SUBLANES, BIAS_SPAN))


def _bias_tiles(tbl_ref, idx_ref, h):
    idx = idx_ref[...]
    base = tbl_ref[FAR_BUCKET, h]
    r = jnp.zeros(idx.shape, F32)
    for b in range(REL_BUCKETS):
        r = jnp.where(idx == b, (tbl_ref[b, h] - base) * LOG2E, r)
    rows = jnp.concatenate([r] * (ATT_QB // SUBLANES), axis=0)
    rolled = pltpu.roll(rows, 0, 1, stride=1, stride_axis=0)
    return rolled[:, 2 * ATT_KB:3 * ATT_KB], rolled[:, ATT_KB:2 * ATT_KB]


def _attn_body(lam_ref, tbl_ref, q_ref, k_ref, v_ref, idx_ref, subg_ref, o_ref, v1_ref, *, lam_init):
    qb, kb = ATT_QB, ATT_KB
    seq = q_ref.shape[0]
    v1_ref[:, 0:LANES] = v_ref[...]
    v1_ref[:, LANES:2 * LANES] = jnp.ones((seq, LANES), BF16)

    lane = lax.broadcasted_iota(jnp.int32, (qb, LANES), 1)
    row = lax.broadcasted_iota(jnp.int32, (qb, kb), 0)
    col = lax.broadcasted_iota(jnp.int32, (qb, kb), 1)
    allowed = (col // CHUNK) <= (row // CHUNK)
    allowed = jnp.concatenate([allowed, allowed], axis=0)
    b_diag, b_prev = _bias_tiles(tbl_ref, idx_ref, pl.program_id(1))
    b_diag = jnp.concatenate([b_diag] * 2, axis=0)
    b_prev = jnp.concatenate([b_prev] * 2, axis=0)
    lam = lam_ref[0]

    for i in reversed(range(seq // qb)):
        kvl = (i + 1) * kb
        q = q_ref[i * qb:(i + 1) * qb, :]
        zero = jnp.zeros_like(q)
        qs = jnp.concatenate([jnp.where(lane < ATT_HEAD_DIM, q, zero),
                              jnp.where(lane >= ATT_HEAD_DIM, q, zero)], axis=0)
        s = lax.dot_general(qs, k_ref[0:kvl, :], (((1,), (1,)), ((), ())),
                            preferred_element_type=F32)
        blocks = [s[:, j * kb:(j + 1) * kb] for j in range(i + 1)]
        blocks[i] = jnp.where(allowed, blocks[i] + b_diag, -1e30)
        if i >= 1:
            blocks[i - 1] = blocks[i - 1] + b_prev
        m = jnp.max(functools.reduce(jnp.maximum, blocks), axis=-1, keepdims=True)
        p = jnp.concatenate([jnp.exp2(blk - m).astype(BF16) for blk in blocks], axis=1)
        acc = jnp.dot(p, v1_ref[0:kvl, :], preferred_element_type=F32)
        o = acc[:, 0:LANES] / acc[:, LANES:2 * LANES]
        o = o[:qb] - lam * o[qb:]
        o = o * lax.rsqrt(jnp.mean(o * o, axis=-1, keepdims=True) + SUBLN_EPS) * subg_ref[...]
        o_ref[i * qb:(i + 1) * qb, :] = (o * (1.0 - lam_init)).astype(BF16)


def _attention(proj, lam, rel_table, sub_g, batch, seq, lam_init):
    t = batch * seq
    kcol = ATT_WIDTH // LANES
    vcol = 2 * ATT_WIDTH // LANES
    return pl.pallas_call(
        functools.partial(_attn_body, lam_init=lam_init),
        grid=(batch, ATT_HEADS),
        in_specs=[
            pl.BlockSpec(memory_space=pltpu.SMEM),
            pl.BlockSpec(memory_space=pltpu.SMEM),
            pl.BlockSpec((seq, LANES), lambda b, h: (b, h)),
            pl.BlockSpec((seq, LANES), lambda b, h: (b, kcol + h)),
            pl.BlockSpec((seq, LANES), lambda b, h: (b, vcol + h)),
            _resident((SUBLANES, BIAS_SPAN)),
            pl.BlockSpec((1, ATT_V_DIM), lambda b, h: (0, 0)),
        ],
        out_specs=pl.BlockSpec((seq, LANES), lambda b, h: (b, h)),
        out_shape=jax.ShapeDtypeStruct((t, ATT_WIDTH), BF16),
        scratch_shapes=[pltpu.VMEM((seq, 2 * LANES), BF16)],
        compiler_params=pltpu.CompilerParams(
            dimension_semantics=("arbitrary", "arbitrary"), vmem_limit_bytes=VMEM_LIMIT),
        name="diff_attn",
    )(lam, rel_table.astype(F32), proj, proj, proj, _bias_bucket_row(), sub_g)


def _split3(x):
    x1 = x.astype(BF16)
    r1 = x - x1.astype(F32)
    x2 = r1.astype(BF16)
    x3 = (r1 - x2.astype(F32)).astype(BF16)
    return x1, x2, x3


def _ssd_body(z_ref, xs_ref, bc_ref, dt_ref, cwx_ref, cbx_ref, cwbc_ref, cbbc_ref, dtb_ref,
              alog_ref, dexp_ref, ng_ref, e3_ref, shift_ref, tril_ref, o_ref, halox_ref, halobc_ref,
              state_ref):
    c = pl.program_id(1)
    q = SSD_Q

    @pl.when(c == 0)
    def _():
        halox_ref[...] = jnp.zeros(halox_ref.shape, F32)
        halobc_ref[...] = jnp.zeros(halobc_ref.shape, F32)
        state_ref[...] = jnp.zeros(state_ref.shape, F32)

    def conv_silu(raw_ref, halo_ref, w_ref, b_ref):
        xb = raw_ref[...]
        x = xb.astype(F32)
        shifted = jnp.dot(shift_ref[...], xb, preferred_element_type=F32)
        acc = b_ref[...] + w_ref[SSM_CONV - 1:SSM_CONV, :] * x
        head = jnp.zeros((HALO, x.shape[1]), F32)
        for j in range(1, SSM_CONV):
            wj = w_ref[SSM_CONV - 1 - j:SSM_CONV - j, :]
            acc = acc + wj * shifted[(j - 1) * q:j * q]
            head = head + wj * halo_ref[HALO - j:2 * HALO - j, :]
        acc = jnp.concatenate([acc[0:HALO] + head, acc[HALO:]], axis=0)
        halo_ref[0:HALO, :] = x[q - HALO:q]
        return acc * _sigmoid(acc)

    xs = conv_silu(xs_ref, halox_ref, cwx_ref, cbx_ref)
    bc = conv_silu(bc_ref, halobc_ref, cwbc_ref, cbbc_ref)

    dtr = dt_ref[...] + dtb_ref[...]
    dt = jnp.maximum(dtr, 0.0) + jnp.log(1.0 + jnp.exp(-jnp.abs(dtr)))
    a = (-LOG2E * jnp.exp(alog_ref[...])) * dt

    row = lax.broadcasted_iota(jnp.int32, (q, q), 0)
    col = lax.broadcasted_iota(jnp.int32, (q, q), 1)
    causal = col <= row
    tril = tril_ref[...]
    acum = sum(jnp.dot(tril, t, preferred_element_type=F32) for t in _split3(a))

    lane = lax.broadcasted_iota(jnp.int32, (q, LANES), 1)

    def expand(x):
        parts = [jnp.where(lane < SSM_HEADS, t.astype(F32), 0.0) for t in _split3(x)]
        packed = parts[0] + pltpu.roll(parts[1], SSM_HEADS, 1) + pltpu.roll(parts[2], 2 * SSM_HEADS, 1)
        return jnp.dot(packed.astype(BF16), e3_ref[...], preferred_element_type=F32)

    dt_exp = expand(dt)
    acum_exp = expand(acum)
    ea_exp = jnp.exp2(acum_exp)
    last_exp = acum_exp[q - 1:q, :]
    dte_exp = jnp.exp2(last_exp - acum_exp)
    ea_last = ea_exp[q - 1:q, :]

    xdt = xs * dt_exp
    xdt_b = xdt.astype(BF16)
    acum_t = acum.T
    pair_lane = lax.broadcasted_iota(jnp.int32, (q, LANES), 1)

    y_groups = []
    for g in range(SSM_GROUPS):
        gsl = slice(g * SSM_GROUP_WIDTH, (g + 1) * SSM_GROUP_WIDTH)
        bg = bc[:, g * SSM_STATE:(g + 1) * SSM_STATE]
        cg = bc[:, (SSM_GROUPS + g) * SSM_STATE:(SSM_GROUPS + g + 1) * SSM_STATE]
        bg_b = bg.astype(BF16)
        cg_b = cg.astype(BF16)
        cb = lax.dot_general(cg_b, bg_b, (((1,), (1,)), ((), ())), preferred_element_type=F32)

        def masked(h):
            seg = acum[:, h:h + 1] - acum_t[h:h + 1, :]
            return (cb * jnp.exp2(jnp.where(causal, seg, -jnp.inf))).astype(BF16)

        pairs = []
        for j in range(SSM_HEADS_PER_GROUP // 2):
            h0 = g * SSM_HEADS_PER_GROUP + 2 * j
            xp = xdt_b[:, h0 * SSM_HEAD_DIM:(h0 + 2) * SSM_HEAD_DIM]
            r0 = jnp.dot(masked(h0), xp, preferred_element_type=F32)
            r1 = jnp.dot(masked(h0 + 1), xp, preferred_element_type=F32)
            pairs.append(jnp.where(pair_lane < SSM_HEAD_DIM, r0, r1))
        y_diag = jnp.concatenate(pairs, axis=1)

        st = state_ref[g]
        y_off = jnp.dot(cg_b, st.astype(BF16), preferred_element_type=F32) * ea_exp[:, gsl]
        w = (xdt[:, gsl] * dte_exp[:, gsl]).astype(BF16)
        s_new = jnp.dot(bg.T.astype(BF16), w, preferred_element_type=F32)
        state_ref[g] = st * ea_last[:, gsl] + s_new

        y = y_diag + y_off + xs[:, gsl] * dexp_ref[:, gsl]
        zf = z_ref[:, gsl].astype(F32)
        gated = y * (zf * _sigmoid(zf))
        gated = gated * lax.rsqrt(jnp.mean(gated * gated, axis=-1, keepdims=True) + SSM_NORM_EPS)
        y_groups.append(gated * ng_ref[:, gsl])
    o_ref[...] = jnp.concatenate(y_groups, axis=1).astype(BF16)


def _ssd(proj, dt_raw, conv_w, conv_b, dt_bias, a_log, d_skip, norm_g, batch, seq):
    nc = seq // SSD_Q
    t = batch * seq
    zcol = 3 * ATT_WIDTH // SSM_WIDTH
    xcol = zcol + 1
    bccol = (3 * ATT_WIDTH + 2 * SSM_WIDTH) // BC_COLS
    pad = LANES - SSM_HEADS
    dtb = jnp.pad(dt_bias.astype(F32), (0, pad)).reshape(1, LANES)
    alog = jnp.pad(a_log.astype(F32), (0, pad)).reshape(1, LANES)
    dexp = jnp.repeat(d_skip.astype(F32), SSM_HEAD_DIM).reshape(1, SSM_WIDTH)
    r = jnp.arange(LANES)[:, None]
    hcol = (jnp.arange(SSM_WIDTH) // SSM_HEAD_DIM)[None, :]
    e3 = jnp.where((r % SSM_HEADS == hcol) & (r < 3 * SSM_HEADS), 1.0, 0.0).astype(BF16)
    cw = conv_w.astype(F32)
    cb = conv_b.astype(F32).reshape(1, -1)
    tq = jnp.arange(SSD_Q)
    delta = tq[:, None] - tq[None, :]
    shift = jnp.concatenate([jnp.where(delta == j, 1.0, 0.0) for j in range(1, SSM_CONV)]).astype(BF16)
    tril = jnp.where(delta >= 0, 1.0, 0.0).astype(BF16)
    row_spec = lambda width, colblk: pl.BlockSpec((SSD_Q, width), lambda b, c: (b * nc + c, colblk))
    return pl.pallas_call(
        _ssd_body,
        grid=(batch, nc),
        in_specs=[
            row_spec(SSM_WIDTH, zcol),
            row_spec(SSM_WIDTH, xcol),
            row_spec(BC_COLS, bccol),
            row_spec(LANES, 0),
            _resident((SSM_CONV, SSM_WIDTH)),
            _resident((1, SSM_WIDTH)),
            _resident((SSM_CONV, BC_COLS)),
            _resident((1, BC_COLS)),
            _resident((1, LANES)),
            _resident((1, LANES)),
            _resident((1, SSM_WIDTH)),
            _resident((1, SSM_WIDTH)),
            _resident((LANES, SSM_WIDTH)),
            _resident(((SSM_CONV - 1) * SSD_Q, SSD_Q)),
            _resident((SSD_Q, SSD_Q)),
        ],
        out_specs=pl.BlockSpec((SSD_Q, SSM_WIDTH), lambda b, c: (b * nc + c, 0)),
        out_shape=jax.ShapeDtypeStruct((t, SSM_WIDTH), BF16),
        scratch_shapes=[
            pltpu.VMEM((2 * HALO, SSM_WIDTH), F32),
            pltpu.VMEM((2 * HALO, BC_COLS), F32),
            pltpu.VMEM((SSM_GROUPS, SSM_STATE, SSM_GROUP_WIDTH), F32),
        ],
        compiler_params=pltpu.CompilerParams(
            dimension_semantics=("arbitrary", "arbitrary"), vmem_limit_bytes=VMEM_LIMIT),
        name="ssd_mixer",
    )(proj, proj, proj, dt_raw, cw[:, :SSM_WIDTH], cb[:, :SSM_WIDTH], cw[:, SSM_WIDTH:],
      cb[:, SSM_WIDTH:], dtb, alog, dexp, norm_g.astype(F32).reshape(1, -1), e3, shift, tril)


N_UBUF = 4


def _ffn_body(x_ref, att_ref, ssm_ref, wo_ref, g2_ref, wup_ref, cw_ref, cb_ref, wdn_ref, gf_ref,
              o_ref, x1_ref, h2_ref, act_ref, uext_ref, carry_ref, *, tiles_per_seq):
    i = pl.program_id(0)
    tm, cw = FFN_TM, FFN_CW

    @pl.when(i % tiles_per_seq == 0)
    def _():
        carry_ref[...] = jnp.zeros(carry_ref.shape, F32)

    x1 = (x_ref[...]
          + jnp.dot(att_ref[...], wo_ref[0:ATT_WIDTH, :], preferred_element_type=F32)
          + jnp.dot(ssm_ref[...], wo_ref[ATT_WIDTH:, :], preferred_element_type=F32))
    x1_ref[...] = x1
    h2_ref[...] = (x1 * lax.rsqrt(jnp.mean(x1 * x1, axis=-1, keepdims=True) + NORM_EPS)
                   * g2_ref[...]).astype(BF16)

    def up_conv(slot, col):
        buf = uext_ref.at[slot % N_UBUF]
        buf[0:HALO, :] = carry_ref[slot]
        buf[HALO:HALO + tm, :] = jnp.dot(h2_ref[...], wup_ref[:, col:col + cw],
                                         preferred_element_type=F32)
        acc = cb_ref[:, col:col + cw] + cw_ref[FFN_CONV - 1:FFN_CONV, col:col + cw] * buf[HALO:HALO + tm, :]
        for k in range(FFN_CONV - 1):
            off = HALO - (FFN_CONV - 1) + k
            acc = acc + cw_ref[k:k + 1, col:col + cw] * buf[off:off + tm, :]
        carry_ref[slot] = buf[tm:tm + HALO, :]
        return acc

    for j in range(FFN_DIM // cw):
        gate = up_conv(2 * j, j * cw)
        val = up_conv(2 * j + 1, FFN_DIM + j * cw)
        act_ref[:, j * cw:(j + 1) * cw] = (gate * _sigmoid(gate) * val).astype(BF16)

    x2 = x1_ref[...] + jnp.dot(act_ref[...], wdn_ref[...], preferred_element_type=F32)
    o_ref[...] = x2 * lax.rsqrt(jnp.mean(x2 * x2, axis=-1, keepdims=True) + NORM_EPS) * gf_ref[...]


def _mix_ffn(x2, att, ssm, w_out, g2, w_up, conv_w, conv_b, w_down, gf, seq):
    t = x2.shape[0]
    n_slots = 2 * (FFN_DIM // FFN_CW)
    tok = lambda width: pl.BlockSpec((FFN_TM, width), lambda i: (i, 0))
    return pl.pallas_call(
        functools.partial(_ffn_body, tiles_per_seq=seq // FFN_TM),
        grid=(t // FFN_TM,),
        in_specs=[
            tok(D_MODEL), tok(ATT_WIDTH), tok(SSM_WIDTH),
            _resident((ATT_WIDTH + SSM_WIDTH, D_MODEL)),
            _resident((1, D_MODEL)),
            _resident((D_MODEL, 2 * FFN_DIM)),
            _resident((FFN_CONV, 2 * FFN_DIM)),
            _resident((1, 2 * FFN_DIM)),
            _resident((FFN_DIM, D_MODEL)),
            _resident((1, D_MODEL)),
        ],
        out_specs=tok(D_MODEL),
        out_shape=jax.ShapeDtypeStruct((t, D_MODEL), F32),
        scratch_shapes=[
            pltpu.VMEM((FFN_TM, D_MODEL), F32),
            pltpu.VMEM((FFN_TM, D_MODEL), BF16),
            pltpu.VMEM((FFN_TM, FFN_DIM), BF16),
            pltpu.VMEM((N_UBUF, HALO + FFN_TM, FFN_CW), F32),
            pltpu.VMEM((n_slots, HALO, FFN_CW), F32),
        ],
        compiler_params=pltpu.CompilerParams(
            dimension_semantics=("arbitrary",), vmem_limit_bytes=VMEM_LIMIT),
        name="mix_ffn",
    )(x2, att, ssm, w_out, g2, w_up, conv_w, conv_b, w_down, gf)


def kernel(x, rel_bias_table, attn_norm_g, w_in, lambda_q1, lambda_k1, lambda_q2, lambda_k2,
           attn_subln_g, ssm_conv_w, ssm_conv_b, ssm_dt_bias, ssm_a_log, ssm_d, ssm_norm_g,
           w_out, ffn_norm_g, ffn_w_up, ffn_conv_w, ffn_conv_b, ffn_w_down, final_norm_g):
    batch, seq, _ = x.shape
    depth = w_in.shape[0]
    assert seq % max(ATT_QB, SSD_Q, FFN_TM) == 0 and (batch * seq) % PROJ_TM == 0
    x2 = x.reshape(batch * seq, D_MODEL)
    row = lambda v: v.astype(F32).reshape(1, -1)
    for i in range(depth):
        lam_init = 0.8 - 0.6 * math.exp(-0.3 * i)
        lam = (jnp.exp(jnp.sum(lambda_q1[i].astype(F32) * lambda_k1[i].astype(F32)))
               - jnp.exp(jnp.sum(lambda_q2[i].astype(F32) * lambda_k2[i].astype(F32)))
               + lam_init).reshape(1)
        col_scale = jnp.where(jnp.arange(MAIN_COLS + DT_COLS) < ATT_WIDTH, LOG2E * ATT_HEAD_DIM ** -0.5, 1.0)
        w_all = (w_in[i] * col_scale.astype(F32)).astype(BF16)
        w_dt = jnp.pad(w_in[i, :, MAIN_COLS:], ((0, 0), (0, LANES - DT_COLS))).astype(BF16)
        proj, dt_raw = _in_proj(x2, row(attn_norm_g[i]), w_all, w_dt)
        att = _attention(proj, lam, rel_bias_table, row(attn_subln_g[i]), batch, seq, lam_init)
        ssm = _ssd(proj, dt_raw, ssm_conv_w[i], ssm_conv_b[i], ssm_dt_bias[i], ssm_a_log[i],
                   ssm_d[i], ssm_norm_g[i], batch, seq)
        assert depth == 1
        x2 = _mix_ffn(x2, att, ssm, w_out[i].astype(BF16), row(ffn_norm_g[i]),
                      ffn_w_up[i].astype(BF16), ffn_conv_w[i].astype(F32), row(ffn_conv_b[i]),
                      ffn_w_down[i].astype(BF16), row(final_norm_g), seq)
    return x2.reshape(batch, seq, D_MODEL)
```

```python
import functools
import math

import jax
import jax.numpy as jnp
from jax import lax
from jax.experimental import pallas as pl
from jax.experimental.pallas import tpu as pltpu

F32 = jnp.float32
BF16 = jnp.bfloat16

D_MODEL = 1024
CHUNK = 64
ATT_HEADS = 8
ATT_HEAD_DIM = 64
ATT_V_DIM = 2 * ATT_HEAD_DIM
ATT_WIDTH = ATT_HEADS * ATT_V_DIM
SSM_HEADS = 16
SSM_HEAD_DIM = 64
SSM_WIDTH = SSM_HEADS * SSM_HEAD_DIM
SSM_GROUPS = 2
SSM_STATE = 128
SSM_CONV = 4
SSM_HEADS_PER_GROUP = SSM_HEADS // SSM_GROUPS
SSM_GROUP_WIDTH = SSM_WIDTH // SSM_GROUPS
FFN_DIM = 2816
FFN_CONV = 3
REL_BUCKETS = 32
REL_MAX_DIST = 128
NORM_EPS = 1e-6
SUBLN_EPS = 1e-5
SSM_NORM_EPS = 1e-5
BC_COLS = 2 * SSM_GROUPS * SSM_STATE
MAIN_COLS = 3 * ATT_WIDTH + SSM_WIDTH + SSM_WIDTH + BC_COLS
DT_COLS = SSM_HEADS
LOG2E = math.log2(math.e)

LANES = 128
SUBLANES = 8
VMEM_LIMIT = 56 * 1024 * 1024

PROJ_TM = 512
PROJ_CN = 512
ATT_QB = 256
ATT_KB = 256
SSD_Q = 256
FFN_TM = 512
FFN_CW = 256
HALO = SUBLANES
OTHER_WORK_PER_BLOCK = (1, 2, 1, 2, 1, 1, 1, 0)


def _resident(shape):
    nd = len(shape)
    return pl.BlockSpec(shape, lambda *_: (0,) * nd, pipeline_mode=pl.Buffered(1))


def _sigmoid(x):
    return 1.0 / (1.0 + jnp.exp(-x))


def _inproj_body(x_ref, g_ref, w_ref, wdt_ref, proj_ref, dt_ref):
    x = x_ref[...]
    h = x * lax.rsqrt(jnp.mean(x * x, axis=-1, keepdims=True) + NORM_EPS) * g_ref[...]
    h = h.astype(BF16)
    for j in range(MAIN_COLS // PROJ_CN):
        sl = slice(j * PROJ_CN, (j + 1) * PROJ_CN)
        proj_ref[:, sl] = jnp.dot(h, w_ref[:, sl], preferred_element_type=F32).astype(BF16)
    dt_ref[...] = jnp.dot(h, wdt_ref[...], preferred_element_type=F32)


def _in_proj(x2, g, w_main, w_dt):
    t = x2.shape[0]
    return pl.pallas_call(
        _inproj_body,
        grid=(t // PROJ_TM,),
        in_specs=[
            pl.BlockSpec((PROJ_TM, D_MODEL), lambda i: (i, 0)),
            _resident((1, D_MODEL)),
            _resident((D_MODEL, MAIN_COLS)),
            _resident((D_MODEL, LANES)),
        ],
        out_specs=[
            pl.BlockSpec((PROJ_TM, MAIN_COLS), lambda i: (i, 0)),
            pl.BlockSpec((PROJ_TM, LANES), lambda i: (i, 0)),
        ],
        out_shape=[
            jax.ShapeDtypeStruct((t, MAIN_COLS), BF16),
            jax.ShapeDtypeStruct((t, LANES), F32),
        ],
        compiler_params=pltpu.CompilerParams(
            dimension_semantics=("arbitrary",), vmem_limit_bytes=VMEM_LIMIT),
        name="in_proj",
    )(x2, g, w_main, w_dt)


FAR_BUCKET = REL_BUCKETS // 2 - 1
BIAS_SPAN = 4 * ATT_KB


def _t5_bucket(rel):
    nb = REL_BUCKETS // 2
    max_exact = nb // 2
    bucket = jnp.where(rel > 0, nb, 0)
    n = jnp.abs(rel)
    nf = jnp.maximum(n, 1).astype(F32)
    large = max_exact + (jnp.log(nf / max_exact) / math.log(REL_MAX_DIST / max_exact)
                         * (nb - max_exact)).astype(jnp.int32)
    large = jnp.minimum(large, nb - 1)
    return bucket + jnp.where(n < max_exact, n, large)


def _bias_bucket_row():
    rel = jnp.arange(BIAS_SPAN, dtype=jnp.int32) - 2 * ATT_KB
    return jnp.broadcast_to(_t5_bucket(rel)[None, :], (SUBLANES, BIAS_SPAN))


def _bias_tiles(tbl_ref, idx_ref, h):
    idx = idx_ref[...]
    base = tbl_ref[FAR_BUCKET, h]
    r = jnp.zeros(idx.shape, F32)
    for b in range(REL_BUCKETS):
        r = jnp.where(idx == b, (tbl_ref[b, h] - base) * LOG2E, r)
    rows = jnp.concatenate([r] * (ATT_QB // SUBLANES), axis=0)
    rolled = pltpu.roll(rows, 0, 1, stride=1, stride_axis=0)
    return rolled[:, 2 * ATT_KB:3 * ATT_KB], rolled[:, ATT_KB:2 * ATT_KB]


def _attn_body(lam_ref, tbl_ref, q_ref, k_ref, v_ref, idx_ref, subg_ref, o_ref, v1_ref, s_ref, m_ref,
               p_ref, *, lam_init, other_work):
    qb, kb = ATT_QB, ATT_KB
    seq = q_ref.shape[0]
    v1_ref[:, 0:LANES] = v_ref[...]
    v1_ref[:, LANES:2 * LANES] = jnp.ones((seq, LANES), BF16)

    lane = lax.broadcasted_iota(jnp.int32, (qb, LANES), 1)
    row = lax.broadcasted_iota(jnp.int32, (qb, kb), 0)
    col = lax.broadcasted_iota(jnp.int32, (qb, kb), 1)
    allowed = (col // CHUNK) <= (row // CHUNK)
    allowed = jnp.concatenate([allowed, allowed], axis=0)
    b_diag, b_prev = _bias_tiles(tbl_ref, idx_ref, pl.program_id(1))
    b_diag = jnp.concatenate([b_diag] * 2, axis=0)
    b_prev = jnp.concatenate([b_prev] * 2, axis=0)
    lam = lam_ref[0]

    def scores(i, slot):
        kvl = (i + 1) * kb
        q = q_ref[i * qb:(i + 1) * qb, :]
        zero = jnp.zeros_like(q)
        qs = jnp.concatenate([jnp.where(lane < ATT_HEAD_DIM, q, zero),
                              jnp.where(lane >= ATT_HEAD_DIM, q, zero)], axis=0)
        s = lax.dot_general(qs, k_ref[0:kvl, :], (((1,), (1,)), ((), ())),
                            preferred_element_type=F32)
        blocks = [s[:, j * kb:(j + 1) * kb] for j in range(i + 1)]
        blocks[i] = jnp.where(allowed, blocks[i] + b_diag, -1e30)
        if i >= 1:
            blocks[i - 1] = blocks[i - 1] + b_prev
        for j, blk in enumerate(blocks):
            s_ref[slot, :, j * kb:(j + 1) * kb] = blk
        m = jnp.max(functools.reduce(jnp.maximum, blocks), axis=-1, keepdims=True)
        m_ref[slot] = jnp.broadcast_to(m, (2 * qb, LANES))

    def probs(i, slot):
        m = jnp.concatenate([m_ref[slot]] * (kb // LANES), axis=1)
        for j in range(i + 1):
            cols = slice(j * kb, (j + 1) * kb)
            p_ref[slot, :, cols] = jnp.exp2(s_ref[slot, :, cols] - m).astype(BF16)

    def finish(i, slot):
        kvl = (i + 1) * kb
        acc = jnp.dot(p_ref[slot, :, 0:kvl], v1_ref[0:kvl, :], preferred_element_type=F32)
        o = acc[:, 0:LANES] / acc[:, LANES:2 * LANES]
        o = o[:qb] - lam * o[qb:]
        o = o * lax.rsqrt(jnp.mean(o * o, axis=-1, keepdims=True) + SUBLN_EPS) * subg_ref[...]
        o_ref[i * qb:(i + 1) * qb, :] = (o * (1.0 - lam_init)).astype(BF16)

    order = list(reversed(range(seq // qb)))
    for n in range(len(order) + 2):
        for _ in range(OTHER_WORK_PER_BLOCK[n] if n < len(OTHER_WORK_PER_BLOCK) else 0):
            next(other_work, None)
        if n < len(order):
            scores(order[n], n % 2)
        if 0 <= n - 1 < len(order):
            probs(order[n - 1], (n - 1) % 2)
        if 0 <= n - 2 < len(order):
            finish(order[n - 2], (n - 2) % 2)


def _attention_call(proj, lam, rel_table, sub_g, batch, seq):
    kcol = ATT_WIDTH // LANES
    vcol = 2 * ATT_WIDTH // LANES
    in_specs = [
        pl.BlockSpec(memory_space=pltpu.SMEM),
        pl.BlockSpec(memory_space=pltpu.SMEM),
        pl.BlockSpec((seq, LANES), lambda b, h: (b, h)),
        pl.BlockSpec((seq, LANES), lambda b, h: (b, kcol + h)),
        pl.BlockSpec((seq, LANES), lambda b, h: (b, vcol + h)),
        _resident((SUBLANES, BIAS_SPAN)),
        pl.BlockSpec((1, ATT_V_DIM), lambda b, h: (0, 0)),
    ]
    operands = (lam, rel_table.astype(F32), proj, proj, proj, _bias_bucket_row(), sub_g)
    out_spec = pl.BlockSpec((seq, LANES), lambda b, h: (b, h))
    out_shape = jax.ShapeDtypeStruct((batch * seq, ATT_WIDTH), BF16)
    scratch = [
        pltpu.VMEM((seq, 2 * LANES), BF16),
        pltpu.VMEM((2, 2 * ATT_QB, seq), F32),
        pltpu.VMEM((2, 2 * ATT_QB, LANES), F32),
        pltpu.VMEM((2, 2 * ATT_QB, seq), BF16),
    ]
    return in_specs, operands, out_spec, out_shape, scratch


def _split3(x):
    x1 = x.astype(BF16)
    r1 = x - x1.astype(F32)
    x2 = r1.astype(BF16)
    x3 = (r1 - x2.astype(F32)).astype(BF16)
    return x1, x2, x3


def _ssd_steps(z_ref, xs_ref, bc_ref, dt_ref, cwx_ref, cbx_ref, cwbc_ref, cbbc_ref, dtb_ref,
               alog_ref, dexp_ref, ng_ref, e3_ref, shift_ref, tril_ref, o_ref, halox_ref, halobc_ref,
               state_ref):
    c = pl.program_id(1)
    q = SSD_Q

    @pl.when(c == 0)
    def _():
        halox_ref[...] = jnp.zeros(halox_ref.shape, F32)
        halobc_ref[...] = jnp.zeros(halobc_ref.shape, F32)
        state_ref[...] = jnp.zeros(state_ref.shape, F32)

    yield

    def conv_silu(raw_ref, halo_ref, w_ref, b_ref):
        xb = raw_ref[...]
        x = xb.astype(F32)
        shifted = jnp.dot(shift_ref[...], xb, preferred_element_type=F32)
        acc = b_ref[...] + w_ref[SSM_CONV - 1:SSM_CONV, :] * x
        head = jnp.zeros((HALO, x.shape[1]), F32)
        for j in range(1, SSM_CONV):
            wj = w_ref[SSM_CONV - 1 - j:SSM_CONV - j, :]
            acc = acc + wj * shifted[(j - 1) * q:j * q]
            head = head + wj * halo_ref[HALO - j:2 * HALO - j, :]
        acc = jnp.concatenate([acc[0:HALO] + head, acc[HALO:]], axis=0)
        halo_ref[0:HALO, :] = x[q - HALO:q]
        return acc * _sigmoid(acc)

    xs = conv_silu(xs_ref, halox_ref, cwx_ref, cbx_ref)
    yield
    bc = conv_silu(bc_ref, halobc_ref, cwbc_ref, cbbc_ref)
    yield

    dtr = dt_ref[...] + dtb_ref[...]
    dt = jnp.maximum(dtr, 0.0) + jnp.log(1.0 + jnp.exp(-jnp.abs(dtr)))
    a = (-LOG2E * jnp.exp(alog_ref[...])) * dt

    row = lax.broadcasted_iota(jnp.int32, (q, q), 0)
    col = lax.broadcasted_iota(jnp.int32, (q, q), 1)
    causal = col <= row
    tril = tril_ref[...]
    acum = sum(jnp.dot(tril, t, preferred_element_type=F32) for t in _split3(a))

    lane = lax.broadcasted_iota(jnp.int32, (q, LANES), 1)

    def expand(x):
        parts = [jnp.where(lane < SSM_HEADS, t.astype(F32), 0.0) for t in _split3(x)]
        packed = parts[0] + pltpu.roll(parts[1], SSM_HEADS, 1) + pltpu.roll(parts[2], 2 * SSM_HEADS, 1)
        return jnp.dot(packed.astype(BF16), e3_ref[...], preferred_element_type=F32)

    dt_exp = expand(dt)
    acum_exp = expand(acum)
    ea_exp = jnp.exp2(acum_exp)
    last_exp = acum_exp[q - 1:q, :]
    dte_exp = jnp.exp2(last_exp - acum_exp)
    ea_last = ea_exp[q - 1:q, :]

    xdt = xs * dt_exp
    xdt_b = xdt.astype(BF16)
    acum_t = acum.T
    pair_lane = lax.broadcasted_iota(jnp.int32, (q, LANES), 1)
    yield

    for g in range(SSM_GROUPS):
        gsl = slice(g * SSM_GROUP_WIDTH, (g + 1) * SSM_GROUP_WIDTH)
        bg = bc[:, g * SSM_STATE:(g + 1) * SSM_STATE]
        cg = bc[:, (SSM_GROUPS + g) * SSM_STATE:(SSM_GROUPS + g + 1) * SSM_STATE]
        bg_b = bg.astype(BF16)
        cg_b = cg.astype(BF16)
        cb = lax.dot_general(cg_b, bg_b, (((1,), (1,)), ((), ())), preferred_element_type=F32)

        def masked(h):
            seg = acum[:, h:h + 1] - acum_t[h:h + 1, :]
            return (cb * jnp.exp2(jnp.where(causal, seg, -jnp.inf))).astype(BF16)

        pairs = []
        for j in range(SSM_HEADS_PER_GROUP // 2):
            h0 = g * SSM_HEADS_PER_GROUP + 2 * j
            xp = xdt_b[:, h0 * SSM_HEAD_DIM:(h0 + 2) * SSM_HEAD_DIM]
            r0 = jnp.dot(masked(h0), xp, preferred_element_type=F32)
            r1 = jnp.dot(masked(h0 + 1), xp, preferred_element_type=F32)
            pairs.append(jnp.where(pair_lane < SSM_HEAD_DIM, r0, r1))
            if j % 2 == 1:
                yield
        y_diag = jnp.concatenate(pairs, axis=1)

        st = state_ref[g]
        y_off = jnp.dot(cg_b, st.astype(BF16), preferred_element_type=F32) * ea_exp[:, gsl]
        w = (xdt[:, gsl] * dte_exp[:, gsl]).astype(BF16)
        s_new = jnp.dot(bg.T.astype(BF16), w, preferred_element_type=F32)
        state_ref[g] = st * ea_last[:, gsl] + s_new

        y = y_diag + y_off + xs[:, gsl] * dexp_ref[:, gsl]
        zf = z_ref[:, gsl].astype(F32)
        gated = y * (zf * _sigmoid(zf))
        gated = gated * lax.rsqrt(jnp.mean(gated * gated, axis=-1, keepdims=True) + SSM_NORM_EPS)
        o_ref[:, gsl] = (gated * ng_ref[:, gsl]).astype(BF16)
        yield


def _ssd_call(proj, dt_raw, conv_w, conv_b, dt_bias, a_log, d_skip, norm_g, batch, seq):
    nc = seq // SSD_Q
    t = batch * seq
    zcol = 3 * ATT_WIDTH // SSM_WIDTH
    xcol = zcol + 1
    bccol = (3 * ATT_WIDTH + 2 * SSM_WIDTH) // BC_COLS
    pad = LANES - SSM_HEADS
    dtb = jnp.pad(dt_bias.astype(F32), (0, pad)).reshape(1, LANES)
    alog = jnp.pad(a_log.astype(F32), (0, pad)).reshape(1, LANES)
    dexp = jnp.repeat(d_skip.astype(F32), SSM_HEAD_DIM).reshape(1, SSM_WIDTH)
    r = jnp.arange(LANES)[:, None]
    hcol = (jnp.arange(SSM_WIDTH) // SSM_HEAD_DIM)[None, :]
    e3 = jnp.where((r % SSM_HEADS == hcol) & (r < 3 * SSM_HEADS), 1.0, 0.0).astype(BF16)
    cw = conv_w.astype(F32)
    cb = conv_b.astype(F32).reshape(1, -1)
    tq = jnp.arange(SSD_Q)
    delta = tq[:, None] - tq[None, :]
    shift = jnp.concatenate([jnp.where(delta == j, 1.0, 0.0) for j in range(1, SSM_CONV)]).astype(BF16)
    tril = jnp.where(delta >= 0, 1.0, 0.0).astype(BF16)
    row_spec = lambda width, colblk: pl.BlockSpec((SSD_Q, width), lambda b, c: (b * nc + c, colblk))
    in_specs = [
        row_spec(SSM_WIDTH, zcol),
        row_spec(SSM_WIDTH, xcol),
        row_spec(BC_COLS, bccol),
        row_spec(LANES, 0),
        _resident((SSM_CONV, SSM_WIDTH)),
        _resident((1, SSM_WIDTH)),
        _resident((SSM_CONV, BC_COLS)),
        _resident((1, BC_COLS)),
        _resident((1, LANES)),
        _resident((1, LANES)),
        _resident((1, SSM_WIDTH)),
        _resident((1, SSM_WIDTH)),
        _resident((LANES, SSM_WIDTH)),
        _resident(((SSM_CONV - 1) * SSD_Q, SSD_Q)),
        _resident((SSD_Q, SSD_Q)),
    ]
    operands = (proj, proj, proj, dt_raw, cw[:, :SSM_WIDTH], cb[:, :SSM_WIDTH], cw[:, SSM_WIDTH:],
                cb[:, SSM_WIDTH:], dtb, alog, dexp, norm_g.astype(F32).reshape(1, -1), e3, shift, tril)
    out_spec = pl.BlockSpec((SSD_Q, SSM_WIDTH), lambda b, c: (b * nc + c, 0))
    out_shape = jax.ShapeDtypeStruct((t, SSM_WIDTH), BF16)
    scratch = [
        pltpu.VMEM((2 * HALO, SSM_WIDTH), F32),
        pltpu.VMEM((2 * HALO, BC_COLS), F32),
        pltpu.VMEM((SSM_GROUPS, SSM_STATE, SSM_GROUP_WIDTH), F32),
    ]
    return in_specs, operands, out_spec, out_shape, scratch


def _mixers_body(*refs, n_att_in, n_ssd_in, n_att_scratch, lam_init):
    att_in = refs[:n_att_in]
    ssd_in = refs[n_att_in:n_att_in + n_ssd_in]
    att_out, ssd_out, *scratch = refs[n_att_in + n_ssd_in:]
    att_scratch, ssd_scratch = scratch[:n_att_scratch], scratch[n_att_scratch:]
    ssd = _ssd_steps(*ssd_in, ssd_out, *ssd_scratch)
    next(ssd)
    _attn_body(*att_in, att_out, *att_scratch, lam_init=lam_init, other_work=ssd)
    for _ in ssd:
        pass


def _mixers(att_call, ssd_call, batch, seq, lam_init):
    assert seq // SSD_Q == ATT_HEADS, "one SSD scan step per attention head in each batch row"
    a_specs, a_ops, a_out, a_shape, a_scratch = att_call
    s_specs, s_ops, s_out, s_shape, s_scratch = ssd_call
    return pl.pallas_call(
        functools.partial(_mixers_body, n_att_in=len(a_specs), n_ssd_in=len(s_specs),
                          n_att_scratch=len(a_scratch), lam_init=lam_init),
        grid=(batch, ATT_HEADS),
        in_specs=a_specs + s_specs,
        out_specs=[a_out, s_out],
        out_shape=[a_shape, s_shape],
        scratch_shapes=a_scratch + s_scratch,
        compiler_params=pltpu.CompilerParams(
            dimension_semantics=("arbitrary", "arbitrary"), vmem_limit_bytes=VMEM_LIMIT),
        name="mixers",
    )(*a_ops, *s_ops)


N_UBUF = 4


def _ffn_body(x_ref, att_ref, ssm_ref, wo_ref, g2_ref, wup_ref, cw_ref, cb_ref, wdn_ref, gf_ref,
              o_ref, x1_ref, h2_ref, act_ref, uext_ref, carry_ref, *, tiles_per_seq):
    i = pl.program_id(0)
    tm, cw = FFN_TM, FFN_CW

    @pl.when(i % tiles_per_seq == 0)
    def _():
        carry_ref[...] = jnp.zeros(carry_ref.shape, F32)

    x1 = (x_ref[...]
          + jnp.dot(att_ref[...], wo_ref[0:ATT_WIDTH, :], preferred_element_type=F32)
          + jnp.dot(ssm_ref[...], wo_ref[ATT_WIDTH:, :], preferred_element_type=F32))
    x1_ref[...] = x1
    h2_ref[...] = (x1 * lax.rsqrt(jnp.mean(x1 * x1, axis=-1, keepdims=True) + NORM_EPS)
                   * g2_ref[...]).astype(BF16)

    def up_conv(slot, col):
        buf = uext_ref.at[slot % N_UBUF]
        buf[0:HALO, :] = carry_ref[slot]
        buf[HALO:HALO + tm, :] = jnp.dot(h2_ref[...], wup_ref[:, col:col + cw],
                                         preferred_element_type=F32)
        acc = cb_ref[:, col:col + cw] + cw_ref[FFN_CONV - 1:FFN_CONV, col:col + cw] * buf[HALO:HALO + tm, :]
        for k in range(FFN_CONV - 1):
            off = HALO - (FFN_CONV - 1) + k
            acc = acc + cw_ref[k:k + 1, col:col + cw] * buf[off:off + tm, :]
        carry_ref[slot] = buf[tm:tm + HALO, :]
        return acc

    for j in range(FFN_DIM // cw):
        gate = up_conv(2 * j, j * cw)
        val = up_conv(2 * j + 1, FFN_DIM + j * cw)
        act_ref[:, j * cw:(j + 1) * cw] = (gate * _sigmoid(gate) * val).astype(BF16)

    x2 = x1_ref[...] + jnp.dot(act_ref[...], wdn_ref[...], preferred_element_type=F32)
    o_ref[...] = x2 * lax.rsqrt(jnp.mean(x2 * x2, axis=-1, keepdims=True) + NORM_EPS) * gf_ref[...]


def _mix_ffn(x2, att, ssm, w_out, g2, w_up, conv_w, conv_b, w_down, gf, seq):
    t = x2.shape[0]
    n_slots = 2 * (FFN_DIM // FFN_CW)
    tok = lambda width: pl.BlockSpec((FFN_TM, width), lambda i: (i, 0))
    return pl.pallas_call(
        functools.partial(_ffn_body, tiles_per_seq=seq // FFN_TM),
        grid=(t // FFN_TM,),
        in_specs=[
            tok(D_MODEL), tok(ATT_WIDTH), tok(SSM_WIDTH),
            _resident((ATT_WIDTH + SSM_WIDTH, D_MODEL)),
            _resident((1, D_MODEL)),
            _resident((D_MODEL, 2 * FFN_DIM)),
            _resident((FFN_CONV, 2 * FFN_DIM)),
            _resident((1, 2 * FFN_DIM)),
            _resident((FFN_DIM, D_MODEL)),
            _resident((1, D_MODEL)),
        ],
        out_specs=tok(D_MODEL),
        out_shape=jax.ShapeDtypeStruct((t, D_MODEL), F32),
        scratch_shapes=[
            pltpu.VMEM((FFN_TM, D_MODEL), F32),
            pltpu.VMEM((FFN_TM, D_MODEL), BF16),
            pltpu.VMEM((FFN_TM, FFN_DIM), BF16),
            pltpu.VMEM((N_UBUF, HALO + FFN_TM, FFN_CW), F32),
            pltpu.VMEM((n_slots, HALO, FFN_CW), F32),
        ],
        compiler_params=pltpu.CompilerParams(
            dimension_semantics=("arbitrary",), vmem_limit_bytes=VMEM_LIMIT),
        name="mix_ffn",
    )(x2, att, ssm, w_out, g2, w_up, conv_w, conv_b, w_down, gf)


def kernel(x, rel_bias_table, attn_norm_g, w_in, lambda_q1, lambda_k1, lambda_q2, lambda_k2,
           attn_subln_g, ssm_conv_w, ssm_conv_b, ssm_dt_bias, ssm_a_log, ssm_d, ssm_norm_g,
           w_out, ffn_norm_g, ffn_w_up, ffn_conv_w, ffn_conv_b, ffn_w_down, final_norm_g):
    batch, seq, _ = x.shape
    depth = w_in.shape[0]
    assert seq % max(ATT_QB, SSD_Q, FFN_TM) == 0 and (batch * seq) % PROJ_TM == 0
    x2 = x.reshape(batch * seq, D_MODEL)
    row = lambda v: v.astype(F32).reshape(1, -1)
    for i in range(depth):
        lam_init = 0.8 - 0.6 * math.exp(-0.3 * i)
        lam = (jnp.exp(jnp.sum(lambda_q1[i].astype(F32) * lambda_k1[i].astype(F32)))
               - jnp.exp(jnp.sum(lambda_q2[i].astype(F32) * lambda_k2[i].astype(F32)))
               + lam_init).reshape(1)
        col_scale = jnp.where(jnp.arange(MAIN_COLS + DT_COLS) < ATT_WIDTH, LOG2E * ATT_HEAD_DIM ** -0.5, 1.0)
        w_all = (w_in[i] * col_scale.astype(F32)).astype(BF16)
        w_dt = jnp.pad(w_in[i, :, MAIN_COLS:], ((0, 0), (0, LANES - DT_COLS))).astype(BF16)
        proj, dt_raw = _in_proj(x2, row(attn_norm_g[i]), w_all, w_dt)
        att, ssm = _mixers(
            _attention_call(proj, lam, rel_bias_table, row(attn_subln_g[i]), batch, seq),
            _ssd_call(proj, dt_raw, ssm_conv_w[i], ssm_conv_b[i], ssm_dt_bias[i], ssm_a_log[i],
                      ssm_d[i], ssm_norm_g[i], batch, seq),
            batch, seq, lam_init)
        assert depth == 1
        x2 = _mix_ffn(x2, att, ssm, w_out[i].astype(BF16), row(ffn_norm_g[i]),
                      ffn_w_up[i].astype(BF16), ffn_conv_w[i].astype(F32), row(ffn_conv_b[i]),
                      ffn_w_down[i].astype(BF16), row(final_norm_g), seq)
    return x2.reshape(batch, seq, D_MODEL)
```

```python
import functools
import math

import jax
import jax.numpy as jnp
from jax import lax
from jax.experimental import pallas as pl
from jax.experimental.pallas import tpu as pltpu

F32 = jnp.float32
BF16 = jnp.bfloat16

D_MODEL = 1024
CHUNK = 64
ATT_HEADS = 8
ATT_HEAD_DIM = 64
ATT_V_DIM = 2 * ATT_HEAD_DIM
ATT_WIDTH = ATT_HEADS * ATT_V_DIM
SSM_HEADS = 16
SSM_HEAD_DIM = 64
SSM_WIDTH = SSM_HEADS * SSM_HEAD_DIM
SSM_GROUPS = 2
SSM_STATE = 128
SSM_CONV = 4
SSM_HEADS_PER_GROUP = SSM_HEADS // SSM_GROUPS
SSM_GROUP_WIDTH = SSM_WIDTH // SSM_GROUPS
FFN_DIM = 2816
FFN_CONV = 3
REL_BUCKETS = 32
REL_MAX_DIST = 128
NORM_EPS = 1e-6
SUBLN_EPS = 1e-5
SSM_NORM_EPS = 1e-5
BC_COLS = 2 * SSM_GROUPS * SSM_STATE
MAIN_COLS = 3 * ATT_WIDTH + SSM_WIDTH + SSM_WIDTH + BC_COLS
DT_COLS = SSM_HEADS
LOG2E = math.log2(math.e)

LANES = 128
SUBLANES = 8
VMEM_LIMIT = 56 * 1024 * 1024

PROJ_TM = 512
PROJ_CN = 512
ATT_QB = 256
ATT_KB = 256
SSD_Q = 256
FFN_TM = 512
FFN_CW = 256
HALO = SUBLANES
OTHER_WORK_PER_BLOCK = (1, 2, 1, 2, 1, 1, 1, 0)


def _resident(shape):
    nd = len(shape)
    return pl.BlockSpec(shape, lambda *_: (0,) * nd, pipeline_mode=pl.Buffered(1))


def _sigmoid(x):
    return 1.0 / (1.0 + jnp.exp(-x))


def _inproj_body(x_ref, g_ref, w_ref, wdt_ref, proj_ref, dt_ref):
    x = x_ref[...]
    h = x * lax.rsqrt(jnp.mean(x * x, axis=-1, keepdims=True) + NORM_EPS) * g_ref[...]
    h = h.astype(BF16)
    for j in range(MAIN_COLS // PROJ_CN):
        sl = slice(j * PROJ_CN, (j + 1) * PROJ_CN)
        proj_ref[:, sl] = jnp.dot(h, w_ref[:, sl], preferred_element_type=F32).astype(BF16)
    dt_ref[...] = jnp.dot(h, wdt_ref[...], preferred_element_type=F32)


def _in_proj(x2, g, w_main, w_dt):
    t = x2.shape[0]
    return pl.pallas_call(
        _inproj_body,
        grid=(t // PROJ_TM,),
        in_specs=[
            pl.BlockSpec((PROJ_TM, D_MODEL), lambda i: (i, 0)),
            _resident((1, D_MODEL)),
            _resident((D_MODEL, MAIN_COLS)),
            _resident((D_MODEL, LANES)),
        ],
        out_specs=[
            pl.BlockSpec((PROJ_TM, MAIN_COLS), lambda i: (i, 0)),
            pl.BlockSpec((PROJ_TM, LANES), lambda i: (i, 0)),
        ],
        out_shape=[
            jax.ShapeDtypeStruct((t, MAIN_COLS), BF16),
            jax.ShapeDtypeStruct((t, LANES), F32),
        ],
        compiler_params=pltpu.CompilerParams(
            dimension_semantics=("arbitrary",), vmem_limit_bytes=VMEM_LIMIT),
        name="in_proj",
    )(x2, g, w_main, w_dt)


FAR_BUCKET = REL_BUCKETS // 2 - 1
BIAS_SPAN = 4 * ATT_KB


def _t5_bucket(rel):
    nb = REL_BUCKETS // 2
    max_exact = nb // 2
    bucket = jnp.where(rel > 0, nb, 0)
    n = jnp.abs(rel)
    nf = jnp.maximum(n, 1).astype(F32)
    large = max_exact + (jnp.log(nf / max_exact) / math.log(REL_MAX_DIST / max_exact)
                         * (nb - max_exact)).astype(jnp.int32)
    large = jnp.minimum(large, nb - 1)
    return bucket + jnp.where(n < max_exact, n, large)


def _bias_bucket_row():
    rel = jnp.arange(BIAS_SPAN, dtype=jnp.int32) - 2 * ATT_KB
    return jnp.broadcast_to(_t5_bucket(rel)[None, :], (SUBLANES, BIAS_SPAN))


def _bias_tiles(tbl_ref, idx_ref, h):
    idx = idx_ref[...]
    base = tbl_ref[FAR_BUCKET, h]
    r = jnp.zeros(idx.shape, F32)
    for b in range(REL_BUCKETS):
        r = jnp.where(idx == b, (tbl_ref[b, h] - base) * LOG2E, r)
    rows = jnp.concatenate([r] * (ATT_QB // SUBLANES), axis=0)
    rolled = pltpu.roll(rows, 0, 1, stride=1, stride_axis=0)
    return rolled[:, 2 * ATT_KB:3 * ATT_KB], rolled[:, ATT_KB:2 * ATT_KB]


def _attn_body(lam_ref, tbl_ref, q_ref, k_ref, v_ref, idx_ref, subg_ref, o_ref, v1_ref, s_ref, m_ref,
               p_ref, *, lam_init, other_work):
    qb, kb = ATT_QB, ATT_KB
    seq = q_ref.shape[0]
    v1_ref[:, 0:LANES] = v_ref[...]
    v1_ref[:, LANES:2 * LANES] = jnp.ones((seq, LANES), BF16)

    lane = lax.broadcasted_iota(jnp.int32, (qb, LANES), 1)
    row = lax.broadcasted_iota(jnp.int32, (qb, kb), 0)
    col = lax.broadcasted_iota(jnp.int32, (qb, kb), 1)
    allowed = (col // CHUNK) <= (row // CHUNK)
    allowed = jnp.concatenate([allowed, allowed], axis=0)
    b_diag, b_prev = _bias_tiles(tbl_ref, idx_ref, pl.program_id(1))
    b_diag = jnp.concatenate([b_diag] * 2, axis=0)
    b_prev = jnp.concatenate([b_prev] * 2, axis=0)
    lam = lam_ref[0]

    def scores(i, slot):
        kvl = (i + 1) * kb
        q = q_ref[i * qb:(i + 1) * qb, :]
        zero = jnp.zeros_like(q)
        qs = jnp.concatenate([jnp.where(lane < ATT_HEAD_DIM, q, zero),
                              jnp.where(lane >= ATT_HEAD_DIM, q, zero)], axis=0)
        s = lax.dot_general(qs, k_ref[0:kvl, :], (((1,), (1,)), ((), ())),
                            preferred_element_type=F32)
        blocks = [s[:, j * kb:(j + 1) * kb] for j in range(i + 1)]
        blocks[i] = jnp.where(allowed, blocks[i] + b_diag, -1e30)
        if i >= 1:
            blocks[i - 1] = blocks[i - 1] + b_prev
        for j, blk in enumerate(blocks):
            s_ref[slot, :, j * kb:(j + 1) * kb] = blk
        m = jnp.max(functools.reduce(jnp.maximum, blocks), axis=-1, keepdims=True)
        m_ref[slot] = jnp.broadcast_to(m, (2 * qb, LANES))

    def probs(i, slot):
        m = jnp.concatenate([m_ref[slot]] * (kb // LANES), axis=1)
        for j in range(i + 1):
            cols = slice(j * kb, (j + 1) * kb)
            p_ref[slot, :, cols] = jnp.exp2(s_ref[slot, :, cols] - m).astype(BF16)

    def finish(i, slot):
        kvl = (i + 1) * kb
        acc = jnp.dot(p_ref[slot, :, 0:kvl], v1_ref[0:kvl, :], preferred_element_type=F32)
        o = acc[:, 0:LANES] / acc[:, LANES:2 * LANES]
        o = o[:qb] - lam * o[qb:]
        o = o * lax.rsqrt(jnp.mean(o * o, axis=-1, keepdims=True) + SUBLN_EPS) * subg_ref[...]
        o_ref[i * qb:(i + 1) * qb, :] = (o * (1.0 - lam_init)).astype(BF16)

    order = list(reversed(range(seq // qb)))
    for n in range(len(order) + 2):
        for _ in range(OTHER_WORK_PER_BLOCK[n] if n < len(OTHER_WORK_PER_BLOCK) else 0):
            next(other_work, None)
        if n < len(order):
            scores(order[n], n % 2)
        if 0 <= n - 1 < len(order):
            probs(order[n - 1], (n - 1) % 2)
        if 0 <= n - 2 < len(order):
            finish(order[n - 2], (n - 2) % 2)


def _attention_call(proj, lam, rel_table, sub_g, batch, seq):
    kcol = ATT_WIDTH // LANES
    vcol = 2 * ATT_WIDTH // LANES
    in_specs = [
        pl.BlockSpec(memory_space=pltpu.SMEM),
        pl.BlockSpec(memory_space=pltpu.SMEM),
        pl.BlockSpec((seq, LANES), lambda b, h: (b, h)),
        pl.BlockSpec((seq, LANES), lambda b, h: (b, kcol + h)),
        pl.BlockSpec((seq, LANES), lambda b, h: (b, vcol + h)),
        _resident((SUBLANES, BIAS_SPAN)),
        pl.BlockSpec((1, ATT_V_DIM), lambda b, h: (0, 0)),
    ]
    operands = (lam, rel_table.astype(F32), proj, proj, proj, _bias_bucket_row(), sub_g)
    out_spec = pl.BlockSpec((seq, LANES), lambda b, h: (b, h))
    out_shape = jax.ShapeDtypeStruct((batch * seq, ATT_WIDTH), BF16)
    scratch = [
        pltpu.VMEM((seq, 2 * LANES), BF16),
        pltpu.VMEM((2, 2 * ATT_QB, seq), F32),
        pltpu.VMEM((2, 2 * ATT_QB, LANES), F32),
        pltpu.VMEM((2, 2 * ATT_QB, seq), BF16),
    ]
    return in_specs, operands, out_spec, out_shape, scratch


def _split3(x):
    x1 = x.astype(BF16)
    r1 = x - x1.astype(F32)
    x2 = r1.astype(BF16)
    x3 = (r1 - x2.astype(F32)).astype(BF16)
    return x1, x2, x3


def _ssd_steps(z_ref, xs_ref, bc_ref, dt_ref, cwx_ref, cbx_ref, cwbc_ref, cbbc_ref, dtb_ref,
               alog_ref, dexp_ref, ng_ref, e3_ref, shift_ref, tril_ref, o_ref, halox_ref, halobc_ref,
               state_ref):
    c = pl.program_id(1)
    q = SSD_Q

    @pl.when(c == 0)
    def _():
        halox_ref[...] = jnp.zeros(halox_ref.shape, F32)
        halobc_ref[...] = jnp.zeros(halobc_ref.shape, F32)
        state_ref[...] = jnp.zeros(state_ref.shape, F32)

    yield

    def conv_silu(raw_ref, halo_ref, w_ref, b_ref):
        xb = raw_ref[...]
        x = xb.astype(F32)
        shifted = jnp.dot(shift_ref[...], xb, preferred_element_type=F32)
        acc = b_ref[...] + w_ref[SSM_CONV - 1:SSM_CONV, :] * x
        head = jnp.zeros((HALO, x.shape[1]), F32)
        for j in range(1, SSM_CONV):
            wj = w_ref[SSM_CONV - 1 - j:SSM_CONV - j, :]
            acc = acc + wj * shifted[(j - 1) * q:j * q]
            head = head + wj * halo_ref[HALO - j:2 * HALO - j, :]
        acc = jnp.concatenate([acc[0:HALO] + head, acc[HALO:]], axis=0)
        halo_ref[0:HALO, :] = x[q - HALO:q]
        return acc * _sigmoid(acc)

    xs = conv_silu(xs_ref, halox_ref, cwx_ref, cbx_ref)
    yield
    bc = conv_silu(bc_ref, halobc_ref, cwbc_ref, cbbc_ref)
    yield

    dtr = dt_ref[...] + dtb_ref[...]
    dt = jnp.maximum(dtr, 0.0) + jnp.log(1.0 + jnp.exp(-jnp.abs(dtr)))
    a = (-LOG2E * jnp.exp(alog_ref[...])) * dt

    row = lax.broadcasted_iota(jnp.int32, (q, q), 0)
    col = lax.broadcasted_iota(jnp.int32, (q, q), 1)
    causal = col <= row
    tril = tril_ref[...]
    acum = sum(jnp.dot(tril, t, preferred_element_type=F32) for t in _split3(a))

    lane = lax.broadcasted_iota(jnp.int32, (q, LANES), 1)

    def expand(x):
        parts = [jnp.where(lane < SSM_HEADS, t.astype(F32), 0.0) for t in _split3(x)]
        packed = parts[0] + pltpu.roll(parts[1], SSM_HEADS, 1) + pltpu.roll(parts[2], 2 * SSM_HEADS, 1)
        return jnp.dot(packed.astype(BF16), e3_ref[...], preferred_element_type=F32)

    dt_exp = expand(dt)
    acum_exp = expand(acum)
    ea_exp = jnp.exp2(acum_exp)
    last_exp = acum_exp[q - 1:q, :]
    dte_exp = jnp.exp2(last_exp - acum_exp)
    ea_last = ea_exp[q - 1:q, :]

    xdt = xs * dt_exp
    xdt_b = xdt.astype(BF16)
    acum_t = acum.T
    pair_lane = lax.broadcasted_iota(jnp.int32, (q, LANES), 1)
    yield

    for g in range(SSM_GROUPS):
        gsl = slice(g * SSM_GROUP_WIDTH, (g + 1) * SSM_GROUP_WIDTH)
        bg = bc[:, g * SSM_STATE:(g + 1) * SSM_STATE]
        cg = bc[:, (SSM_GROUPS + g) * SSM_STATE:(SSM_GROUPS + g + 1) * SSM_STATE]
        bg_b = bg.astype(BF16)
        cg_b = cg.astype(BF16)
        cb = lax.dot_general(cg_b, bg_b, (((1,), (1,)), ((), ())), preferred_element_type=F32)

        def masked(h):
            seg = acum[:, h:h + 1] - acum_t[h:h + 1, :]
            return (cb * jnp.exp2(jnp.where(causal, seg, -jnp.inf))).astype(BF16)

        pairs = []
        for j in range(SSM_HEADS_PER_GROUP // 2):
            h0 = g * SSM_HEADS_PER_GROUP + 2 * j
            xp = xdt_b[:, h0 * SSM_HEAD_DIM:(h0 + 2) * SSM_HEAD_DIM]
            r0 = jnp.dot(masked(h0), xp, preferred_element_type=F32)
            r1 = jnp.dot(masked(h0 + 1), xp, preferred_element_type=F32)
            pairs.append(jnp.where(pair_lane < SSM_HEAD_DIM, r0, r1))
            if j % 2 == 1:
                yield
        y_diag = jnp.concatenate(pairs, axis=1)

        st = state_ref[g]
        y_off = jnp.dot(cg_b, st.astype(BF16), preferred_element_type=F32) * ea_exp[:, gsl]
        w = (xdt[:, gsl] * dte_exp[:, gsl]).astype(BF16)
        s_new = jnp.dot(bg.T.astype(BF16), w, preferred_element_type=F32)
        state_ref[g] = st * ea_last[:, gsl] + s_new

        y = y_diag + y_off + xs[:, gsl] * dexp_ref[:, gsl]
        zf = z_ref[:, gsl].astype(F32)
        gated = y * (zf * _sigmoid(zf))
        gated = gated * lax.rsqrt(jnp.mean(gated * gated, axis=-1, keepdims=True) + SSM_NORM_EPS)
        o_ref[:, gsl] = (gated * ng_ref[:, gsl]).astype(BF16)
        yield


def _ssd_call(proj, dt_raw, conv_w, conv_b, dt_bias, a_log, d_skip, norm_g, batch, seq):
    nc = seq // SSD_Q
    t = batch * seq
    zcol = 3 * ATT_WIDTH // SSM_WIDTH
    xcol = zcol + 1
    bccol = (3 * ATT_WIDTH + 2 * SSM_WIDTH) // BC_COLS
    pad = LANES - SSM_HEADS
    dtb = jnp.pad(dt_bias.astype(F32), (0, pad)).reshape(1, LANES)
    alog = jnp.pad(a_log.astype(F32), (0, pad)).reshape(1, LANES)
    dexp = jnp.repeat(d_skip.astype(F32), SSM_HEAD_DIM).reshape(1, SSM_WIDTH)
    r = jnp.arange(LANES)[:, None]
    hcol = (jnp.arange(SSM_WIDTH) // SSM_HEAD_DIM)[None, :]
    e3 = jnp.where((r % SSM_HEADS == hcol) & (r < 3 * SSM_HEADS), 1.0, 0.0).astype(BF16)
    cw = conv_w.astype(F32)
    cb = conv_b.astype(F32).reshape(1, -1)
    tq = jnp.arange(SSD_Q)
    delta = tq[:, None] - tq[None, :]
    shift = jnp.concatenate([jnp.where(delta == j, 1.0, 0.0) for j in range(1, SSM_CONV)]).astype(BF16)
    tril = jnp.where(delta >= 0, 1.0, 0.0).astype(BF16)
    row_spec = lambda width, colblk: pl.BlockSpec((SSD_Q, width), lambda b, c: (b * nc + c, colblk))
    in_specs = [
        row_spec(SSM_WIDTH, zcol),
        row_spec(SSM_WIDTH, xcol),
        row_spec(BC_COLS, bccol),
        row_spec(LANES, 0),
        _resident((SSM_CONV, SSM_WIDTH)),
        _resident((1, SSM_WIDTH)),
        _resident((SSM_CONV, BC_COLS)),
        _resident((1, BC_COLS)),
        _resident((1, LANES)),
        _resident((1, LANES)),
        _resident((1, SSM_WIDTH)),
        _resident((1, SSM_WIDTH)),
        _resident((LANES, SSM_WIDTH)),
        _resident(((SSM_CONV - 1) * SSD_Q, SSD_Q)),
        _resident((SSD_Q, SSD_Q)),
    ]
    operands = (proj, proj, proj, dt_raw, cw[:, :SSM_WIDTH], cb[:, :SSM_WIDTH], cw[:, SSM_WIDTH:],
                cb[:, SSM_WIDTH:], dtb, alog, dexp, norm_g.astype(F32).reshape(1, -1), e3, shift, tril)
    out_spec = pl.BlockSpec((SSD_Q, SSM_WIDTH), lambda b, c: (b * nc + c, 0))
    out_shape = jax.ShapeDtypeStruct((t, SSM_WIDTH), BF16)
    scratch = [
        pltpu.VMEM((2 * HALO, SSM_WIDTH), F32),
        pltpu.VMEM((2 * HALO, BC_COLS), F32),
        pltpu.VMEM((SSM_GROUPS, SSM_STATE, SSM_GROUP_WIDTH), F32),
    ]
    return in_specs, operands, out_spec, out_shape, scratch


def _mixers_body(*refs, n_att_in, n_ssd_in, n_att_scratch, lam_init):
    att_in = refs[:n_att_in]
    ssd_in = refs[n_att_in:n_att_in + n_ssd_in]
    att_out, ssd_out, *scratch = refs[n_att_in + n_ssd_in:]
    att_scratch, ssd_scratch = scratch[:n_att_scratch], scratch[n_att_scratch:]
    ssd = _ssd_steps(*ssd_in, ssd_out, *ssd_scratch)
    next(ssd)
    _attn_body(*att_in, att_out, *att_scratch, lam_init=lam_init, other_work=ssd)
    for _ in ssd:
        pass


def _mixers(att_call, ssd_call, batch, seq, lam_init):
    assert seq // SSD_Q == ATT_HEADS, "one SSD scan step per attention head in each batch row"
    a_specs, a_ops, a_out, a_shape, a_scratch = att_call
    s_specs, s_ops, s_out, s_shape, s_scratch = ssd_call
    return pl.pallas_call(
        functools.partial(_mixers_body, n_att_in=len(a_specs), n_ssd_in=len(s_specs),
                          n_att_scratch=len(a_scratch), lam_init=lam_init),
        grid=(batch, ATT_HEADS),
        in_specs=a_specs + s_specs,
        out_specs=[a_out, s_out],
        out_shape=[a_shape, s_shape],
        scratch_shapes=a_scratch + s_scratch,
        compiler_params=pltpu.CompilerParams(
            dimension_semantics=("arbitrary", "arbitrary"), vmem_limit_bytes=VMEM_LIMIT),
        name="mixers",
    )(*a_ops, *s_ops)


def _ffn_body(x_ref, att_ref, ssm_ref, wo_ref, g2_ref, wup_ref, cw_ref, cb_ref, wdn_ref, gf_ref,
              o_ref, xc_ref, x1p_ref, h2_ref, oc_ref, carry_ref, *, tiles_per_seq):
    i = pl.program_id(0)
    tm, cw = FFN_TM, FFN_CW
    ph = tm // SUBLANES
    n_lane_blocks = D_MODEL // LANES

    @pl.when(i % tiles_per_seq == 0)
    def _():
        carry_ref[...] = jnp.zeros(carry_ref.shape, F32)

    x1 = (x_ref[...]
          + jnp.dot(att_ref[...], wo_ref[0:ATT_WIDTH, :], preferred_element_type=F32)
          + jnp.dot(ssm_ref[...], wo_ref[ATT_WIDTH:, :], preferred_element_type=F32))
    for c in range(n_lane_blocks):
        xc_ref[c] = x1[:, c * LANES:(c + 1) * LANES]
    x1p = jnp.concatenate(
        [jnp.concatenate([xc_ref[c, pl.ds(k, ph, stride=SUBLANES), :] for k in range(SUBLANES)], axis=0)
         for c in range(n_lane_blocks)], axis=1)
    x1p_ref[...] = x1p
    h2_ref[...] = (x1p * lax.rsqrt(jnp.mean(x1p * x1p, axis=-1, keepdims=True) + NORM_EPS)
                   * g2_ref[...]).astype(BF16)

    first_row = lax.broadcasted_iota(jnp.int32, (ph, cw), 0) == 0

    def prev_token(block, carry_slot):
        tail = carry_ref[carry_slot]
        carry_ref[carry_slot] = block[ph - HALO:ph]
        return jnp.where(first_row, jnp.broadcast_to(tail[HALO - 1:HALO, :], (ph, cw)),
                         pltpu.roll(block, 1, 0))

    def up_conv(slot, col):
        u = jnp.dot(h2_ref[...], wup_ref[:, col:col + cw], preferred_element_type=F32)
        blocks = [u[k * ph:(k + 1) * ph] for k in range(SUBLANES)]
        back1 = prev_token(blocks[SUBLANES - 1], 2 * slot)
        back2 = prev_token(blocks[SUBLANES - 2], 2 * slot + 1)
        hist = [back2, back1] + blocks
        w = [cw_ref[t:t + 1, col:col + cw] for t in range(FFN_CONV)]
        b = cb_ref[:, col:col + cw]
        return jnp.concatenate(
            [b + w[2] * hist[k + 2] + w[1] * hist[k + 1] + w[0] * hist[k] for k in range(SUBLANES)],
            axis=0)

    acts = []
    for j in range(FFN_DIM // cw):
        gate = up_conv(2 * j, j * cw)
        val = up_conv(2 * j + 1, FFN_DIM + j * cw)
        acts.append((gate * _sigmoid(gate) * val).astype(BF16))

    x2 = x1p_ref[...] + jnp.dot(jnp.concatenate(acts, axis=1), wdn_ref[...], preferred_element_type=F32)
    out = x2 * lax.rsqrt(jnp.mean(x2 * x2, axis=-1, keepdims=True) + NORM_EPS) * gf_ref[...]
    for c in range(n_lane_blocks):
        for k in range(SUBLANES):
            oc_ref[c, pl.ds(k, ph, stride=SUBLANES), :] = out[k * ph:(k + 1) * ph, c * LANES:(c + 1) * LANES]
    for c in range(n_lane_blocks):
        o_ref[:, c * LANES:(c + 1) * LANES] = oc_ref[c]


def _mix_ffn(x2, att, ssm, w_out, g2, w_up, conv_w, conv_b, w_down, gf, seq):
    t = x2.shape[0]
    n_carry = 2 * (FFN_CONV - 1) * (FFN_DIM // FFN_CW)
    tok = lambda width: pl.BlockSpec((FFN_TM, width), lambda i: (i, 0))
    return pl.pallas_call(
        functools.partial(_ffn_body, tiles_per_seq=seq // FFN_TM),
        grid=(t // FFN_TM,),
        in_specs=[
            tok(D_MODEL), tok(ATT_WIDTH), tok(SSM_WIDTH),
            _resident((ATT_WIDTH + SSM_WIDTH, D_MODEL)),
            _resident((1, D_MODEL)),
            _resident((D_MODEL, 2 * FFN_DIM)),
            _resident((FFN_CONV, 2 * FFN_DIM)),
            _resident((1, 2 * FFN_DIM)),
            _resident((FFN_DIM, D_MODEL)),
            _resident((1, D_MODEL)),
        ],
        out_specs=tok(D_MODEL),
        out_shape=jax.ShapeDtypeStruct((t, D_MODEL), F32),
        scratch_shapes=[
            pltpu.VMEM((D_MODEL // LANES, FFN_TM, LANES), F32),
            pltpu.VMEM((FFN_TM, D_MODEL), F32),
            pltpu.VMEM((FFN_TM, D_MODEL), BF16),
            pltpu.VMEM((D_MODEL // LANES, FFN_TM, LANES), F32),
            pltpu.VMEM((n_carry, HALO, FFN_CW), F32),
        ],
        compiler_params=pltpu.CompilerParams(
            dimension_semantics=("arbitrary",), vmem_limit_bytes=VMEM_LIMIT),
        name="mix_ffn",
    )(x2, att, ssm, w_out, g2, w_up, conv_w, conv_b, w_down, gf)


def kernel(x, rel_bias_table, attn_norm_g, w_in, lambda_q1, lambda_k1, lambda_q2, lambda_k2,
           attn_subln_g, ssm_conv_w, ssm_conv_b, ssm_dt_bias, ssm_a_log, ssm_d, ssm_norm_g,
           w_out, ffn_norm_g, ffn_w_up, ffn_conv_w, ffn_conv_b, ffn_w_down, final_norm_g):
    batch, seq, _ = x.shape
    depth = w_in.shape[0]
    assert seq % max(ATT_QB, SSD_Q, FFN_TM) == 0 and (batch * seq) % PROJ_TM == 0
    x2 = x.reshape(batch * seq, D_MODEL)
    row = lambda v: v.astype(F32).reshape(1, -1)
    for i in range(depth):
        lam_init = 0.8 - 0.6 * math.exp(-0.3 * i)
        lam = (jnp.exp(jnp.sum(lambda_q1[i].astype(F32) * lambda_k1[i].astype(F32)))
               - jnp.exp(jnp.sum(lambda_q2[i].astype(F32) * lambda_k2[i].astype(F32)))
               + lam_init).reshape(1)
        col_scale = jnp.where(jnp.arange(MAIN_COLS + DT_COLS) < ATT_WIDTH, LOG2E * ATT_HEAD_DIM ** -0.5, 1.0)
        w_all = (w_in[i] * col_scale.astype(F32)).astype(BF16)
        w_dt = jnp.pad(w_in[i, :, MAIN_COLS:], ((0, 0), (0, LANES - DT_COLS))).astype(BF16)
        proj, dt_raw = _in_proj(x2, row(attn_norm_g[i]), w_all, w_dt)
        att, ssm = _mixers(
            _attention_call(proj, lam, rel_bias_table, row(attn_subln_g[i]), batch, seq),
            _ssd_call(proj, dt_raw, ssm_conv_w[i], ssm_conv_b[i], ssm_dt_bias[i], ssm_a_log[i],
                      ssm_d[i], ssm_norm_g[i], batch, seq),
            batch, seq, lam_init)
        assert depth == 1
        x2 = _mix_ffn(x2, att, ssm, w_out[i].astype(BF16), row(ffn_norm_g[i]),
                      ffn_w_up[i].astype(BF16), ffn_conv_w[i].astype(F32), row(ffn_conv_b[i]),
                      ffn_w_down[i].astype(BF16), row(final_norm_g), seq)
    return x2.reshape(batch, seq, D_MODEL)
```

```python
import functools
import math

import jax
import jax.numpy as jnp
from jax import lax
from jax.experimental import pallas as pl
from jax.experimental.pallas import tpu as pltpu

F32 = jnp.float32
BF16 = jnp.bfloat16

D_MODEL = 1024
CHUNK = 64
ATT_HEADS = 8
ATT_HEAD_DIM = 64
ATT_V_DIM = 2 * ATT_HEAD_DIM
ATT_WIDTH = ATT_HEADS * ATT_V_DIM
SSM_HEADS = 16
SSM_HEAD_DIM = 64
SSM_WIDTH = SSM_HEADS * SSM_HEAD_DIM
SSM_GROUPS = 2
SSM_STATE = 128
SSM_CONV = 4
SSM_HEADS_PER_GROUP = SSM_HEADS // SSM_GROUPS
SSM_GROUP_WIDTH = SSM_WIDTH // SSM_GROUPS
FFN_DIM = 2816
FFN_CONV = 3
REL_BUCKETS = 32
REL_MAX_DIST = 128
NORM_EPS = 1e-6
SUBLN_EPS = 1e-5
SSM_NORM_EPS = 1e-5
BC_COLS = 2 * SSM_GROUPS * SSM_STATE
MAIN_COLS = 3 * ATT_WIDTH + SSM_WIDTH + SSM_WIDTH + BC_COLS
DT_COLS = SSM_HEADS
LOG2E = math.log2(math.e)

LANES = 128
SUBLANES = 8
VMEM_LIMIT = 56 * 1024 * 1024

PROJ_TM = 512
PROJ_CN = 512
ATT_QB = 256
ATT_KB = 256
SSD_Q = 256
FFN_TM = 512
FFN_CW = 256
HALO = SUBLANES
OTHER_WORK_PER_BLOCK = (1, 2, 1, 2, 1, 1, 1, 0)


def _resident(shape):
    nd = len(shape)
    return pl.BlockSpec(shape, lambda *_: (0,) * nd, pipeline_mode=pl.Buffered(1))


def _sigmoid(x):
    return 1.0 / (1.0 + jnp.exp2(x * -LOG2E))


W_PAD_COLS = MAIN_COLS + LANES
WPREP_ROWS = 256


def _wprep_body(w_ref, o_ref):
    q_scale = LOG2E * ATT_HEAD_DIM ** -0.5
    o_ref[:, 0:ATT_WIDTH] = (w_ref[:, 0:ATT_WIDTH] * q_scale).astype(BF16)
    o_ref[:, ATT_WIDTH:MAIN_COLS] = w_ref[:, ATT_WIDTH:MAIN_COLS].astype(BF16)
    o_ref[:, MAIN_COLS:W_PAD_COLS] = jnp.zeros((WPREP_ROWS, LANES), BF16)
    o_ref[:, MAIN_COLS:MAIN_COLS + DT_COLS] = w_ref[:, MAIN_COLS:MAIN_COLS + DT_COLS].astype(BF16)


def _w_prep(w):
    return pl.pallas_call(
        _wprep_body,
        grid=(D_MODEL // WPREP_ROWS,),
        in_specs=[pl.BlockSpec((WPREP_ROWS, MAIN_COLS + DT_COLS), lambda i: (i, 0))],
        out_specs=pl.BlockSpec((WPREP_ROWS, W_PAD_COLS), lambda i: (i, 0)),
        out_shape=jax.ShapeDtypeStruct((D_MODEL, W_PAD_COLS), BF16),
        compiler_params=pltpu.CompilerParams(dimension_semantics=("arbitrary",)),
        name="w_prep",
    )(w)


def _inproj_body(x_ref, g_ref, w_ref, wdt_ref, proj_ref, dt_ref):
    x = x_ref[...]
    h = x * lax.rsqrt(jnp.mean(x * x, axis=-1, keepdims=True) + NORM_EPS) * g_ref[...]
    h = h.astype(BF16)
    for j in range(MAIN_COLS // PROJ_CN):
        sl = slice(j * PROJ_CN, (j + 1) * PROJ_CN)
        proj_ref[:, sl] = jnp.dot(h, w_ref[:, sl], preferred_element_type=F32).astype(BF16)
    dt_ref[...] = jnp.dot(h, wdt_ref[...], preferred_element_type=F32)


def _in_proj(x2, g, w_pad):
    t = x2.shape[0]
    return pl.pallas_call(
        _inproj_body,
        grid=(t // PROJ_TM,),
        in_specs=[
            pl.BlockSpec((PROJ_TM, D_MODEL), lambda i: (i, 0)),
            _resident((1, D_MODEL)),
            _resident((D_MODEL, MAIN_COLS)),
            pl.BlockSpec((D_MODEL, LANES), lambda i: (0, MAIN_COLS // LANES), pipeline_mode=pl.Buffered(1)),
        ],
        out_specs=[
            pl.BlockSpec((PROJ_TM, MAIN_COLS), lambda i: (i, 0)),
            pl.BlockSpec((PROJ_TM, LANES), lambda i: (i, 0)),
        ],
        out_shape=[
            jax.ShapeDtypeStruct((t, MAIN_COLS), BF16),
            jax.ShapeDtypeStruct((t, LANES), F32),
        ],
        compiler_params=pltpu.CompilerParams(
            dimension_semantics=("arbitrary",), vmem_limit_bytes=VMEM_LIMIT),
        name="in_proj",
    )(x2, g, w_pad, w_pad)


FAR_BUCKET = REL_BUCKETS // 2 - 1
BIAS_SPAN = 4 * ATT_KB


def _t5_bucket(rel):
    nb = REL_BUCKETS // 2
    max_exact = nb // 2
    bucket = jnp.where(rel > 0, nb, 0)
    n = jnp.abs(rel)
    nf = jnp.maximum(n, 1).astype(F32)
    large = max_exact + (jnp.log(nf / max_exact) / math.log(REL_MAX_DIST / max_exact)
                         * (nb - max_exact)).astype(jnp.int32)
    large = jnp.minimum(large, nb - 1)
    return bucket + jnp.where(n < max_exact, n, large)


def _bias_bucket_row():
    rel = jnp.arange(BIAS_SPAN, dtype=jnp.int32) - 2 * ATT_KB
    return jnp.broadcast_to(_t5_bucket(rel)[None, :], (SUBLANES, BIAS_SPAN))


def _bias_tiles(tbl_ref, idx_ref, h):
    idx = idx_ref[...]
    base = tbl_ref[FAR_BUCKET, h]
    r = jnp.zeros(idx.shape, F32)
    for b in range(REL_BUCKETS):
        r = jnp.where(idx == b, (tbl_ref[b, h] - base) * LOG2E, r)
    rows = jnp.concatenate([r] * (ATT_QB // SUBLANES), axis=0)
    rolled = pltpu.roll(rows, 0, 1, stride=1, stride_axis=0)
    return rolled[:, 2 * ATT_KB:3 * ATT_KB], rolled[:, ATT_KB:2 * ATT_KB]


def _attn_body(lam_ref, tbl_ref, q_ref, k_ref, v_ref, idx_ref, subg_ref, o_ref, v1_ref, *, lam_init,
               other_work):
    qb, kb = ATT_QB, ATT_KB
    seq = q_ref.shape[0]
    lane = lax.broadcasted_iota(jnp.int32, (qb, LANES), 1)
    lam = lam_ref[0]

    @functools.cache
    def bias_and_mask():
        row = lax.broadcasted_iota(jnp.int32, (qb, kb), 0)
        col = lax.broadcasted_iota(jnp.int32, (qb, kb), 1)
        allowed = (col // CHUNK) <= (row // CHUNK)
        b_diag, b_prev = _bias_tiles(tbl_ref, idx_ref, pl.program_id(1))
        return tuple(jnp.concatenate([t, t], axis=0) for t in (allowed, b_diag, b_prev))

    @functools.cache
    def build_v1():
        v1_ref[:, 0:LANES] = v_ref[...]
        v1_ref[:, LANES:2 * LANES] = jnp.ones((seq, LANES), BF16)

    def scores(i):
        kvl = (i + 1) * kb
        q = q_ref[i * qb:(i + 1) * qb, :]
        zero = jnp.zeros_like(q)
        qs = jnp.concatenate([jnp.where(lane < ATT_HEAD_DIM, q, zero),
                              jnp.where(lane >= ATT_HEAD_DIM, q, zero)], axis=0)
        s = lax.dot_general(qs, k_ref[0:kvl, :], (((1,), (1,)), ((), ())),
                            preferred_element_type=F32)
        allowed, b_diag, b_prev = bias_and_mask()
        blocks = [s[:, j * kb:(j + 1) * kb] for j in range(i + 1)]
        blocks[i] = jnp.where(allowed, blocks[i] + b_diag, -1e30)
        if i >= 1:
            blocks[i - 1] = blocks[i - 1] + b_prev
        m = jnp.max(functools.reduce(jnp.maximum, blocks), axis=-1, keepdims=True)
        return blocks, m

    def probs(blocks, m):
        return jnp.concatenate([jnp.exp2(blk - m).astype(BF16) for blk in blocks], axis=1)

    def finish(i, p):
        build_v1()
        kvl = (i + 1) * kb
        acc = jnp.dot(p, v1_ref[0:kvl, :], preferred_element_type=F32)
        o = acc[:, 0:LANES] / acc[:, LANES:2 * LANES]
        o = o[:qb] - lam * o[qb:]
        o = o * lax.rsqrt(jnp.mean(o * o, axis=-1, keepdims=True) + SUBLN_EPS) * subg_ref[...]
        o_ref[i * qb:(i + 1) * qb, :] = (o * (1.0 - lam_init)).astype(BF16)

    order = list(reversed(range(seq // qb)))
    scored, exped = {}, {}
    for n in range(len(order) + 2):
        for _ in range(OTHER_WORK_PER_BLOCK[n] if n < len(OTHER_WORK_PER_BLOCK) else 0):
            next(other_work, None)
        if n < len(order):
            scored[n] = scores(order[n])
        if 0 <= n - 1 < len(order):
            exped[n - 1] = probs(*scored.pop(n - 1))
        if 0 <= n - 2 < len(order):
            finish(order[n - 2], exped.pop(n - 2))


def _attention_call(proj, lam, rel_table, sub_g, batch, seq):
    kcol = ATT_WIDTH // LANES
    vcol = 2 * ATT_WIDTH // LANES
    in_specs = [
        pl.BlockSpec(memory_space=pltpu.SMEM),
        pl.BlockSpec(memory_space=pltpu.SMEM),
        pl.BlockSpec((seq, LANES), lambda b, h: (b, h)),
        pl.BlockSpec((seq, LANES), lambda b, h: (b, kcol + h)),
        pl.BlockSpec((seq, LANES), lambda b, h: (b, vcol + h)),
        _resident((SUBLANES, BIAS_SPAN)),
        pl.BlockSpec((1, ATT_V_DIM), lambda b, h: (0, 0)),
    ]
    operands = (lam, rel_table.astype(F32), proj, proj, proj, _bias_bucket_row(), sub_g)
    out_spec = pl.BlockSpec((seq, LANES), lambda b, h: (b, h))
    out_shape = jax.ShapeDtypeStruct((batch * seq, ATT_WIDTH), BF16)
    scratch = [pltpu.VMEM((seq, 2 * LANES), BF16)]
    return in_specs, operands, out_spec, out_shape, scratch


def _split3(x):
    x1 = x.astype(BF16)
    r1 = x - x1.astype(F32)
    x2 = r1.astype(BF16)
    x3 = (r1 - x2.astype(F32)).astype(BF16)
    return x1, x2, x3


def _ssd_steps(z_ref, xs_ref, bc_ref, dt_ref, cwx_ref, cbx_ref, cwbc_ref, cbbc_ref, dtb_ref,
               alog_ref, dexp_ref, ng_ref, e3_ref, shift_ref, tril_ref, o_ref, halox_ref, halobc_ref,
               state_ref):
    c = pl.program_id(1)
    q = SSD_Q

    @pl.when(c == 0)
    def _():
        halox_ref[...] = jnp.zeros(halox_ref.shape, F32)
        halobc_ref[...] = jnp.zeros(halobc_ref.shape, F32)
        state_ref[...] = jnp.zeros(state_ref.shape, F32)

    yield

    def conv_silu(raw_ref, halo_ref, w_ref, b_ref):
        xb = raw_ref[...]
        x = xb.astype(F32)
        shifted = jnp.dot(shift_ref[...], xb, preferred_element_type=F32)
        acc = b_ref[...] + w_ref[SSM_CONV - 1:SSM_CONV, :] * x
        head = jnp.zeros((HALO, x.shape[1]), F32)
        for j in range(1, SSM_CONV):
            wj = w_ref[SSM_CONV - 1 - j:SSM_CONV - j, :]
            acc = acc + wj * shifted[(j - 1) * q:j * q]
            head = head + wj * halo_ref[HALO - j:2 * HALO - j, :]
        acc = jnp.concatenate([acc[0:HALO] + head, acc[HALO:]], axis=0)
        halo_ref[0:HALO, :] = x[q - HALO:q]
        return acc * _sigmoid(acc)

    xs = conv_silu(xs_ref, halox_ref, cwx_ref, cbx_ref)
    yield
    bc = conv_silu(bc_ref, halobc_ref, cwbc_ref, cbbc_ref)
    yield

    dtr = dt_ref[...] + dtb_ref[...]
    dt = jnp.maximum(dtr, 0.0) + jnp.log(1.0 + jnp.exp(-jnp.abs(dtr)))
    a = (-LOG2E * jnp.exp(alog_ref[...])) * dt

    row = lax.broadcasted_iota(jnp.int32, (q, q), 0)
    col = lax.broadcasted_iota(jnp.int32, (q, q), 1)
    causal = col <= row
    tril = tril_ref[...]
    acum = sum(jnp.dot(tril, t, preferred_element_type=F32) for t in _split3(a))

    lane = lax.broadcasted_iota(jnp.int32, (q, LANES), 1)

    def expand(x):
        parts = [jnp.where(lane < SSM_HEADS, t.astype(F32), 0.0) for t in _split3(x)]
        packed = parts[0] + pltpu.roll(parts[1], SSM_HEADS, 1) + pltpu.roll(parts[2], 2 * SSM_HEADS, 1)
        return jnp.dot(packed.astype(BF16), e3_ref[...], preferred_element_type=F32)

    dt_exp = expand(dt)
    acum_exp = expand(acum)
    ea_exp = jnp.exp2(acum_exp)
    last_exp = acum_exp[q - 1:q, :]
    dte_exp = jnp.exp2(last_exp - acum_exp)
    ea_last = ea_exp[q - 1:q, :]

    xdt = xs * dt_exp
    xdt_b = xdt.astype(BF16)
    acum_t = acum.T
    pair_lane = lax.broadcasted_iota(jnp.int32, (q, LANES), 1)
    yield

    for g in range(SSM_GROUPS):
        gsl = slice(g * SSM_GROUP_WIDTH, (g + 1) * SSM_GROUP_WIDTH)
        bg = bc[:, g * SSM_STATE:(g + 1) * SSM_STATE]
        cg = bc[:, (SSM_GROUPS + g) * SSM_STATE:(SSM_GROUPS + g + 1) * SSM_STATE]
        bg_b = bg.astype(BF16)
        cg_b = cg.astype(BF16)
        cb = lax.dot_general(cg_b, bg_b, (((1,), (1,)), ((), ())), preferred_element_type=F32)

        def masked(h):
            seg = acum[:, h:h + 1] - acum_t[h:h + 1, :]
            return (cb * jnp.exp2(jnp.where(causal, seg, -jnp.inf))).astype(BF16)

        pairs = []
        for j in range(SSM_HEADS_PER_GROUP // 2):
            h0 = g * SSM_HEADS_PER_GROUP + 2 * j
            xp = xdt_b[:, h0 * SSM_HEAD_DIM:(h0 + 2) * SSM_HEAD_DIM]
            r0 = jnp.dot(masked(h0), xp, preferred_element_type=F32)
            r1 = jnp.dot(masked(h0 + 1), xp, preferred_element_type=F32)
            pairs.append(jnp.where(pair_lane < SSM_HEAD_DIM, r0, r1))
            if j % 2 == 1:
                yield
        y_diag = jnp.concatenate(pairs, axis=1)

        st = state_ref[g]
        y_off = jnp.dot(cg_b, st.astype(BF16), preferred_element_type=F32) * ea_exp[:, gsl]
        w = (xdt[:, gsl] * dte_exp[:, gsl]).astype(BF16)
        s_new = jnp.dot(bg.T.astype(BF16), w, preferred_element_type=F32)
        state_ref[g] = st * ea_last[:, gsl] + s_new

        y = y_diag + y_off + xs[:, gsl] * dexp_ref[:, gsl]
        zf = z_ref[:, gsl].astype(F32)
        gated = y * (zf * _sigmoid(zf))
        gated = gated * lax.rsqrt(jnp.mean(gated * gated, axis=-1, keepdims=True) + SSM_NORM_EPS)
        o_ref[:, gsl] = (gated * ng_ref[:, gsl]).astype(BF16)
        yield


def _ssd_call(proj, dt_raw, conv_w, conv_b, dt_bias, a_log, d_skip, norm_g, batch, seq):
    nc = seq // SSD_Q
    t = batch * seq
    zcol = 3 * ATT_WIDTH // SSM_WIDTH
    xcol = zcol + 1
    bccol = (3 * ATT_WIDTH + 2 * SSM_WIDTH) // BC_COLS
    pad = LANES - SSM_HEADS
    dtb = jnp.pad(dt_bias.astype(F32), (0, pad)).reshape(1, LANES)
    alog = jnp.pad(a_log.astype(F32), (0, pad)).reshape(1, LANES)
    dexp = jnp.repeat(d_skip.astype(F32), SSM_HEAD_DIM).reshape(1, SSM_WIDTH)
    r = jnp.arange(LANES)[:, None]
    hcol = (jnp.arange(SSM_WIDTH) // SSM_HEAD_DIM)[None, :]
    e3 = jnp.where((r % SSM_HEADS == hcol) & (r < 3 * SSM_HEADS), 1.0, 0.0).astype(BF16)
    cw = conv_w.astype(F32)
    cb = conv_b.astype(F32).reshape(1, -1)
    tq = jnp.arange(SSD_Q)
    delta = tq[:, None] - tq[None, :]
    shift = jnp.concatenate([jnp.where(delta == j, 1.0, 0.0) for j in range(1, SSM_CONV)]).astype(BF16)
    tril = jnp.where(delta >= 0, 1.0, 0.0).astype(BF16)
    row_spec = lambda width, colblk: pl.BlockSpec((SSD_Q, width), lambda b, c: (b * nc + c, colblk))
    in_specs = [
        row_spec(SSM_WIDTH, zcol),
        row_spec(SSM_WIDTH, xcol),
        row_spec(BC_COLS, bccol),
        row_spec(LANES, 0),
        _resident((SSM_CONV, SSM_WIDTH)),
        _resident((1, SSM_WIDTH)),
        _resident((SSM_CONV, BC_COLS)),
        _resident((1, BC_COLS)),
        _resident((1, LANES)),
        _resident((1, LANES)),
        _resident((1, SSM_WIDTH)),
        _resident((1, SSM_WIDTH)),
        _resident((LANES, SSM_WIDTH)),
        _resident(((SSM_CONV - 1) * SSD_Q, SSD_Q)),
        _resident((SSD_Q, SSD_Q)),
    ]
    operands = (proj, proj, proj, dt_raw, cw[:, :SSM_WIDTH], cb[:, :SSM_WIDTH], cw[:, SSM_WIDTH:],
                cb[:, SSM_WIDTH:], dtb, alog, dexp, norm_g.astype(F32).reshape(1, -1), e3, shift, tril)
    out_spec = pl.BlockSpec((SSD_Q, SSM_WIDTH), lambda b, c: (b * nc + c, 0))
    out_shape = jax.ShapeDtypeStruct((t, SSM_WIDTH), BF16)
    scratch = [
        pltpu.VMEM((2 * HALO, SSM_WIDTH), F32),
        pltpu.VMEM((2 * HALO, BC_COLS), F32),
        pltpu.VMEM((SSM_GROUPS, SSM_STATE, SSM_GROUP_WIDTH), F32),
    ]
    return in_specs, operands, out_spec, out_shape, scratch


def _mixers_body(*refs, n_att_in, n_ssd_in, n_att_scratch, lam_init):
    att_in = refs[:n_att_in]
    ssd_in = refs[n_att_in:n_att_in + n_ssd_in]
    att_out, ssd_out, *scratch = refs[n_att_in + n_ssd_in:]
    att_scratch, ssd_scratch = scratch[:n_att_scratch], scratch[n_att_scratch:]
    ssd = _ssd_steps(*ssd_in, ssd_out, *ssd_scratch)
    next(ssd)
    _attn_body(*att_in, att_out, *att_scratch, lam_init=lam_init, other_work=ssd)
    for _ in ssd:
        pass


def _mixers(att_call, ssd_call, batch, seq, lam_init):
    assert seq // SSD_Q == ATT_HEADS, "one SSD scan step per attention head in each batch row"
    a_specs, a_ops, a_out, a_shape, a_scratch = att_call
    s_specs, s_ops, s_out, s_shape, s_scratch = ssd_call
    return pl.pallas_call(
        functools.partial(_mixers_body, n_att_in=len(a_specs), n_ssd_in=len(s_specs),
                          n_att_scratch=len(a_scratch), lam_init=lam_init),
        grid=(batch, ATT_HEADS),
        in_specs=a_specs + s_specs,
        out_specs=[a_out, s_out],
        out_shape=[a_shape, s_shape],
        scratch_shapes=a_scratch + s_scratch,
        compiler_params=pltpu.CompilerParams(
            dimension_semantics=("arbitrary", "arbitrary"), vmem_limit_bytes=VMEM_LIMIT),
        name="mixers",
    )(*a_ops, *s_ops)


def _ffn_body(x_ref, att_ref, ssm_ref, wo_ref, g2_ref, wup_ref, cw_ref, cb_ref, wdn_ref, gf_ref,
              o_ref, xc_ref, x1p_ref, h2_ref, oc_ref, carry_ref, *, tiles_per_seq):
    i = pl.program_id(0)
    tm, cw = FFN_TM, FFN_CW
    ph = tm // SUBLANES
    n_lane_blocks = D_MODEL // LANES

    @pl.when(i % tiles_per_seq == 0)
    def _():
        carry_ref[...] = jnp.zeros(carry_ref.shape, F32)

    x1 = (x_ref[...]
          + jnp.dot(att_ref[...], wo_ref[0:ATT_WIDTH, :], preferred_element_type=F32)
          + jnp.dot(ssm_ref[...], wo_ref[ATT_WIDTH:, :], preferred_element_type=F32))
    for c in range(n_lane_blocks):
        xc_ref[c] = x1[:, c * LANES:(c + 1) * LANES]
    x1p = jnp.concatenate(
        [jnp.concatenate([xc_ref[c, pl.ds(k, ph, stride=SUBLANES), :] for k in range(SUBLANES)], axis=0)
         for c in range(n_lane_blocks)], axis=1)
    x1p_ref[...] = x1p
    h2_ref[...] = (x1p * lax.rsqrt(jnp.mean(x1p * x1p, axis=-1, keepdims=True) + NORM_EPS)
                   * g2_ref[...]).astype(BF16)

    first_row = lax.broadcasted_iota(jnp.int32, (ph, cw), 0) == 0

    def prev_token(block, carry_slot):
        tail = carry_ref[carry_slot]
        carry_ref[carry_slot] = block[ph - HALO:ph]
        return jnp.where(first_row, jnp.broadcast_to(tail[HALO - 1:HALO, :], (ph, cw)),
                         pltpu.roll(block, 1, 0))

    def up_conv(slot, col):
        u = jnp.dot(h2_ref[...], wup_ref[:, col:col + cw], preferred_element_type=F32)
        blocks = [u[k * ph:(k + 1) * ph] for k in range(SUBLANES)]
        back1 = prev_token(blocks[SUBLANES - 1], 2 * slot)
        back2 = prev_token(blocks[SUBLANES - 2], 2 * slot + 1)
        hist = [back2, back1] + blocks
        w = [cw_ref[t:t + 1, col:col + cw] for t in range(FFN_CONV)]
        b = cb_ref[:, col:col + cw]
        return jnp.concatenate(
            [b + w[2] * hist[k + 2] + w[1] * hist[k + 1] + w[0] * hist[k] for k in range(SUBLANES)],
            axis=0)

    acts = []
    for j in range(FFN_DIM // cw):
        gate = up_conv(2 * j, j * cw)
        val = up_conv(2 * j + 1, FFN_DIM + j * cw)
        acts.append((gate * _sigmoid(gate) * val).astype(BF16))

    x2 = x1p_ref[...] + jnp.dot(jnp.concatenate(acts, axis=1), wdn_ref[...], preferred_element_type=F32)
    out = x2 * lax.rsqrt(jnp.mean(x2 * x2, axis=-1, keepdims=True) + NORM_EPS) * gf_ref[...]
    for c in range(n_lane_blocks):
        for k in range(SUBLANES):
            oc_ref[c, pl.ds(k, ph, stride=SUBLANES), :] = out[k * ph:(k + 1) * ph, c * LANES:(c + 1) * LANES]
    for c in range(n_lane_blocks):
        o_ref[:, c * LANES:(c + 1) * LANES] = oc_ref[c]


def _mix_ffn(x2, att, ssm, w_out, g2, w_up, conv_w, conv_b, w_down, gf, seq):
    t = x2.shape[0]
    n_carry = 2 * (FFN_CONV - 1) * (FFN_DIM // FFN_CW)
    tok = lambda width: pl.BlockSpec((FFN_TM, width), lambda i: (i, 0))
    return pl.pallas_call(
        functools.partial(_ffn_body, tiles_per_seq=seq // FFN_TM),
        grid=(t // FFN_TM,),
        in_specs=[
            tok(D_MODEL), tok(ATT_WIDTH), tok(SSM_WIDTH),
            _resident((ATT_WIDTH + SSM_WIDTH, D_MODEL)),
            _resident((1, D_MODEL)),
            _resident((D_MODEL, 2 * FFN_DIM)),
            _resident((FFN_CONV, 2 * FFN_DIM)),
            _resident((1, 2 * FFN_DIM)),
            _resident((FFN_DIM, D_MODEL)),
            _resident((1, D_MODEL)),
        ],
        out_specs=tok(D_MODEL),
        out_shape=jax.ShapeDtypeStruct((t, D_MODEL), F32),
        scratch_shapes=[
            pltpu.VMEM((D_MODEL // LANES, FFN_TM, LANES), F32),
            pltpu.VMEM((FFN_TM, D_MODEL), F32),
            pltpu.VMEM((FFN_TM, D_MODEL), BF16),
            pltpu.VMEM((D_MODEL // LANES, FFN_TM, LANES), F32),
            pltpu.VMEM((n_carry, HALO, FFN_CW), F32),
        ],
        compiler_params=pltpu.CompilerParams(
            dimension_semantics=("arbitrary",), vmem_limit_bytes=VMEM_LIMIT),
        name="mix_ffn",
    )(x2, att, ssm, w_out, g2, w_up, conv_w, conv_b, w_down, gf)


def kernel(x, rel_bias_table, attn_norm_g, w_in, lambda_q1, lambda_k1, lambda_q2, lambda_k2,
           attn_subln_g, ssm_conv_w, ssm_conv_b, ssm_dt_bias, ssm_a_log, ssm_d, ssm_norm_g,
           w_out, ffn_norm_g, ffn_w_up, ffn_conv_w, ffn_conv_b, ffn_w_down, final_norm_g):
    batch, seq, _ = x.shape
    depth = w_in.shape[0]
    assert seq % max(ATT_QB, SSD_Q, FFN_TM) == 0 and (batch * seq) % PROJ_TM == 0
    x2 = x.reshape(batch * seq, D_MODEL)
    row = lambda v: v.astype(F32).reshape(1, -1)
    for i in range(depth):
        lam_init = 0.8 - 0.6 * math.exp(-0.3 * i)
        lam = (jnp.exp(jnp.sum(lambda_q1[i].astype(F32) * lambda_k1[i].astype(F32)))
               - jnp.exp(jnp.sum(lambda_q2[i].astype(F32) * lambda_k2[i].astype(F32)))
               + lam_init).reshape(1)
        proj, dt_raw = _in_proj(x2, row(attn_norm_g[i]), _w_prep(w_in[i].astype(F32)))
        att, ssm = _mixers(
            _attention_call(proj, lam, rel_bias_table, row(attn_subln_g[i]), batch, seq),
            _ssd_call(proj, dt_raw, ssm_conv_w[i], ssm_conv_b[i], ssm_dt_bias[i], ssm_a_log[i],
                      ssm_d[i], ssm_norm_g[i], batch, seq),
            batch, seq, lam_init)
        assert depth == 1
        x2 = _mix_ffn(x2, att, ssm, w_out[i].astype(BF16), row(ffn_norm_g[i]),
                      ffn_w_up[i].astype(BF16), ffn_conv_w[i].astype(F32), row(ffn_conv_b[i]),
                      ffn_w_down[i].astype(BF16), row(final_norm_g), seq)
    return x2.reshape(batch, seq, D_MODEL)
```

```python
import functools
import math

import jax
import jax.numpy as jnp
from jax import lax
from jax.experimental import pallas as pl
from jax.experimental.pallas import tpu as pltpu

F32 = jnp.float32
BF16 = jnp.bfloat16

D_MODEL = 1024
CHUNK = 64
ATT_HEADS = 8
ATT_HEAD_DIM = 64
ATT_V_DIM = 2 * ATT_HEAD_DIM
ATT_WIDTH = ATT_HEADS * ATT_V_DIM
SSM_HEADS = 16
SSM_HEAD_DIM = 64
SSM_WIDTH = SSM_HEADS * SSM_HEAD_DIM
SSM_GROUPS = 2
SSM_STATE = 128
SSM_CONV = 4
SSM_HEADS_PER_GROUP = SSM_HEADS // SSM_GROUPS
SSM_GROUP_WIDTH = SSM_WIDTH // SSM_GROUPS
FFN_DIM = 2816
FFN_CONV = 3
REL_BUCKETS = 32
REL_MAX_DIST = 128
NORM_EPS = 1e-6
SUBLN_EPS = 1e-5
SSM_NORM_EPS = 1e-5
BC_COLS = 2 * SSM_GROUPS * SSM_STATE
MAIN_COLS = 3 * ATT_WIDTH + SSM_WIDTH + SSM_WIDTH + BC_COLS
DT_COLS = SSM_HEADS
LOG2E = math.log2(math.e)

LANES = 128
SUBLANES = 8
VMEM_LIMIT = 56 * 1024 * 1024

PROJ_TM = 512
PROJ_CN = 512
ATT_QB = 256
ATT_KB = 256
SSD_Q = 256
FFN_TM = 512
FFN_CW = 256
HALO = SUBLANES
OTHER_WORK_PER_BLOCK = (1, 2, 1, 2, 1, 1, 1, 0)


def _resident(shape):
    nd = len(shape)
    return pl.BlockSpec(shape, lambda *_: (0,) * nd, pipeline_mode=pl.Buffered(1))


def _sigmoid(x):
    return 1.0 / (1.0 + jnp.exp2(x * -LOG2E))


W_PAD_COLS = MAIN_COLS + LANES


def _wprep_body(w_ref, o_ref):
    q_scale = LOG2E * ATT_HEAD_DIM ** -0.5
    o_ref[0:ATT_WIDTH, :] = (w_ref[0:ATT_WIDTH, :] * q_scale).astype(BF16)
    o_ref[ATT_WIDTH:MAIN_COLS + DT_COLS, :] = w_ref[ATT_WIDTH:MAIN_COLS + DT_COLS, :].astype(BF16)
    o_ref[MAIN_COLS + DT_COLS:W_PAD_COLS, :] = jnp.zeros((LANES - DT_COLS, D_MODEL), BF16)


def _w_prep(w_t):
    whole = lambda shape: pl.BlockSpec(shape, lambda: (0, 0))
    return pl.pallas_call(
        _wprep_body,
        in_specs=[whole((MAIN_COLS + DT_COLS, D_MODEL))],
        out_specs=whole((W_PAD_COLS, D_MODEL)),
        out_shape=jax.ShapeDtypeStruct((W_PAD_COLS, D_MODEL), BF16),
        compiler_params=pltpu.CompilerParams(vmem_limit_bytes=VMEM_LIMIT),
        name="w_prep",
    )(w_t)


def _inproj_body(x_ref, g_ref, w_ref, wdt_ref, proj_ref, dt_ref):
    x = x_ref[...]
    h = x * lax.rsqrt(jnp.mean(x * x, axis=-1, keepdims=True) + NORM_EPS) * g_ref[...]
    h = h.astype(BF16)
    nt = (((1,), (1,)), ((), ()))
    for j in range(MAIN_COLS // PROJ_CN):
        sl = slice(j * PROJ_CN, (j + 1) * PROJ_CN)
        proj_ref[:, sl] = lax.dot_general(h, w_ref[sl, :], nt, preferred_element_type=F32).astype(BF16)
    dt_ref[...] = lax.dot_general(h, wdt_ref[...], nt, preferred_element_type=F32)


def _in_proj(x2, g, w_pad):
    t = x2.shape[0]
    return pl.pallas_call(
        _inproj_body,
        grid=(t // PROJ_TM,),
        in_specs=[
            pl.BlockSpec((PROJ_TM, D_MODEL), lambda i: (i, 0)),
            _resident((1, D_MODEL)),
            _resident((MAIN_COLS, D_MODEL)),
            pl.BlockSpec((LANES, D_MODEL), lambda i: (MAIN_COLS // LANES, 0), pipeline_mode=pl.Buffered(1)),
        ],
        out_specs=[
            pl.BlockSpec((PROJ_TM, MAIN_COLS), lambda i: (i, 0)),
            pl.BlockSpec((PROJ_TM, LANES), lambda i: (i, 0)),
        ],
        out_shape=[
            jax.ShapeDtypeStruct((t, MAIN_COLS), BF16),
            jax.ShapeDtypeStruct((t, LANES), F32),
        ],
        compiler_params=pltpu.CompilerParams(
            dimension_semantics=("arbitrary",), vmem_limit_bytes=VMEM_LIMIT),
        name="in_proj",
    )(x2, g, w_pad, w_pad)


FAR_BUCKET = REL_BUCKETS // 2 - 1
BIAS_SPAN = 4 * ATT_KB


def _t5_bucket(rel):
    nb = REL_BUCKETS // 2
    max_exact = nb // 2
    bucket = jnp.where(rel > 0, nb, 0)
    n = jnp.abs(rel)
    nf = jnp.maximum(n, 1).astype(F32)
    large = max_exact + (jnp.log(nf / max_exact) / math.log(REL_MAX_DIST / max_exact)
                         * (nb - max_exact)).astype(jnp.int32)
    large = jnp.minimum(large, nb - 1)
    return bucket + jnp.where(n < max_exact, n, large)


def _bias_bucket_row():
    rel = jnp.arange(BIAS_SPAN, dtype=jnp.int32) - 2 * ATT_KB
    return jnp.broadcast_to(_t5_bucket(rel)[None, :], (SUBLANES, BIAS_SPAN))


def _bias_tiles(tbl_ref, idx_ref, h):
    idx = idx_ref[...]
    base = tbl_ref[FAR_BUCKET, h]
    r = jnp.zeros(idx.shape, F32)
    for b in range(REL_BUCKETS):
        r = jnp.where(idx == b, (tbl_ref[b, h] - base) * LOG2E, r)
    rows = jnp.concatenate([r] * (ATT_QB // SUBLANES), axis=0)
    rolled = pltpu.roll(rows, 0, 1, stride=1, stride_axis=0)
    return rolled[:, 2 * ATT_KB:3 * ATT_KB], rolled[:, ATT_KB:2 * ATT_KB]


def _attn_body(lam_ref, tbl_ref, q_ref, k_ref, v_ref, idx_ref, subg_ref, o_ref, v1_ref, *, lam_init,
               other_work):
    qb, kb = ATT_QB, ATT_KB
    seq = q_ref.shape[0]
    lane = lax.broadcasted_iota(jnp.int32, (qb, LANES), 1)
    lam = lam_ref[0]

    @functools.cache
    def bias_and_mask():
        row = lax.broadcasted_iota(jnp.int32, (qb, kb), 0)
        col = lax.broadcasted_iota(jnp.int32, (qb, kb), 1)
        allowed = (col // CHUNK) <= (row // CHUNK)
        b_diag, b_prev = _bias_tiles(tbl_ref, idx_ref, pl.program_id(1))
        return tuple(jnp.concatenate([t, t], axis=0) for t in (allowed, b_diag, b_prev))

    @functools.cache
    def build_v1():
        v1_ref[:, 0:LANES] = v_ref[...]
        v1_ref[:, LANES:2 * LANES] = jnp.ones((seq, LANES), BF16)

    def scores(i):
        kvl = (i + 1) * kb
        q = q_ref[i * qb:(i + 1) * qb, :]
        zero = jnp.zeros_like(q)
        qs = jnp.concatenate([jnp.where(lane < ATT_HEAD_DIM, q, zero),
                              jnp.where(lane >= ATT_HEAD_DIM, q, zero)], axis=0)
        s = lax.dot_general(qs, k_ref[0:kvl, :], (((1,), (1,)), ((), ())),
                            preferred_element_type=F32)
        allowed, b_diag, b_prev = bias_and_mask()
        blocks = [s[:, j * kb:(j + 1) * kb] for j in range(i + 1)]
        blocks[i] = jnp.where(allowed, blocks[i] + b_diag, -1e30)
        if i >= 1:
            blocks[i - 1] = blocks[i - 1] + b_prev
        m = jnp.max(functools.reduce(jnp.maximum, blocks), axis=-1, keepdims=True)
        return blocks, m

    def probs(blocks, m):
        return jnp.concatenate([jnp.exp2(blk - m).astype(BF16) for blk in blocks], axis=1)

    def finish(i, p):
        build_v1()
        kvl = (i + 1) * kb
        acc = jnp.dot(p, v1_ref[0:kvl, :], preferred_element_type=F32)
        o = acc[:, 0:LANES] / acc[:, LANES:2 * LANES]
        o = o[:qb] - lam * o[qb:]
        o = o * lax.rsqrt(jnp.mean(o * o, axis=-1, keepdims=True) + SUBLN_EPS) * subg_ref[...]
        o_ref[i * qb:(i + 1) * qb, :] = (o * (1.0 - lam_init)).astype(BF16)

    order = list(reversed(range(seq // qb)))
    scored, exped = {}, {}
    for n in range(len(order) + 2):
        for _ in range(OTHER_WORK_PER_BLOCK[n] if n < len(OTHER_WORK_PER_BLOCK) else 0):
            next(other_work, None)
        if n < len(order):
            scored[n] = scores(order[n])
        if 0 <= n - 1 < len(order):
            exped[n - 1] = probs(*scored.pop(n - 1))
        if 0 <= n - 2 < len(order):
            finish(order[n - 2], exped.pop(n - 2))


def _attention_call(proj, lam, rel_table, sub_g, batch, seq):
    kcol = ATT_WIDTH // LANES
    vcol = 2 * ATT_WIDTH // LANES
    in_specs = [
        pl.BlockSpec(memory_space=pltpu.SMEM),
        pl.BlockSpec(memory_space=pltpu.SMEM),
        pl.BlockSpec((seq, LANES), lambda b, h: (b, h)),
        pl.BlockSpec((seq, LANES), lambda b, h: (b, kcol + h)),
        pl.BlockSpec((seq, LANES), lambda b, h: (b, vcol + h)),
        _resident((SUBLANES, BIAS_SPAN)),
        pl.BlockSpec((1, ATT_V_DIM), lambda b, h: (0, 0)),
    ]
    operands = (lam, rel_table.astype(F32), proj, proj, proj, _bias_bucket_row(), sub_g)
    out_spec = pl.BlockSpec((seq, LANES), lambda b, h: (b, h))
    out_shape = jax.ShapeDtypeStruct((batch * seq, ATT_WIDTH), BF16)
    scratch = [pltpu.VMEM((seq, 2 * LANES), BF16)]
    return in_specs, operands, out_spec, out_shape, scratch


def _split3(x):
    x1 = x.astype(BF16)
    r1 = x - x1.astype(F32)
    x2 = r1.astype(BF16)
    x3 = (r1 - x2.astype(F32)).astype(BF16)
    return x1, x2, x3


def _ssd_steps(z_ref, xs_ref, bc_ref, dt_ref, cwx_ref, cbx_ref, cwbc_ref, cbbc_ref, dtb_ref,
               alog_ref, dexp_ref, ng_ref, e3_ref, shift_ref, tril_ref, o_ref, halox_ref, halobc_ref,
               state_ref):
    c = pl.program_id(1)
    q = SSD_Q

    @pl.when(c == 0)
    def _():
        halox_ref[...] = jnp.zeros(halox_ref.shape, F32)
        halobc_ref[...] = jnp.zeros(halobc_ref.shape, F32)
        state_ref[...] = jnp.zeros(state_ref.shape, F32)

    yield

    def conv_silu(raw_ref, halo_ref, w_ref, b_ref):
        xb = raw_ref[...]
        x = xb.astype(F32)
        shifted = jnp.dot(shift_ref[...], xb, preferred_element_type=F32)
        acc = b_ref[...] + w_ref[SSM_CONV - 1:SSM_CONV, :] * x
        head = jnp.zeros((HALO, x.shape[1]), F32)
        for j in range(1, SSM_CONV):
            wj = w_ref[SSM_CONV - 1 - j:SSM_CONV - j, :]
            acc = acc + wj * shifted[(j - 1) * q:j * q]
            head = head + wj * halo_ref[HALO - j:2 * HALO - j, :]
        acc = jnp.concatenate([acc[0:HALO] + head, acc[HALO:]], axis=0)
        halo_ref[0:HALO, :] = x[q - HALO:q]
        return acc * _sigmoid(acc)

    xs = conv_silu(xs_ref, halox_ref, cwx_ref, cbx_ref)
    yield
    bc = conv_silu(bc_ref, halobc_ref, cwbc_ref, cbbc_ref)
    yield

    dtr = dt_ref[...] + dtb_ref[...]
    dt = jnp.maximum(dtr, 0.0) + jnp.log(1.0 + jnp.exp(-jnp.abs(dtr)))
    a = (-LOG2E * jnp.exp(alog_ref[...])) * dt

    row = lax.broadcasted_iota(jnp.int32, (q, q), 0)
    col = lax.broadcasted_iota(jnp.int32, (q, q), 1)
    causal = col <= row
    tril = tril_ref[...]
    acum = sum(jnp.dot(tril, t, preferred_element_type=F32) for t in _split3(a))

    lane = lax.broadcasted_iota(jnp.int32, (q, LANES), 1)

    def expand(x):
        parts = [jnp.where(lane < SSM_HEADS, t.astype(F32), 0.0) for t in _split3(x)]
        packed = parts[0] + pltpu.roll(parts[1], SSM_HEADS, 1) + pltpu.roll(parts[2], 2 * SSM_HEADS, 1)
        return jnp.dot(packed.astype(BF16), e3_ref[...], preferred_element_type=F32)

    dt_exp = expand(dt)
    acum_exp = expand(acum)
    ea_exp = jnp.exp2(acum_exp)
    last_exp = acum_exp[q - 1:q, :]
    dte_exp = jnp.exp2(last_exp - acum_exp)
    ea_last = ea_exp[q - 1:q, :]

    xdt = xs * dt_exp
    xdt_b = xdt.astype(BF16)
    acum_t = acum.T
    pair_lane = lax.broadcasted_iota(jnp.int32, (q, LANES), 1)
    yield

    for g in range(SSM_GROUPS):
        gsl = slice(g * SSM_GROUP_WIDTH, (g + 1) * SSM_GROUP_WIDTH)
        bg = bc[:, g * SSM_STATE:(g + 1) * SSM_STATE]
        cg = bc[:, (SSM_GROUPS + g) * SSM_STATE:(SSM_GROUPS + g + 1) * SSM_STATE]
        bg_b = bg.astype(BF16)
        cg_b = cg.astype(BF16)
        cb = lax.dot_general(cg_b, bg_b, (((1,), (1,)), ((), ())), preferred_element_type=F32)

        def masked(h):
            seg = acum[:, h:h + 1] - acum_t[h:h + 1, :]
            return (cb * jnp.exp2(jnp.where(causal, seg, -jnp.inf))).astype(BF16)

        pairs = []
        for j in range(SSM_HEADS_PER_GROUP // 2):
            h0 = g * SSM_HEADS_PER_GROUP + 2 * j
            xp = xdt_b[:, h0 * SSM_HEAD_DIM:(h0 + 2) * SSM_HEAD_DIM]
            r0 = jnp.dot(masked(h0), xp, preferred_element_type=F32)
            r1 = jnp.dot(masked(h0 + 1), xp, preferred_element_type=F32)
            pairs.append(jnp.where(pair_lane < SSM_HEAD_DIM, r0, r1))
            if j % 2 == 1:
                yield
        y_diag = jnp.concatenate(pairs, axis=1)

        st = state_ref[g]
        y_off = jnp.dot(cg_b, st.astype(BF16), preferred_element_type=F32) * ea_exp[:, gsl]
        w = (xdt[:, gsl] * dte_exp[:, gsl]).astype(BF16)
        s_new = jnp.dot(bg.T.astype(BF16), w, preferred_element_type=F32)
        state_ref[g] = st * ea_last[:, gsl] + s_new

        y = y_diag + y_off + xs[:, gsl] * dexp_ref[:, gsl]
        zf = z_ref[:, gsl].astype(F32)
        gated = y * (zf * _sigmoid(zf))
        gated = gated * lax.rsqrt(jnp.mean(gated * gated, axis=-1, keepdims=True) + SSM_NORM_EPS)
        o_ref[:, gsl] = (gated * ng_ref[:, gsl]).astype(BF16)
        yield


def _ssd_call(proj, dt_raw, conv_w, conv_b, dt_bias, a_log, d_skip, norm_g, batch, seq):
    nc = seq // SSD_Q
    t = batch * seq
    zcol = 3 * ATT_WIDTH // SSM_WIDTH
    xcol = zcol + 1
    bccol = (3 * ATT_WIDTH + 2 * SSM_WIDTH) // BC_COLS
    pad = LANES - SSM_HEADS
    dtb = jnp.pad(dt_bias.astype(F32), (0, pad)).reshape(1, LANES)
    alog = jnp.pad(a_log.astype(F32), (0, pad)).reshape(1, LANES)
    dexp = jnp.repeat(d_skip.astype(F32), SSM_HEAD_DIM).reshape(1, SSM_WIDTH)
    r = jnp.arange(LANES)[:, None]
    hcol = (jnp.arange(SSM_WIDTH) // SSM_HEAD_DIM)[None, :]
    e3 = jnp.where((r % SSM_HEADS == hcol) & (r < 3 * SSM_HEADS), 1.0, 0.0).astype(BF16)
    cw = conv_w.astype(F32)
    cb = conv_b.astype(F32).reshape(1, -1)
    tq = jnp.arange(SSD_Q)
    delta = tq[:, None] - tq[None, :]
    shift = jnp.concatenate([jnp.where(delta == j, 1.0, 0.0) for j in range(1, SSM_CONV)]).astype(BF16)
    tril = jnp.where(delta >= 0, 1.0, 0.0).astype(BF16)
    row_spec = lambda width, colblk: pl.BlockSpec((SSD_Q, width), lambda b, c: (b * nc + c, colblk))
    in_specs = [
        row_spec(SSM_WIDTH, zcol),
        row_spec(SSM_WIDTH, xcol),
        row_spec(BC_COLS, bccol),
        row_spec(LANES, 0),
        _resident((SSM_CONV, SSM_WIDTH)),
        _resident((1, SSM_WIDTH)),
        _resident((SSM_CONV, BC_COLS)),
        _resident((1, BC_COLS)),
        _resident((1, LANES)),
        _resident((1, LANES)),
        _resident((1, SSM_WIDTH)),
        _resident((1, SSM_WIDTH)),
        _resident((LANES, SSM_WIDTH)),
        _resident(((SSM_CONV - 1) * SSD_Q, SSD_Q)),
        _resident((SSD_Q, SSD_Q)),
    ]
    operands = (proj, proj, proj, dt_raw, cw[:, :SSM_WIDTH], cb[:, :SSM_WIDTH], cw[:, SSM_WIDTH:],
                cb[:, SSM_WIDTH:], dtb, alog, dexp, norm_g.astype(F32).reshape(1, -1), e3, shift, tril)
    out_spec = pl.BlockSpec((SSD_Q, SSM_WIDTH), lambda b, c: (b * nc + c, 0))
    out_shape = jax.ShapeDtypeStruct((t, SSM_WIDTH), BF16)
    scratch = [
        pltpu.VMEM((2 * HALO, SSM_WIDTH), F32),
        pltpu.VMEM((2 * HALO, BC_COLS), F32),
        pltpu.VMEM((SSM_GROUPS, SSM_STATE, SSM_GROUP_WIDTH), F32),
    ]
    return in_specs, operands, out_spec, out_shape, scratch


def _mixers_body(*refs, n_att_in, n_ssd_in, n_att_scratch, lam_init):
    att_in = refs[:n_att_in]
    ssd_in = refs[n_att_in:n_att_in + n_ssd_in]
    att_out, ssd_out, *scratch = refs[n_att_in + n_ssd_in:]
    att_scratch, ssd_scratch = scratch[:n_att_scratch], scratch[n_att_scratch:]
    ssd = _ssd_steps(*ssd_in, ssd_out, *ssd_scratch)
    next(ssd)
    _attn_body(*att_in, att_out, *att_scratch, lam_init=lam_init, other_work=ssd)
    for _ in ssd:
        pass


def _mixers(att_call, ssd_call, batch, seq, lam_init):
    assert seq // SSD_Q == ATT_HEADS, "one SSD scan step per attention head in each batch row"
    a_specs, a_ops, a_out, a_shape, a_scratch = att_call
    s_specs, s_ops, s_out, s_shape, s_scratch = ssd_call
    return pl.pallas_call(
        functools.partial(_mixers_body, n_att_in=len(a_specs), n_ssd_in=len(s_specs),
                          n_att_scratch=len(a_scratch), lam_init=lam_init),
        grid=(batch, ATT_HEADS),
        in_specs=a_specs + s_specs,
        out_specs=[a_out, s_out],
        out_shape=[a_shape, s_shape],
        scratch_shapes=a_scratch + s_scratch,
        compiler_params=pltpu.CompilerParams(
            dimension_semantics=("arbitrary", "arbitrary"), vmem_limit_bytes=VMEM_LIMIT),
        name="mixers",
    )(*a_ops, *s_ops)


def _ffn_body(x_ref, att_ref, ssm_ref, wo_ref, g2_ref, wup_ref, cw_ref, cb_ref, wdn_ref, gf_ref,
              o_ref, xc_ref, x1p_ref, h2_ref, oc_ref, carry_ref, *, tiles_per_seq):
    i = pl.program_id(0)
    tm, cw = FFN_TM, FFN_CW
    ph = tm // SUBLANES
    n_lane_blocks = D_MODEL // LANES

    @pl.when(i % tiles_per_seq == 0)
    def _():
        carry_ref[...] = jnp.zeros(carry_ref.shape, F32)

    x1 = (x_ref[...]
          + jnp.dot(att_ref[...], wo_ref[0:ATT_WIDTH, :], preferred_element_type=F32)
          + jnp.dot(ssm_ref[...], wo_ref[ATT_WIDTH:, :], preferred_element_type=F32))
    for c in range(n_lane_blocks):
        xc_ref[c] = x1[:, c * LANES:(c + 1) * LANES]
    x1p = jnp.concatenate(
        [jnp.concatenate([xc_ref[c, pl.ds(k, ph, stride=SUBLANES), :] for k in range(SUBLANES)], axis=0)
         for c in range(n_lane_blocks)], axis=1)
    x1p_ref[...] = x1p
    h2_ref[...] = (x1p * lax.rsqrt(jnp.mean(x1p * x1p, axis=-1, keepdims=True) + NORM_EPS)
                   * g2_ref[...]).astype(BF16)

    first_row = lax.broadcasted_iota(jnp.int32, (ph, cw), 0) == 0

    def prev_token(block, carry_slot):
        tail = carry_ref[carry_slot]
        carry_ref[carry_slot] = block[ph - HALO:ph]
        return jnp.where(first_row, jnp.broadcast_to(tail[HALO - 1:HALO, :], (ph, cw)),
                         pltpu.roll(block, 1, 0))

    def up_conv(slot, col):
        u = jnp.dot(h2_ref[...], wup_ref[:, col:col + cw], preferred_element_type=F32)
        blocks = [u[k * ph:(k + 1) * ph] for k in range(SUBLANES)]
        back1 = prev_token(blocks[SUBLANES - 1], 2 * slot)
        back2 = prev_token(blocks[SUBLANES - 2], 2 * slot + 1)
        hist = [back2, back1] + blocks
        w = [cw_ref[t:t + 1, col:col + cw] for t in range(FFN_CONV)]
        b = cb_ref[:, col:col + cw]
        return jnp.concatenate(
            [b + w[2] * hist[k + 2] + w[1] * hist[k + 1] + w[0] * hist[k] for k in range(SUBLANES)],
            axis=0)

    acts = []
    for j in range(FFN_DIM // cw):
        gate = up_conv(2 * j, j * cw)
        val = up_conv(2 * j + 1, FFN_DIM + j * cw)
        acts.append((gate * _sigmoid(gate) * val).astype(BF16))

    x2 = x1p_ref[...] + jnp.dot(jnp.concatenate(acts, axis=1), wdn_ref[...], preferred_element_type=F32)
    out = x2 * lax.rsqrt(jnp.mean(x2 * x2, axis=-1, keepdims=True) + NORM_EPS) * gf_ref[...]
    for c in range(n_lane_blocks):
        for k in range(SUBLANES):
            oc_ref[c, pl.ds(k, ph, stride=SUBLANES), :] = out[k * ph:(k + 1) * ph, c * LANES:(c + 1) * LANES]
    for c in range(n_lane_blocks):
        o_ref[:, c * LANES:(c + 1) * LANES] = oc_ref[c]


def _mix_ffn(x2, att, ssm, w_out, g2, w_up, conv_w, conv_b, w_down, gf, seq):
    t = x2.shape[0]
    n_carry = 2 * (FFN_CONV - 1) * (FFN_DIM // FFN_CW)
    tok = lambda width: pl.BlockSpec((FFN_TM, width), lambda i: (i, 0))
    return pl.pallas_call(
        functools.partial(_ffn_body, tiles_per_seq=seq // FFN_TM),
        grid=(t // FFN_TM,),
        in_specs=[
            tok(D_MODEL), tok(ATT_WIDTH), tok(SSM_WIDTH),
            _resident((ATT_WIDTH + SSM_WIDTH, D_MODEL)),
            _resident((1, D_MODEL)),
            _resident((D_MODEL, 2 * FFN_DIM)),
            _resident((FFN_CONV, 2 * FFN_DIM)),
            _resident((1, 2 * FFN_DIM)),
            _resident((FFN_DIM, D_MODEL)),
            _resident((1, D_MODEL)),
        ],
        out_specs=tok(D_MODEL),
        out_shape=jax.ShapeDtypeStruct((t, D_MODEL), F32),
        scratch_shapes=[
            pltpu.VMEM((D_MODEL // LANES, FFN_TM, LANES), F32),
            pltpu.VMEM((FFN_TM, D_MODEL), F32),
            pltpu.VMEM((FFN_TM, D_MODEL), BF16),
            pltpu.VMEM((D_MODEL // LANES, FFN_TM, LANES), F32),
            pltpu.VMEM((n_carry, HALO, FFN_CW), F32),
        ],
        compiler_params=pltpu.CompilerParams(
            dimension_semantics=("arbitrary",), vmem_limit_bytes=VMEM_LIMIT),
        name="mix_ffn",
    )(x2, att, ssm, w_out, g2, w_up, conv_w, conv_b, w_down, gf)


def kernel(x, rel_bias_table, attn_norm_g, w_in, lambda_q1, lambda_k1, lambda_q2, lambda_k2,
           attn_subln_g, ssm_conv_w, ssm_conv_b, ssm_dt_bias, ssm_a_log, ssm_d, ssm_norm_g,
           w_out, ffn_norm_g, ffn_w_up, ffn_conv_w, ffn_conv_b, ffn_w_down, final_norm_g):
    batch, seq, _ = x.shape
    depth = w_in.shape[0]
    assert seq % max(ATT_QB, SSD_Q, FFN_TM) == 0 and (batch * seq) % PROJ_TM == 0
    x2 = x.reshape(batch * seq, D_MODEL)
    row = lambda v: v.astype(F32).reshape(1, -1)
    for i in range(depth):
        lam_init = 0.8 - 0.6 * math.exp(-0.3 * i)
        lam = (jnp.exp(jnp.sum(lambda_q1[i].astype(F32) * lambda_k1[i].astype(F32)))
               - jnp.exp(jnp.sum(lambda_q2[i].astype(F32) * lambda_k2[i].astype(F32)))
               + lam_init).reshape(1)
        proj, dt_raw = _in_proj(x2, row(attn_norm_g[i]), _w_prep(w_in[i].astype(F32).T))
        att, ssm = _mixers(
            _attention_call(proj, lam, rel_bias_table, row(attn_subln_g[i]), batch, seq),
            _ssd_call(proj, dt_raw, ssm_conv_w[i], ssm_conv_b[i], ssm_dt_bias[i], ssm_a_log[i],
                      ssm_d[i], ssm_norm_g[i], batch, seq),
            batch, seq, lam_init)
        assert depth == 1
        x2 = _mix_ffn(x2, att, ssm, w_out[i].astype(BF16), row(ffn_norm_g[i]),
                      ffn_w_up[i].astype(BF16), ffn_conv_w[i].astype(F32), row(ffn_conv_b[i]),
                      ffn_w_down[i].astype(BF16), row(final_norm_g), seq)
    return x2.reshape(batch, seq, D_MODEL)
```

```python
import functools
import math

import jax
import jax.numpy as jnp
from jax import lax
from jax.experimental import pallas as pl
from jax.experimental.pallas import tpu as pltpu

F32 = jnp.float32
BF16 = jnp.bfloat16

D_MODEL = 1024
CHUNK = 64
ATT_HEADS = 8
ATT_HEAD_DIM = 64
ATT_V_DIM = 2 * ATT_HEAD_DIM
ATT_WIDTH = ATT_HEADS * ATT_V_DIM
SSM_HEADS = 16
SSM_HEAD_DIM = 64
SSM_WIDTH = SSM_HEADS * SSM_HEAD_DIM
SSM_GROUPS = 2
SSM_STATE = 128
SSM_CONV = 4
SSM_HEADS_PER_GROUP = SSM_HEADS // SSM_GROUPS
SSM_GROUP_WIDTH = SSM_WIDTH // SSM_GROUPS
FFN_DIM = 2816
FFN_CONV = 3
REL_BUCKETS = 32
REL_MAX_DIST = 128
NORM_EPS = 1e-6
SUBLN_EPS = 1e-5
SSM_NORM_EPS = 1e-5
BC_COLS = 2 * SSM_GROUPS * SSM_STATE
MAIN_COLS = 3 * ATT_WIDTH + SSM_WIDTH + SSM_WIDTH + BC_COLS
DT_COLS = SSM_HEADS
LOG2E = math.log2(math.e)

LANES = 128
SUBLANES = 8
VMEM_LIMIT = 56 * 1024 * 1024

PROJ_TM = 512
PROJ_CN = 512
ATT_QB = 256
ATT_KB = 256
SSD_Q = 256
FFN_TM = 512
FFN_CW = 256
HALO = SUBLANES
OTHER_WORK_PER_BLOCK = (1, 2, 1, 2, 1, 1, 1, 0)


def _resident(shape):
    nd = len(shape)
    return pl.BlockSpec(shape, lambda *_: (0,) * nd, pipeline_mode=pl.Buffered(1))


def _sigmoid(x):
    return 1.0 / (1.0 + jnp.exp2(x * -LOG2E))


W_PAD_COLS = MAIN_COLS + LANES


def _wprep_body(w_ref, o_ref):
    q_scale = LOG2E * ATT_HEAD_DIM ** -0.5
    o_ref[0:ATT_WIDTH, :] = (w_ref[0:ATT_WIDTH, :] * q_scale).astype(BF16)
    o_ref[ATT_WIDTH:MAIN_COLS + DT_COLS, :] = w_ref[ATT_WIDTH:MAIN_COLS + DT_COLS, :].astype(BF16)
    o_ref[MAIN_COLS + DT_COLS:W_PAD_COLS, :] = jnp.zeros((LANES - DT_COLS, D_MODEL), BF16)


def _w_prep(w_t):
    whole = lambda shape: pl.BlockSpec(shape, lambda: (0, 0))
    return pl.pallas_call(
        _wprep_body,
        in_specs=[whole((MAIN_COLS + DT_COLS, D_MODEL))],
        out_specs=whole((W_PAD_COLS, D_MODEL)),
        out_shape=jax.ShapeDtypeStruct((W_PAD_COLS, D_MODEL), BF16),
        compiler_params=pltpu.CompilerParams(vmem_limit_bytes=VMEM_LIMIT),
        name="w_prep",
    )(w_t)


FAR_BUCKET = REL_BUCKETS // 2 - 1
BIAS_SPAN = 4 * ATT_KB


def _t5_bucket(rel):
    nb = REL_BUCKETS // 2
    max_exact = nb // 2
    bucket = jnp.where(rel > 0, nb, 0)
    n = jnp.abs(rel)
    nf = jnp.maximum(n, 1).astype(F32)
    large = max_exact + (jnp.log(nf / max_exact) / math.log(REL_MAX_DIST / max_exact)
                         * (nb - max_exact)).astype(jnp.int32)
    large = jnp.minimum(large, nb - 1)
    return bucket + jnp.where(n < max_exact, n, large)


def _bias_bucket_row():
    rel = jnp.arange(BIAS_SPAN, dtype=jnp.int32) - 2 * ATT_KB
    return jnp.broadcast_to(_t5_bucket(rel)[None, :], (SUBLANES, BIAS_SPAN))


def _bias_tiles(tbl_ref, idx_ref, h):
    idx = idx_ref[...]
    base = tbl_ref[FAR_BUCKET, h]
    r = jnp.zeros(idx.shape, F32)
    for b in range(REL_BUCKETS):
        r = jnp.where(idx == b, (tbl_ref[b, h] - base) * LOG2E, r)
    rows = jnp.concatenate([r] * (ATT_QB // SUBLANES), axis=0)
    rolled = pltpu.roll(rows, 0, 1, stride=1, stride_axis=0)
    return rolled[:, 2 * ATT_KB:3 * ATT_KB], rolled[:, ATT_KB:2 * ATT_KB]


def _attn_body(lam_ref, tbl_ref, q_ref, k_ref, v_ref, idx_ref, subg_ref, o_ref, v1_ref, *, lam_init,
               other_work):
    qb, kb = ATT_QB, ATT_KB
    seq = q_ref.shape[0]
    lane = lax.broadcasted_iota(jnp.int32, (qb, LANES), 1)
    lam = lam_ref[0]

    @functools.cache
    def bias_and_mask():
        row = lax.broadcasted_iota(jnp.int32, (qb, kb), 0)
        col = lax.broadcasted_iota(jnp.int32, (qb, kb), 1)
        allowed = (col // CHUNK) <= (row // CHUNK)
        b_diag, b_prev = _bias_tiles(tbl_ref, idx_ref, pl.program_id(1))
        return tuple(jnp.concatenate([t, t], axis=0) for t in (allowed, b_diag, b_prev))

    @functools.cache
    def build_v1():
        v1_ref[:, 0:LANES] = v_ref[...]
        v1_ref[:, LANES:2 * LANES] = jnp.ones((seq, LANES), BF16)

    def scores(i):
        kvl = (i + 1) * kb
        q = q_ref[i * qb:(i + 1) * qb, :]
        zero = jnp.zeros_like(q)
        qs = jnp.concatenate([jnp.where(lane < ATT_HEAD_DIM, q, zero),
                              jnp.where(lane >= ATT_HEAD_DIM, q, zero)], axis=0)
        s = lax.dot_general(qs, k_ref[0:kvl, :], (((1,), (1,)), ((), ())),
                            preferred_element_type=F32)
        allowed, b_diag, b_prev = bias_and_mask()
        blocks = [s[:, j * kb:(j + 1) * kb] for j in range(i + 1)]
        blocks[i] = jnp.where(allowed, blocks[i] + b_diag, -1e30)
        if i >= 1:
            blocks[i - 1] = blocks[i - 1] + b_prev
        m = jnp.max(functools.reduce(jnp.maximum, blocks), axis=-1, keepdims=True)
        return blocks, m

    def probs(blocks, m):
        return jnp.concatenate([jnp.exp2(blk - m).astype(BF16) for blk in blocks], axis=1)

    def finish(i, p):
        build_v1()
        kvl = (i + 1) * kb
        acc = jnp.dot(p, v1_ref[0:kvl, :], preferred_element_type=F32)
        o = acc[:, 0:LANES] / acc[:, LANES:2 * LANES]
        o = o[:qb] - lam * o[qb:]
        o = o * lax.rsqrt(jnp.mean(o * o, axis=-1, keepdims=True) + SUBLN_EPS) * subg_ref[...]
        o_ref[i * qb:(i + 1) * qb, :] = (o * (1.0 - lam_init)).astype(BF16)

    order = list(reversed(range(seq // qb)))
    scored, exped = {}, {}
    for n in range(len(order) + 2):
        for _ in range(OTHER_WORK_PER_BLOCK[n] if n < len(OTHER_WORK_PER_BLOCK) else 0):
            next(other_work, None)
        if n < len(order):
            scored[n] = scores(order[n])
        if 0 <= n - 1 < len(order):
            exped[n - 1] = probs(*scored.pop(n - 1))
        if 0 <= n - 2 < len(order):
            finish(order[n - 2], exped.pop(n - 2))


def _attention_call(proj, lam, rel_table, sub_g, batch, seq):
    kcol = ATT_WIDTH // LANES
    vcol = 2 * ATT_WIDTH // LANES
    in_specs = [
        pl.BlockSpec(memory_space=pltpu.SMEM),
        pl.BlockSpec(memory_space=pltpu.SMEM),
        pl.BlockSpec((seq, LANES), lambda b, h: (b, h)),
        pl.BlockSpec((seq, LANES), lambda b, h: (b, kcol + h)),
        pl.BlockSpec((seq, LANES), lambda b, h: (b, vcol + h)),
        _resident((SUBLANES, BIAS_SPAN)),
        pl.BlockSpec((1, ATT_V_DIM), lambda b, h: (0, 0)),
    ]
    operands = (lam, rel_table.astype(F32), proj, proj, proj, _bias_bucket_row(), sub_g)
    out_spec = pl.BlockSpec((seq, LANES), lambda b, h: (b, h))
    out_shape = jax.ShapeDtypeStruct((batch * seq, ATT_WIDTH), BF16)
    scratch = [pltpu.VMEM((seq, 2 * LANES), BF16)]
    return in_specs, operands, out_spec, out_shape, scratch


def _split3(x):
    x1 = x.astype(BF16)
    r1 = x - x1.astype(F32)
    x2 = r1.astype(BF16)
    x3 = (r1 - x2.astype(F32)).astype(BF16)
    return x1, x2, x3


def _ssd_steps(z_ref, xs_ref, bc_ref, dt_ref, cwx_ref, cbx_ref, cwbc_ref, cbbc_ref, dtb_ref,
               alog_ref, dexp_ref, ng_ref, e3_ref, shift_ref, tril_ref, o_ref, halox_ref, halobc_ref,
               state_ref):
    q = SSD_Q

    def conv_silu(raw_ref, halo_ref, w_ref, b_ref):
        xb = raw_ref[...]
        x = xb.astype(F32)
        shifted = jnp.dot(shift_ref[...], xb, preferred_element_type=F32)
        acc = b_ref[...] + w_ref[SSM_CONV - 1:SSM_CONV, :] * x
        head = jnp.zeros((HALO, x.shape[1]), F32)
        for j in range(1, SSM_CONV):
            wj = w_ref[SSM_CONV - 1 - j:SSM_CONV - j, :]
            acc = acc + wj * shifted[(j - 1) * q:j * q]
            head = head + wj * halo_ref[HALO - j:2 * HALO - j, :]
        acc = jnp.concatenate([acc[0:HALO] + head, acc[HALO:]], axis=0)
        halo_ref[0:HALO, :] = x[q - HALO:q]
        return acc * _sigmoid(acc)

    xs = conv_silu(xs_ref, halox_ref, cwx_ref, cbx_ref)
    yield
    bc = conv_silu(bc_ref, halobc_ref, cwbc_ref, cbbc_ref)
    yield

    dtr = dt_ref[...] + dtb_ref[...]
    dt = jnp.maximum(dtr, 0.0) + jnp.log(1.0 + jnp.exp(-jnp.abs(dtr)))
    a = (-LOG2E * jnp.exp(alog_ref[...])) * dt

    row = lax.broadcasted_iota(jnp.int32, (q, q), 0)
    col = lax.broadcasted_iota(jnp.int32, (q, q), 1)
    causal = col <= row
    tril = tril_ref[...]
    acum = sum(jnp.dot(tril, t, preferred_element_type=F32) for t in _split3(a))

    lane = lax.broadcasted_iota(jnp.int32, (q, LANES), 1)

    def expand(x):
        parts = [jnp.where(lane < SSM_HEADS, t.astype(F32), 0.0) for t in _split3(x)]
        packed = parts[0] + pltpu.roll(parts[1], SSM_HEADS, 1) + pltpu.roll(parts[2], 2 * SSM_HEADS, 1)
        return jnp.dot(packed.astype(BF16), e3_ref[...], preferred_element_type=F32)

    dt_exp = expand(dt)
    acum_exp = expand(acum)
    ea_exp = jnp.exp2(acum_exp)
    last_exp = acum_exp[q - 1:q, :]
    dte_exp = jnp.exp2(last_exp - acum_exp)
    ea_last = ea_exp[q - 1:q, :]

    xdt = xs * dt_exp
    xdt_b = xdt.astype(BF16)
    acum_t = acum.T
    pair_lane = lax.broadcasted_iota(jnp.int32, (q, LANES), 1)
    yield

    for g in range(SSM_GROUPS):
        gsl = slice(g * SSM_GROUP_WIDTH, (g + 1) * SSM_GROUP_WIDTH)
        bg = bc[:, g * SSM_STATE:(g + 1) * SSM_STATE]
        cg = bc[:, (SSM_GROUPS + g) * SSM_STATE:(SSM_GROUPS + g + 1) * SSM_STATE]
        bg_b = bg.astype(BF16)
        cg_b = cg.astype(BF16)
        cb = lax.dot_general(cg_b, bg_b, (((1,), (1,)), ((), ())), preferred_element_type=F32)

        def masked(h):
            seg = acum[:, h:h + 1] - acum_t[h:h + 1, :]
            return (cb * jnp.exp2(jnp.where(causal, seg, -jnp.inf))).astype(BF16)

        pairs = []
        for j in range(SSM_HEADS_PER_GROUP // 2):
            h0 = g * SSM_HEADS_PER_GROUP + 2 * j
            xp = xdt_b[:, h0 * SSM_HEAD_DIM:(h0 + 2) * SSM_HEAD_DIM]
            r0 = jnp.dot(masked(h0), xp, preferred_element_type=F32)
            r1 = jnp.dot(masked(h0 + 1), xp, preferred_element_type=F32)
            pairs.append(jnp.where(pair_lane < SSM_HEAD_DIM, r0, r1))
            if j % 2 == 1:
                yield
        y_diag = jnp.concatenate(pairs, axis=1)

        st = state_ref[g]
        y_off = jnp.dot(cg_b, st.astype(BF16), preferred_element_type=F32) * ea_exp[:, gsl]
        w = (xdt[:, gsl] * dte_exp[:, gsl]).astype(BF16)
        s_new = jnp.dot(bg.T.astype(BF16), w, preferred_element_type=F32)
        state_ref[g] = st * ea_last[:, gsl] + s_new

        y = y_diag + y_off + xs[:, gsl] * dexp_ref[:, gsl]
        zf = z_ref[:, gsl].astype(F32)
        gated = y * (zf * _sigmoid(zf))
        gated = gated * lax.rsqrt(jnp.mean(gated * gated, axis=-1, keepdims=True) + SSM_NORM_EPS)
        o_ref[:, gsl] = (gated * ng_ref[:, gsl]).astype(BF16)
        yield


def _ssd_consts(conv_w, conv_b, dt_bias, a_log, d_skip, norm_g):
    pad = LANES - SSM_HEADS
    dtb = jnp.pad(dt_bias.astype(F32), (0, pad)).reshape(1, LANES)
    alog = jnp.pad(a_log.astype(F32), (0, pad)).reshape(1, LANES)
    dexp = jnp.repeat(d_skip.astype(F32), SSM_HEAD_DIM).reshape(1, SSM_WIDTH)
    r = jnp.arange(LANES)[:, None]
    hcol = (jnp.arange(SSM_WIDTH) // SSM_HEAD_DIM)[None, :]
    e3 = jnp.where((r % SSM_HEADS == hcol) & (r < 3 * SSM_HEADS), 1.0, 0.0).astype(BF16)
    cw = conv_w.astype(F32)
    cb = conv_b.astype(F32).reshape(1, -1)
    tq = jnp.arange(SSD_Q)
    delta = tq[:, None] - tq[None, :]
    shift = jnp.concatenate([jnp.where(delta == j, 1.0, 0.0) for j in range(1, SSM_CONV)]).astype(BF16)
    tril = jnp.where(delta >= 0, 1.0, 0.0).astype(BF16)
    in_specs = [
        _resident((SSM_CONV, SSM_WIDTH)),
        _resident((1, SSM_WIDTH)),
        _resident((SSM_CONV, BC_COLS)),
        _resident((1, BC_COLS)),
        _resident((1, LANES)),
        _resident((1, LANES)),
        _resident((1, SSM_WIDTH)),
        _resident((1, SSM_WIDTH)),
        _resident((LANES, SSM_WIDTH)),
        _resident(((SSM_CONV - 1) * SSD_Q, SSD_Q)),
        _resident((SSD_Q, SSD_Q)),
    ]
    operands = (cw[:, :SSM_WIDTH], cb[:, :SSM_WIDTH], cw[:, SSM_WIDTH:], cb[:, SSM_WIDTH:], dtb, alog,
                dexp, norm_g.astype(F32).reshape(1, -1), e3, shift, tril)
    return in_specs, operands


SSD_COLS = 2 * SSM_WIDTH + BC_COLS
SCAN_PIECES_AFTER_QKV_DOT = (2, 2, 2, 3, 3, 2)


def _proj_ssd_body(x_ref, g_ref, w_ref, wdt_ref, *rest, tiles_per_seq):
    consts, (qkv_ref, ssm_ref, zxbc_ref, dt_ref, halox_ref, halobc_ref, state_ref) = rest[:-7], rest[-7:]

    @pl.when(pl.program_id(0) % tiles_per_seq == 0)
    def _():
        halox_ref[...] = jnp.zeros(halox_ref.shape, F32)
        halobc_ref[...] = jnp.zeros(halobc_ref.shape, F32)
        state_ref[...] = jnp.zeros(state_ref.shape, F32)

    x = x_ref[...]
    h = x * lax.rsqrt(jnp.mean(x * x, axis=-1, keepdims=True) + NORM_EPS) * g_ref[...]
    h = h.astype(BF16)
    nt = (((1,), (1,)), ((), ()))

    qkv_cols = 3 * ATT_WIDTH

    def project(col):
        y = lax.dot_general(h, w_ref[col:col + PROJ_CN, :], nt, preferred_element_type=F32).astype(BF16)
        if col < qkv_cols:
            qkv_ref[:, col:col + PROJ_CN] = y
        else:
            zxbc_ref[:, col - qkv_cols:col - qkv_cols + PROJ_CN] = y

    def project_dt(_):
        dt_ref[...] = lax.dot_general(h, wdt_ref[...], nt, preferred_element_type=F32)

    def scan_steps():
        for c in range(PROJ_TM // SSD_Q):
            rows = pl.ds(c * SSD_Q, SSD_Q)
            yield from _ssd_steps(
                zxbc_ref.at[rows, pl.ds(0, SSM_WIDTH)],
                zxbc_ref.at[rows, pl.ds(SSM_WIDTH, SSM_WIDTH)],
                zxbc_ref.at[rows, pl.ds(2 * SSM_WIDTH, BC_COLS)],
                dt_ref.at[rows, :],
                *consts, ssm_ref.at[rows, :], halox_ref, halobc_ref, state_ref)

    z0, x0, bc0 = qkv_cols, qkv_cols + SSM_WIDTH, qkv_cols + 2 * SSM_WIDTH
    plan = [(project, x0, 0), (project, x0 + PROJ_CN, 1), (project, bc0, 0), (project_dt, None, 1),
            (project, z0, 1), (project, z0 + PROJ_CN, 1)]
    plan += [(project, col, pieces) for col, pieces in
             zip(range(0, qkv_cols, PROJ_CN), SCAN_PIECES_AFTER_QKV_DOT)]
    scan = scan_steps()
    for emit, col, pieces in plan:
        emit(col)
        for _ in range(pieces):
            next(scan, None)
    for _ in scan:
        pass


def _proj_ssd(x2, g, w_pad, ssd_consts, seq):
    t = x2.shape[0]
    c_specs, c_ops = ssd_consts
    tok = lambda width: pl.BlockSpec((PROJ_TM, width), lambda i: (i, 0))
    return pl.pallas_call(
        functools.partial(_proj_ssd_body, tiles_per_seq=seq // PROJ_TM),
        grid=(t // PROJ_TM,),
        in_specs=[
            tok(D_MODEL),
            _resident((1, D_MODEL)),
            _resident((MAIN_COLS, D_MODEL)),
            pl.BlockSpec((LANES, D_MODEL), lambda i: (MAIN_COLS // LANES, 0), pipeline_mode=pl.Buffered(1)),
        ] + c_specs,
        out_specs=[tok(3 * ATT_WIDTH), tok(SSM_WIDTH)],
        out_shape=[
            jax.ShapeDtypeStruct((t, 3 * ATT_WIDTH), BF16),
            jax.ShapeDtypeStruct((t, SSM_WIDTH), BF16),
        ],
        scratch_shapes=[
            pltpu.VMEM((PROJ_TM, SSD_COLS), BF16),
            pltpu.VMEM((PROJ_TM, LANES), F32),
            pltpu.VMEM((2 * HALO, SSM_WIDTH), F32),
            pltpu.VMEM((2 * HALO, BC_COLS), F32),
            pltpu.VMEM((SSM_GROUPS, SSM_STATE, SSM_GROUP_WIDTH), F32),
        ],
        compiler_params=pltpu.CompilerParams(
            dimension_semantics=("arbitrary",), vmem_limit_bytes=VMEM_LIMIT),
        name="proj_ssd",
    )(x2, g, w_pad, w_pad, *c_ops)


def _attention(qkv, lam, rel_table, sub_g, batch, seq, lam_init):
    in_specs, operands, out_spec, out_shape, scratch = _attention_call(qkv, lam, rel_table, sub_g, batch, seq)
    return pl.pallas_call(
        functools.partial(_attn_body, lam_init=lam_init, other_work=iter(())),
        grid=(batch, ATT_HEADS),
        in_specs=in_specs,
        out_specs=out_spec,
        out_shape=out_shape,
        scratch_shapes=scratch,
        compiler_params=pltpu.CompilerParams(
            dimension_semantics=("arbitrary", "arbitrary"), vmem_limit_bytes=VMEM_LIMIT),
        name="diff_attn",
    )(*operands)


def _ffn_body(x_ref, att_ref, ssm_ref, wo_ref, g2_ref, wup_ref, cw_ref, cb_ref, wdn_ref, gf_ref,
              o_ref, xc_ref, x1p_ref, h2_ref, oc_ref, carry_ref, *, tiles_per_seq):
    i = pl.program_id(0)
    tm, cw = FFN_TM, FFN_CW
    ph = tm // SUBLANES
    n_lane_blocks = D_MODEL // LANES

    @pl.when(i % tiles_per_seq == 0)
    def _():
        carry_ref[...] = jnp.zeros(carry_ref.shape, F32)

    x1 = (x_ref[...]
          + jnp.dot(att_ref[...], wo_ref[0:ATT_WIDTH, :], preferred_element_type=F32)
          + jnp.dot(ssm_ref[...], wo_ref[ATT_WIDTH:, :], preferred_element_type=F32))
    for c in range(n_lane_blocks):
        xc_ref[c] = x1[:, c * LANES:(c + 1) * LANES]
    x1p = jnp.concatenate(
        [jnp.concatenate([xc_ref[c, pl.ds(k, ph, stride=SUBLANES), :] for k in range(SUBLANES)], axis=0)
         for c in range(n_lane_blocks)], axis=1)
    x1p_ref[...] = x1p
    h2_ref[...] = (x1p * lax.rsqrt(jnp.mean(x1p * x1p, axis=-1, keepdims=True) + NORM_EPS)
                   * g2_ref[...]).astype(BF16)

    first_row = lax.broadcasted_iota(jnp.int32, (ph, cw), 0) == 0

    def prev_token(block, carry_slot):
        tail = carry_ref[carry_slot]
        carry_ref[carry_slot] = block[ph - HALO:ph]
        return jnp.where(first_row, jnp.broadcast_to(tail[HALO - 1:HALO, :], (ph, cw)),
                         pltpu.roll(block, 1, 0))

    def up_conv(slot, col):
        u = jnp.dot(h2_ref[...], wup_ref[:, col:col + cw], preferred_element_type=F32)
        blocks = [u[k * ph:(k + 1) * ph] for k in range(SUBLANES)]
        back1 = prev_token(blocks[SUBLANES - 1], 2 * slot)
        back2 = prev_token(blocks[SUBLANES - 2], 2 * slot + 1)
        hist = [back2, back1] + blocks
        w = [cw_ref[t:t + 1, col:col + cw] for t in range(FFN_CONV)]
        b = cb_ref[:, col:col + cw]
        return jnp.concatenate(
            [b + w[2] * hist[k + 2] + w[1] * hist[k + 1] + w[0] * hist[k] for k in range(SUBLANES)],
            axis=0)

    acts = []
    for j in range(FFN_DIM // cw):
        gate = up_conv(2 * j, j * cw)
        val = up_conv(2 * j + 1, FFN_DIM + j * cw)
        acts.append((gate * _sigmoid(gate) * val).astype(BF16))

    x2 = x1p_ref[...] + jnp.dot(jnp.concatenate(acts, axis=1), wdn_ref[...], preferred_element_type=F32)
    out = x2 * lax.rsqrt(jnp.mean(x2 * x2, axis=-1, keepdims=True) + NORM_EPS) * gf_ref[...]
    for c in range(n_lane_blocks):
        for k in range(SUBLANES):
            oc_ref[c, pl.ds(k, ph, stride=SUBLANES), :] = out[k * ph:(k + 1) * ph, c * LANES:(c + 1) * LANES]
    for c in range(n_lane_blocks):
        o_ref[:, c * LANES:(c + 1) * LANES] = oc_ref[c]


def _mix_ffn(x2, att, ssm, w_out, g2, w_up, conv_w, conv_b, w_down, gf, seq):
    t = x2.shape[0]
    n_carry = 2 * (FFN_CONV - 1) * (FFN_DIM // FFN_CW)
    tok = lambda width: pl.BlockSpec((FFN_TM, width), lambda i: (i, 0))
    return pl.pallas_call(
        functools.partial(_ffn_body, tiles_per_seq=seq // FFN_TM),
        grid=(t // FFN_TM,),
        in_specs=[
            tok(D_MODEL), tok(ATT_WIDTH), tok(SSM_WIDTH),
            _resident((ATT_WIDTH + SSM_WIDTH, D_MODEL)),
            _resident((1, D_MODEL)),
            _resident((D_MODEL, 2 * FFN_DIM)),
            _resident((FFN_CONV, 2 * FFN_DIM)),
            _resident((1, 2 * FFN_DIM)),
            _resident((FFN_DIM, D_MODEL)),
            _resident((1, D_MODEL)),
        ],
        out_specs=tok(D_MODEL),
        out_shape=jax.ShapeDtypeStruct((t, D_MODEL), F32),
        scratch_shapes=[
            pltpu.VMEM((D_MODEL // LANES, FFN_TM, LANES), F32),
            pltpu.VMEM((FFN_TM, D_MODEL), F32),
            pltpu.VMEM((FFN_TM, D_MODEL), BF16),
            pltpu.VMEM((D_MODEL // LANES, FFN_TM, LANES), F32),
            pltpu.VMEM((n_carry, HALO, FFN_CW), F32),
        ],
        compiler_params=pltpu.CompilerParams(
            dimension_semantics=("arbitrary",), vmem_limit_bytes=VMEM_LIMIT),
        name="mix_ffn",
    )(x2, att, ssm, w_out, g2, w_up, conv_w, conv_b, w_down, gf)


def kernel(x, rel_bias_table, attn_norm_g, w_in, lambda_q1, lambda_k1, lambda_q2, lambda_k2,
           attn_subln_g, ssm_conv_w, ssm_conv_b, ssm_dt_bias, ssm_a_log, ssm_d, ssm_norm_g,
           w_out, ffn_norm_g, ffn_w_up, ffn_conv_w, ffn_conv_b, ffn_w_down, final_norm_g):
    batch, seq, _ = x.shape
    depth = w_in.shape[0]
    assert seq % max(ATT_QB, SSD_Q, FFN_TM) == 0 and (batch * seq) % PROJ_TM == 0
    x2 = x.reshape(batch * seq, D_MODEL)
    row = lambda v: v.astype(F32).reshape(1, -1)
    for i in range(depth):
        lam_init = 0.8 - 0.6 * math.exp(-0.3 * i)
        lam = (jnp.exp(jnp.sum(lambda_q1[i].astype(F32) * lambda_k1[i].astype(F32)))
               - jnp.exp(jnp.sum(lambda_q2[i].astype(F32) * lambda_k2[i].astype(F32)))
               + lam_init).reshape(1)
        qkv, ssm = _proj_ssd(
            x2, row(attn_norm_g[i]), _w_prep(w_in[i].astype(F32).T),
            _ssd_consts(ssm_conv_w[i], ssm_conv_b[i], ssm_dt_bias[i], ssm_a_log[i], ssm_d[i], ssm_norm_g[i]),
            seq)
        att = _attention(qkv, lam, rel_bias_table, row(attn_subln_g[i]), batch, seq, lam_init)
        assert depth == 1
        x2 = _mix_ffn(x2, att, ssm, w_out[i].astype(BF16), row(ffn_norm_g[i]),
                      ffn_w_up[i].astype(BF16), ffn_conv_w[i].astype(F32), row(ffn_conv_b[i]),
                      ffn_w_down[i].astype(BF16), row(final_norm_g), seq)
    return x2.reshape(batch, seq, D_MODEL)
```

```python
import functools
import math

import jax
import jax.numpy as jnp
from jax import lax
from jax.experimental import pallas as pl
from jax.experimental.pallas import tpu as pltpu

F32 = jnp.float32
BF16 = jnp.bfloat16

D_MODEL = 1024
CHUNK = 64
ATT_HEADS = 8
ATT_HEAD_DIM = 64
ATT_V_DIM = 2 * ATT_HEAD_DIM
ATT_WIDTH = ATT_HEADS * ATT_V_DIM
SSM_HEADS = 16
SSM_HEAD_DIM = 64
SSM_WIDTH = SSM_HEADS * SSM_HEAD_DIM
SSM_GROUPS = 2
SSM_STATE = 128
SSM_CONV = 4
SSM_HEADS_PER_GROUP = SSM_HEADS // SSM_GROUPS
SSM_GROUP_WIDTH = SSM_WIDTH // SSM_GROUPS
FFN_DIM = 2816
FFN_CONV = 3
REL_BUCKETS = 32
REL_MAX_DIST = 128
NORM_EPS = 1e-6
SUBLN_EPS = 1e-5
SSM_NORM_EPS = 1e-5
BC_COLS = 2 * SSM_GROUPS * SSM_STATE
MAIN_COLS = 3 * ATT_WIDTH + SSM_WIDTH + SSM_WIDTH + BC_COLS
DT_COLS = SSM_HEADS
LOG2E = math.log2(math.e)

LANES = 128
SUBLANES = 8
VMEM_LIMIT = 56 * 1024 * 1024

PROJ_TM = 512
PROJ_CN = 512
ATT_QB = 256
ATT_KB = 256
SSD_Q = 256
FFN_TM = 512
FFN_CW = 256
HALO = SUBLANES

def _resident(shape):
    nd = len(shape)
    return pl.BlockSpec(shape, lambda *_: (0,) * nd, pipeline_mode=pl.Buffered(1))


def _sigmoid(x):
    return 1.0 / (1.0 + jnp.exp2(x * -LOG2E))


W_PAD_COLS = MAIN_COLS + LANES


def _wprep_body(w_ref, o_ref):
    q_scale = LOG2E * ATT_HEAD_DIM ** -0.5
    o_ref[0:ATT_WIDTH, :] = (w_ref[0:ATT_WIDTH, :] * q_scale).astype(BF16)
    o_ref[ATT_WIDTH:MAIN_COLS + DT_COLS, :] = w_ref[ATT_WIDTH:MAIN_COLS + DT_COLS, :].astype(BF16)
    o_ref[MAIN_COLS + DT_COLS:W_PAD_COLS, :] = jnp.zeros((LANES - DT_COLS, D_MODEL), BF16)


def _w_prep(w_t):
    whole = lambda shape: pl.BlockSpec(shape, lambda: (0, 0))
    return pl.pallas_call(
        _wprep_body,
        in_specs=[whole((MAIN_COLS + DT_COLS, D_MODEL))],
        out_specs=whole((W_PAD_COLS, D_MODEL)),
        out_shape=jax.ShapeDtypeStruct((W_PAD_COLS, D_MODEL), BF16),
        compiler_params=pltpu.CompilerParams(vmem_limit_bytes=VMEM_LIMIT),
        name="w_prep",
    )(w_t)


FAR_BUCKET = REL_BUCKETS // 2 - 1
BIAS_SPAN = 4 * ATT_KB


def _t5_bucket(rel):
    nb = REL_BUCKETS // 2
    max_exact = nb // 2
    bucket = jnp.where(rel > 0, nb, 0)
    n = jnp.abs(rel)
    nf = jnp.maximum(n, 1).astype(F32)
    large = max_exact + (jnp.log(nf / max_exact) / math.log(REL_MAX_DIST / max_exact)
                         * (nb - max_exact)).astype(jnp.int32)
    large = jnp.minimum(large, nb - 1)
    return bucket + jnp.where(n < max_exact, n, large)


def _bias_bucket_row():
    rel = jnp.arange(BIAS_SPAN, dtype=jnp.int32) - 2 * ATT_KB
    return jnp.broadcast_to(_t5_bucket(rel)[None, :], (SUBLANES, BIAS_SPAN))


def _bias_tiles(tbl_ref, idx_ref, h):
    idx = idx_ref[...]
    base = tbl_ref[FAR_BUCKET, h]
    r = jnp.zeros(idx.shape, F32)
    for b in range(REL_BUCKETS):
        r = jnp.where(idx == b, (tbl_ref[b, h] - base) * LOG2E, r)
    rows = jnp.concatenate([r] * (ATT_QB // SUBLANES), axis=0)
    rolled = pltpu.roll(rows, 0, 1, stride=1, stride_axis=0)
    return rolled[:, 2 * ATT_KB:3 * ATT_KB], rolled[:, ATT_KB:2 * ATT_KB]


ATT_STREAMS = 2


def _attn_body(lam_ref, tbl_ref, q_ref, k_ref, v_ref, idx_ref, subg_ref, o_ref, v1_ref, *, lam_init):
    qb, kb = ATT_QB, ATT_KB
    seq = q_ref.shape[1]
    lane = lax.broadcasted_iota(jnp.int32, (qb, LANES), 1)
    lam = lam_ref[0]

    @functools.cache
    def bias_and_mask():
        row = lax.broadcasted_iota(jnp.int32, (qb, kb), 0)
        col = lax.broadcasted_iota(jnp.int32, (qb, kb), 1)
        allowed = (col // CHUNK) <= (row // CHUNK)
        b_diag, b_prev = _bias_tiles(tbl_ref, idx_ref, pl.program_id(1))
        return tuple(jnp.concatenate([t, t], axis=0) for t in (allowed, b_diag, b_prev))

    @functools.cache
    def build_v1(s):
        v1_ref[s, :, 0:LANES] = v_ref[s]
        v1_ref[s, :, LANES:2 * LANES] = jnp.ones((seq, LANES), BF16)

    def scores(s, i):
        kvl = (i + 1) * kb
        q = q_ref[s, i * qb:(i + 1) * qb, :]
        zero = jnp.zeros_like(q)
        qs = jnp.concatenate([jnp.where(lane < ATT_HEAD_DIM, q, zero),
                              jnp.where(lane >= ATT_HEAD_DIM, q, zero)], axis=0)
        sc = lax.dot_general(qs, k_ref[s, 0:kvl, :], (((1,), (1,)), ((), ())),
                             preferred_element_type=F32)
        allowed, b_diag, b_prev = bias_and_mask()
        blocks = [sc[:, j * kb:(j + 1) * kb] for j in range(i + 1)]
        blocks[i] = jnp.where(allowed, blocks[i] + b_diag, -1e30)
        if i >= 1:
            blocks[i - 1] = blocks[i - 1] + b_prev
        m = jnp.max(functools.reduce(jnp.maximum, blocks), axis=-1, keepdims=True)
        return blocks, m

    def probs(blocks, m):
        return jnp.concatenate([jnp.exp2(blk - m).astype(BF16) for blk in blocks], axis=1)

    def finish(s, i, p):
        build_v1(s)
        kvl = (i + 1) * kb
        acc = jnp.dot(p, v1_ref[s, 0:kvl, :], preferred_element_type=F32)
        o = acc[:, 0:LANES] / acc[:, LANES:2 * LANES]
        o = o[:qb] - lam * o[qb:]
        o = o * lax.rsqrt(jnp.mean(o * o, axis=-1, keepdims=True) + SUBLN_EPS) * subg_ref[...]
        o_ref[s, i * qb:(i + 1) * qb, :] = (o * (1.0 - lam_init)).astype(BF16)

    order = [(s, i) for i in reversed(range(seq // qb)) for s in range(ATT_STREAMS)]
    scored, exped = {}, {}
    for n in range(len(order) + 2 * ATT_STREAMS):
        if n < len(order):
            scored[n] = scores(*order[n])
        n1, n2 = n - ATT_STREAMS, n - 2 * ATT_STREAMS
        if 0 <= n1 < len(order):
            exped[n1] = probs(*scored.pop(n1))
        if 0 <= n2 < len(order):
            finish(*order[n2], exped.pop(n2))


def _attention(qkv, lam, rel_table, sub_g, batch, seq, lam_init):
    assert batch % ATT_STREAMS == 0
    kcol = ATT_WIDTH // LANES
    vcol = 2 * ATT_WIDTH // LANES
    rows = lambda col0: pl.BlockSpec((None, ATT_STREAMS, seq, LANES), lambda b, h: (b, 0, 0, col0 + h))
    out = pl.pallas_call(
        functools.partial(_attn_body, lam_init=lam_init),
        grid=(batch // ATT_STREAMS, ATT_HEADS),
        in_specs=[
            pl.BlockSpec(memory_space=pltpu.SMEM),
            pl.BlockSpec(memory_space=pltpu.SMEM),
            rows(0), rows(kcol), rows(vcol),
            _resident((SUBLANES, BIAS_SPAN)),
            pl.BlockSpec((1, ATT_V_DIM), lambda b, h: (0, 0)),
        ],
        out_specs=rows(0),
        out_shape=jax.ShapeDtypeStruct((batch // ATT_STREAMS, ATT_STREAMS, seq, ATT_WIDTH), BF16),
        scratch_shapes=[pltpu.VMEM((ATT_STREAMS, seq, 2 * LANES), BF16)],
        compiler_params=pltpu.CompilerParams(
            dimension_semantics=("arbitrary", "arbitrary"), vmem_limit_bytes=VMEM_LIMIT),
        name="diff_attn",
    )(lam, rel_table.astype(F32), *[qkv.reshape(batch // ATT_STREAMS, ATT_STREAMS, seq, -1)] * 3,
      _bias_bucket_row(), sub_g)
    return out.reshape(batch * seq, ATT_WIDTH)


def _split3(x):
    x1 = x.astype(BF16)
    r1 = x - x1.astype(F32)
    x2 = r1.astype(BF16)
    x3 = (r1 - x2.astype(F32)).astype(BF16)
    return x1, x2, x3


def _ssd_steps(z_ref, xs_ref, bc_ref, dt_ref, cwx_ref, cbx_ref, cwbc_ref, cbbc_ref, dtb_ref,
               alog_ref, dexp_ref, ng_ref, e3_ref, tril_ref, o_ref, halox_ref, halobc_ref,
               state_ref):
    q = SSD_Q

    def conv_silu(raw_ref, halo_ref, w_ref, b_ref):
        x = raw_ref[...].astype(F32)
        first_rows = lax.broadcasted_iota(jnp.int32, (HALO, x.shape[1]), 0)
        acc = b_ref[...] + w_ref[SSM_CONV - 1:SSM_CONV, :] * x
        for j in range(1, SSM_CONV):
            wj = w_ref[SSM_CONV - 1 - j:SSM_CONV - j, :]
            rolled = pltpu.roll(x, j, 0)
            head = jnp.where(first_rows < j, halo_ref[HALO - j:2 * HALO - j, :], rolled[0:HALO])
            acc = acc + wj * jnp.concatenate([head, rolled[HALO:]], axis=0)
        halo_ref[0:HALO, :] = x[q - HALO:q]
        return acc * _sigmoid(acc)

    xs = conv_silu(xs_ref, halox_ref, cwx_ref, cbx_ref)
    yield
    bc = conv_silu(bc_ref, halobc_ref, cwbc_ref, cbbc_ref)
    yield

    dtr = dt_ref[...] + dtb_ref[...]
    dt = jnp.maximum(dtr, 0.0) + jnp.log(1.0 + jnp.exp(-jnp.abs(dtr)))
    a = (-LOG2E * jnp.exp(alog_ref[...])) * dt

    row = lax.broadcasted_iota(jnp.int32, (q, q), 0)
    col = lax.broadcasted_iota(jnp.int32, (q, q), 1)
    causal = col <= row
    tril = tril_ref[...]
    acum = sum(jnp.dot(tril, t, preferred_element_type=F32) for t in _split3(a))

    lane = lax.broadcasted_iota(jnp.int32, (q, LANES), 1)

    def expand(x):
        parts = [jnp.where(lane < SSM_HEADS, t.astype(F32), 0.0) for t in _split3(x)]
        packed = parts[0] + pltpu.roll(parts[1], SSM_HEADS, 1) + pltpu.roll(parts[2], 2 * SSM_HEADS, 1)
        return jnp.dot(packed.astype(BF16), e3_ref[...], preferred_element_type=F32)

    dt_exp = expand(dt)
    acum_exp = expand(acum)
    ea_exp = jnp.exp2(acum_exp)
    last_exp = acum_exp[q - 1:q, :]
    dte_exp = jnp.exp2(last_exp - acum_exp)
    ea_last = ea_exp[q - 1:q, :]

    xdt = xs * dt_exp
    xdt_b = xdt.astype(BF16)
    acum_t = acum.T
    pair_lane = lax.broadcasted_iota(jnp.int32, (q, LANES), 1)
    yield

    for g in range(SSM_GROUPS):
        gsl = slice(g * SSM_GROUP_WIDTH, (g + 1) * SSM_GROUP_WIDTH)
        bg = bc[:, g * SSM_STATE:(g + 1) * SSM_STATE]
        cg = bc[:, (SSM_GROUPS + g) * SSM_STATE:(SSM_GROUPS + g + 1) * SSM_STATE]
        bg_b = bg.astype(BF16)
        cg_b = cg.astype(BF16)
        cb = lax.dot_general(cg_b, bg_b, (((1,), (1,)), ((), ())), preferred_element_type=F32)

        def masked(h):
            seg = acum[:, h:h + 1] - acum_t[h:h + 1, :]
            return (cb * jnp.exp2(jnp.where(causal, seg, -jnp.inf))).astype(BF16)

        pairs = []
        for j in range(SSM_HEADS_PER_GROUP // 2):
            h0 = g * SSM_HEADS_PER_GROUP + 2 * j
            xp = xdt_b[:, h0 * SSM_HEAD_DIM:(h0 + 2) * SSM_HEAD_DIM]
            r0 = jnp.dot(masked(h0), xp, preferred_element_type=F32)
            r1 = jnp.dot(masked(h0 + 1), xp, preferred_element_type=F32)
            pairs.append(jnp.where(pair_lane < SSM_HEAD_DIM, r0, r1))
            if j % 2 == 1:
                yield
        y_diag = jnp.concatenate(pairs, axis=1)

        st = state_ref[g]
        y_off = jnp.dot(cg_b, st.astype(BF16), preferred_element_type=F32) * ea_exp[:, gsl]
        w = (xdt[:, gsl] * dte_exp[:, gsl]).astype(BF16)
        s_new = jnp.dot(bg.T.astype(BF16), w, preferred_element_type=F32)
        state_ref[g] = st * ea_last[:, gsl] + s_new

        y = y_diag + y_off + xs[:, gsl] * dexp_ref[:, gsl]
        zf = z_ref[:, gsl].astype(F32)
        gated = y * (zf * _sigmoid(zf))
        gated = gated * lax.rsqrt(jnp.mean(gated * gated, axis=-1, keepdims=True) + SSM_NORM_EPS)
        o_ref[:, gsl] = (gated * ng_ref[:, gsl]).astype(BF16)
        yield


def _ssd_consts(conv_w, conv_b, dt_bias, a_log, d_skip, norm_g):
    pad = LANES - SSM_HEADS
    dtb = jnp.pad(dt_bias.astype(F32), (0, pad)).reshape(1, LANES)
    alog = jnp.pad(a_log.astype(F32), (0, pad)).reshape(1, LANES)
    dexp = jnp.repeat(d_skip.astype(F32), SSM_HEAD_DIM).reshape(1, SSM_WIDTH)
    r = jnp.arange(LANES)[:, None]
    hcol = (jnp.arange(SSM_WIDTH) // SSM_HEAD_DIM)[None, :]
    e3 = jnp.where((r % SSM_HEADS == hcol) & (r < 3 * SSM_HEADS), 1.0, 0.0).astype(BF16)
    cw = conv_w.astype(F32)
    cb = conv_b.astype(F32).reshape(1, -1)
    tq = jnp.arange(SSD_Q)
    tril = jnp.where(tq[:, None] >= tq[None, :], 1.0, 0.0).astype(BF16)
    in_specs = [
        _resident((SSM_CONV, SSM_WIDTH)),
        _resident((1, SSM_WIDTH)),
        _resident((SSM_CONV, BC_COLS)),
        _resident((1, BC_COLS)),
        _resident((1, LANES)),
        _resident((1, LANES)),
        _resident((1, SSM_WIDTH)),
        _resident((1, SSM_WIDTH)),
        _resident((LANES, SSM_WIDTH)),
        _resident((SSD_Q, SSD_Q)),
    ]
    operands = (cw[:, :SSM_WIDTH], cb[:, :SSM_WIDTH], cw[:, SSM_WIDTH:], cb[:, SSM_WIDTH:], dtb, alog,
                dexp, norm_g.astype(F32).reshape(1, -1), e3, tril)
    return in_specs, operands


SSD_COLS = 2 * SSM_WIDTH + BC_COLS
SCAN_PIECES_AFTER_QKV_DOT = (2, 2, 2, 3, 3, 2)


def _proj_ssd_body(x_ref, g_ref, w_ref, wdt_ref, *rest, tiles_per_seq):
    consts, (qkv_ref, ssm_ref, zxbc_ref, dt_ref, halox_ref, halobc_ref, state_ref) = rest[:-7], rest[-7:]

    @pl.when(pl.program_id(0) % tiles_per_seq == 0)
    def _():
        halox_ref[...] = jnp.zeros(halox_ref.shape, F32)
        halobc_ref[...] = jnp.zeros(halobc_ref.shape, F32)
        state_ref[...] = jnp.zeros(state_ref.shape, F32)

    x = x_ref[...]
    h = x * lax.rsqrt(jnp.mean(x * x, axis=-1, keepdims=True) + NORM_EPS) * g_ref[...]
    h = h.astype(BF16)
    nt = (((1,), (1,)), ((), ()))

    qkv_cols = 3 * ATT_WIDTH

    def project(col):
        y = lax.dot_general(h, w_ref[col:col + PROJ_CN, :], nt, preferred_element_type=F32).astype(BF16)
        if col < qkv_cols:
            qkv_ref[:, col:col + PROJ_CN] = y
        else:
            zxbc_ref[:, col - qkv_cols:col - qkv_cols + PROJ_CN] = y

    def project_dt(_):
        dt_ref[...] = lax.dot_general(h, wdt_ref[...], nt, preferred_element_type=F32)

    def scan_steps():
        for c in range(PROJ_TM // SSD_Q):
            rows = pl.ds(c * SSD_Q, SSD_Q)
            yield from _ssd_steps(
                zxbc_ref.at[rows, pl.ds(0, SSM_WIDTH)],
                zxbc_ref.at[rows, pl.ds(SSM_WIDTH, SSM_WIDTH)],
                zxbc_ref.at[rows, pl.ds(2 * SSM_WIDTH, BC_COLS)],
                dt_ref.at[rows, :],
                *consts, ssm_ref.at[rows, :], halox_ref, halobc_ref, state_ref)

    z0, x0, bc0 = qkv_cols, qkv_cols + SSM_WIDTH, qkv_cols + 2 * SSM_WIDTH
    plan = [(project, x0, 0), (project, x0 + PROJ_CN, 1), (project, bc0, 0), (project_dt, None, 1),
            (project, z0, 1), (project, z0 + PROJ_CN, 1)]
    plan += [(project, col, pieces) for col, pieces in
             zip(range(0, qkv_cols, PROJ_CN), SCAN_PIECES_AFTER_QKV_DOT)]
    scan = scan_steps()
    for emit, col, pieces in plan:
        emit(col)
        for _ in range(pieces):
            next(scan, None)
    for _ in scan:
        pass


def _proj_ssd(x2, g, w_pad, ssd_consts, seq):
    t = x2.shape[0]
    c_specs, c_ops = ssd_consts
    tok = lambda width: pl.BlockSpec((PROJ_TM, width), lambda i: (i, 0))
    return pl.pallas_call(
        functools.partial(_proj_ssd_body, tiles_per_seq=seq // PROJ_TM),
        grid=(t // PROJ_TM,),
        in_specs=[
            tok(D_MODEL),
            _resident((1, D_MODEL)),
            _resident((MAIN_COLS, D_MODEL)),
            pl.BlockSpec((LANES, D_MODEL), lambda i: (MAIN_COLS // LANES, 0), pipeline_mode=pl.Buffered(1)),
        ] + c_specs,
        out_specs=[tok(3 * ATT_WIDTH), tok(SSM_WIDTH)],
        out_shape=[
            jax.ShapeDtypeStruct((t, 3 * ATT_WIDTH), BF16),
            jax.ShapeDtypeStruct((t, SSM_WIDTH), BF16),
        ],
        scratch_shapes=[
            pltpu.VMEM((PROJ_TM, SSD_COLS), BF16),
            pltpu.VMEM((PROJ_TM, LANES), F32),
            pltpu.VMEM((2 * HALO, SSM_WIDTH), F32),
            pltpu.VMEM((2 * HALO, BC_COLS), F32),
            pltpu.VMEM((SSM_GROUPS, SSM_STATE, SSM_GROUP_WIDTH), F32),
        ],
        compiler_params=pltpu.CompilerParams(
            dimension_semantics=("arbitrary",), vmem_limit_bytes=VMEM_LIMIT),
        name="proj_ssd",
    )(x2, g, w_pad, w_pad, *c_ops)


def _ffn_body(x_ref, att_ref, ssm_ref, wo_ref, g2_ref, wup_ref, cw_ref, cb_ref, wdn_ref, gf_ref,
              o_ref, xc_ref, x1p_ref, h2_ref, oc_ref, carry_ref, *, tiles_per_seq):
    i = pl.program_id(0)
    tm, cw = FFN_TM, FFN_CW
    ph = tm // SUBLANES
    n_lane_blocks = D_MODEL // LANES

    @pl.when(i % tiles_per_seq == 0)
    def _():
        carry_ref[...] = jnp.zeros(carry_ref.shape, F32)

    x1 = (x_ref[...]
          + jnp.dot(att_ref[...], wo_ref[0:ATT_WIDTH, :], preferred_element_type=F32)
          + jnp.dot(ssm_ref[...], wo_ref[ATT_WIDTH:, :], preferred_element_type=F32))
    for c in range(n_lane_blocks):
        xc_ref[c] = x1[:, c * LANES:(c + 1) * LANES]
    x1p = jnp.concatenate(
        [jnp.concatenate([xc_ref[c, pl.ds(k, ph, stride=SUBLANES), :] for k in range(SUBLANES)], axis=0)
         for c in range(n_lane_blocks)], axis=1)
    x1p_ref[...] = x1p
    h2_ref[...] = (x1p * lax.rsqrt(jnp.mean(x1p * x1p, axis=-1, keepdims=True) + NORM_EPS)
                   * g2_ref[...]).astype(BF16)

    first_row = lax.broadcasted_iota(jnp.int32, (ph, cw), 0) == 0

    def prev_token(block, carry_slot):
        tail = carry_ref[carry_slot]
        carry_ref[carry_slot] = block[ph - HALO:ph]
        return jnp.where(first_row, jnp.broadcast_to(tail[HALO - 1:HALO, :], (ph, cw)),
                         pltpu.roll(block, 1, 0))

    def up_conv(slot, col):
        u = jnp.dot(h2_ref[...], wup_ref[:, col:col + cw], preferred_element_type=F32)
        blocks = [u[k * ph:(k + 1) * ph] for k in range(SUBLANES)]
        back1 = prev_token(blocks[SUBLANES - 1], 2 * slot)
        back2 = prev_token(blocks[SUBLANES - 2], 2 * slot + 1)
        hist = [back2, back1] + blocks
        w = [cw_ref[t:t + 1, col:col + cw] for t in range(FFN_CONV)]
        b = cb_ref[:, col:col + cw]
        return jnp.concatenate(
            [b + w[2] * hist[k + 2] + w[1] * hist[k + 1] + w[0] * hist[k] for k in range(SUBLANES)],
            axis=0)

    acts = []
    for j in range(FFN_DIM // cw):
        gate = up_conv(2 * j, j * cw)
        val = up_conv(2 * j + 1, FFN_DIM + j * cw)
        acts.append((gate * _sigmoid(gate) * val).astype(BF16))

    x2 = x1p_ref[...] + jnp.dot(jnp.concatenate(acts, axis=1), wdn_ref[...], preferred_element_type=F32)
    out = x2 * lax.rsqrt(jnp.mean(x2 * x2, axis=-1, keepdims=True) + NORM_EPS) * gf_ref[...]
    for c in range(n_lane_blocks):
        for k in range(SUBLANES):
            oc_ref[c, pl.ds(k, ph, stride=SUBLANES), :] = out[k * ph:(k + 1) * ph, c * LANES:(c + 1) * LANES]
    for c in range(n_lane_blocks):
        o_ref[:, c * LANES:(c + 1) * LANES] = oc_ref[c]


def _mix_ffn(x2, att, ssm, w_out, g2, w_up, conv_w, conv_b, w_down, gf, seq):
    t = x2.shape[0]
    n_carry = 2 * (FFN_CONV - 1) * (FFN_DIM // FFN_CW)
    tok = lambda width: pl.BlockSpec((FFN_TM, width), lambda i: (i, 0))
    return pl.pallas_call(
        functools.partial(_ffn_body, tiles_per_seq=seq // FFN_TM),
        grid=(t // FFN_TM,),
        in_specs=[
            tok(D_MODEL), tok(ATT_WIDTH), tok(SSM_WIDTH),
            _resident((ATT_WIDTH + SSM_WIDTH, D_MODEL)),
            _resident((1, D_MODEL)),
            _resident((D_MODEL, 2 * FFN_DIM)),
            _resident((FFN_CONV, 2 * FFN_DIM)),
            _resident((1, 2 * FFN_DIM)),
            _resident((FFN_DIM, D_MODEL)),
            _resident((1, D_MODEL)),
        ],
        out_specs=tok(D_MODEL),
        out_shape=jax.ShapeDtypeStruct((t, D_MODEL), F32),
        scratch_shapes=[
            pltpu.VMEM((D_MODEL // LANES, FFN_TM, LANES), F32),
            pltpu.VMEM((FFN_TM, D_MODEL), F32),
            pltpu.VMEM((FFN_TM, D_MODEL), BF16),
            pltpu.VMEM((D_MODEL // LANES, FFN_TM, LANES), F32),
            pltpu.VMEM((n_carry, HALO, FFN_CW), F32),
        ],
        compiler_params=pltpu.CompilerParams(
            dimension_semantics=("arbitrary",), vmem_limit_bytes=VMEM_LIMIT),
        name="mix_ffn",
    )(x2, att, ssm, w_out, g2, w_up, conv_w, conv_b, w_down, gf)


def kernel(x, rel_bias_table, attn_norm_g, w_in, lambda_q1, lambda_k1, lambda_q2, lambda_k2,
           attn_subln_g, ssm_conv_w, ssm_conv_b, ssm_dt_bias, ssm_a_log, ssm_d, ssm_norm_g,
           w_out, ffn_norm_g, ffn_w_up, ffn_conv_w, ffn_conv_b, ffn_w_down, final_norm_g):
    batch, seq, _ = x.shape
    depth = w_in.shape[0]
    assert seq % max(ATT_QB, SSD_Q, FFN_TM) == 0 and (batch * seq) % PROJ_TM == 0
    x2 = x.reshape(batch * seq, D_MODEL)
    row = lambda v: v.astype(F32).reshape(1, -1)
    for i in range(depth):
        lam_init = 0.8 - 0.6 * math.exp(-0.3 * i)
        lam = (jnp.exp(jnp.sum(lambda_q1[i].astype(F32) * lambda_k1[i].astype(F32)))
               - jnp.exp(jnp.sum(lambda_q2[i].astype(F32) * lambda_k2[i].astype(F32)))
               + lam_init).reshape(1)
        qkv, ssm = _proj_ssd(
            x2, row(attn_norm_g[i]), _w_prep(w_in[i].astype(F32).T),
            _ssd_consts(ssm_conv_w[i], ssm_conv_b[i], ssm_dt_bias[i], ssm_a_log[i], ssm_d[i], ssm_norm_g[i]),
            seq)
        att = _attention(qkv, lam, rel_bias_table, row(attn_subln_g[i]), batch, seq, lam_init)
        assert depth == 1
        x2 = _mix_ffn(x2, att, ssm, w_out[i].astype(BF16), row(ffn_norm_g[i]),
                      ffn_w_up[i].astype(BF16), ffn_conv_w[i].astype(F32), row(ffn_conv_b[i]),
                      ffn_w_down[i].astype(BF16), row(final_norm_g), seq)
    return x2.reshape(batch, seq, D_MODEL)
```

```python
import functools
import math

import jax
import jax.numpy as jnp
from jax import lax
from jax.experimental import pallas as pl
from jax.experimental.pallas import tpu as pltpu

F32 = jnp.float32
BF16 = jnp.bfloat16

D_MODEL = 1024
CHUNK = 64
ATT_HEADS = 8
ATT_HEAD_DIM = 64
ATT_V_DIM = 2 * ATT_HEAD_DIM
ATT_WIDTH = ATT_HEADS * ATT_V_DIM
SSM_HEADS = 16
SSM_HEAD_DIM = 64
SSM_WIDTH = SSM_HEADS * SSM_HEAD_DIM
SSM_GROUPS = 2
SSM_STATE = 128
SSM_CONV = 4
SSM_HEADS_PER_GROUP = SSM_HEADS // SSM_GROUPS
SSM_GROUP_WIDTH = SSM_WIDTH // SSM_GROUPS
FFN_DIM = 2816
FFN_CONV = 3
REL_BUCKETS = 32
REL_MAX_DIST = 128
NORM_EPS = 1e-6
SUBLN_EPS = 1e-5
SSM_NORM_EPS = 1e-5
BC_COLS = 2 * SSM_GROUPS * SSM_STATE
MAIN_COLS = 3 * ATT_WIDTH + SSM_WIDTH + SSM_WIDTH + BC_COLS
DT_COLS = SSM_HEADS
LOG2E = math.log2(math.e)

LANES = 128
SUBLANES = 8
VMEM_LIMIT = 56 * 1024 * 1024

PROJ_TM = 512
PROJ_CN = 512
ATT_QB = 256
ATT_KB = 256
SSD_Q = 256
FFN_TM = 512
FFN_SUB = 256
FFN_CW = 256
HALO = SUBLANES

def _resident(shape):
    nd = len(shape)
    return pl.BlockSpec(shape, lambda *_: (0,) * nd, pipeline_mode=pl.Buffered(1))


def _sigmoid(x):
    return 1.0 / (1.0 + jnp.exp2(x * -LOG2E))


W_PAD_COLS = MAIN_COLS + LANES


def _wprep_body(w_ref, o_ref):
    q_scale = LOG2E * ATT_HEAD_DIM ** -0.5
    o_ref[0:ATT_WIDTH, :] = (w_ref[0:ATT_WIDTH, :] * q_scale).astype(BF16)
    o_ref[ATT_WIDTH:MAIN_COLS + DT_COLS, :] = w_ref[ATT_WIDTH:MAIN_COLS + DT_COLS, :].astype(BF16)
    o_ref[MAIN_COLS + DT_COLS:W_PAD_COLS, :] = jnp.zeros((LANES - DT_COLS, D_MODEL), BF16)


def _w_prep(w_t):
    whole = lambda shape: pl.BlockSpec(shape, lambda: (0, 0))
    return pl.pallas_call(
        _wprep_body,
        in_specs=[whole((MAIN_COLS + DT_COLS, D_MODEL))],
        out_specs=whole((W_PAD_COLS, D_MODEL)),
        out_shape=jax.ShapeDtypeStruct((W_PAD_COLS, D_MODEL), BF16),
        compiler_params=pltpu.CompilerParams(vmem_limit_bytes=VMEM_LIMIT),
        name="w_prep",
    )(w_t)


FAR_BUCKET = REL_BUCKETS // 2 - 1
BIAS_SPAN = 4 * ATT_KB


def _t5_bucket(rel):
    nb = REL_BUCKETS // 2
    max_exact = nb // 2
    bucket = jnp.where(rel > 0, nb, 0)
    n = jnp.abs(rel)
    nf = jnp.maximum(n, 1).astype(F32)
    large = max_exact + (jnp.log(nf / max_exact) / math.log(REL_MAX_DIST / max_exact)
                         * (nb - max_exact)).astype(jnp.int32)
    large = jnp.minimum(large, nb - 1)
    return bucket + jnp.where(n < max_exact, n, large)


def _bias_bucket_row():
    rel = jnp.arange(BIAS_SPAN, dtype=jnp.int32) - 2 * ATT_KB
    return jnp.broadcast_to(_t5_bucket(rel)[None, :], (SUBLANES, BIAS_SPAN))


def _bias_tiles(tbl_ref, idx_ref, h):
    idx = idx_ref[...]
    base = tbl_ref[FAR_BUCKET, h]
    r = jnp.zeros(idx.shape, F32)
    for b in range(REL_BUCKETS):
        r = jnp.where(idx == b, (tbl_ref[b, h] - base) * LOG2E, r)
    rows = jnp.concatenate([r] * (ATT_QB // SUBLANES), axis=0)
    rolled = pltpu.roll(rows, 0, 1, stride=1, stride_axis=0)
    return rolled[:, 2 * ATT_KB:3 * ATT_KB], rolled[:, ATT_KB:2 * ATT_KB]


ATT_STREAMS = 2


def _attn_body(lam_ref, tbl_ref, q_ref, k_ref, v_ref, idx_ref, subg_ref, o_ref, v1_ref, *, lam_init):
    qb, kb = ATT_QB, ATT_KB
    seq = q_ref.shape[1]
    lane = lax.broadcasted_iota(jnp.int32, (qb, LANES), 1)
    lam = lam_ref[0]

    @functools.cache
    def bias_and_mask():
        row = lax.broadcasted_iota(jnp.int32, (qb, kb), 0)
        col = lax.broadcasted_iota(jnp.int32, (qb, kb), 1)
        allowed = (col // CHUNK) <= (row // CHUNK)
        b_diag, b_prev = _bias_tiles(tbl_ref, idx_ref, pl.program_id(1))
        return tuple(jnp.concatenate([t, t], axis=0) for t in (allowed, b_diag, b_prev))

    @functools.cache
    def build_v1(s):
        v1_ref[s, :, 0:LANES] = v_ref[s]
        v1_ref[s, :, LANES:2 * LANES] = jnp.ones((seq, LANES), BF16)

    def scores(s, i):
        kvl = (i + 1) * kb
        q = q_ref[s, i * qb:(i + 1) * qb, :]
        zero = jnp.zeros_like(q)
        qs = jnp.concatenate([jnp.where(lane < ATT_HEAD_DIM, q, zero),
                              jnp.where(lane >= ATT_HEAD_DIM, q, zero)], axis=0)
        sc = lax.dot_general(qs, k_ref[s, 0:kvl, :], (((1,), (1,)), ((), ())),
                             preferred_element_type=F32)
        allowed, b_diag, b_prev = bias_and_mask()
        blocks = [sc[:, j * kb:(j + 1) * kb] for j in range(i + 1)]
        blocks[i] = jnp.where(allowed, blocks[i] + b_diag, -1e30)
        if i >= 1:
            blocks[i - 1] = blocks[i - 1] + b_prev
        m = jnp.max(functools.reduce(jnp.maximum, blocks), axis=-1, keepdims=True)
        return blocks, m

    def probs(blocks, m):
        return jnp.concatenate([jnp.exp2(blk - m).astype(BF16) for blk in blocks], axis=1)

    def finish(s, i, p):
        build_v1(s)
        kvl = (i + 1) * kb
        acc = jnp.dot(p, v1_ref[s, 0:kvl, :], preferred_element_type=F32)
        o = acc[:, 0:LANES] / acc[:, LANES:2 * LANES]
        o = o[:qb] - lam * o[qb:]
        o = o * lax.rsqrt(jnp.mean(o * o, axis=-1, keepdims=True) + SUBLN_EPS) * subg_ref[...]
        o_ref[s, i * qb:(i + 1) * qb, :] = (o * (1.0 - lam_init)).astype(BF16)

    order = [(s, i) for i in reversed(range(seq // qb)) for s in range(ATT_STREAMS)]
    scored, exped = {}, {}
    for n in range(len(order) + 2 * ATT_STREAMS):
        if n < len(order):
            scored[n] = scores(*order[n])
        n1, n2 = n - ATT_STREAMS, n - 2 * ATT_STREAMS
        if 0 <= n1 < len(order):
            exped[n1] = probs(*scored.pop(n1))
        if 0 <= n2 < len(order):
            finish(*order[n2], exped.pop(n2))


def _attention(qkv, lam, rel_table, sub_g, batch, seq, lam_init):
    assert batch % ATT_STREAMS == 0
    kcol = ATT_WIDTH // LANES
    vcol = 2 * ATT_WIDTH // LANES
    rows = lambda col0: pl.BlockSpec((None, ATT_STREAMS, seq, LANES), lambda b, h: (b, 0, 0, col0 + h))
    out = pl.pallas_call(
        functools.partial(_attn_body, lam_init=lam_init),
        grid=(batch // ATT_STREAMS, ATT_HEADS),
        in_specs=[
            pl.BlockSpec(memory_space=pltpu.SMEM),
            pl.BlockSpec(memory_space=pltpu.SMEM),
            rows(0), rows(kcol), rows(vcol),
            _resident((SUBLANES, BIAS_SPAN)),
            pl.BlockSpec((1, ATT_V_DIM), lambda b, h: (0, 0)),
        ],
        out_specs=rows(0),
        out_shape=jax.ShapeDtypeStruct((batch // ATT_STREAMS, ATT_STREAMS, seq, ATT_WIDTH), BF16),
        scratch_shapes=[pltpu.VMEM((ATT_STREAMS, seq, 2 * LANES), BF16)],
        compiler_params=pltpu.CompilerParams(
            dimension_semantics=("arbitrary", "arbitrary"), vmem_limit_bytes=VMEM_LIMIT),
        name="diff_attn",
    )(lam, rel_table.astype(F32), *[qkv.reshape(batch // ATT_STREAMS, ATT_STREAMS, seq, -1)] * 3,
      _bias_bucket_row(), sub_g)
    return out.reshape(batch * seq, ATT_WIDTH)


def _split3(x):
    x1 = x.astype(BF16)
    r1 = x - x1.astype(F32)
    x2 = r1.astype(BF16)
    x3 = (r1 - x2.astype(F32)).astype(BF16)
    return x1, x2, x3


def _ssd_steps(z_ref, xs_ref, bc_ref, dt_ref, cwx_ref, cbx_ref, cwbc_ref, cbbc_ref, dtb_ref,
               alog_ref, dexp_ref, ng_ref, e3_ref, tril_ref, o_ref, halox_ref, halobc_ref,
               state_ref):
    q = SSD_Q

    def conv_silu(raw_ref, halo_ref, w_ref, b_ref):
        x = raw_ref[...].astype(F32)
        first_rows = lax.broadcasted_iota(jnp.int32, (HALO, x.shape[1]), 0)
        acc = b_ref[...] + w_ref[SSM_CONV - 1:SSM_CONV, :] * x
        for j in range(1, SSM_CONV):
            wj = w_ref[SSM_CONV - 1 - j:SSM_CONV - j, :]
            rolled = pltpu.roll(x, j, 0)
            head = jnp.where(first_rows < j, halo_ref[HALO - j:2 * HALO - j, :], rolled[0:HALO])
            acc = acc + wj * jnp.concatenate([head, rolled[HALO:]], axis=0)
        halo_ref[0:HALO, :] = x[q - HALO:q]
        return acc * _sigmoid(acc)

    xs = conv_silu(xs_ref, halox_ref, cwx_ref, cbx_ref)
    yield
    bc = conv_silu(bc_ref, halobc_ref, cwbc_ref, cbbc_ref)
    yield

    dtr = dt_ref[...] + dtb_ref[...]
    dt = jnp.maximum(dtr, 0.0) + jnp.log(1.0 + jnp.exp(-jnp.abs(dtr)))
    a = (-LOG2E * jnp.exp(alog_ref[...])) * dt

    row = lax.broadcasted_iota(jnp.int32, (q, q), 0)
    col = lax.broadcasted_iota(jnp.int32, (q, q), 1)
    causal = col <= row
    tril = tril_ref[...]
    acum = sum(jnp.dot(tril, t, preferred_element_type=F32) for t in _split3(a))

    lane = lax.broadcasted_iota(jnp.int32, (q, LANES), 1)

    def expand(x):
        parts = [jnp.where(lane < SSM_HEADS, t.astype(F32), 0.0) for t in _split3(x)]
        packed = parts[0] + pltpu.roll(parts[1], SSM_HEADS, 1) + pltpu.roll(parts[2], 2 * SSM_HEADS, 1)
        return jnp.dot(packed.astype(BF16), e3_ref[...], preferred_element_type=F32)

    dt_exp = expand(dt)
    acum_exp = expand(acum)
    ea_exp = jnp.exp2(acum_exp)
    last_exp = acum_exp[q - 1:q, :]
    dte_exp = jnp.exp2(last_exp - acum_exp)
    ea_last = ea_exp[q - 1:q, :]

    xdt = xs * dt_exp
    xdt_b = xdt.astype(BF16)
    acum_t = acum.T
    pair_lane = lax.broadcasted_iota(jnp.int32, (q, LANES), 1)
    yield

    for g in range(SSM_GROUPS):
        gsl = slice(g * SSM_GROUP_WIDTH, (g + 1) * SSM_GROUP_WIDTH)
        bg = bc[:, g * SSM_STATE:(g + 1) * SSM_STATE]
        cg = bc[:, (SSM_GROUPS + g) * SSM_STATE:(SSM_GROUPS + g + 1) * SSM_STATE]
        bg_b = bg.astype(BF16)
        cg_b = cg.astype(BF16)
        cb = lax.dot_general(cg_b, bg_b, (((1,), (1,)), ((), ())), preferred_element_type=F32)

        def masked(h):
            seg = acum[:, h:h + 1] - acum_t[h:h + 1, :]
            return (cb * jnp.exp2(jnp.where(causal, seg, -jnp.inf))).astype(BF16)

        pairs = []
        for j in range(SSM_HEADS_PER_GROUP // 2):
            h0 = g * SSM_HEADS_PER_GROUP + 2 * j
            xp = xdt_b[:, h0 * SSM_HEAD_DIM:(h0 + 2) * SSM_HEAD_DIM]
            r0 = jnp.dot(masked(h0), xp, preferred_element_type=F32)
            r1 = jnp.dot(masked(h0 + 1), xp, preferred_element_type=F32)
            pairs.append(jnp.where(pair_lane < SSM_HEAD_DIM, r0, r1))
            if j % 2 == 1:
                yield
        y_diag = jnp.concatenate(pairs, axis=1)

        st = state_ref[g]
        y_off = jnp.dot(cg_b, st.astype(BF16), preferred_element_type=F32) * ea_exp[:, gsl]
        w = (xdt[:, gsl] * dte_exp[:, gsl]).astype(BF16)
        s_new = jnp.dot(bg.T.astype(BF16), w, preferred_element_type=F32)
        state_ref[g] = st * ea_last[:, gsl] + s_new

        y = y_diag + y_off + xs[:, gsl] * dexp_ref[:, gsl]
        zf = z_ref[:, gsl].astype(F32)
        gated = y * (zf * _sigmoid(zf))
        gated = gated * lax.rsqrt(jnp.mean(gated * gated, axis=-1, keepdims=True) + SSM_NORM_EPS)
        o_ref[:, gsl] = (gated * ng_ref[:, gsl]).astype(BF16)
        yield


def _ssd_consts(conv_w, conv_b, dt_bias, a_log, d_skip, norm_g):
    pad = LANES - SSM_HEADS
    dtb = jnp.pad(dt_bias.astype(F32), (0, pad)).reshape(1, LANES)
    alog = jnp.pad(a_log.astype(F32), (0, pad)).reshape(1, LANES)
    dexp = jnp.repeat(d_skip.astype(F32), SSM_HEAD_DIM).reshape(1, SSM_WIDTH)
    r = jnp.arange(LANES)[:, None]
    hcol = (jnp.arange(SSM_WIDTH) // SSM_HEAD_DIM)[None, :]
    e3 = jnp.where((r % SSM_HEADS == hcol) & (r < 3 * SSM_HEADS), 1.0, 0.0).astype(BF16)
    cw = conv_w.astype(F32)
    cb = conv_b.astype(F32).reshape(1, -1)
    tq = jnp.arange(SSD_Q)
    tril = jnp.where(tq[:, None] >= tq[None, :], 1.0, 0.0).astype(BF16)
    in_specs = [
        _resident((SSM_CONV, SSM_WIDTH)),
        _resident((1, SSM_WIDTH)),
        _resident((SSM_CONV, BC_COLS)),
        _resident((1, BC_COLS)),
        _resident((1, LANES)),
        _resident((1, LANES)),
        _resident((1, SSM_WIDTH)),
        _resident((1, SSM_WIDTH)),
        _resident((LANES, SSM_WIDTH)),
        _resident((SSD_Q, SSD_Q)),
    ]
    operands = (cw[:, :SSM_WIDTH], cb[:, :SSM_WIDTH], cw[:, SSM_WIDTH:], cb[:, SSM_WIDTH:], dtb, alog,
                dexp, norm_g.astype(F32).reshape(1, -1), e3, tril)
    return in_specs, operands


SSD_COLS = 2 * SSM_WIDTH + BC_COLS
SCAN_PIECES_AFTER_QKV_DOT = (2, 2, 2, 3, 3, 2)


def _proj_ssd_body(x_ref, g_ref, w_ref, wdt_ref, *rest, tiles_per_seq):
    consts, (qkv_ref, ssm_ref, zxbc_ref, dt_ref, halox_ref, halobc_ref, state_ref) = rest[:-7], rest[-7:]

    @pl.when(pl.program_id(0) % tiles_per_seq == 0)
    def _():
        halox_ref[...] = jnp.zeros(halox_ref.shape, F32)
        halobc_ref[...] = jnp.zeros(halobc_ref.shape, F32)
        state_ref[...] = jnp.zeros(state_ref.shape, F32)

    x = x_ref[...]
    h = x * lax.rsqrt(jnp.mean(x * x, axis=-1, keepdims=True) + NORM_EPS) * g_ref[...]
    h = h.astype(BF16)
    nt = (((1,), (1,)), ((), ()))

    qkv_cols = 3 * ATT_WIDTH

    def project(col):
        y = lax.dot_general(h, w_ref[col:col + PROJ_CN, :], nt, preferred_element_type=F32).astype(BF16)
        if col < qkv_cols:
            qkv_ref[:, col:col + PROJ_CN] = y
        else:
            zxbc_ref[:, col - qkv_cols:col - qkv_cols + PROJ_CN] = y

    def project_dt(_):
        dt_ref[...] = lax.dot_general(h, wdt_ref[...], nt, preferred_element_type=F32)

    def scan_steps():
        for c in range(PROJ_TM // SSD_Q):
            rows = pl.ds(c * SSD_Q, SSD_Q)
            yield from _ssd_steps(
                zxbc_ref.at[rows, pl.ds(0, SSM_WIDTH)],
                zxbc_ref.at[rows, pl.ds(SSM_WIDTH, SSM_WIDTH)],
                zxbc_ref.at[rows, pl.ds(2 * SSM_WIDTH, BC_COLS)],
                dt_ref.at[rows, :],
                *consts, ssm_ref.at[rows, :], halox_ref, halobc_ref, state_ref)

    z0, x0, bc0 = qkv_cols, qkv_cols + SSM_WIDTH, qkv_cols + 2 * SSM_WIDTH
    plan = [(project, x0, 0), (project, x0 + PROJ_CN, 1), (project, bc0, 0), (project_dt, None, 1),
            (project, z0, 1), (project, z0 + PROJ_CN, 1)]
    plan += [(project, col, pieces) for col, pieces in
             zip(range(0, qkv_cols, PROJ_CN), SCAN_PIECES_AFTER_QKV_DOT)]
    scan = scan_steps()
    for emit, col, pieces in plan:
        emit(col)
        for _ in range(pieces):
            next(scan, None)
    for _ in scan:
        pass


def _proj_ssd(x2, g, w_pad, ssd_consts, seq):
    t = x2.shape[0]
    c_specs, c_ops = ssd_consts
    tok = lambda width: pl.BlockSpec((PROJ_TM, width), lambda i: (i, 0))
    return pl.pallas_call(
        functools.partial(_proj_ssd_body, tiles_per_seq=seq // PROJ_TM),
        grid=(t // PROJ_TM,),
        in_specs=[
            tok(D_MODEL),
            _resident((1, D_MODEL)),
            _resident((MAIN_COLS, D_MODEL)),
            pl.BlockSpec((LANES, D_MODEL), lambda i: (MAIN_COLS // LANES, 0), pipeline_mode=pl.Buffered(1)),
        ] + c_specs,
        out_specs=[tok(3 * ATT_WIDTH), tok(SSM_WIDTH)],
        out_shape=[
            jax.ShapeDtypeStruct((t, 3 * ATT_WIDTH), BF16),
            jax.ShapeDtypeStruct((t, SSM_WIDTH), BF16),
        ],
        scratch_shapes=[
            pltpu.VMEM((PROJ_TM, SSD_COLS), BF16),
            pltpu.VMEM((PROJ_TM, LANES), F32),
            pltpu.VMEM((2 * HALO, SSM_WIDTH), F32),
            pltpu.VMEM((2 * HALO, BC_COLS), F32),
            pltpu.VMEM((SSM_GROUPS, SSM_STATE, SSM_GROUP_WIDTH), F32),
        ],
        compiler_params=pltpu.CompilerParams(
            dimension_semantics=("arbitrary",), vmem_limit_bytes=VMEM_LIMIT),
        name="proj_ssd",
    )(x2, g, w_pad, w_pad, *c_ops)


def _ffn_body(x_ref, att_ref, ssm_ref, wo_ref, g2_ref, wup_ref, cw_ref, cb_ref, wdn_ref, gf_ref,
              o_ref, xc_ref, x1p_ref, h2_ref, oc_ref, carry_ref, *, tiles_per_seq):
    i = pl.program_id(0)
    sub, cw = FFN_SUB, FFN_CW
    ph = sub // SUBLANES
    n_lane_blocks = D_MODEL // LANES

    @pl.when(i % tiles_per_seq == 0)
    def _():
        carry_ref[...] = jnp.zeros(carry_ref.shape, F32)

    def mix_in(t):
        rows = slice(t * sub, (t + 1) * sub)
        x1 = (x_ref[rows, :]
              + jnp.dot(att_ref[rows, :], wo_ref[0:ATT_WIDTH, :], preferred_element_type=F32)
              + jnp.dot(ssm_ref[rows, :], wo_ref[ATT_WIDTH:, :], preferred_element_type=F32))
        for c in range(n_lane_blocks):
            xc_ref[t, c] = x1[:, c * LANES:(c + 1) * LANES]
        x1p = jnp.concatenate(
            [jnp.concatenate([xc_ref[t, c, pl.ds(k, ph, stride=SUBLANES), :] for k in range(SUBLANES)], axis=0)
             for c in range(n_lane_blocks)], axis=1)
        x1p_ref[t] = x1p
        h2_ref[t] = (x1p * lax.rsqrt(jnp.mean(x1p * x1p, axis=-1, keepdims=True) + NORM_EPS)
                     * g2_ref[...]).astype(BF16)

    first_row = lax.broadcasted_iota(jnp.int32, (ph, cw), 0) == 0

    def prev_token(block, carry_slot):
        tail = carry_ref[carry_slot]
        carry_ref[carry_slot] = block[ph - HALO:ph]
        return jnp.where(first_row, jnp.broadcast_to(tail[HALO - 1:HALO, :], (ph, cw)),
                         pltpu.roll(block, 1, 0))

    def up_conv(t, slot, col):
        u = jnp.dot(h2_ref[t], wup_ref[:, col:col + cw], preferred_element_type=F32)
        blocks = [u[k * ph:(k + 1) * ph] for k in range(SUBLANES)]
        back1 = prev_token(blocks[SUBLANES - 1], 2 * slot)
        back2 = prev_token(blocks[SUBLANES - 2], 2 * slot + 1)
        hist = [back2, back1] + blocks
        w = [cw_ref[t:t + 1, col:col + cw] for t in range(FFN_CONV)]
        b = cb_ref[:, col:col + cw]
        return jnp.concatenate(
            [b + w[2] * hist[k + 2] + w[1] * hist[k + 1] + w[0] * hist[k] for k in range(SUBLANES)],
            axis=0)

    def ffn(t):
        acts = []
        for j in range(FFN_DIM // cw):
            gate = up_conv(t, 2 * j, j * cw)
            val = up_conv(t, 2 * j + 1, FFN_DIM + j * cw)
            acts.append((gate * _sigmoid(gate) * val).astype(BF16))
        x2 = x1p_ref[t] + jnp.dot(jnp.concatenate(acts, axis=1), wdn_ref[...], preferred_element_type=F32)
        out = x2 * lax.rsqrt(jnp.mean(x2 * x2, axis=-1, keepdims=True) + NORM_EPS) * gf_ref[...]
        for c in range(n_lane_blocks):
            for k in range(SUBLANES):
                oc_ref[t, c, pl.ds(k, ph, stride=SUBLANES), :] = (
                    out[k * ph:(k + 1) * ph, c * LANES:(c + 1) * LANES])
        for c in range(n_lane_blocks):
            o_ref[t * sub:(t + 1) * sub, c * LANES:(c + 1) * LANES] = oc_ref[t, c]

    n_sub = FFN_TM // sub
    for t in range(n_sub):
        mix_in(t)
    for t in range(n_sub):
        ffn(t)


def _mix_ffn(x2, att, ssm, w_out, g2, w_up, conv_w, conv_b, w_down, gf, seq):
    t = x2.shape[0]
    n_carry = 2 * (FFN_CONV - 1) * (FFN_DIM // FFN_CW)
    n_sub = FFN_TM // FFN_SUB
    tok = lambda width: pl.BlockSpec((FFN_TM, width), lambda i: (i, 0))
    return pl.pallas_call(
        functools.partial(_ffn_body, tiles_per_seq=seq // FFN_TM),
        grid=(t // FFN_TM,),
        in_specs=[
            tok(D_MODEL), tok(ATT_WIDTH), tok(SSM_WIDTH),
            _resident((ATT_WIDTH + SSM_WIDTH, D_MODEL)),
            _resident((1, D_MODEL)),
            _resident((D_MODEL, 2 * FFN_DIM)),
            _resident((FFN_CONV, 2 * FFN_DIM)),
            _resident((1, 2 * FFN_DIM)),
            _resident((FFN_DIM, D_MODEL)),
            _resident((1, D_MODEL)),
        ],
        out_specs=tok(D_MODEL),
        out_shape=jax.ShapeDtypeStruct((t, D_MODEL), F32),
        scratch_shapes=[
            pltpu.VMEM((n_sub, D_MODEL // LANES, FFN_SUB, LANES), F32),
            pltpu.VMEM((n_sub, FFN_SUB, D_MODEL), F32),
            pltpu.VMEM((n_sub, FFN_SUB, D_MODEL), BF16),
            pltpu.VMEM((n_sub, D_MODEL // LANES, FFN_SUB, LANES), F32),
            pltpu.VMEM((n_carry, HALO, FFN_CW), F32),
        ],
        compiler_params=pltpu.CompilerParams(
            dimension_semantics=("arbitrary",), vmem_limit_bytes=VMEM_LIMIT),
        name="mix_ffn",
    )(x2, att, ssm, w_out, g2, w_up, conv_w, conv_b, w_down, gf)


def kernel(x, rel_bias_table, attn_norm_g, w_in, lambda_q1, lambda_k1, lambda_q2, lambda_k2,
           attn_subln_g, ssm_conv_w, ssm_conv_b, ssm_dt_bias, ssm_a_log, ssm_d, ssm_norm_g,
           w_out, ffn_norm_g, ffn_w_up, ffn_conv_w, ffn_conv_b, ffn_w_down, final_norm_g):
    batch, seq, _ = x.shape
    depth = w_in.shape[0]
    assert seq % max(ATT_QB, SSD_Q, FFN_TM) == 0 and (batch * seq) % PROJ_TM == 0
    x2 = x.reshape(batch * seq, D_MODEL)
    row = lambda v: v.astype(F32).reshape(1, -1)
    for i in range(depth):
        lam_init = 0.8 - 0.6 * math.exp(-0.3 * i)
        lam = (jnp.exp(jnp.sum(lambda_q1[i].astype(F32) * lambda_k1[i].astype(F32)))
               - jnp.exp(jnp.sum(lambda_q2[i].astype(F32) * lambda_k2[i].astype(F32)))
               + lam_init).reshape(1)
        qkv, ssm = _proj_ssd(
            x2, row(attn_norm_g[i]), _w_prep(w_in[i].astype(F32).T),
            _ssd_consts(ssm_conv_w[i], ssm_conv_b[i], ssm_dt_bias[i], ssm_a_log[i], ssm_d[i], ssm_norm_g[i]),
            seq)
        att = _attention(qkv, lam, rel_bias_table, row(attn_subln_g[i]), batch, seq, lam_init)
        assert depth == 1
        x2 = _mix_ffn(x2, att, ssm, w_out[i].astype(BF16), row(ffn_norm_g[i]),
                      ffn_w_up[i].astype(BF16), ffn_conv_w[i].astype(F32), row(ffn_conv_b[i]),
                      ffn_w_down[i].astype(BF16), row(final_norm_g), seq)
    return x2.reshape(batch, seq, D_MODEL)
```

```python
import functools
import math

import jax
import jax.numpy as jnp
from jax import lax
from jax.experimental import pallas as pl
from jax.experimental.pallas import tpu as pltpu

F32 = jnp.float32
BF16 = jnp.bfloat16

D_MODEL = 1024
CHUNK = 64
ATT_HEADS = 8
ATT_HEAD_DIM = 64
ATT_V_DIM = 2 * ATT_HEAD_DIM
ATT_WIDTH = ATT_HEADS * ATT_V_DIM
SSM_HEADS = 16
SSM_HEAD_DIM = 64
SSM_WIDTH = SSM_HEADS * SSM_HEAD_DIM
SSM_GROUPS = 2
SSM_STATE = 128
SSM_CONV = 4
SSM_HEADS_PER_GROUP = SSM_HEADS // SSM_GROUPS
SSM_GROUP_WIDTH = SSM_WIDTH // SSM_GROUPS
FFN_DIM = 2816
FFN_CONV = 3
REL_BUCKETS = 32
REL_MAX_DIST = 128
NORM_EPS = 1e-6
SUBLN_EPS = 1e-5
SSM_NORM_EPS = 1e-5
BC_COLS = 2 * SSM_GROUPS * SSM_STATE
MAIN_COLS = 3 * ATT_WIDTH + SSM_WIDTH + SSM_WIDTH + BC_COLS
DT_COLS = SSM_HEADS
LOG2E = math.log2(math.e)

LANES = 128
SUBLANES = 8
VMEM_LIMIT = 56 * 1024 * 1024

PROJ_TM = 512
PROJ_CN = 512
ATT_QB = 256
ATT_KB = 256
SSD_Q = 256
FFN_TM = 512
FFN_CW = 256
W_STAGE_ROWS_D = 256
W_STAGE_ROWS_F = 64
HALO = SUBLANES

def _resident(shape):
    nd = len(shape)
    return pl.BlockSpec(shape, lambda *_: (0,) * nd, pipeline_mode=pl.Buffered(1))


def _sigmoid(x):
    return 1.0 / (1.0 + jnp.exp2(x * -LOG2E))


W_PAD_COLS = MAIN_COLS + LANES


def _wprep_body(w_ref, o_ref):
    q_scale = LOG2E * ATT_HEAD_DIM ** -0.5
    o_ref[0:ATT_WIDTH, :] = (w_ref[0:ATT_WIDTH, :] * q_scale).astype(BF16)
    o_ref[ATT_WIDTH:MAIN_COLS + DT_COLS, :] = w_ref[ATT_WIDTH:MAIN_COLS + DT_COLS, :].astype(BF16)
    o_ref[MAIN_COLS + DT_COLS:W_PAD_COLS, :] = jnp.zeros((LANES - DT_COLS, D_MODEL), BF16)


def _w_prep(w_t):
    whole = lambda shape: pl.BlockSpec(shape, lambda: (0, 0))
    return pl.pallas_call(
        _wprep_body,
        in_specs=[whole((MAIN_COLS + DT_COLS, D_MODEL))],
        out_specs=whole((W_PAD_COLS, D_MODEL)),
        out_shape=jax.ShapeDtypeStruct((W_PAD_COLS, D_MODEL), BF16),
        compiler_params=pltpu.CompilerParams(vmem_limit_bytes=VMEM_LIMIT),
        name="w_prep",
    )(w_t)


FAR_BUCKET = REL_BUCKETS // 2 - 1
BIAS_SPAN = 4 * ATT_KB


def _t5_bucket(rel):
    nb = REL_BUCKETS // 2
    max_exact = nb // 2
    bucket = jnp.where(rel > 0, nb, 0)
    n = jnp.abs(rel)
    nf = jnp.maximum(n, 1).astype(F32)
    large = max_exact + (jnp.log(nf / max_exact) / math.log(REL_MAX_DIST / max_exact)
                         * (nb - max_exact)).astype(jnp.int32)
    large = jnp.minimum(large, nb - 1)
    return bucket + jnp.where(n < max_exact, n, large)


def _bias_bucket_row():
    rel = jnp.arange(BIAS_SPAN, dtype=jnp.int32) - 2 * ATT_KB
    return jnp.broadcast_to(_t5_bucket(rel)[None, :], (SUBLANES, BIAS_SPAN))


def _bias_tiles(tbl_ref, idx_ref, h):
    idx = idx_ref[...]
    base = tbl_ref[FAR_BUCKET, h]
    r = jnp.zeros(idx.shape, F32)
    for b in range(REL_BUCKETS):
        r = jnp.where(idx == b, (tbl_ref[b, h] - base) * LOG2E, r)
    rows = jnp.concatenate([r] * (ATT_QB // SUBLANES), axis=0)
    rolled = pltpu.roll(rows, 0, 1, stride=1, stride_axis=0)
    return rolled[:, 2 * ATT_KB:3 * ATT_KB], rolled[:, ATT_KB:2 * ATT_KB]


ATT_STREAMS = 2


def _attn_body(lam_ref, tbl_ref, q_ref, k_ref, v_ref, idx_ref, subg_ref, o_ref, v1_ref, *, lam_init):
    qb, kb = ATT_QB, ATT_KB
    seq = q_ref.shape[1]
    lane = lax.broadcasted_iota(jnp.int32, (qb, LANES), 1)
    lam = lam_ref[0]

    @functools.cache
    def bias_and_mask():
        row = lax.broadcasted_iota(jnp.int32, (qb, kb), 0)
        col = lax.broadcasted_iota(jnp.int32, (qb, kb), 1)
        allowed = (col // CHUNK) <= (row // CHUNK)
        b_diag, b_prev = _bias_tiles(tbl_ref, idx_ref, pl.program_id(1))
        return tuple(jnp.concatenate([t, t], axis=0) for t in (allowed, b_diag, b_prev))

    @functools.cache
    def build_v1(s):
        v1_ref[s, :, 0:LANES] = v_ref[s]
        v1_ref[s, :, LANES:2 * LANES] = jnp.ones((seq, LANES), BF16)

    def scores(s, i):
        kvl = (i + 1) * kb
        q = q_ref[s, i * qb:(i + 1) * qb, :]
        zero = jnp.zeros_like(q)
        qs = jnp.concatenate([jnp.where(lane < ATT_HEAD_DIM, q, zero),
                              jnp.where(lane >= ATT_HEAD_DIM, q, zero)], axis=0)
        sc = lax.dot_general(qs, k_ref[s, 0:kvl, :], (((1,), (1,)), ((), ())),
                             preferred_element_type=F32)
        allowed, b_diag, b_prev = bias_and_mask()
        blocks = [sc[:, j * kb:(j + 1) * kb] for j in range(i + 1)]
        blocks[i] = jnp.where(allowed, blocks[i] + b_diag, -1e30)
        if i >= 1:
            blocks[i - 1] = blocks[i - 1] + b_prev
        m = jnp.max(functools.reduce(jnp.maximum, blocks), axis=-1, keepdims=True)
        return blocks, m

    def probs(blocks, m):
        return jnp.concatenate([jnp.exp2(blk - m).astype(BF16) for blk in blocks], axis=1)

    def finish(s, i, p):
        build_v1(s)
        kvl = (i + 1) * kb
        acc = jnp.dot(p, v1_ref[s, 0:kvl, :], preferred_element_type=F32)
        o = acc[:, 0:LANES] / acc[:, LANES:2 * LANES]
        o = o[:qb] - lam * o[qb:]
        o = o * lax.rsqrt(jnp.mean(o * o, axis=-1, keepdims=True) + SUBLN_EPS) * subg_ref[...]
        o_ref[s, i * qb:(i + 1) * qb, :] = (o * (1.0 - lam_init)).astype(BF16)

    order = [(s, i) for i in reversed(range(seq // qb)) for s in range(ATT_STREAMS)]
    scored, exped = {}, {}
    for n in range(len(order) + 2 * ATT_STREAMS):
        if n < len(order):
            scored[n] = scores(*order[n])
        n1, n2 = n - ATT_STREAMS, n - 2 * ATT_STREAMS
        if 0 <= n1 < len(order):
            exped[n1] = probs(*scored.pop(n1))
        if 0 <= n2 < len(order):
            finish(*order[n2], exped.pop(n2))


def _attention(qkv, lam, rel_table, sub_g, batch, seq, lam_init):
    assert batch % ATT_STREAMS == 0
    kcol = ATT_WIDTH // LANES
    vcol = 2 * ATT_WIDTH // LANES
    rows = lambda col0: pl.BlockSpec((None, ATT_STREAMS, seq, LANES), lambda b, h: (b, 0, 0, col0 + h))
    out = pl.pallas_call(
        functools.partial(_attn_body, lam_init=lam_init),
        grid=(batch // ATT_STREAMS, ATT_HEADS),
        in_specs=[
            pl.BlockSpec(memory_space=pltpu.SMEM),
            pl.BlockSpec(memory_space=pltpu.SMEM),
            rows(0), rows(kcol), rows(vcol),
            _resident((SUBLANES, BIAS_SPAN)),
            pl.BlockSpec((1, ATT_V_DIM), lambda b, h: (0, 0)),
        ],
        out_specs=rows(0),
        out_shape=jax.ShapeDtypeStruct((batch // ATT_STREAMS, ATT_STREAMS, seq, ATT_WIDTH), BF16),
        scratch_shapes=[pltpu.VMEM((ATT_STREAMS, seq, 2 * LANES), BF16)],
        compiler_params=pltpu.CompilerParams(
            dimension_semantics=("arbitrary", "arbitrary"), vmem_limit_bytes=VMEM_LIMIT),
        name="diff_attn",
    )(lam, rel_table.astype(F32), *[qkv.reshape(batch // ATT_STREAMS, ATT_STREAMS, seq, -1)] * 3,
      _bias_bucket_row(), sub_g)
    return out.reshape(batch * seq, ATT_WIDTH)


def _split3(x):
    x1 = x.astype(BF16)
    r1 = x - x1.astype(F32)
    x2 = r1.astype(BF16)
    x3 = (r1 - x2.astype(F32)).astype(BF16)
    return x1, x2, x3


def _ssd_steps(z_ref, xs_ref, bc_ref, dt_ref, cwx_ref, cbx_ref, cwbc_ref, cbbc_ref, dtb_ref,
               alog_ref, dexp_ref, ng_ref, e3_ref, tril_ref, o_ref, halox_ref, halobc_ref,
               state_ref):
    q = SSD_Q

    def conv_silu(raw_ref, halo_ref, w_ref, b_ref):
        x = raw_ref[...].astype(F32)
        first_rows = lax.broadcasted_iota(jnp.int32, (HALO, x.shape[1]), 0)
        acc = b_ref[...] + w_ref[SSM_CONV - 1:SSM_CONV, :] * x
        for j in range(1, SSM_CONV):
            wj = w_ref[SSM_CONV - 1 - j:SSM_CONV - j, :]
            rolled = pltpu.roll(x, j, 0)
            head = jnp.where(first_rows < j, halo_ref[HALO - j:2 * HALO - j, :], rolled[0:HALO])
            acc = acc + wj * jnp.concatenate([head, rolled[HALO:]], axis=0)
        halo_ref[0:HALO, :] = x[q - HALO:q]
        return acc * _sigmoid(acc)

    xs = conv_silu(xs_ref, halox_ref, cwx_ref, cbx_ref)
    yield
    bc = conv_silu(bc_ref, halobc_ref, cwbc_ref, cbbc_ref)
    yield

    dtr = dt_ref[...] + dtb_ref[...]
    dt = jnp.maximum(dtr, 0.0) + jnp.log(1.0 + jnp.exp(-jnp.abs(dtr)))
    a = (-LOG2E * jnp.exp(alog_ref[...])) * dt

    row = lax.broadcasted_iota(jnp.int32, (q, q), 0)
    col = lax.broadcasted_iota(jnp.int32, (q, q), 1)
    causal = col <= row
    tril = tril_ref[...]
    acum = sum(jnp.dot(tril, t, preferred_element_type=F32) for t in _split3(a))

    lane = lax.broadcasted_iota(jnp.int32, (q, LANES), 1)

    def expand(x):
        parts = [jnp.where(lane < SSM_HEADS, t.astype(F32), 0.0) for t in _split3(x)]
        packed = parts[0] + pltpu.roll(parts[1], SSM_HEADS, 1) + pltpu.roll(parts[2], 2 * SSM_HEADS, 1)
        return jnp.dot(packed.astype(BF16), e3_ref[...], preferred_element_type=F32)

    dt_exp = expand(dt)
    acum_exp = expand(acum)
    ea_exp = jnp.exp2(acum_exp)
    last_exp = acum_exp[q - 1:q, :]
    dte_exp = jnp.exp2(last_exp - acum_exp)
    ea_last = ea_exp[q - 1:q, :]

    xdt = xs * dt_exp
    xdt_b = xdt.astype(BF16)
    acum_t = acum.T
    pair_lane = lax.broadcasted_iota(jnp.int32, (q, LANES), 1)
    yield

    for g in range(SSM_GROUPS):
        gsl = slice(g * SSM_GROUP_WIDTH, (g + 1) * SSM_GROUP_WIDTH)
        bg = bc[:, g * SSM_STATE:(g + 1) * SSM_STATE]
        cg = bc[:, (SSM_GROUPS + g) * SSM_STATE:(SSM_GROUPS + g + 1) * SSM_STATE]
        bg_b = bg.astype(BF16)
        cg_b = cg.astype(BF16)
        cb = lax.dot_general(cg_b, bg_b, (((1,), (1,)), ((), ())), preferred_element_type=F32)

        def masked(h):
            seg = acum[:, h:h + 1] - acum_t[h:h + 1, :]
            return (cb * jnp.exp2(jnp.where(causal, seg, -jnp.inf))).astype(BF16)

        pairs = []
        for j in range(SSM_HEADS_PER_GROUP // 2):
            h0 = g * SSM_HEADS_PER_GROUP + 2 * j
            xp = xdt_b[:, h0 * SSM_HEAD_DIM:(h0 + 2) * SSM_HEAD_DIM]
            r0 = jnp.dot(masked(h0), xp, preferred_element_type=F32)
            r1 = jnp.dot(masked(h0 + 1), xp, preferred_element_type=F32)
            pairs.append(jnp.where(pair_lane < SSM_HEAD_DIM, r0, r1))
            if j % 2 == 1:
                yield
        y_diag = jnp.concatenate(pairs, axis=1)

        st = state_ref[g]
        y_off = jnp.dot(cg_b, st.astype(BF16), preferred_element_type=F32) * ea_exp[:, gsl]
        w = (xdt[:, gsl] * dte_exp[:, gsl]).astype(BF16)
        s_new = jnp.dot(bg.T.astype(BF16), w, preferred_element_type=F32)
        state_ref[g] = st * ea_last[:, gsl] + s_new

        y = y_diag + y_off + xs[:, gsl] * dexp_ref[:, gsl]
        zf = z_ref[:, gsl].astype(F32)
        gated = y * (zf * _sigmoid(zf))
        gated = gated * lax.rsqrt(jnp.mean(gated * gated, axis=-1, keepdims=True) + SSM_NORM_EPS)
        o_ref[:, gsl] = (gated * ng_ref[:, gsl]).astype(BF16)
        yield


def _ssd_consts(conv_w, conv_b, dt_bias, a_log, d_skip, norm_g):
    pad = LANES - SSM_HEADS
    dtb = jnp.pad(dt_bias.astype(F32), (0, pad)).reshape(1, LANES)
    alog = jnp.pad(a_log.astype(F32), (0, pad)).reshape(1, LANES)
    dexp = jnp.repeat(d_skip.astype(F32), SSM_HEAD_DIM).reshape(1, SSM_WIDTH)
    r = jnp.arange(LANES)[:, None]
    hcol = (jnp.arange(SSM_WIDTH) // SSM_HEAD_DIM)[None, :]
    e3 = jnp.where((r % SSM_HEADS == hcol) & (r < 3 * SSM_HEADS), 1.0, 0.0).astype(BF16)
    cw = conv_w.astype(F32)
    cb = conv_b.astype(F32).reshape(1, -1)
    tq = jnp.arange(SSD_Q)
    tril = jnp.where(tq[:, None] >= tq[None, :], 1.0, 0.0).astype(BF16)
    in_specs = [
        _resident((SSM_CONV, SSM_WIDTH)),
        _resident((1, SSM_WIDTH)),
        _resident((SSM_CONV, BC_COLS)),
        _resident((1, BC_COLS)),
        _resident((1, LANES)),
        _resident((1, LANES)),
        _resident((1, SSM_WIDTH)),
        _resident((1, SSM_WIDTH)),
        _resident((LANES, SSM_WIDTH)),
        _resident((SSD_Q, SSD_Q)),
    ]
    operands = (cw[:, :SSM_WIDTH], cb[:, :SSM_WIDTH], cw[:, SSM_WIDTH:], cb[:, SSM_WIDTH:], dtb, alog,
                dexp, norm_g.astype(F32).reshape(1, -1), e3, tril)
    return in_specs, operands


SSD_COLS = 2 * SSM_WIDTH + BC_COLS
SCAN_PIECES_AFTER_QKV_DOT = (2, 2, 2, 3, 3, 2)


def _proj_ssd_body(x_ref, g_ref, w_ref, wdt_ref, *rest, tiles_per_seq):
    consts, (qkv_ref, ssm_ref, zxbc_ref, dt_ref, halox_ref, halobc_ref, state_ref) = rest[:-7], rest[-7:]

    @pl.when(pl.program_id(0) % tiles_per_seq == 0)
    def _():
        halox_ref[...] = jnp.zeros(halox_ref.shape, F32)
        halobc_ref[...] = jnp.zeros(halobc_ref.shape, F32)
        state_ref[...] = jnp.zeros(state_ref.shape, F32)

    x = x_ref[...]
    h = x * lax.rsqrt(jnp.mean(x * x, axis=-1, keepdims=True) + NORM_EPS) * g_ref[...]
    h = h.astype(BF16)
    nt = (((1,), (1,)), ((), ()))

    qkv_cols = 3 * ATT_WIDTH

    def project(col):
        y = lax.dot_general(h, w_ref[col:col + PROJ_CN, :], nt, preferred_element_type=F32).astype(BF16)
        if col < qkv_cols:
            qkv_ref[:, col:col + PROJ_CN] = y
        else:
            zxbc_ref[:, col - qkv_cols:col - qkv_cols + PROJ_CN] = y

    def project_dt(_):
        dt_ref[...] = lax.dot_general(h, wdt_ref[...], nt, preferred_element_type=F32)

    def scan_steps():
        for c in range(PROJ_TM // SSD_Q):
            rows = pl.ds(c * SSD_Q, SSD_Q)
            yield from _ssd_steps(
                zxbc_ref.at[rows, pl.ds(0, SSM_WIDTH)],
                zxbc_ref.at[rows, pl.ds(SSM_WIDTH, SSM_WIDTH)],
                zxbc_ref.at[rows, pl.ds(2 * SSM_WIDTH, BC_COLS)],
                dt_ref.at[rows, :],
                *consts, ssm_ref.at[rows, :], halox_ref, halobc_ref, state_ref)

    z0, x0, bc0 = qkv_cols, qkv_cols + SSM_WIDTH, qkv_cols + 2 * SSM_WIDTH
    plan = [(project, x0, 0), (project, x0 + PROJ_CN, 1), (project, bc0, 0), (project_dt, None, 1),
            (project, z0, 1), (project, z0 + PROJ_CN, 1)]
    plan += [(project, col, pieces) for col, pieces in
             zip(range(0, qkv_cols, PROJ_CN), SCAN_PIECES_AFTER_QKV_DOT)]
    scan = scan_steps()
    for emit, col, pieces in plan:
        emit(col)
        for _ in range(pieces):
            next(scan, None)
    for _ in scan:
        pass


def _proj_ssd(x2, g, w_pad, ssd_consts, seq):
    t = x2.shape[0]
    c_specs, c_ops = ssd_consts
    tok = lambda width: pl.BlockSpec((PROJ_TM, width), lambda i: (i, 0))
    return pl.pallas_call(
        functools.partial(_proj_ssd_body, tiles_per_seq=seq // PROJ_TM),
        grid=(t // PROJ_TM,),
        in_specs=[
            tok(D_MODEL),
            _resident((1, D_MODEL)),
            _resident((MAIN_COLS, D_MODEL)),
            pl.BlockSpec((LANES, D_MODEL), lambda i: (MAIN_COLS // LANES, 0), pipeline_mode=pl.Buffered(1)),
        ] + c_specs,
        out_specs=[tok(3 * ATT_WIDTH), tok(SSM_WIDTH)],
        out_shape=[
            jax.ShapeDtypeStruct((t, 3 * ATT_WIDTH), BF16),
            jax.ShapeDtypeStruct((t, SSM_WIDTH), BF16),
        ],
        scratch_shapes=[
            pltpu.VMEM((PROJ_TM, SSD_COLS), BF16),
            pltpu.VMEM((PROJ_TM, LANES), F32),
            pltpu.VMEM((2 * HALO, SSM_WIDTH), F32),
            pltpu.VMEM((2 * HALO, BC_COLS), F32),
            pltpu.VMEM((SSM_GROUPS, SSM_STATE, SSM_GROUP_WIDTH), F32),
        ],
        compiler_params=pltpu.CompilerParams(
            dimension_semantics=("arbitrary",), vmem_limit_bytes=VMEM_LIMIT),
        name="proj_ssd",
    )(x2, g, w_pad, w_pad, *c_ops)


def _load_weight_bf16(src_hbm, dst_ref, stage_ref, sem):
    n_rows, width = src_hbm.shape
    _, rows, stage_width = stage_ref.shape
    assert stage_width == width and n_rows % rows == 0

    def chunk_copy(c):
        return pltpu.make_async_copy(src_hbm.at[pl.ds(c * rows, rows), :], stage_ref.at[c % 2], sem.at[c % 2])

    n = n_rows // rows
    chunk_copy(0).start()
    for c in range(n):
        if c + 1 < n:
            chunk_copy(c + 1).start()
        chunk_copy(c).wait()
        dst_ref[c * rows:(c + 1) * rows, :] = stage_ref[c % 2].astype(BF16)


def _ffn_body(x_ref, att_ref, ssm_ref, wo_hbm, g2_ref, wup_hbm, cw_ref, cb_ref, wdn_hbm, gf_ref,
              o_ref, xc_ref, x1p_ref, h2_ref, oc_ref, carry_ref, wo_ref, wup_ref, wdn_ref, stage_d_ref,
              stage_f_ref, sem, *, tiles_per_seq):
    i = pl.program_id(0)
    tm, cw = FFN_TM, FFN_CW
    ph = tm // SUBLANES
    n_lane_blocks = D_MODEL // LANES

    @pl.when(i == 0)
    def _():
        _load_weight_bf16(wo_hbm, wo_ref, stage_d_ref, sem)
        _load_weight_bf16(wdn_hbm, wdn_ref, stage_d_ref, sem)
        _load_weight_bf16(wup_hbm, wup_ref, stage_f_ref, sem)

    @pl.when(i % tiles_per_seq == 0)
    def _():
        carry_ref[...] = jnp.zeros(carry_ref.shape, F32)

    x1 = (x_ref[...]
          + jnp.dot(att_ref[...], wo_ref[0:ATT_WIDTH, :], preferred_element_type=F32)
          + jnp.dot(ssm_ref[...], wo_ref[ATT_WIDTH:, :], preferred_element_type=F32))
    for c in range(n_lane_blocks):
        xc_ref[c] = x1[:, c * LANES:(c + 1) * LANES]
    x1p = jnp.concatenate(
        [jnp.concatenate([xc_ref[c, pl.ds(k, ph, stride=SUBLANES), :] for k in range(SUBLANES)], axis=0)
         for c in range(n_lane_blocks)], axis=1)
    x1p_ref[...] = x1p
    h2_ref[...] = (x1p * lax.rsqrt(jnp.mean(x1p * x1p, axis=-1, keepdims=True) + NORM_EPS)
                   * g2_ref[...]).astype(BF16)

    first_row = lax.broadcasted_iota(jnp.int32, (ph, cw), 0) == 0

    def prev_token(block, carry_slot):
        tail = carry_ref[carry_slot]
        carry_ref[carry_slot] = block[ph - HALO:ph]
        return jnp.where(first_row, jnp.broadcast_to(tail[HALO - 1:HALO, :], (ph, cw)),
                         pltpu.roll(block, 1, 0))

    def up_conv(slot, col):
        u = jnp.dot(h2_ref[...], wup_ref[:, col:col + cw], preferred_element_type=F32)
        blocks = [u[k * ph:(k + 1) * ph] for k in range(SUBLANES)]
        back1 = prev_token(blocks[SUBLANES - 1], 2 * slot)
        back2 = prev_token(blocks[SUBLANES - 2], 2 * slot + 1)
        hist = [back2, back1] + blocks
        w = [cw_ref[t:t + 1, col:col + cw] for t in range(FFN_CONV)]
        b = cb_ref[:, col:col + cw]
        return jnp.concatenate(
            [b + w[2] * hist[k + 2] + w[1] * hist[k + 1] + w[0] * hist[k] for k in range(SUBLANES)],
            axis=0)

    acts = []
    for j in range(FFN_DIM // cw):
        gate = up_conv(2 * j, j * cw)
        val = up_conv(2 * j + 1, FFN_DIM + j * cw)
        acts.append((gate * _sigmoid(gate) * val).astype(BF16))

    x2 = x1p_ref[...] + jnp.dot(jnp.concatenate(acts, axis=1), wdn_ref[...], preferred_element_type=F32)
    out = x2 * lax.rsqrt(jnp.mean(x2 * x2, axis=-1, keepdims=True) + NORM_EPS) * gf_ref[...]
    for c in range(n_lane_blocks):
        for k in range(SUBLANES):
            oc_ref[c, pl.ds(k, ph, stride=SUBLANES), :] = out[k * ph:(k + 1) * ph, c * LANES:(c + 1) * LANES]
    for c in range(n_lane_blocks):
        o_ref[:, c * LANES:(c + 1) * LANES] = oc_ref[c]


def _mix_ffn(x2, att, ssm, w_out, g2, w_up, conv_w, conv_b, w_down, gf, seq):
    t = x2.shape[0]
    n_carry = 2 * (FFN_CONV - 1) * (FFN_DIM // FFN_CW)
    tok = lambda width: pl.BlockSpec((FFN_TM, width), lambda i: (i, 0))
    return pl.pallas_call(
        functools.partial(_ffn_body, tiles_per_seq=seq // FFN_TM),
        grid=(t // FFN_TM,),
        in_specs=[
            tok(D_MODEL), tok(ATT_WIDTH), tok(SSM_WIDTH),
            pl.BlockSpec(memory_space=pl.ANY),
            _resident((1, D_MODEL)),
            pl.BlockSpec(memory_space=pl.ANY),
            _resident((FFN_CONV, 2 * FFN_DIM)),
            _resident((1, 2 * FFN_DIM)),
            pl.BlockSpec(memory_space=pl.ANY),
            _resident((1, D_MODEL)),
        ],
        out_specs=tok(D_MODEL),
        out_shape=jax.ShapeDtypeStruct((t, D_MODEL), F32),
        scratch_shapes=[
            pltpu.VMEM((D_MODEL // LANES, FFN_TM, LANES), F32),
            pltpu.VMEM((FFN_TM, D_MODEL), F32),
            pltpu.VMEM((FFN_TM, D_MODEL), BF16),
            pltpu.VMEM((D_MODEL // LANES, FFN_TM, LANES), F32),
            pltpu.VMEM((n_carry, HALO, FFN_CW), F32),
            pltpu.VMEM((ATT_WIDTH + SSM_WIDTH, D_MODEL), BF16),
            pltpu.VMEM((D_MODEL, 2 * FFN_DIM), BF16),
            pltpu.VMEM((FFN_DIM, D_MODEL), BF16),
            pltpu.VMEM((2, W_STAGE_ROWS_D, D_MODEL), F32),
            pltpu.VMEM((2, W_STAGE_ROWS_F, 2 * FFN_DIM), F32),
            pltpu.SemaphoreType.DMA((2,)),
        ],
        compiler_params=pltpu.CompilerParams(
            dimension_semantics=("arbitrary",), vmem_limit_bytes=VMEM_LIMIT),
        name="mix_ffn",
    )(x2, att, ssm, w_out, g2, w_up, conv_w, conv_b, w_down, gf)


def kernel(x, rel_bias_table, attn_norm_g, w_in, lambda_q1, lambda_k1, lambda_q2, lambda_k2,
           attn_subln_g, ssm_conv_w, ssm_conv_b, ssm_dt_bias, ssm_a_log, ssm_d, ssm_norm_g,
           w_out, ffn_norm_g, ffn_w_up, ffn_conv_w, ffn_conv_b, ffn_w_down, final_norm_g):
    batch, seq, _ = x.shape
    depth = w_in.shape[0]
    assert seq % max(ATT_QB, SSD_Q, FFN_TM) == 0 and (batch * seq) % PROJ_TM == 0
    x2 = x.reshape(batch * seq, D_MODEL)
    row = lambda v: v.astype(F32).reshape(1, -1)
    for i in range(depth):
        lam_init = 0.8 - 0.6 * math.exp(-0.3 * i)
        lam = (jnp.exp(jnp.sum(lambda_q1[i].astype(F32) * lambda_k1[i].astype(F32)))
               - jnp.exp(jnp.sum(lambda_q2[i].astype(F32) * lambda_k2[i].astype(F32)))
               + lam_init).reshape(1)
        qkv, ssm = _proj_ssd(
            x2, row(attn_norm_g[i]), _w_prep(w_in[i].astype(F32).T),
            _ssd_consts(ssm_conv_w[i], ssm_conv_b[i], ssm_dt_bias[i], ssm_a_log[i], ssm_d[i], ssm_norm_g[i]),
            seq)
        att = _attention(qkv, lam, rel_bias_table, row(attn_subln_g[i]), batch, seq, lam_init)
        assert depth == 1
        x2 = _mix_ffn(x2, att, ssm, w_out[i].astype(F32), row(ffn_norm_g[i]),
                      ffn_w_up[i].astype(F32), ffn_conv_w[i].astype(F32), row(ffn_conv_b[i]),
                      ffn_w_down[i].astype(F32), row(final_norm_g), seq)
    return x2.reshape(batch, seq, D_MODEL)
```

```python
import functools
import math

import jax
import jax.numpy as jnp
from jax import lax
from jax.experimental import pallas as pl
from jax.experimental.pallas import tpu as pltpu

F32 = jnp.float32
BF16 = jnp.bfloat16

D_MODEL = 1024
CHUNK = 64
ATT_HEADS = 8
ATT_HEAD_DIM = 64
ATT_V_DIM = 2 * ATT_HEAD_DIM
ATT_WIDTH = ATT_HEADS * ATT_V_DIM
SSM_HEADS = 16
SSM_HEAD_DIM = 64
SSM_WIDTH = SSM_HEADS * SSM_HEAD_DIM
SSM_GROUPS = 2
SSM_STATE = 128
SSM_CONV = 4
SSM_HEADS_PER_GROUP = SSM_HEADS // SSM_GROUPS
SSM_GROUP_WIDTH = SSM_WIDTH // SSM_GROUPS
FFN_DIM = 2816
FFN_CONV = 3
REL_BUCKETS = 32
REL_MAX_DIST = 128
NORM_EPS = 1e-6
SUBLN_EPS = 1e-5
SSM_NORM_EPS = 1e-5
BC_COLS = 2 * SSM_GROUPS * SSM_STATE
MAIN_COLS = 3 * ATT_WIDTH + SSM_WIDTH + SSM_WIDTH + BC_COLS
DT_COLS = SSM_HEADS
LOG2E = math.log2(math.e)

LANES = 128
SUBLANES = 8
VMEM_LIMIT = 56 * 1024 * 1024

PROJ_TM = 512
PROJ_CN = 512
ATT_QB = 256
ATT_KB = 256
SSD_Q = 256
FFN_TM = 512
FFN_CW = 256
HALO = SUBLANES

def _resident(shape):
    nd = len(shape)
    return pl.BlockSpec(shape, lambda *_: (0,) * nd, pipeline_mode=pl.Buffered(1))


def _sigmoid(x):
    return 1.0 / (1.0 + jnp.exp2(x * -LOG2E))


W_PAD_COLS = MAIN_COLS + LANES


def _wprep_body(w_ref, o_ref):
    q_scale = LOG2E * ATT_HEAD_DIM ** -0.5
    o_ref[0:ATT_WIDTH, :] = (w_ref[0:ATT_WIDTH, :] * q_scale).astype(BF16)
    o_ref[ATT_WIDTH:MAIN_COLS + DT_COLS, :] = w_ref[ATT_WIDTH:MAIN_COLS + DT_COLS, :].astype(BF16)
    o_ref[MAIN_COLS + DT_COLS:W_PAD_COLS, :] = jnp.zeros((LANES - DT_COLS, D_MODEL), BF16)


def _w_prep(w_t):
    whole = lambda shape: pl.BlockSpec(shape, lambda: (0, 0))
    return pl.pallas_call(
        _wprep_body,
        in_specs=[whole((MAIN_COLS + DT_COLS, D_MODEL))],
        out_specs=whole((W_PAD_COLS, D_MODEL)),
        out_shape=jax.ShapeDtypeStruct((W_PAD_COLS, D_MODEL), BF16),
        compiler_params=pltpu.CompilerParams(vmem_limit_bytes=VMEM_LIMIT),
        name="w_prep",
    )(w_t)


FAR_BUCKET = REL_BUCKETS // 2 - 1
BIAS_SPAN = 4 * ATT_KB


def _t5_bucket(rel):
    nb = REL_BUCKETS // 2
    max_exact = nb // 2
    bucket = jnp.where(rel > 0, nb, 0)
    n = jnp.abs(rel)
    nf = jnp.maximum(n, 1).astype(F32)
    large = max_exact + (jnp.log(nf / max_exact) / math.log(REL_MAX_DIST / max_exact)
                         * (nb - max_exact)).astype(jnp.int32)
    large = jnp.minimum(large, nb - 1)
    return bucket + jnp.where(n < max_exact, n, large)


def _bias_bucket_row():
    rel = jnp.arange(BIAS_SPAN, dtype=jnp.int32) - 2 * ATT_KB
    return jnp.broadcast_to(_t5_bucket(rel)[None, :], (SUBLANES, BIAS_SPAN))


def _bias_tiles(tbl_ref, idx_ref, h):
    idx = idx_ref[...]
    base = tbl_ref[FAR_BUCKET, h]
    r = jnp.zeros(idx.shape, F32)
    for b in range(REL_BUCKETS):
        r = jnp.where(idx == b, (tbl_ref[b, h] - base) * LOG2E, r)
    rows = jnp.concatenate([r] * (ATT_QB // SUBLANES), axis=0)
    rolled = pltpu.roll(rows, 0, 1, stride=1, stride_axis=0)
    return rolled[:, 2 * ATT_KB:3 * ATT_KB], rolled[:, ATT_KB:2 * ATT_KB]


ATT_STREAMS = 2


def _attn_body(lam_ref, tbl_ref, q_ref, k_ref, v_ref, idx_ref, subg_ref, *rest, lam_init):
    n_cast = (len(rest) - 2) // 2
    cast_in, (o_ref, *cast_out, v1_ref) = rest[:n_cast], rest[n_cast:]
    for src, dst in zip(cast_in, cast_out):
        dst[...] = src[...].astype(BF16)

    qb, kb = ATT_QB, ATT_KB
    seq = q_ref.shape[1]
    lane = lax.broadcasted_iota(jnp.int32, (qb, LANES), 1)
    lam = lam_ref[0]

    @functools.cache
    def bias_and_mask():
        row = lax.broadcasted_iota(jnp.int32, (qb, kb), 0)
        col = lax.broadcasted_iota(jnp.int32, (qb, kb), 1)
        allowed = (col // CHUNK) <= (row // CHUNK)
        b_diag, b_prev = _bias_tiles(tbl_ref, idx_ref, pl.program_id(1))
        return tuple(jnp.concatenate([t, t], axis=0) for t in (allowed, b_diag, b_prev))

    @functools.cache
    def build_v1(s):
        v1_ref[s, :, 0:LANES] = v_ref[s]
        v1_ref[s, :, LANES:2 * LANES] = jnp.ones((seq, LANES), BF16)

    def scores(s, i):
        kvl = (i + 1) * kb
        q = q_ref[s, i * qb:(i + 1) * qb, :]
        zero = jnp.zeros_like(q)
        qs = jnp.concatenate([jnp.where(lane < ATT_HEAD_DIM, q, zero),
                              jnp.where(lane >= ATT_HEAD_DIM, q, zero)], axis=0)
        sc = lax.dot_general(qs, k_ref[s, 0:kvl, :], (((1,), (1,)), ((), ())),
                             preferred_element_type=F32)
        allowed, b_diag, b_prev = bias_and_mask()
        blocks = [sc[:, j * kb:(j + 1) * kb] for j in range(i + 1)]
        blocks[i] = jnp.where(allowed, blocks[i] + b_diag, -1e30)
        if i >= 1:
            blocks[i - 1] = blocks[i - 1] + b_prev
        m = jnp.max(functools.reduce(jnp.maximum, blocks), axis=-1, keepdims=True)
        return blocks, m

    def probs(blocks, m):
        return jnp.concatenate([jnp.exp2(blk - m).astype(BF16) for blk in blocks], axis=1)

    def finish(s, i, p):
        build_v1(s)
        kvl = (i + 1) * kb
        acc = jnp.dot(p, v1_ref[s, 0:kvl, :], preferred_element_type=F32)
        o = acc[:, 0:LANES] / acc[:, LANES:2 * LANES]
        o = o[:qb] - lam * o[qb:]
        o = o * lax.rsqrt(jnp.mean(o * o, axis=-1, keepdims=True) + SUBLN_EPS) * subg_ref[...]
        o_ref[s, i * qb:(i + 1) * qb, :] = (o * (1.0 - lam_init)).astype(BF16)

    order = [(s, i) for i in reversed(range(seq // qb)) for s in range(ATT_STREAMS)]
    scored, exped = {}, {}
    for n in range(len(order) + 2 * ATT_STREAMS):
        if n < len(order):
            scored[n] = scores(*order[n])
        n1, n2 = n - ATT_STREAMS, n - 2 * ATT_STREAMS
        if 0 <= n1 < len(order):
            exped[n1] = probs(*scored.pop(n1))
        if 0 <= n2 < len(order):
            finish(*order[n2], exped.pop(n2))


BF16_ROWS = 2 * SUBLANES


def _attention(qkv, lam, rel_table, sub_g, batch, seq, lam_init, cast_weights):
    assert batch % ATT_STREAMS == 0
    kcol = ATT_WIDTH // LANES
    vcol = 2 * ATT_WIDTH // LANES
    grid = (batch // ATT_STREAMS, ATT_HEADS)
    n_steps = grid[0] * grid[1]
    rows = lambda col0: pl.BlockSpec((None, ATT_STREAMS, seq, LANES), lambda b, h: (b, 0, 0, col0 + h))

    def cast_spec(w):
        n_blocks = next(n for n in range(n_steps, 0, -1)
                        if n_steps % n == 0 and w.shape[0] % (n * BF16_ROWS) == 0)
        return pl.BlockSpec((w.shape[0] // n_blocks, w.shape[1]),
                            lambda b, h: ((b * ATT_HEADS + h) * n_blocks // n_steps, 0))

    cast_specs = [cast_spec(w) for w in cast_weights]
    att, *cast = pl.pallas_call(
        functools.partial(_attn_body, lam_init=lam_init),
        grid=grid,
        in_specs=[
            pl.BlockSpec(memory_space=pltpu.SMEM),
            pl.BlockSpec(memory_space=pltpu.SMEM),
            rows(0), rows(kcol), rows(vcol),
            _resident((SUBLANES, BIAS_SPAN)),
            pl.BlockSpec((1, ATT_V_DIM), lambda b, h: (0, 0)),
        ] + cast_specs,
        out_specs=[rows(0)] + cast_specs,
        out_shape=[jax.ShapeDtypeStruct((batch // ATT_STREAMS, ATT_STREAMS, seq, ATT_WIDTH), BF16)]
        + [jax.ShapeDtypeStruct(w.shape, BF16) for w in cast_weights],
        scratch_shapes=[pltpu.VMEM((ATT_STREAMS, seq, 2 * LANES), BF16)],
        compiler_params=pltpu.CompilerParams(
            dimension_semantics=("arbitrary", "arbitrary"), vmem_limit_bytes=VMEM_LIMIT),
        name="diff_attn",
    )(lam, rel_table.astype(F32), *[qkv.reshape(batch // ATT_STREAMS, ATT_STREAMS, seq, -1)] * 3,
      _bias_bucket_row(), sub_g, *cast_weights)
    return att.reshape(batch * seq, ATT_WIDTH), cast


def _split3(x):
    x1 = x.astype(BF16)
    r1 = x - x1.astype(F32)
    x2 = r1.astype(BF16)
    x3 = (r1 - x2.astype(F32)).astype(BF16)
    return x1, x2, x3


def _ssd_steps(z_ref, xs_ref, bc_ref, dt_ref, cwx_ref, cbx_ref, cwbc_ref, cbbc_ref, dtb_ref,
               alog_ref, dexp_ref, ng_ref, e3_ref, tril_ref, o_ref, halox_ref, halobc_ref,
               state_ref):
    q = SSD_Q

    def conv_silu(raw_ref, halo_ref, w_ref, b_ref):
        x = raw_ref[...].astype(F32)
        first_rows = lax.broadcasted_iota(jnp.int32, (HALO, x.shape[1]), 0)
        acc = b_ref[...] + w_ref[SSM_CONV - 1:SSM_CONV, :] * x
        for j in range(1, SSM_CONV):
            wj = w_ref[SSM_CONV - 1 - j:SSM_CONV - j, :]
            rolled = pltpu.roll(x, j, 0)
            head = jnp.where(first_rows < j, halo_ref[HALO - j:2 * HALO - j, :], rolled[0:HALO])
            acc = acc + wj * jnp.concatenate([head, rolled[HALO:]], axis=0)
        halo_ref[0:HALO, :] = x[q - HALO:q]
        return acc * _sigmoid(acc)

    xs = conv_silu(xs_ref, halox_ref, cwx_ref, cbx_ref)
    yield
    bc = conv_silu(bc_ref, halobc_ref, cwbc_ref, cbbc_ref)
    yield

    dtr = dt_ref[...] + dtb_ref[...]
    dt = jnp.maximum(dtr, 0.0) + jnp.log(1.0 + jnp.exp(-jnp.abs(dtr)))
    a = (-LOG2E * jnp.exp(alog_ref[...])) * dt

    row = lax.broadcasted_iota(jnp.int32, (q, q), 0)
    col = lax.broadcasted_iota(jnp.int32, (q, q), 1)
    causal = col <= row
    tril = tril_ref[...]
    acum = sum(jnp.dot(tril, t, preferred_element_type=F32) for t in _split3(a))

    lane = lax.broadcasted_iota(jnp.int32, (q, LANES), 1)

    def expand(x):
        parts = [jnp.where(lane < SSM_HEADS, t.astype(F32), 0.0) for t in _split3(x)]
        packed = parts[0] + pltpu.roll(parts[1], SSM_HEADS, 1) + pltpu.roll(parts[2], 2 * SSM_HEADS, 1)
        return jnp.dot(packed.astype(BF16), e3_ref[...], preferred_element_type=F32)

    dt_exp = expand(dt)
    acum_exp = expand(acum)
    ea_exp = jnp.exp2(acum_exp)
    last_exp = acum_exp[q - 1:q, :]
    dte_exp = jnp.exp2(last_exp - acum_exp)
    ea_last = ea_exp[q - 1:q, :]

    xdt = xs * dt_exp
    xdt_b = xdt.astype(BF16)
    acum_t = acum.T
    pair_lane = lax.broadcasted_iota(jnp.int32, (q, LANES), 1)
    yield

    for g in range(SSM_GROUPS):
        gsl = slice(g * SSM_GROUP_WIDTH, (g + 1) * SSM_GROUP_WIDTH)
        bg = bc[:, g * SSM_STATE:(g + 1) * SSM_STATE]
        cg = bc[:, (SSM_GROUPS + g) * SSM_STATE:(SSM_GROUPS + g + 1) * SSM_STATE]
        bg_b = bg.astype(BF16)
        cg_b = cg.astype(BF16)
        cb = lax.dot_general(cg_b, bg_b, (((1,), (1,)), ((), ())), preferred_element_type=F32)

        def masked(h):
            seg = acum[:, h:h + 1] - acum_t[h:h + 1, :]
            return (cb * jnp.exp2(jnp.where(causal, seg, -jnp.inf))).astype(BF16)

        pairs = []
        for j in range(SSM_HEADS_PER_GROUP // 2):
            h0 = g * SSM_HEADS_PER_GROUP + 2 * j
            xp = xdt_b[:, h0 * SSM_HEAD_DIM:(h0 + 2) * SSM_HEAD_DIM]
            r0 = jnp.dot(masked(h0), xp, preferred_element_type=F32)
            r1 = jnp.dot(masked(h0 + 1), xp, preferred_element_type=F32)
            pairs.append(jnp.where(pair_lane < SSM_HEAD_DIM, r0, r1))
            if j % 2 == 1:
                yield
        y_diag = jnp.concatenate(pairs, axis=1)

        st = state_ref[g]
        y_off = jnp.dot(cg_b, st.astype(BF16), preferred_element_type=F32) * ea_exp[:, gsl]
        w = (xdt[:, gsl] * dte_exp[:, gsl]).astype(BF16)
        s_new = jnp.dot(bg.T.astype(BF16), w, preferred_element_type=F32)
        state_ref[g] = st * ea_last[:, gsl] + s_new

        y = y_diag + y_off + xs[:, gsl] * dexp_ref[:, gsl]
        zf = z_ref[:, gsl].astype(F32)
        gated = y * (zf * _sigmoid(zf))
        gated = gated * lax.rsqrt(jnp.mean(gated * gated, axis=-1, keepdims=True) + SSM_NORM_EPS)
        o_ref[:, gsl] = (gated * ng_ref[:, gsl]).astype(BF16)
        yield


def _ssd_consts(conv_w, conv_b, dt_bias, a_log, d_skip, norm_g):
    pad = LANES - SSM_HEADS
    dtb = jnp.pad(dt_bias.astype(F32), (0, pad)).reshape(1, LANES)
    alog = jnp.pad(a_log.astype(F32), (0, pad)).reshape(1, LANES)
    dexp = jnp.repeat(d_skip.astype(F32), SSM_HEAD_DIM).reshape(1, SSM_WIDTH)
    r = jnp.arange(LANES)[:, None]
    hcol = (jnp.arange(SSM_WIDTH) // SSM_HEAD_DIM)[None, :]
    e3 = jnp.where((r % SSM_HEADS == hcol) & (r < 3 * SSM_HEADS), 1.0, 0.0).astype(BF16)
    cw = conv_w.astype(F32)
    cb = conv_b.astype(F32).reshape(1, -1)
    tq = jnp.arange(SSD_Q)
    tril = jnp.where(tq[:, None] >= tq[None, :], 1.0, 0.0).astype(BF16)
    in_specs = [
        _resident((SSM_CONV, SSM_WIDTH)),
        _resident((1, SSM_WIDTH)),
        _resident((SSM_CONV, BC_COLS)),
        _resident((1, BC_COLS)),
        _resident((1, LANES)),
        _resident((1, LANES)),
        _resident((1, SSM_WIDTH)),
        _resident((1, SSM_WIDTH)),
        _resident((LANES, SSM_WIDTH)),
        _resident((SSD_Q, SSD_Q)),
    ]
    operands = (cw[:, :SSM_WIDTH], cb[:, :SSM_WIDTH], cw[:, SSM_WIDTH:], cb[:, SSM_WIDTH:], dtb, alog,
                dexp, norm_g.astype(F32).reshape(1, -1), e3, tril)
    return in_specs, operands


SSD_COLS = 2 * SSM_WIDTH + BC_COLS
SCAN_PIECES_AFTER_QKV_DOT = (2, 2, 2, 3, 3, 2)


def _proj_ssd_body(x_ref, g_ref, w_ref, wdt_ref, *rest, tiles_per_seq):
    consts, (qkv_ref, ssm_ref, zxbc_ref, dt_ref, halox_ref, halobc_ref, state_ref) = rest[:-7], rest[-7:]

    @pl.when(pl.program_id(0) % tiles_per_seq == 0)
    def _():
        halox_ref[...] = jnp.zeros(halox_ref.shape, F32)
        halobc_ref[...] = jnp.zeros(halobc_ref.shape, F32)
        state_ref[...] = jnp.zeros(state_ref.shape, F32)

    x = x_ref[...]
    h = x * lax.rsqrt(jnp.mean(x * x, axis=-1, keepdims=True) + NORM_EPS) * g_ref[...]
    h = h.astype(BF16)
    nt = (((1,), (1,)), ((), ()))

    qkv_cols = 3 * ATT_WIDTH

    def project(col):
        y = lax.dot_general(h, w_ref[col:col + PROJ_CN, :], nt, preferred_element_type=F32).astype(BF16)
        if col < qkv_cols:
            qkv_ref[:, col:col + PROJ_CN] = y
        else:
            zxbc_ref[:, col - qkv_cols:col - qkv_cols + PROJ_CN] = y

    def project_dt(_):
        dt_ref[...] = lax.dot_general(h, wdt_ref[...], nt, preferred_element_type=F32)

    def scan_steps():
        for c in range(PROJ_TM // SSD_Q):
            rows = pl.ds(c * SSD_Q, SSD_Q)
            yield from _ssd_steps(
                zxbc_ref.at[rows, pl.ds(0, SSM_WIDTH)],
                zxbc_ref.at[rows, pl.ds(SSM_WIDTH, SSM_WIDTH)],
                zxbc_ref.at[rows, pl.ds(2 * SSM_WIDTH, BC_COLS)],
                dt_ref.at[rows, :],
                *consts, ssm_ref.at[rows, :], halox_ref, halobc_ref, state_ref)

    z0, x0, bc0 = qkv_cols, qkv_cols + SSM_WIDTH, qkv_cols + 2 * SSM_WIDTH
    plan = [(project, x0, 0), (project, x0 + PROJ_CN, 1), (project, bc0, 0), (project_dt, None, 1),
            (project, z0, 1), (project, z0 + PROJ_CN, 1)]
    plan += [(project, col, pieces) for col, pieces in
             zip(range(0, qkv_cols, PROJ_CN), SCAN_PIECES_AFTER_QKV_DOT)]
    scan = scan_steps()
    for emit, col, pieces in plan:
        emit(col)
        for _ in range(pieces):
            next(scan, None)
    for _ in scan:
        pass


def _proj_ssd(x2, g, w_pad, ssd_consts, seq):
    t = x2.shape[0]
    c_specs, c_ops = ssd_consts
    tok = lambda width: pl.BlockSpec((PROJ_TM, width), lambda i: (i, 0))
    return pl.pallas_call(
        functools.partial(_proj_ssd_body, tiles_per_seq=seq // PROJ_TM),
        grid=(t // PROJ_TM,),
        in_specs=[
            tok(D_MODEL),
            _resident((1, D_MODEL)),
            _resident((MAIN_COLS, D_MODEL)),
            pl.BlockSpec((LANES, D_MODEL), lambda i: (MAIN_COLS // LANES, 0), pipeline_mode=pl.Buffered(1)),
        ] + c_specs,
        out_specs=[tok(3 * ATT_WIDTH), tok(SSM_WIDTH)],
        out_shape=[
            jax.ShapeDtypeStruct((t, 3 * ATT_WIDTH), BF16),
            jax.ShapeDtypeStruct((t, SSM_WIDTH), BF16),
        ],
        scratch_shapes=[
            pltpu.VMEM((PROJ_TM, SSD_COLS), BF16),
            pltpu.VMEM((PROJ_TM, LANES), F32),
            pltpu.VMEM((2 * HALO, SSM_WIDTH), F32),
            pltpu.VMEM((2 * HALO, BC_COLS), F32),
            pltpu.VMEM((SSM_GROUPS, SSM_STATE, SSM_GROUP_WIDTH), F32),
        ],
        compiler_params=pltpu.CompilerParams(
            dimension_semantics=("arbitrary",), vmem_limit_bytes=VMEM_LIMIT),
        name="proj_ssd",
    )(x2, g, w_pad, w_pad, *c_ops)


def _ffn_body(x_ref, att_ref, ssm_ref, wo_ref, g2_ref, wup_ref, cw_ref, cb_ref, wdn_ref, gf_ref,
              o_ref, xc_ref, x1p_ref, h2_ref, oc_ref, carry_ref, *, tiles_per_seq):
    i = pl.program_id(0)
    tm, cw = FFN_TM, FFN_CW
    ph = tm // SUBLANES
    n_lane_blocks = D_MODEL // LANES

    @pl.when(i % tiles_per_seq == 0)
    def _():
        carry_ref[...] = jnp.zeros(carry_ref.shape, F32)

    x1 = (x_ref[...]
          + jnp.dot(att_ref[...], wo_ref[0:ATT_WIDTH, :], preferred_element_type=F32)
          + jnp.dot(ssm_ref[...], wo_ref[ATT_WIDTH:, :], preferred_element_type=F32))
    for c in range(n_lane_blocks):
        xc_ref[c] = x1[:, c * LANES:(c + 1) * LANES]
    x1p = jnp.concatenate(
        [jnp.concatenate([xc_ref[c, pl.ds(k, ph, stride=SUBLANES), :] for k in range(SUBLANES)], axis=0)
         for c in range(n_lane_blocks)], axis=1)
    x1p_ref[...] = x1p
    h2_ref[...] = (x1p * lax.rsqrt(jnp.mean(x1p * x1p, axis=-1, keepdims=True) + NORM_EPS)
                   * g2_ref[...]).astype(BF16)

    first_row = lax.broadcasted_iota(jnp.int32, (ph, cw), 0) == 0

    def prev_token(block, carry_slot):
        tail = carry_ref[carry_slot]
        carry_ref[carry_slot] = block[ph - HALO:ph]
        return jnp.where(first_row, jnp.broadcast_to(tail[HALO - 1:HALO, :], (ph, cw)),
                         pltpu.roll(block, 1, 0))

    def up_conv(slot, col):
        u = jnp.dot(h2_ref[...], wup_ref[:, col:col + cw], preferred_element_type=F32)
        blocks = [u[k * ph:(k + 1) * ph] for k in range(SUBLANES)]
        back1 = prev_token(blocks[SUBLANES - 1], 2 * slot)
        back2 = prev_token(blocks[SUBLANES - 2], 2 * slot + 1)
        hist = [back2, back1] + blocks
        w = [cw_ref[t:t + 1, col:col + cw] for t in range(FFN_CONV)]
        b = cb_ref[:, col:col + cw]
        return jnp.concatenate(
            [b + w[2] * hist[k + 2] + w[1] * hist[k + 1] + w[0] * hist[k] for k in range(SUBLANES)],
            axis=0)

    acts = []
    for j in range(FFN_DIM // cw):
        gate = up_conv(2 * j, j * cw)
        val = up_conv(2 * j + 1, FFN_DIM + j * cw)
        acts.append((gate * _sigmoid(gate) * val).astype(BF16))

    x2 = x1p_ref[...] + jnp.dot(jnp.concatenate(acts, axis=1), wdn_ref[...], preferred_element_type=F32)
    out = x2 * lax.rsqrt(jnp.mean(x2 * x2, axis=-1, keepdims=True) + NORM_EPS) * gf_ref[...]
    for c in range(n_lane_blocks):
        for k in range(SUBLANES):
            oc_ref[c, pl.ds(k, ph, stride=SUBLANES), :] = out[k * ph:(k + 1) * ph, c * LANES:(c + 1) * LANES]
    for c in range(n_lane_blocks):
        o_ref[:, c * LANES:(c + 1) * LANES] = oc_ref[c]


def _mix_ffn(x2, att, ssm, w_out, g2, w_up, conv_w, conv_b, w_down, gf, seq):
    t = x2.shape[0]
    n_carry = 2 * (FFN_CONV - 1) * (FFN_DIM // FFN_CW)
    tok = lambda width: pl.BlockSpec((FFN_TM, width), lambda i: (i, 0))
    return pl.pallas_call(
        functools.partial(_ffn_body, tiles_per_seq=seq // FFN_TM),
        grid=(t // FFN_TM,),
        in_specs=[
            tok(D_MODEL), tok(ATT_WIDTH), tok(SSM_WIDTH),
            _resident((ATT_WIDTH + SSM_WIDTH, D_MODEL)),
            _resident((1, D_MODEL)),
            _resident((D_MODEL, 2 * FFN_DIM)),
            _resident((FFN_CONV, 2 * FFN_DIM)),
            _resident((1, 2 * FFN_DIM)),
            _resident((FFN_DIM, D_MODEL)),
            _resident((1, D_MODEL)),
        ],
        out_specs=tok(D_MODEL),
        out_shape=jax.ShapeDtypeStruct((t, D_MODEL), F32),
        scratch_shapes=[
            pltpu.VMEM((D_MODEL // LANES, FFN_TM, LANES), F32),
            pltpu.VMEM((FFN_TM, D_MODEL), F32),
            pltpu.VMEM((FFN_TM, D_MODEL), BF16),
            pltpu.VMEM((D_MODEL // LANES, FFN_TM, LANES), F32),
            pltpu.VMEM((n_carry, HALO, FFN_CW), F32),
        ],
        compiler_params=pltpu.CompilerParams(
            dimension_semantics=("arbitrary",), vmem_limit_bytes=VMEM_LIMIT),
        name="mix_ffn",
    )(x2, att, ssm, w_out, g2, w_up, conv_w, conv_b, w_down, gf)


def kernel(x, rel_bias_table, attn_norm_g, w_in, lambda_q1, lambda_k1, lambda_q2, lambda_k2,
           attn_subln_g, ssm_conv_w, ssm_conv_b, ssm_dt_bias, ssm_a_log, ssm_d, ssm_norm_g,
           w_out, ffn_norm_g, ffn_w_up, ffn_conv_w, ffn_conv_b, ffn_w_down, final_norm_g):
    batch, seq, _ = x.shape
    depth = w_in.shape[0]
    assert seq % max(ATT_QB, SSD_Q, FFN_TM) == 0 and (batch * seq) % PROJ_TM == 0
    x2 = x.reshape(batch * seq, D_MODEL)
    row = lambda v: v.astype(F32).reshape(1, -1)
    for i in range(depth):
        lam_init = 0.8 - 0.6 * math.exp(-0.3 * i)
        lam = (jnp.exp(jnp.sum(lambda_q1[i].astype(F32) * lambda_k1[i].astype(F32)))
               - jnp.exp(jnp.sum(lambda_q2[i].astype(F32) * lambda_k2[i].astype(F32)))
               + lam_init).reshape(1)
        qkv, ssm = _proj_ssd(
            x2, row(attn_norm_g[i]), _w_prep(w_in[i].astype(F32).T),
            _ssd_consts(ssm_conv_w[i], ssm_conv_b[i], ssm_dt_bias[i], ssm_a_log[i], ssm_d[i], ssm_norm_g[i]),
            seq)
        att, (w_out_b, w_up_b, w_down_b) = _attention(
            qkv, lam, rel_bias_table, row(attn_subln_g[i]), batch, seq, lam_init,
            [w_out[i].astype(F32), ffn_w_up[i].astype(F32), ffn_w_down[i].astype(F32)])
        assert depth == 1
        x2 = _mix_ffn(x2, att, ssm, w_out_b, row(ffn_norm_g[i]), w_up_b, ffn_conv_w[i].astype(F32),
                      row(ffn_conv_b[i]), w_down_b, row(final_norm_g), seq)
    return x2.reshape(batch, seq, D_MODEL)
```

```python
import functools
import math

import jax
import jax.numpy as jnp
import numpy as np
from jax import lax
from jax.experimental import pallas as pl
from jax.experimental.pallas import tpu as pltpu

F32 = jnp.float32
BF16 = jnp.bfloat16

D_MODEL = 1024
CHUNK = 64
ATT_HEADS = 8
ATT_HEAD_DIM = 64
ATT_V_DIM = 2 * ATT_HEAD_DIM
ATT_WIDTH = ATT_HEADS * ATT_V_DIM
SSM_HEADS = 16
SSM_HEAD_DIM = 64
SSM_WIDTH = SSM_HEADS * SSM_HEAD_DIM
SSM_GROUPS = 2
SSM_STATE = 128
SSM_CONV = 4
SSM_HEADS_PER_GROUP = SSM_HEADS // SSM_GROUPS
SSM_GROUP_WIDTH = SSM_WIDTH // SSM_GROUPS
FFN_DIM = 2816
FFN_CONV = 3
REL_BUCKETS = 32
REL_MAX_DIST = 128
NORM_EPS = 1e-6
SUBLN_EPS = 1e-5
SSM_NORM_EPS = 1e-5
BC_COLS = 2 * SSM_GROUPS * SSM_STATE
MAIN_COLS = 3 * ATT_WIDTH + SSM_WIDTH + SSM_WIDTH + BC_COLS
DT_COLS = SSM_HEADS
LOG2E = math.log2(math.e)

LANES = 128
SUBLANES = 8
VMEM_LIMIT = 56 * 1024 * 1024

PROJ_TM = 512
PROJ_CN = 512
ATT_QB = 256
ATT_KB = 256
SSD_Q = 256
FFN_TM = 512
FFN_CW = 256
HALO = SUBLANES

def _resident(shape):
    nd = len(shape)
    return pl.BlockSpec(shape, lambda *_: (0,) * nd, pipeline_mode=pl.Buffered(1))


def _sigmoid(x):
    return 1.0 / (1.0 + jnp.exp2(x * -LOG2E))


W_PAD_COLS = MAIN_COLS + LANES


def _wprep_body(w_ref, o_ref):
    q_scale = LOG2E * ATT_HEAD_DIM ** -0.5
    o_ref[0:ATT_WIDTH, :] = (w_ref[0:ATT_WIDTH, :] * q_scale).astype(BF16)
    o_ref[ATT_WIDTH:MAIN_COLS + DT_COLS, :] = w_ref[ATT_WIDTH:MAIN_COLS + DT_COLS, :].astype(BF16)
    o_ref[MAIN_COLS + DT_COLS:W_PAD_COLS, :] = jnp.zeros((LANES - DT_COLS, D_MODEL), BF16)


def _w_prep(w_t):
    whole = lambda shape: pl.BlockSpec(shape, lambda: (0, 0))
    return pl.pallas_call(
        _wprep_body,
        in_specs=[whole((MAIN_COLS + DT_COLS, D_MODEL))],
        out_specs=whole((W_PAD_COLS, D_MODEL)),
        out_shape=jax.ShapeDtypeStruct((W_PAD_COLS, D_MODEL), BF16),
        compiler_params=pltpu.CompilerParams(vmem_limit_bytes=VMEM_LIMIT),
        name="w_prep",
    )(w_t)


FAR_BUCKET = REL_BUCKETS // 2 - 1
BIAS_SPAN = 4 * ATT_KB


def _t5_bucket(rel):
    nb = REL_BUCKETS // 2
    max_exact = nb // 2
    bucket = jnp.where(rel > 0, nb, 0)
    n = jnp.abs(rel)
    nf = jnp.maximum(n, 1).astype(F32)
    large = max_exact + (jnp.log(nf / max_exact) / math.log(REL_MAX_DIST / max_exact)
                         * (nb - max_exact)).astype(jnp.int32)
    large = jnp.minimum(large, nb - 1)
    return bucket + jnp.where(n < max_exact, n, large)


def _bias_bucket_row():
    rel = jnp.arange(BIAS_SPAN, dtype=jnp.int32) - 2 * ATT_KB
    return jnp.broadcast_to(_t5_bucket(rel)[None, :], (SUBLANES, BIAS_SPAN))


def _bias_tiles(tbl_ref, idx_ref, h):
    idx = idx_ref[...]
    base = tbl_ref[FAR_BUCKET, h]
    r = jnp.zeros(idx.shape, F32)
    for b in range(REL_BUCKETS):
        r = jnp.where(idx == b, (tbl_ref[b, h] - base) * LOG2E, r)
    rows = jnp.concatenate([r] * (ATT_QB // SUBLANES), axis=0)
    rolled = pltpu.roll(rows, 0, 1, stride=1, stride_axis=0)
    return rolled[:, 2 * ATT_KB:3 * ATT_KB], rolled[:, ATT_KB:2 * ATT_KB]


ATT_STREAMS = 2


def _attn_body(lam_ref, tbl_ref, q_ref, k_ref, v_ref, idx_ref, subg_ref, *rest, lam_init):
    n_cast = (len(rest) - 2) // 2
    cast_in, (o_ref, *cast_out, v1_ref) = rest[:n_cast], rest[n_cast:]
    for src, dst in zip(cast_in, cast_out):
        dst[...] = src[...].astype(BF16)

    qb, kb = ATT_QB, ATT_KB
    seq = q_ref.shape[1]
    lane = lax.broadcasted_iota(jnp.int32, (qb, LANES), 1)
    lam = lam_ref[0]

    @functools.cache
    def bias_and_mask():
        row = lax.broadcasted_iota(jnp.int32, (qb, kb), 0)
        col = lax.broadcasted_iota(jnp.int32, (qb, kb), 1)
        allowed = (col // CHUNK) <= (row // CHUNK)
        b_diag, b_prev = _bias_tiles(tbl_ref, idx_ref, pl.program_id(1))
        return tuple(jnp.concatenate([t, t], axis=0) for t in (allowed, b_diag, b_prev))

    @functools.cache
    def build_v1(s):
        v1_ref[s, :, 0:LANES] = v_ref[s]
        v1_ref[s, :, LANES:2 * LANES] = jnp.ones((seq, LANES), BF16)

    def scores(s, i):
        kvl = (i + 1) * kb
        q = q_ref[s, i * qb:(i + 1) * qb, :]
        zero = jnp.zeros_like(q)
        qs = jnp.concatenate([jnp.where(lane < ATT_HEAD_DIM, q, zero),
                              jnp.where(lane >= ATT_HEAD_DIM, q, zero)], axis=0)
        sc = lax.dot_general(qs, k_ref[s, 0:kvl, :], (((1,), (1,)), ((), ())),
                             preferred_element_type=F32)
        allowed, b_diag, b_prev = bias_and_mask()
        blocks = [sc[:, j * kb:(j + 1) * kb] for j in range(i + 1)]
        blocks[i] = jnp.where(allowed, blocks[i] + b_diag, -1e30)
        if i >= 1:
            blocks[i - 1] = blocks[i - 1] + b_prev
        m = jnp.max(functools.reduce(jnp.maximum, blocks), axis=-1, keepdims=True)
        return blocks, m

    def probs(blocks, m):
        return jnp.concatenate([jnp.exp2(blk - m).astype(BF16) for blk in blocks], axis=1)

    def finish(s, i, p):
        build_v1(s)
        kvl = (i + 1) * kb
        acc = jnp.dot(p, v1_ref[s, 0:kvl, :], preferred_element_type=F32)
        o = acc[:, 0:LANES] / acc[:, LANES:2 * LANES]
        o = o[:qb] - lam * o[qb:]
        o = o * lax.rsqrt(jnp.mean(o * o, axis=-1, keepdims=True) + SUBLN_EPS) * subg_ref[...]
        o_ref[s, i * qb:(i + 1) * qb, :] = (o * (1.0 - lam_init)).astype(BF16)

    order = [(s, i) for i in reversed(range(seq // qb)) for s in range(ATT_STREAMS)]
    scored, exped = {}, {}
    for n in range(len(order) + 2 * ATT_STREAMS):
        if n < len(order):
            scored[n] = scores(*order[n])
        n1, n2 = n - ATT_STREAMS, n - 2 * ATT_STREAMS
        if 0 <= n1 < len(order):
            exped[n1] = probs(*scored.pop(n1))
        if 0 <= n2 < len(order):
            finish(*order[n2], exped.pop(n2))


BF16_ROWS = 2 * SUBLANES


def _attention(qkv, lam, rel_table, sub_g, batch, seq, lam_init, cast_weights):
    assert batch % ATT_STREAMS == 0
    kcol = ATT_WIDTH // LANES
    vcol = 2 * ATT_WIDTH // LANES
    grid = (batch // ATT_STREAMS, ATT_HEADS)
    n_steps = grid[0] * grid[1]
    rows = lambda col0: pl.BlockSpec((None, ATT_STREAMS, seq, LANES), lambda b, h: (b, 0, 0, col0 + h))

    def cast_spec(w):
        n_blocks = next(n for n in range(n_steps, 0, -1)
                        if n_steps % n == 0 and w.shape[0] % (n * BF16_ROWS) == 0)
        return pl.BlockSpec((w.shape[0] // n_blocks, w.shape[1]),
                            lambda b, h: ((b * ATT_HEADS + h) * n_blocks // n_steps, 0))

    cast_specs = [cast_spec(w) for w in cast_weights]
    att, *cast = pl.pallas_call(
        functools.partial(_attn_body, lam_init=lam_init),
        grid=grid,
        in_specs=[
            pl.BlockSpec(memory_space=pltpu.SMEM),
            pl.BlockSpec(memory_space=pltpu.SMEM),
            rows(0), rows(kcol), rows(vcol),
            _resident((SUBLANES, BIAS_SPAN)),
            pl.BlockSpec((1, ATT_V_DIM), lambda b, h: (0, 0)),
        ] + cast_specs,
        out_specs=[rows(0)] + cast_specs,
        out_shape=[jax.ShapeDtypeStruct((batch // ATT_STREAMS, ATT_STREAMS, seq, ATT_WIDTH), BF16)]
        + [jax.ShapeDtypeStruct(w.shape, BF16) for w in cast_weights],
        scratch_shapes=[pltpu.VMEM((ATT_STREAMS, seq, 2 * LANES), BF16)],
        compiler_params=pltpu.CompilerParams(
            dimension_semantics=("arbitrary", "arbitrary"), vmem_limit_bytes=VMEM_LIMIT),
        name="diff_attn",
    )(lam, rel_table.astype(F32), *[qkv.reshape(batch // ATT_STREAMS, ATT_STREAMS, seq, -1)] * 3,
      _bias_bucket_row(), sub_g, *cast_weights)
    return att.reshape(batch * seq, ATT_WIDTH), cast


def _split3(x):
    x1 = x.astype(BF16)
    r1 = x - x1.astype(F32)
    x2 = r1.astype(BF16)
    x3 = (r1 - x2.astype(F32)).astype(BF16)
    return x1, x2, x3


def _ssd_steps(z_ref, xs_ref, bc_ref, dt_ref, cw_ref, cb_ref, dtb_ref, alog_ref, dexp_ref, ng_ref,
               e3_ref, tril_ref, o_ref, halox_ref, halobc_ref, state_ref):
    q = SSD_Q
    x_cols, bc_cols = pl.ds(0, SSM_WIDTH), pl.ds(SSM_WIDTH, BC_COLS)
    cwx_ref, cbx_ref = cw_ref.at[:, x_cols], cb_ref.at[:, x_cols]
    cwbc_ref, cbbc_ref = cw_ref.at[:, bc_cols], cb_ref.at[:, bc_cols]

    def conv_silu(raw_ref, halo_ref, w_ref, b_ref):
        x = raw_ref[...].astype(F32)
        first_rows = lax.broadcasted_iota(jnp.int32, (HALO, x.shape[1]), 0)
        acc = b_ref[...] + w_ref[SSM_CONV - 1:SSM_CONV, :] * x
        for j in range(1, SSM_CONV):
            wj = w_ref[SSM_CONV - 1 - j:SSM_CONV - j, :]
            rolled = pltpu.roll(x, j, 0)
            head = jnp.where(first_rows < j, halo_ref[HALO - j:2 * HALO - j, :], rolled[0:HALO])
            acc = acc + wj * jnp.concatenate([head, rolled[HALO:]], axis=0)
        halo_ref[0:HALO, :] = x[q - HALO:q]
        return acc * _sigmoid(acc)

    xs = conv_silu(xs_ref, halox_ref, cwx_ref, cbx_ref)
    yield
    bc = conv_silu(bc_ref, halobc_ref, cwbc_ref, cbbc_ref)
    yield

    dtr = dt_ref[...] + dtb_ref[...]
    dt = jnp.maximum(dtr, 0.0) + jnp.log(1.0 + jnp.exp(-jnp.abs(dtr)))
    a = (-LOG2E * jnp.exp(alog_ref[...])) * dt

    row = lax.broadcasted_iota(jnp.int32, (q, q), 0)
    col = lax.broadcasted_iota(jnp.int32, (q, q), 1)
    causal = col <= row
    tril = tril_ref[...]
    acum = sum(jnp.dot(tril, t, preferred_element_type=F32) for t in _split3(a))

    lane = lax.broadcasted_iota(jnp.int32, (q, LANES), 1)

    def expand(x):
        parts = [jnp.where(lane < SSM_HEADS, t.astype(F32), 0.0) for t in _split3(x)]
        packed = parts[0] + pltpu.roll(parts[1], SSM_HEADS, 1) + pltpu.roll(parts[2], 2 * SSM_HEADS, 1)
        return jnp.dot(packed.astype(BF16), e3_ref[...], preferred_element_type=F32)

    dt_exp = expand(dt)
    acum_exp = expand(acum)
    ea_exp = jnp.exp2(acum_exp)
    last_exp = acum_exp[q - 1:q, :]
    dte_exp = jnp.exp2(last_exp - acum_exp)
    ea_last = ea_exp[q - 1:q, :]

    xdt = xs * dt_exp
    xdt_b = xdt.astype(BF16)
    acum_t = acum.T
    pair_lane = lax.broadcasted_iota(jnp.int32, (q, LANES), 1)
    yield

    for g in range(SSM_GROUPS):
        gsl = slice(g * SSM_GROUP_WIDTH, (g + 1) * SSM_GROUP_WIDTH)
        bg = bc[:, g * SSM_STATE:(g + 1) * SSM_STATE]
        cg = bc[:, (SSM_GROUPS + g) * SSM_STATE:(SSM_GROUPS + g + 1) * SSM_STATE]
        bg_b = bg.astype(BF16)
        cg_b = cg.astype(BF16)
        cb = lax.dot_general(cg_b, bg_b, (((1,), (1,)), ((), ())), preferred_element_type=F32)

        def masked(h):
            seg = acum[:, h:h + 1] - acum_t[h:h + 1, :]
            return (cb * jnp.exp2(jnp.where(causal, seg, -jnp.inf))).astype(BF16)

        pairs = []
        for j in range(SSM_HEADS_PER_GROUP // 2):
            h0 = g * SSM_HEADS_PER_GROUP + 2 * j
            xp = xdt_b[:, h0 * SSM_HEAD_DIM:(h0 + 2) * SSM_HEAD_DIM]
            r0 = jnp.dot(masked(h0), xp, preferred_element_type=F32)
            r1 = jnp.dot(masked(h0 + 1), xp, preferred_element_type=F32)
            pairs.append(jnp.where(pair_lane < SSM_HEAD_DIM, r0, r1))
            if j % 2 == 1:
                yield
        y_diag = jnp.concatenate(pairs, axis=1)

        st = state_ref[g]
        y_off = jnp.dot(cg_b, st.astype(BF16), preferred_element_type=F32) * ea_exp[:, gsl]
        w = (xdt[:, gsl] * dte_exp[:, gsl]).astype(BF16)
        s_new = jnp.dot(bg.T.astype(BF16), w, preferred_element_type=F32)
        state_ref[g] = st * ea_last[:, gsl] + s_new

        y = y_diag + y_off + xs[:, gsl] * dexp_ref[:, gsl]
        zf = z_ref[:, gsl].astype(F32)
        gated = y * (zf * _sigmoid(zf))
        gated = gated * lax.rsqrt(jnp.mean(gated * gated, axis=-1, keepdims=True) + SSM_NORM_EPS)
        o_ref[:, gsl] = (gated * ng_ref[:, gsl]).astype(BF16)
        yield


def _ssd_consts(conv_w, conv_b, dt_bias, a_log, d_skip, norm_g):
    pad = LANES - SSM_HEADS
    dtb = jnp.pad(dt_bias.astype(F32), (0, pad)).reshape(1, LANES)
    alog = jnp.pad(a_log.astype(F32), (0, pad)).reshape(1, LANES)
    dexp = jnp.repeat(d_skip.astype(F32), SSM_HEAD_DIM).reshape(1, SSM_WIDTH)
    r = np.arange(LANES)[:, None]
    hcol = (np.arange(SSM_WIDTH) // SSM_HEAD_DIM)[None, :]
    e3 = jnp.asarray(((r % SSM_HEADS == hcol) & (r < 3 * SSM_HEADS)).astype(np.float32), dtype=BF16)
    tq = np.arange(SSD_Q)
    tril = jnp.asarray((tq[:, None] >= tq[None, :]).astype(np.float32), dtype=BF16)
    in_specs = [
        _resident((SSM_CONV, SSM_WIDTH + BC_COLS)),
        _resident((1, SSM_WIDTH + BC_COLS)),
        _resident((1, LANES)),
        _resident((1, LANES)),
        _resident((1, SSM_WIDTH)),
        _resident((1, SSM_WIDTH)),
        _resident((LANES, SSM_WIDTH)),
        _resident((SSD_Q, SSD_Q)),
    ]
    operands = (conv_w.astype(F32), conv_b.astype(F32).reshape(1, -1), dtb, alog, dexp,
                norm_g.astype(F32).reshape(1, -1), e3, tril)
    return in_specs, operands


SSD_COLS = 2 * SSM_WIDTH + BC_COLS
SCAN_PIECES_AFTER_QKV_DOT = (2, 2, 2, 3, 3, 2)


def _proj_ssd_body(x_ref, g_ref, w_ref, wdt_ref, *rest, tiles_per_seq):
    consts, (qkv_ref, ssm_ref, zxbc_ref, dt_ref, halox_ref, halobc_ref, state_ref) = rest[:-7], rest[-7:]

    @pl.when(pl.program_id(0) % tiles_per_seq == 0)
    def _():
        halox_ref[...] = jnp.zeros(halox_ref.shape, F32)
        halobc_ref[...] = jnp.zeros(halobc_ref.shape, F32)
        state_ref[...] = jnp.zeros(state_ref.shape, F32)

    x = x_ref[...]
    h = x * lax.rsqrt(jnp.mean(x * x, axis=-1, keepdims=True) + NORM_EPS) * g_ref[...]
    h = h.astype(BF16)
    nt = (((1,), (1,)), ((), ()))

    qkv_cols = 3 * ATT_WIDTH

    def project(col):
        y = lax.dot_general(h, w_ref[col:col + PROJ_CN, :], nt, preferred_element_type=F32).astype(BF16)
        if col < qkv_cols:
            qkv_ref[:, col:col + PROJ_CN] = y
        else:
            zxbc_ref[:, col - qkv_cols:col - qkv_cols + PROJ_CN] = y

    def project_dt(_):
        dt_ref[...] = lax.dot_general(h, wdt_ref[...], nt, preferred_element_type=F32)

    def scan_steps():
        for c in range(PROJ_TM // SSD_Q):
            rows = pl.ds(c * SSD_Q, SSD_Q)
            yield from _ssd_steps(
                zxbc_ref.at[rows, pl.ds(0, SSM_WIDTH)],
                zxbc_ref.at[rows, pl.ds(SSM_WIDTH, SSM_WIDTH)],
                zxbc_ref.at[rows, pl.ds(2 * SSM_WIDTH, BC_COLS)],
                dt_ref.at[rows, :],
                *consts, ssm_ref.at[rows, :], halox_ref, halobc_ref, state_ref)

    z0, x0, bc0 = qkv_cols, qkv_cols + SSM_WIDTH, qkv_cols + 2 * SSM_WIDTH
    plan = [(project, x0, 0), (project, x0 + PROJ_CN, 1), (project, bc0, 0), (project_dt, None, 1),
            (project, z0, 1), (project, z0 + PROJ_CN, 1)]
    plan += [(project, col, pieces) for col, pieces in
             zip(range(0, qkv_cols, PROJ_CN), SCAN_PIECES_AFTER_QKV_DOT)]
    scan = scan_steps()
    for emit, col, pieces in plan:
        emit(col)
        for _ in range(pieces):
            next(scan, None)
    for _ in scan:
        pass


def _proj_ssd(x2, g, w_pad, ssd_consts, seq):
    t = x2.shape[0]
    c_specs, c_ops = ssd_consts
    tok = lambda width: pl.BlockSpec((PROJ_TM, width), lambda i: (i, 0))
    return pl.pallas_call(
        functools.partial(_proj_ssd_body, tiles_per_seq=seq // PROJ_TM),
        grid=(t // PROJ_TM,),
        in_specs=[
            tok(D_MODEL),
            _resident((1, D_MODEL)),
            _resident((MAIN_COLS, D_MODEL)),
            pl.BlockSpec((LANES, D_MODEL), lambda i: (MAIN_COLS // LANES, 0), pipeline_mode=pl.Buffered(1)),
        ] + c_specs,
        out_specs=[tok(3 * ATT_WIDTH), tok(SSM_WIDTH)],
        out_shape=[
            jax.ShapeDtypeStruct((t, 3 * ATT_WIDTH), BF16),
            jax.ShapeDtypeStruct((t, SSM_WIDTH), BF16),
        ],
        scratch_shapes=[
            pltpu.VMEM((PROJ_TM, SSD_COLS), BF16),
            pltpu.VMEM((PROJ_TM, LANES), F32),
            pltpu.VMEM((2 * HALO, SSM_WIDTH), F32),
            pltpu.VMEM((2 * HALO, BC_COLS), F32),
            pltpu.VMEM((SSM_GROUPS, SSM_STATE, SSM_GROUP_WIDTH), F32),
        ],
        compiler_params=pltpu.CompilerParams(
            dimension_semantics=("arbitrary",), vmem_limit_bytes=VMEM_LIMIT),
        name="proj_ssd",
    )(x2, g, w_pad, w_pad, *c_ops)


def _ffn_body(x_ref, att_ref, ssm_ref, wo_ref, g2_ref, wup_ref, cw_ref, cb_ref, wdn_ref, gf_ref,
              o_ref, xc_ref, x1p_ref, h2_ref, oc_ref, carry_ref, *, tiles_per_seq):
    i = pl.program_id(0)
    tm, cw = FFN_TM, FFN_CW
    ph = tm // SUBLANES
    n_lane_blocks = D_MODEL // LANES

    @pl.when(i % tiles_per_seq == 0)
    def _():
        carry_ref[...] = jnp.zeros(carry_ref.shape, F32)

    x1 = (x_ref[...]
          + jnp.dot(att_ref[...], wo_ref[0:ATT_WIDTH, :], preferred_element_type=F32)
          + jnp.dot(ssm_ref[...], wo_ref[ATT_WIDTH:, :], preferred_element_type=F32))
    for c in range(n_lane_blocks):
        xc_ref[c] = x1[:, c * LANES:(c + 1) * LANES]
    x1p = jnp.concatenate(
        [jnp.concatenate([xc_ref[c, pl.ds(k, ph, stride=SUBLANES), :] for k in range(SUBLANES)], axis=0)
         for c in range(n_lane_blocks)], axis=1)
    x1p_ref[...] = x1p
    h2_ref[...] = (x1p * lax.rsqrt(jnp.mean(x1p * x1p, axis=-1, keepdims=True) + NORM_EPS)
                   * g2_ref[...]).astype(BF16)

    first_row = lax.broadcasted_iota(jnp.int32, (ph, cw), 0) == 0

    def prev_token(block, carry_slot):
        tail = carry_ref[carry_slot]
        carry_ref[carry_slot] = block[ph - HALO:ph]
        return jnp.where(first_row, jnp.broadcast_to(tail[HALO - 1:HALO, :], (ph, cw)),
                         pltpu.roll(block, 1, 0))

    def up_conv(slot, col):
        u = jnp.dot(h2_ref[...], wup_ref[:, col:col + cw], preferred_element_type=F32)
        blocks = [u[k * ph:(k + 1) * ph] for k in range(SUBLANES)]
        back1 = prev_token(blocks[SUBLANES - 1], 2 * slot)
        back2 = prev_token(blocks[SUBLANES - 2], 2 * slot + 1)
        hist = [back2, back1] + blocks
        w = [cw_ref[t:t + 1, col:col + cw] for t in range(FFN_CONV)]
        b = cb_ref[:, col:col + cw]
        return jnp.concatenate(
            [b + w[2] * hist[k + 2] + w[1] * hist[k + 1] + w[0] * hist[k] for k in range(SUBLANES)],
            axis=0)

    acts = []
    for j in range(FFN_DIM // cw):
        gate = up_conv(2 * j, j * cw)
        val = up_conv(2 * j + 1, FFN_DIM + j * cw)
        acts.append((gate * _sigmoid(gate) * val).astype(BF16))

    x2 = x1p_ref[...] + jnp.dot(jnp.concatenate(acts, axis=1), wdn_ref[...], preferred_element_type=F32)
    out = x2 * lax.rsqrt(jnp.mean(x2 * x2, axis=-1, keepdims=True) + NORM_EPS) * gf_ref[...]
    for c in range(n_lane_blocks):
        for k in range(SUBLANES):
            oc_ref[c, pl.ds(k, ph, stride=SUBLANES), :] = out[k * ph:(k + 1) * ph, c * LANES:(c + 1) * LANES]
    for c in range(n_lane_blocks):
        o_ref[:, c * LANES:(c + 1) * LANES] = oc_ref[c]


def _mix_ffn(x2, att, ssm, w_out, g2, w_up, conv_w, conv_b, w_down, gf, seq):
    t = x2.shape[0]
    n_carry = 2 * (FFN_CONV - 1) * (FFN_DIM // FFN_CW)
    tok = lambda width: pl.BlockSpec((FFN_TM, width), lambda i: (i, 0))
    return pl.pallas_call(
        functools.partial(_ffn_body, tiles_per_seq=seq // FFN_TM),
        grid=(t // FFN_TM,),
        in_specs=[
            tok(D_MODEL), tok(ATT_WIDTH), tok(SSM_WIDTH),
            _resident((ATT_WIDTH + SSM_WIDTH, D_MODEL)),
            _resident((1, D_MODEL)),
            _resident((D_MODEL, 2 * FFN_DIM)),
            _resident((FFN_CONV, 2 * FFN_DIM)),
            _resident((1, 2 * FFN_DIM)),
            _resident((FFN_DIM, D_MODEL)),
            _resident((1, D_MODEL)),
        ],
        out_specs=tok(D_MODEL),
        out_shape=jax.ShapeDtypeStruct((t, D_MODEL), F32),
        scratch_shapes=[
            pltpu.VMEM((D_MODEL // LANES, FFN_TM, LANES), F32),
            pltpu.VMEM((FFN_TM, D_MODEL), F32),
            pltpu.VMEM((FFN_TM, D_MODEL), BF16),
            pltpu.VMEM((D_MODEL // LANES, FFN_TM, LANES), F32),
            pltpu.VMEM((n_carry, HALO, FFN_CW), F32),
        ],
        compiler_params=pltpu.CompilerParams(
            dimension_semantics=("arbitrary",), vmem_limit_bytes=VMEM_LIMIT),
        name="mix_ffn",
    )(x2, att, ssm, w_out, g2, w_up, conv_w, conv_b, w_down, gf)


def kernel(x, rel_bias_table, attn_norm_g, w_in, lambda_q1, lambda_k1, lambda_q2, lambda_k2,
           attn_subln_g, ssm_conv_w, ssm_conv_b, ssm_dt_bias, ssm_a_log, ssm_d, ssm_norm_g,
           w_out, ffn_norm_g, ffn_w_up, ffn_conv_w, ffn_conv_b, ffn_w_down, final_norm_g):
    batch, seq, _ = x.shape
    depth = w_in.shape[0]
    assert seq % max(ATT_QB, SSD_Q, FFN_TM) == 0 and (batch * seq) % PROJ_TM == 0
    x2 = x.reshape(batch * seq, D_MODEL)
    row = lambda v: v.astype(F32).reshape(1, -1)
    for i in range(depth):
        lam_init = 0.8 - 0.6 * math.exp(-0.3 * i)
        lam = (jnp.exp(jnp.sum(lambda_q1[i].astype(F32) * lambda_k1[i].astype(F32)))
               - jnp.exp(jnp.sum(lambda_q2[i].astype(F32) * lambda_k2[i].astype(F32)))
               + lam_init).reshape(1)
        qkv, ssm = _proj_ssd(
            x2, row(attn_norm_g[i]), _w_prep(w_in[i].astype(F32).T),
            _ssd_consts(ssm_conv_w[i], ssm_conv_b[i], ssm_dt_bias[i], ssm_a_log[i], ssm_d[i], ssm_norm_g[i]),
            seq)
        att, (w_out_b, w_up_b, w_down_b) = _attention(
            qkv, lam, rel_bias_table, row(attn_subln_g[i]), batch, seq, lam_init,
            [w_out[i].astype(F32), ffn_w_up[i].astype(F32), ffn_w_down[i].astype(F32)])
        assert depth == 1
        x2 = _mix_ffn(x2, att, ssm, w_out_b, row(ffn_norm_g[i]), w_up_b, ffn_conv_w[i].astype(F32),
                      row(ffn_conv_b[i]), w_down_b, row(final_norm_g), seq)
    return x2.reshape(batch, seq, D_MODEL)
```

```python
import functools
import math

import jax
import jax.numpy as jnp
import numpy as np
from jax import lax
from jax.experimental import pallas as pl
from jax.experimental.pallas import tpu as pltpu

F32 = jnp.float32
BF16 = jnp.bfloat16

D_MODEL = 1024
CHUNK = 64
ATT_HEADS = 8
ATT_HEAD_DIM = 64
ATT_V_DIM = 2 * ATT_HEAD_DIM
ATT_WIDTH = ATT_HEADS * ATT_V_DIM
SSM_HEADS = 16
SSM_HEAD_DIM = 64
SSM_WIDTH = SSM_HEADS * SSM_HEAD_DIM
SSM_GROUPS = 2
SSM_STATE = 128
SSM_CONV = 4
SSM_HEADS_PER_GROUP = SSM_HEADS // SSM_GROUPS
SSM_GROUP_WIDTH = SSM_WIDTH // SSM_GROUPS
FFN_DIM = 2816
FFN_CONV = 3
REL_BUCKETS = 32
REL_MAX_DIST = 128
NORM_EPS = 1e-6
SUBLN_EPS = 1e-5
SSM_NORM_EPS = 1e-5
BC_COLS = 2 * SSM_GROUPS * SSM_STATE
MAIN_COLS = 3 * ATT_WIDTH + SSM_WIDTH + SSM_WIDTH + BC_COLS
DT_COLS = SSM_HEADS
LOG2E = math.log2(math.e)

LANES = 128
SUBLANES = 8
VMEM_LIMIT = 56 * 1024 * 1024

PROJ_TM = 512
PROJ_CN = 512
ATT_QB = 256
ATT_KB = 256
SSD_Q = 256
FFN_TM = 512
FFN_CW = 256
HALO = SUBLANES

def _resident(shape):
    nd = len(shape)
    return pl.BlockSpec(shape, lambda *_: (0,) * nd, pipeline_mode=pl.Buffered(1))


def _sigmoid(x):
    return 1.0 / (1.0 + jnp.exp2(x * -LOG2E))


W_PAD_COLS = MAIN_COLS + LANES


def _wprep_body(w_ref, o_ref):
    q_scale = LOG2E * ATT_HEAD_DIM ** -0.5
    o_ref[0:ATT_WIDTH, :] = (w_ref[0:ATT_WIDTH, :] * q_scale).astype(BF16)
    o_ref[ATT_WIDTH:MAIN_COLS + DT_COLS, :] = w_ref[ATT_WIDTH:MAIN_COLS + DT_COLS, :].astype(BF16)
    o_ref[MAIN_COLS + DT_COLS:W_PAD_COLS, :] = jnp.zeros((LANES - DT_COLS, D_MODEL), BF16)


def _w_prep(w_t):
    whole = lambda shape: pl.BlockSpec(shape, lambda: (0, 0))
    return pl.pallas_call(
        _wprep_body,
        in_specs=[whole((MAIN_COLS + DT_COLS, D_MODEL))],
        out_specs=whole((W_PAD_COLS, D_MODEL)),
        out_shape=jax.ShapeDtypeStruct((W_PAD_COLS, D_MODEL), BF16),
        compiler_params=pltpu.CompilerParams(vmem_limit_bytes=VMEM_LIMIT),
        name="w_prep",
    )(w_t)


FAR_BUCKET = REL_BUCKETS // 2 - 1
BIAS_SPAN = 4 * ATT_KB


def _t5_bucket(rel):
    nb = REL_BUCKETS // 2
    max_exact = nb // 2
    bucket = jnp.where(rel > 0, nb, 0)
    n = jnp.abs(rel)
    nf = jnp.maximum(n, 1).astype(F32)
    large = max_exact + (jnp.log(nf / max_exact) / math.log(REL_MAX_DIST / max_exact)
                         * (nb - max_exact)).astype(jnp.int32)
    large = jnp.minimum(large, nb - 1)
    return bucket + jnp.where(n < max_exact, n, large)


def _bias_bucket_row():
    rel = jnp.arange(BIAS_SPAN, dtype=jnp.int32) - 2 * ATT_KB
    return jnp.broadcast_to(_t5_bucket(rel)[None, :], (SUBLANES, BIAS_SPAN))


def _bias_tiles(tbl_ref, idx_ref, h):
    idx = idx_ref[...]
    base = tbl_ref[FAR_BUCKET, h]
    r = jnp.zeros(idx.shape, F32)
    for b in range(REL_BUCKETS):
        r = jnp.where(idx == b, (tbl_ref[b, h] - base) * LOG2E, r)
    rows = jnp.concatenate([r] * (ATT_QB // SUBLANES), axis=0)
    rolled = pltpu.roll(rows, 0, 1, stride=1, stride_axis=0)
    return rolled[:, 2 * ATT_KB:3 * ATT_KB], rolled[:, ATT_KB:2 * ATT_KB]


ATT_STREAMS = 2


def _attn_body(lam_ref, tbl_ref, q_ref, k_ref, v_ref, idx_ref, subg_ref, *rest, lam_init):
    n_cast = (len(rest) - 2) // 2
    cast_in, (o_ref, *cast_out, v1_ref) = rest[:n_cast], rest[n_cast:]
    for src, dst in zip(cast_in, cast_out):
        dst[...] = src[...].astype(BF16)

    qb, kb = ATT_QB, ATT_KB
    seq = q_ref.shape[1]
    lane = lax.broadcasted_iota(jnp.int32, (qb, LANES), 1)
    lam = lam_ref[0]

    @functools.cache
    def bias_and_mask():
        row = lax.broadcasted_iota(jnp.int32, (qb, kb), 0)
        col = lax.broadcasted_iota(jnp.int32, (qb, kb), 1)
        allowed = (col // CHUNK) <= (row // CHUNK)
        b_diag, b_prev = _bias_tiles(tbl_ref, idx_ref, pl.program_id(1))
        return tuple(jnp.concatenate([t, t], axis=0) for t in (allowed, b_diag, b_prev))

    @functools.cache
    def build_v1(s):
        v1_ref[s, :, 0:LANES] = v_ref[s]
        v1_ref[s, :, LANES:2 * LANES] = jnp.ones((seq, LANES), BF16)

    def scores(s, i):
        kvl = (i + 1) * kb
        q = q_ref[s, i * qb:(i + 1) * qb, :]
        zero = jnp.zeros_like(q)
        qs = jnp.concatenate([jnp.where(lane < ATT_HEAD_DIM, q, zero),
                              jnp.where(lane >= ATT_HEAD_DIM, q, zero)], axis=0)
        sc = lax.dot_general(qs, k_ref[s, 0:kvl, :], (((1,), (1,)), ((), ())),
                             preferred_element_type=F32)
        allowed, b_diag, b_prev = bias_and_mask()
        blocks = [sc[:, j * kb:(j + 1) * kb] for j in range(i + 1)]
        blocks[i] = jnp.where(allowed, blocks[i] + b_diag, -1e30)
        if i >= 1:
            blocks[i - 1] = blocks[i - 1] + b_prev
        m = jnp.max(functools.reduce(jnp.maximum, blocks), axis=-1, keepdims=True)
        return blocks, m

    def probs(blocks, m):
        return jnp.concatenate([jnp.exp2(blk - m).astype(BF16) for blk in blocks], axis=1)

    def finish(s, i, p):
        build_v1(s)
        kvl = (i + 1) * kb
        acc = jnp.dot(p, v1_ref[s, 0:kvl, :], preferred_element_type=F32)
        o = acc[:, 0:LANES] / acc[:, LANES:2 * LANES]
        o = o[:qb] - lam * o[qb:]
        o = o * lax.rsqrt(jnp.mean(o * o, axis=-1, keepdims=True) + SUBLN_EPS) * subg_ref[...]
        o_ref[s, i * qb:(i + 1) * qb, :] = (o * (1.0 - lam_init)).astype(BF16)

    order = [(s, i) for i in reversed(range(seq // qb)) for s in range(ATT_STREAMS)]
    scored, exped = {}, {}
    for n in range(len(order) + 2 * ATT_STREAMS):
        if n < len(order):
            scored[n] = scores(*order[n])
        n1, n2 = n - ATT_STREAMS, n - 2 * ATT_STREAMS
        if 0 <= n1 < len(order):
            exped[n1] = probs(*scored.pop(n1))
        if 0 <= n2 < len(order):
            finish(*order[n2], exped.pop(n2))


BF16_ROWS = 2 * SUBLANES


def _attention(qkv, lam, rel_table, sub_g, batch, seq, lam_init, cast_weights):
    assert batch % ATT_STREAMS == 0
    kcol = ATT_WIDTH // LANES
    vcol = 2 * ATT_WIDTH // LANES
    grid = (batch // ATT_STREAMS, ATT_HEADS)
    n_steps = grid[0] * grid[1]
    rows = lambda col0: pl.BlockSpec((None, ATT_STREAMS, seq, LANES), lambda b, h: (b, 0, 0, col0 + h))

    def cast_spec(w):
        n_blocks = next(n for n in range(n_steps, 0, -1)
                        if n_steps % n == 0 and w.shape[0] % (n * BF16_ROWS) == 0)
        return pl.BlockSpec((w.shape[0] // n_blocks, w.shape[1]),
                            lambda b, h: ((b * ATT_HEADS + h) * n_blocks // n_steps, 0))

    cast_specs = [cast_spec(w) for w in cast_weights]
    att, *cast = pl.pallas_call(
        functools.partial(_attn_body, lam_init=lam_init),
        grid=grid,
        in_specs=[
            pl.BlockSpec(memory_space=pltpu.SMEM),
            pl.BlockSpec(memory_space=pltpu.SMEM),
            rows(0), rows(kcol), rows(vcol),
            _resident((SUBLANES, BIAS_SPAN)),
            pl.BlockSpec((1, ATT_V_DIM), lambda b, h: (0, 0)),
        ] + cast_specs,
        out_specs=[rows(0)] + cast_specs,
        out_shape=[jax.ShapeDtypeStruct((batch // ATT_STREAMS, ATT_STREAMS, seq, ATT_WIDTH), BF16)]
        + [jax.ShapeDtypeStruct(w.shape, BF16) for w in cast_weights],
        scratch_shapes=[pltpu.VMEM((ATT_STREAMS, seq, 2 * LANES), BF16)],
        compiler_params=pltpu.CompilerParams(
            dimension_semantics=("arbitrary", "arbitrary"), vmem_limit_bytes=VMEM_LIMIT),
        name="diff_attn",
    )(lam, rel_table.astype(F32), *[qkv.reshape(batch // ATT_STREAMS, ATT_STREAMS, seq, -1)] * 3,
      _bias_bucket_row(), sub_g, *cast_weights)
    return att.reshape(batch * seq, ATT_WIDTH), cast


def _split3(x):
    x1 = x.astype(BF16)
    r1 = x - x1.astype(F32)
    x2 = r1.astype(BF16)
    x3 = (r1 - x2.astype(F32)).astype(BF16)
    return x1, x2, x3


def _ssd_steps(z_ref, xs_ref, bc_ref, dt_ref, cw_ref, cb_ref, dtb_ref, alog_ref, dexp_ref, ng_ref,
               tril_ref, o_ref, halox_ref, halobc_ref, state_ref):
    q = SSD_Q
    x_cols, bc_cols = pl.ds(0, SSM_WIDTH), pl.ds(SSM_WIDTH, BC_COLS)
    cwx_ref, cbx_ref = cw_ref.at[:, x_cols], cb_ref.at[:, x_cols]
    cwbc_ref, cbbc_ref = cw_ref.at[:, bc_cols], cb_ref.at[:, bc_cols]

    def conv_silu(raw_ref, halo_ref, w_ref, b_ref):
        x = raw_ref[...].astype(F32)
        first_rows = lax.broadcasted_iota(jnp.int32, (HALO, x.shape[1]), 0)
        acc = b_ref[...] + w_ref[SSM_CONV - 1:SSM_CONV, :] * x
        for j in range(1, SSM_CONV):
            wj = w_ref[SSM_CONV - 1 - j:SSM_CONV - j, :]
            rolled = pltpu.roll(x, j, 0)
            head = jnp.where(first_rows < j, halo_ref[HALO - j:2 * HALO - j, :], rolled[0:HALO])
            acc = acc + wj * jnp.concatenate([head, rolled[HALO:]], axis=0)
        halo_ref[0:HALO, :] = x[q - HALO:q]
        return acc * _sigmoid(acc)

    xs = conv_silu(xs_ref, halox_ref, cwx_ref, cbx_ref)
    yield
    bc = conv_silu(bc_ref, halobc_ref, cwbc_ref, cbbc_ref)
    yield

    dtr = dt_ref[...] + dtb_ref[...]
    dt = jnp.maximum(dtr, 0.0) + jnp.log(1.0 + jnp.exp(-jnp.abs(dtr)))
    a = (-LOG2E * jnp.exp(alog_ref[...])) * dt

    row = lax.broadcasted_iota(jnp.int32, (q, q), 0)
    col = lax.broadcasted_iota(jnp.int32, (q, q), 1)
    causal = col <= row
    tril = tril_ref[...]
    acum = sum(jnp.dot(tril, t, preferred_element_type=F32) for t in _split3(a))

    lane = lax.broadcasted_iota(jnp.int32, (q, LANES), 1)

    def expand(x):
        cols = [jnp.broadcast_to(x[:, h:h + 1], (q, LANES)) for h in range(SSM_HEADS)]
        return jnp.concatenate([jnp.where(lane < SSM_HEAD_DIM, cols[h], cols[h + 1])
                                for h in range(0, SSM_HEADS, 2)], axis=1)

    dt_exp = expand(dt)
    acum_exp = expand(acum)
    ea_exp = jnp.exp2(acum_exp)
    last_exp = acum_exp[q - 1:q, :]
    dte_exp = jnp.exp2(last_exp - acum_exp)
    ea_last = ea_exp[q - 1:q, :]

    xdt = xs * dt_exp
    xdt_b = xdt.astype(BF16)
    acum_t = acum.T
    pair_lane = lax.broadcasted_iota(jnp.int32, (q, LANES), 1)
    yield

    for g in range(SSM_GROUPS):
        gsl = slice(g * SSM_GROUP_WIDTH, (g + 1) * SSM_GROUP_WIDTH)
        bg = bc[:, g * SSM_STATE:(g + 1) * SSM_STATE]
        cg = bc[:, (SSM_GROUPS + g) * SSM_STATE:(SSM_GROUPS + g + 1) * SSM_STATE]
        bg_b = bg.astype(BF16)
        cg_b = cg.astype(BF16)
        cb = lax.dot_general(cg_b, bg_b, (((1,), (1,)), ((), ())), preferred_element_type=F32)

        def masked(h):
            seg = acum[:, h:h + 1] - acum_t[h:h + 1, :]
            return (cb * jnp.exp2(jnp.where(causal, seg, -jnp.inf))).astype(BF16)

        pairs = []
        for j in range(SSM_HEADS_PER_GROUP // 2):
            h0 = g * SSM_HEADS_PER_GROUP + 2 * j
            xp = xdt_b[:, h0 * SSM_HEAD_DIM:(h0 + 2) * SSM_HEAD_DIM]
            r0 = jnp.dot(masked(h0), xp, preferred_element_type=F32)
            r1 = jnp.dot(masked(h0 + 1), xp, preferred_element_type=F32)
            pairs.append(jnp.where(pair_lane < SSM_HEAD_DIM, r0, r1))
            if j % 2 == 1:
                yield
        y_diag = jnp.concatenate(pairs, axis=1)

        st = state_ref[g]
        y_off = jnp.dot(cg_b, st.astype(BF16), preferred_element_type=F32) * ea_exp[:, gsl]
        w = (xdt[:, gsl] * dte_exp[:, gsl]).astype(BF16)
        s_new = jnp.dot(bg.T.astype(BF16), w, preferred_element_type=F32)
        state_ref[g] = st * ea_last[:, gsl] + s_new

        y = y_diag + y_off + xs[:, gsl] * dexp_ref[:, gsl]
        zf = z_ref[:, gsl].astype(F32)
        gated = y * (zf * _sigmoid(zf))
        gated = gated * lax.rsqrt(jnp.mean(gated * gated, axis=-1, keepdims=True) + SSM_NORM_EPS)
        o_ref[:, gsl] = (gated * ng_ref[:, gsl]).astype(BF16)
        yield


def _ssd_consts(conv_w, conv_b, dt_bias, a_log, d_skip, norm_g):
    pad = LANES - SSM_HEADS
    dtb = jnp.pad(dt_bias.astype(F32), (0, pad)).reshape(1, LANES)
    alog = jnp.pad(a_log.astype(F32), (0, pad)).reshape(1, LANES)
    dexp = jnp.repeat(d_skip.astype(F32), SSM_HEAD_DIM).reshape(1, SSM_WIDTH)
    tq = np.arange(SSD_Q)
    tril = jnp.asarray((tq[:, None] >= tq[None, :]).astype(np.float32), dtype=BF16)
    in_specs = [
        _resident((SSM_CONV, SSM_WIDTH + BC_COLS)),
        _resident((1, SSM_WIDTH + BC_COLS)),
        _resident((1, LANES)),
        _resident((1, LANES)),
        _resident((1, SSM_WIDTH)),
        _resident((1, SSM_WIDTH)),
        _resident((SSD_Q, SSD_Q)),
    ]
    operands = (conv_w.astype(F32), conv_b.astype(F32).reshape(1, -1), dtb, alog, dexp,
                norm_g.astype(F32).reshape(1, -1), tril)
    return in_specs, operands


SSD_COLS = 2 * SSM_WIDTH + BC_COLS
SCAN_PIECES_AFTER_QKV_DOT = (2, 2, 2, 3, 3, 2)


def _proj_ssd_body(x_ref, g_ref, w_ref, wdt_ref, *rest, tiles_per_seq):
    consts, (qkv_ref, ssm_ref, zxbc_ref, dt_ref, halox_ref, halobc_ref, state_ref) = rest[:-7], rest[-7:]

    @pl.when(pl.program_id(0) % tiles_per_seq == 0)
    def _():
        halox_ref[...] = jnp.zeros(halox_ref.shape, F32)
        halobc_ref[...] = jnp.zeros(halobc_ref.shape, F32)
        state_ref[...] = jnp.zeros(state_ref.shape, F32)

    x = x_ref[...]
    h = x * lax.rsqrt(jnp.mean(x * x, axis=-1, keepdims=True) + NORM_EPS) * g_ref[...]
    h = h.astype(BF16)
    nt = (((1,), (1,)), ((), ()))

    qkv_cols = 3 * ATT_WIDTH

    def project(col):
        y = lax.dot_general(h, w_ref[col:col + PROJ_CN, :], nt, preferred_element_type=F32).astype(BF16)
        if col < qkv_cols:
            qkv_ref[:, col:col + PROJ_CN] = y
        else:
            zxbc_ref[:, col - qkv_cols:col - qkv_cols + PROJ_CN] = y

    def project_dt(_):
        dt_ref[...] = lax.dot_general(h, wdt_ref[...], nt, preferred_element_type=F32)

    def scan_steps():
        for c in range(PROJ_TM // SSD_Q):
            rows = pl.ds(c * SSD_Q, SSD_Q)
            yield from _ssd_steps(
                zxbc_ref.at[rows, pl.ds(0, SSM_WIDTH)],
                zxbc_ref.at[rows, pl.ds(SSM_WIDTH, SSM_WIDTH)],
                zxbc_ref.at[rows, pl.ds(2 * SSM_WIDTH, BC_COLS)],
                dt_ref.at[rows, :],
                *consts, ssm_ref.at[rows, :], halox_ref, halobc_ref, state_ref)

    z0, x0, bc0 = qkv_cols, qkv_cols + SSM_WIDTH, qkv_cols + 2 * SSM_WIDTH
    plan = [(project, x0, 0), (project, x0 + PROJ_CN, 1), (project, bc0, 0), (project_dt, None, 1),
            (project, z0, 1), (project, z0 + PROJ_CN, 1)]
    plan += [(project, col, pieces) for col, pieces in
             zip(range(0, qkv_cols, PROJ_CN), SCAN_PIECES_AFTER_QKV_DOT)]
    scan = scan_steps()
    for emit, col, pieces in plan:
        emit(col)
        for _ in range(pieces):
            next(scan, None)
    for _ in scan:
        pass


def _proj_ssd(x2, g, w_pad, ssd_consts, seq):
    t = x2.shape[0]
    c_specs, c_ops = ssd_consts
    tok = lambda width: pl.BlockSpec((PROJ_TM, width), lambda i: (i, 0))
    return pl.pallas_call(
        functools.partial(_proj_ssd_body, tiles_per_seq=seq // PROJ_TM),
        grid=(t // PROJ_TM,),
        in_specs=[
            tok(D_MODEL),
            _resident((1, D_MODEL)),
            _resident((MAIN_COLS, D_MODEL)),
            pl.BlockSpec((LANES, D_MODEL), lambda i: (MAIN_COLS // LANES, 0), pipeline_mode=pl.Buffered(1)),
        ] + c_specs,
        out_specs=[tok(3 * ATT_WIDTH), tok(SSM_WIDTH)],
        out_shape=[
            jax.ShapeDtypeStruct((t, 3 * ATT_WIDTH), BF16),
            jax.ShapeDtypeStruct((t, SSM_WIDTH), BF16),
        ],
        scratch_shapes=[
            pltpu.VMEM((PROJ_TM, SSD_COLS), BF16),
            pltpu.VMEM((PROJ_TM, LANES), F32),
            pltpu.VMEM((2 * HALO, SSM_WIDTH), F32),
            pltpu.VMEM((2 * HALO, BC_COLS), F32),
            pltpu.VMEM((SSM_GROUPS, SSM_STATE, SSM_GROUP_WIDTH), F32),
        ],
        compiler_params=pltpu.CompilerParams(
            dimension_semantics=("arbitrary",), vmem_limit_bytes=VMEM_LIMIT),
        name="proj_ssd",
    )(x2, g, w_pad, w_pad, *c_ops)


def _ffn_body(x_ref, att_ref, ssm_ref, wo_ref, g2_ref, wup_ref, cw_ref, cb_ref, wdn_ref, gf_ref,
              o_ref, xc_ref, x1p_ref, h2_ref, oc_ref, carry_ref, *, tiles_per_seq):
    i = pl.program_id(0)
    tm, cw = FFN_TM, FFN_CW
    ph = tm // SUBLANES
    n_lane_blocks = D_MODEL // LANES

    @pl.when(i % tiles_per_seq == 0)
    def _():
        carry_ref[...] = jnp.zeros(carry_ref.shape, F32)

    x1 = (x_ref[...]
          + jnp.dot(att_ref[...], wo_ref[0:ATT_WIDTH, :], preferred_element_type=F32)
          + jnp.dot(ssm_ref[...], wo_ref[ATT_WIDTH:, :], preferred_element_type=F32))
    for c in range(n_lane_blocks):
        xc_ref[c] = x1[:, c * LANES:(c + 1) * LANES]
    x1p = jnp.concatenate(
        [jnp.concatenate([xc_ref[c, pl.ds(k, ph, stride=SUBLANES), :] for k in range(SUBLANES)], axis=0)
         for c in range(n_lane_blocks)], axis=1)
    x1p_ref[...] = x1p
    h2_ref[...] = (x1p * lax.rsqrt(jnp.mean(x1p * x1p, axis=-1, keepdims=True) + NORM_EPS)
                   * g2_ref[...]).astype(BF16)

    first_row = lax.broadcasted_iota(jnp.int32, (ph, cw), 0) == 0

    def prev_token(block, carry_slot):
        tail = carry_ref[carry_slot]
        carry_ref[carry_slot] = block[ph - HALO:ph]
        return jnp.where(first_row, jnp.broadcast_to(tail[HALO - 1:HALO, :], (ph, cw)),
                         pltpu.roll(block, 1, 0))

    def up_conv(slot, col):
        u = jnp.dot(h2_ref[...], wup_ref[:, col:col + cw], preferred_element_type=F32)
        blocks = [u[k * ph:(k + 1) * ph] for k in range(SUBLANES)]
        back1 = prev_token(blocks[SUBLANES - 1], 2 * slot)
        back2 = prev_token(blocks[SUBLANES - 2], 2 * slot + 1)
        hist = [back2, back1] + blocks
        w = [cw_ref[t:t + 1, col:col + cw] for t in range(FFN_CONV)]
        b = cb_ref[:, col:col + cw]
        return jnp.concatenate(
            [b + w[2] * hist[k + 2] + w[1] * hist[k + 1] + w[0] * hist[k] for k in range(SUBLANES)],
            axis=0)

    acts = []
    for j in range(FFN_DIM // cw):
        gate = up_conv(2 * j, j * cw)
        val = up_conv(2 * j + 1, FFN_DIM + j * cw)
        acts.append((gate * _sigmoid(gate) * val).astype(BF16))

    x2 = x1p_ref[...] + jnp.dot(jnp.concatenate(acts, axis=1), wdn_ref[...], preferred_element_type=F32)
    out = x2 * lax.rsqrt(jnp.mean(x2 * x2, axis=-1, keepdims=True) + NORM_EPS) * gf_ref[...]
    for c in range(n_lane_blocks):
        for k in range(SUBLANES):
            oc_ref[c, pl.ds(k, ph, stride=SUBLANES), :] = out[k * ph:(k + 1) * ph, c * LANES:(c + 1) * LANES]
    for c in range(n_lane_blocks):
        o_ref[:, c * LANES:(c + 1) * LANES] = oc_ref[c]


def _mix_ffn(x2, att, ssm, w_out, g2, w_up, conv_w, conv_b, w_down, gf, seq):
    t = x2.shape[0]
    n_carry = 2 * (FFN_CONV - 1) * (FFN_DIM // FFN_CW)
    tok = lambda width: pl.BlockSpec((FFN_TM, width), lambda i: (i, 0))
    return pl.pallas_call(
        functools.partial(_ffn_body, tiles_per_seq=seq // FFN_TM),
        grid=(t // FFN_TM,),
        in_specs=[
            tok(D_MODEL), tok(ATT_WIDTH), tok(SSM_WIDTH),
            _resident((ATT_WIDTH + SSM_WIDTH, D_MODEL)),
            _resident((1, D_MODEL)),
            _resident((D_MODEL, 2 * FFN_DIM)),
            _resident((FFN_CONV, 2 * FFN_DIM)),
            _resident((1, 2 * FFN_DIM)),
            _resident((FFN_DIM, D_MODEL)),
            _resident((1, D_MODEL)),
        ],
        out_specs=tok(D_MODEL),
        out_shape=jax.ShapeDtypeStruct((t, D_MODEL), F32),
        scratch_shapes=[
            pltpu.VMEM((D_MODEL // LANES, FFN_TM, LANES), F32),
            pltpu.VMEM((FFN_TM, D_MODEL), F32),
            pltpu.VMEM((FFN_TM, D_MODEL), BF16),
            pltpu.VMEM((D_MODEL // LANES, FFN_TM, LANES), F32),
            pltpu.VMEM((n_carry, HALO, FFN_CW), F32),
        ],
        compiler_params=pltpu.CompilerParams(
            dimension_semantics=("arbitrary",), vmem_limit_bytes=VMEM_LIMIT),
        name="mix_ffn",
    )(x2, att, ssm, w_out, g2, w_up, conv_w, conv_b, w_down, gf)


def kernel(x, rel_bias_table, attn_norm_g, w_in, lambda_q1, lambda_k1, lambda_q2, lambda_k2,
           attn_subln_g, ssm_conv_w, ssm_conv_b, ssm_dt_bias, ssm_a_log, ssm_d, ssm_norm_g,
           w_out, ffn_norm_g, ffn_w_up, ffn_conv_w, ffn_conv_b, ffn_w_down, final_norm_g):
    batch, seq, _ = x.shape
    depth = w_in.shape[0]
    assert seq % max(ATT_QB, SSD_Q, FFN_TM) == 0 and (batch * seq) % PROJ_TM == 0
    x2 = x.reshape(batch * seq, D_MODEL)
    row = lambda v: v.astype(F32).reshape(1, -1)
    for i in range(depth):
        lam_init = 0.8 - 0.6 * math.exp(-0.3 * i)
        lam = (jnp.exp(jnp.sum(lambda_q1[i].astype(F32) * lambda_k1[i].astype(F32)))
               - jnp.exp(jnp.sum(lambda_q2[i].astype(F32) * lambda_k2[i].astype(F32)))
               + lam_init).reshape(1)
        qkv, ssm = _proj_ssd(
            x2, row(attn_norm_g[i]), _w_prep(w_in[i].astype(F32).T),
            _ssd_consts(ssm_conv_w[i], ssm_conv_b[i], ssm_dt_bias[i], ssm_a_log[i], ssm_d[i], ssm_norm_g[i]),
            seq)
        att, (w_out_b, w_up_b, w_down_b) = _attention(
            qkv, lam, rel_bias_table, row(attn_subln_g[i]), batch, seq, lam_init,
            [w_out[i].astype(F32), ffn_w_up[i].astype(F32), ffn_w_down[i].astype(F32)])
        assert depth == 1
        x2 = _mix_ffn(x2, att, ssm, w_out_b, row(ffn_norm_g[i]), w_up_b, ffn_conv_w[i].astype(F32),
                      row(ffn_conv_b[i]), w_down_b, row(final_norm_g), seq)
    return x2.reshape(batch, seq, D_MODEL)
```

```python
import functools
import math

import jax
import jax.numpy as jnp
from jax import lax
from jax.experimental import pallas as pl
from jax.experimental.pallas import tpu as pltpu

F32 = jnp.float32
BF16 = jnp.bfloat16

D_MODEL = 1024
CHUNK = 64
ATT_HEADS = 8
ATT_HEAD_DIM = 64
ATT_V_DIM = 2 * ATT_HEAD_DIM
ATT_WIDTH = ATT_HEADS * ATT_V_DIM
SSM_HEADS = 16
SSM_HEAD_DIM = 64
SSM_WIDTH = SSM_HEADS * SSM_HEAD_DIM
SSM_GROUPS = 2
SSM_STATE = 128
SSM_CONV = 4
SSM_HEADS_PER_GROUP = SSM_HEADS // SSM_GROUPS
SSM_GROUP_WIDTH = SSM_WIDTH // SSM_GROUPS
FFN_DIM = 2816
FFN_CONV = 3
REL_BUCKETS = 32
REL_MAX_DIST = 128
NORM_EPS = 1e-6
SUBLN_EPS = 1e-5
SSM_NORM_EPS = 1e-5
BC_COLS = 2 * SSM_GROUPS * SSM_STATE
MAIN_COLS = 3 * ATT_WIDTH + SSM_WIDTH + SSM_WIDTH + BC_COLS
DT_COLS = SSM_HEADS
LOG2E = math.log2(math.e)

LANES = 128
SUBLANES = 8
VMEM_LIMIT = 56 * 1024 * 1024

PROJ_TM = 512
PROJ_CN = 512
ATT_QB = 256
ATT_KB = 256
SSD_Q = 256
FFN_TM = 512
FFN_CW = 256
HALO = SUBLANES

def _resident(shape):
    nd = len(shape)
    return pl.BlockSpec(shape, lambda *_: (0,) * nd, pipeline_mode=pl.Buffered(1))


def _sigmoid(x):
    return 1.0 / (1.0 + jnp.exp2(x * -LOG2E))


W_PAD_COLS = MAIN_COLS + LANES


def _wprep_body(w_ref, o_ref):
    q_scale = LOG2E * ATT_HEAD_DIM ** -0.5
    o_ref[0:ATT_WIDTH, :] = (w_ref[0:ATT_WIDTH, :] * q_scale).astype(BF16)
    o_ref[ATT_WIDTH:MAIN_COLS + DT_COLS, :] = w_ref[ATT_WIDTH:MAIN_COLS + DT_COLS, :].astype(BF16)
    o_ref[MAIN_COLS + DT_COLS:W_PAD_COLS, :] = jnp.zeros((LANES - DT_COLS, D_MODEL), BF16)


def _w_prep(w_t):
    whole = lambda shape: pl.BlockSpec(shape, lambda: (0, 0))
    return pl.pallas_call(
        _wprep_body,
        in_specs=[whole((MAIN_COLS + DT_COLS, D_MODEL))],
        out_specs=whole((W_PAD_COLS, D_MODEL)),
        out_shape=jax.ShapeDtypeStruct((W_PAD_COLS, D_MODEL), BF16),
        compiler_params=pltpu.CompilerParams(vmem_limit_bytes=VMEM_LIMIT),
        name="w_prep",
    )(w_t)


FAR_BUCKET = REL_BUCKETS // 2 - 1
BIAS_SPAN = 4 * ATT_KB


def _t5_bucket(rel):
    nb = REL_BUCKETS // 2
    max_exact = nb // 2
    bucket = jnp.where(rel > 0, nb, 0)
    n = jnp.abs(rel)
    nf = jnp.maximum(n, 1).astype(F32)
    large = max_exact + (jnp.log(nf / max_exact) / math.log(REL_MAX_DIST / max_exact)
                         * (nb - max_exact)).astype(jnp.int32)
    large = jnp.minimum(large, nb - 1)
    return bucket + jnp.where(n < max_exact, n, large)


def _bias_bucket_row():
    rel = jnp.arange(BIAS_SPAN, dtype=jnp.int32) - 2 * ATT_KB
    return jnp.broadcast_to(_t5_bucket(rel)[None, :], (SUBLANES, BIAS_SPAN))


def _bias_tiles(tbl_ref, idx_ref, h):
    idx = idx_ref[...]
    base = tbl_ref[FAR_BUCKET, h]
    r = jnp.zeros(idx.shape, F32)
    for b in range(REL_BUCKETS):
        r = jnp.where(idx == b, (tbl_ref[b, h] - base) * LOG2E, r)
    rows = jnp.concatenate([r] * (ATT_QB // SUBLANES), axis=0)
    rolled = pltpu.roll(rows, 0, 1, stride=1, stride_axis=0)
    return rolled[:, 2 * ATT_KB:3 * ATT_KB], rolled[:, ATT_KB:2 * ATT_KB]


ATT_STREAMS = 2


def _attn_body(lam_ref, tbl_ref, q_ref, k_ref, v_ref, idx_ref, subg_ref, *rest, lam_init):
    n_cast = (len(rest) - 2) // 2
    cast_in, (o_ref, *cast_out, v1_ref) = rest[:n_cast], rest[n_cast:]
    for src, dst in zip(cast_in, cast_out):
        dst[...] = src[...].astype(BF16)

    qb, kb = ATT_QB, ATT_KB
    seq = q_ref.shape[1]
    lane = lax.broadcasted_iota(jnp.int32, (qb, LANES), 1)
    lam = lam_ref[0]

    @functools.cache
    def bias_and_mask():
        row = lax.broadcasted_iota(jnp.int32, (qb, kb), 0)
        col = lax.broadcasted_iota(jnp.int32, (qb, kb), 1)
        allowed = (col // CHUNK) <= (row // CHUNK)
        b_diag, b_prev = _bias_tiles(tbl_ref, idx_ref, pl.program_id(1))
        return tuple(jnp.concatenate([t, t], axis=0) for t in (allowed, b_diag, b_prev))

    @functools.cache
    def build_v1(s):
        v1_ref[s, :, 0:LANES] = v_ref[s]
        v1_ref[s, :, LANES:2 * LANES] = jnp.ones((seq, LANES), BF16)

    def scores(s, i):
        kvl = (i + 1) * kb
        q = q_ref[s, i * qb:(i + 1) * qb, :]
        zero = jnp.zeros_like(q)
        qs = jnp.concatenate([jnp.where(lane < ATT_HEAD_DIM, q, zero),
                              jnp.where(lane >= ATT_HEAD_DIM, q, zero)], axis=0)
        sc = lax.dot_general(qs, k_ref[s, 0:kvl, :], (((1,), (1,)), ((), ())),
                             preferred_element_type=F32)
        allowed, b_diag, b_prev = bias_and_mask()
        blocks = [sc[:, j * kb:(j + 1) * kb] for j in range(i + 1)]
        blocks[i] = jnp.where(allowed, blocks[i] + b_diag, -1e30)
        if i >= 1:
            blocks[i - 1] = blocks[i - 1] + b_prev
        m = jnp.max(functools.reduce(jnp.maximum, blocks), axis=-1, keepdims=True)
        return blocks, m

    def probs(blocks, m):
        return jnp.concatenate([jnp.exp2(blk - m).astype(BF16) for blk in blocks], axis=1)

    def finish(s, i, p):
        build_v1(s)
        kvl = (i + 1) * kb
        acc = jnp.dot(p, v1_ref[s, 0:kvl, :], preferred_element_type=F32)
        o = acc[:, 0:LANES] / acc[:, LANES:2 * LANES]
        o = o[:qb] - lam * o[qb:]
        o = o * lax.rsqrt(jnp.mean(o * o, axis=-1, keepdims=True) + SUBLN_EPS) * subg_ref[...]
        o_ref[s, i * qb:(i + 1) * qb, :] = (o * (1.0 - lam_init)).astype(BF16)

    order = [(s, i) for i in reversed(range(seq // qb)) for s in range(ATT_STREAMS)]
    scored, exped = {}, {}
    for n in range(len(order) + 2 * ATT_STREAMS):
        if n < len(order):
            scored[n] = scores(*order[n])
        n1, n2 = n - ATT_STREAMS, n - 2 * ATT_STREAMS
        if 0 <= n1 < len(order):
            exped[n1] = probs(*scored.pop(n1))
        if 0 <= n2 < len(order):
            finish(*order[n2], exped.pop(n2))


BF16_ROWS = 2 * SUBLANES


def _attention(qkv, lam, rel_table, sub_g, batch, seq, lam_init, cast_weights):
    assert batch % ATT_STREAMS == 0
    kcol = ATT_WIDTH // LANES
    vcol = 2 * ATT_WIDTH // LANES
    grid = (batch // ATT_STREAMS, ATT_HEADS)
    n_steps = grid[0] * grid[1]
    rows = lambda col0: pl.BlockSpec((None, ATT_STREAMS, seq, LANES), lambda b, h: (b, 0, 0, col0 + h))

    def cast_spec(w):
        n_blocks = next(n for n in range(n_steps, 0, -1)
                        if n_steps % n == 0 and w.shape[0] % (n * BF16_ROWS) == 0)
        return pl.BlockSpec((w.shape[0] // n_blocks, w.shape[1]),
                            lambda b, h: ((b * ATT_HEADS + h) * n_blocks // n_steps, 0))

    cast_specs = [cast_spec(w) for w in cast_weights]
    att, *cast = pl.pallas_call(
        functools.partial(_attn_body, lam_init=lam_init),
        grid=grid,
        in_specs=[
            pl.BlockSpec(memory_space=pltpu.SMEM),
            pl.BlockSpec(memory_space=pltpu.SMEM),
            rows(0), rows(kcol), rows(vcol),
            _resident((SUBLANES, BIAS_SPAN)),
            pl.BlockSpec((1, ATT_V_DIM), lambda b, h: (0, 0)),
        ] + cast_specs,
        out_specs=[rows(0)] + cast_specs,
        out_shape=[jax.ShapeDtypeStruct((batch // ATT_STREAMS, ATT_STREAMS, seq, ATT_WIDTH), BF16)]
        + [jax.ShapeDtypeStruct(w.shape, BF16) for w in cast_weights],
        scratch_shapes=[pltpu.VMEM((ATT_STREAMS, seq, 2 * LANES), BF16)],
        compiler_params=pltpu.CompilerParams(
            dimension_semantics=("arbitrary", "arbitrary"), vmem_limit_bytes=VMEM_LIMIT),
        name="diff_attn",
    )(lam, rel_table.astype(F32), *[qkv.reshape(batch // ATT_STREAMS, ATT_STREAMS, seq, -1)] * 3,
      _bias_bucket_row(), sub_g, *cast_weights)
    return att.reshape(batch * seq, ATT_WIDTH), cast


def _ssd_steps(z_ref, xs_ref, bc_ref, dt_ref, cw_ref, cb_ref, dtb_ref, alog_ref, dexp_ref, ng_ref,
               o_ref, halox_ref, halobc_ref, state_ref):
    q = SSD_Q
    x_cols, bc_cols = pl.ds(0, SSM_WIDTH), pl.ds(SSM_WIDTH, BC_COLS)
    cwx_ref, cbx_ref = cw_ref.at[:, x_cols], cb_ref.at[:, x_cols]
    cwbc_ref, cbbc_ref = cw_ref.at[:, bc_cols], cb_ref.at[:, bc_cols]

    def conv_silu(raw_ref, halo_ref, w_ref, b_ref):
        x = raw_ref[...].astype(F32)
        first_rows = lax.broadcasted_iota(jnp.int32, (HALO, x.shape[1]), 0)
        acc = b_ref[...] + w_ref[SSM_CONV - 1:SSM_CONV, :] * x
        for j in range(1, SSM_CONV):
            wj = w_ref[SSM_CONV - 1 - j:SSM_CONV - j, :]
            rolled = pltpu.roll(x, j, 0)
            head = jnp.where(first_rows < j, halo_ref[HALO - j:2 * HALO - j, :], rolled[0:HALO])
            acc = acc + wj * jnp.concatenate([head, rolled[HALO:]], axis=0)
        halo_ref[0:HALO, :] = x[q - HALO:q]
        return acc * _sigmoid(acc)

    xs = conv_silu(xs_ref, halox_ref, cwx_ref, cbx_ref)
    yield
    bc = conv_silu(bc_ref, halobc_ref, cwbc_ref, cbbc_ref)
    yield

    dtr = dt_ref[...] + dtb_ref[...]
    dt = jnp.maximum(dtr, 0.0) + jnp.log(1.0 + jnp.exp(-jnp.abs(dtr)))
    a = (-LOG2E * jnp.exp(alog_ref[...])) * dt

    row = lax.broadcasted_iota(jnp.int32, (q, q), 0)
    col = lax.broadcasted_iota(jnp.int32, (q, q), 1)
    causal = col <= row
    acum = a
    row8 = lax.broadcasted_iota(jnp.int32, (q, LANES), 0)
    sh = 1
    while sh < q:
        if sh < SUBLANES:
            shifted = jnp.where(row8 < sh, 0.0, pltpu.roll(acum, sh, 0))
        else:
            shifted = jnp.concatenate([jnp.zeros((sh, LANES), F32), acum[:q - sh]], axis=0)
        acum = acum + shifted
        sh *= 2

    lane = lax.broadcasted_iota(jnp.int32, (q, LANES), 1)

    def expand(x):
        cols = [jnp.broadcast_to(x[:, h:h + 1], (q, LANES)) for h in range(SSM_HEADS)]
        return jnp.concatenate([jnp.where(lane < SSM_HEAD_DIM, cols[h], cols[h + 1])
                                for h in range(0, SSM_HEADS, 2)], axis=1)

    dt_exp = expand(dt)
    acum_exp = expand(acum)
    ea_exp = jnp.exp2(acum_exp)
    last_exp = acum_exp[q - 1:q, :]
    dte_exp = jnp.exp2(last_exp - acum_exp)
    ea_last = ea_exp[q - 1:q, :]

    xdt = xs * dt_exp
    xdt_b = xdt.astype(BF16)
    acum_t = acum.T
    pair_lane = lax.broadcasted_iota(jnp.int32, (q, LANES), 1)
    yield

    for g in range(SSM_GROUPS):
        gsl = slice(g * SSM_GROUP_WIDTH, (g + 1) * SSM_GROUP_WIDTH)
        bg = bc[:, g * SSM_STATE:(g + 1) * SSM_STATE]
        cg = bc[:, (SSM_GROUPS + g) * SSM_STATE:(SSM_GROUPS + g + 1) * SSM_STATE]
        bg_b = bg.astype(BF16)
        cg_b = cg.astype(BF16)
        cb = lax.dot_general(cg_b, bg_b, (((1,), (1,)), ((), ())), preferred_element_type=F32)

        def masked(h):
            seg = acum[:, h:h + 1] - acum_t[h:h + 1, :]
            return (cb * jnp.exp2(jnp.where(causal, seg, -jnp.inf))).astype(BF16)

        pairs = []
        for j in range(SSM_HEADS_PER_GROUP // 2):
            h0 = g * SSM_HEADS_PER_GROUP + 2 * j
            xp = xdt_b[:, h0 * SSM_HEAD_DIM:(h0 + 2) * SSM_HEAD_DIM]
            r0 = jnp.dot(masked(h0), xp, preferred_element_type=F32)
            r1 = jnp.dot(masked(h0 + 1), xp, preferred_element_type=F32)
            pairs.append(jnp.where(pair_lane < SSM_HEAD_DIM, r0, r1))
            if j % 2 == 1:
                yield
        y_diag = jnp.concatenate(pairs, axis=1)

        st = state_ref[g]
        y_off = jnp.dot(cg_b, st.astype(BF16), preferred_element_type=F32) * ea_exp[:, gsl]
        w = (xdt[:, gsl] * dte_exp[:, gsl]).astype(BF16)
        s_new = jnp.dot(bg.T.astype(BF16), w, preferred_element_type=F32)
        state_ref[g] = st * ea_last[:, gsl] + s_new

        y = y_diag + y_off + xs[:, gsl] * dexp_ref[:, gsl]
        zf = z_ref[:, gsl].astype(F32)
        gated = y * (zf * _sigmoid(zf))
        gated = gated * lax.rsqrt(jnp.mean(gated * gated, axis=-1, keepdims=True) + SSM_NORM_EPS)
        o_ref[:, gsl] = (gated * ng_ref[:, gsl]).astype(BF16)
        yield


def _ssd_consts(conv_w, conv_b, dt_bias, a_log, d_skip, norm_g):
    pad = LANES - SSM_HEADS
    dtb = jnp.pad(dt_bias.astype(F32), (0, pad)).reshape(1, LANES)
    alog = jnp.pad(a_log.astype(F32), (0, pad)).reshape(1, LANES)
    dexp = jnp.repeat(d_skip.astype(F32), SSM_HEAD_DIM).reshape(1, SSM_WIDTH)
    in_specs = [
        _resident((SSM_CONV, SSM_WIDTH + BC_COLS)),
        _resident((1, SSM_WIDTH + BC_COLS)),
        _resident((1, LANES)),
        _resident((1, LANES)),
        _resident((1, SSM_WIDTH)),
        _resident((1, SSM_WIDTH)),
    ]
    operands = (conv_w.astype(F32), conv_b.astype(F32).reshape(1, -1), dtb, alog, dexp,
                norm_g.astype(F32).reshape(1, -1))
    return in_specs, operands


SSD_COLS = 2 * SSM_WIDTH + BC_COLS
SCAN_PIECES_AFTER_QKV_DOT = (2, 2, 2, 3, 3, 2)


def _proj_ssd_body(x_ref, g_ref, w_ref, wdt_ref, *rest, tiles_per_seq):
    consts, (qkv_ref, ssm_ref, zxbc_ref, dt_ref, halox_ref, halobc_ref, state_ref) = rest[:-7], rest[-7:]

    @pl.when(pl.program_id(0) % tiles_per_seq == 0)
    def _():
        halox_ref[...] = jnp.zeros(halox_ref.shape, F32)
        halobc_ref[...] = jnp.zeros(halobc_ref.shape, F32)
        state_ref[...] = jnp.zeros(state_ref.shape, F32)

    x = x_ref[...]
    h = x * lax.rsqrt(jnp.mean(x * x, axis=-1, keepdims=True) + NORM_EPS) * g_ref[...]
    h = h.astype(BF16)
    nt = (((1,), (1,)), ((), ()))

    qkv_cols = 3 * ATT_WIDTH

    def project(col):
        y = lax.dot_general(h, w_ref[col:col + PROJ_CN, :], nt, preferred_element_type=F32).astype(BF16)
        if col < qkv_cols:
            qkv_ref[:, col:col + PROJ_CN] = y
        else:
            zxbc_ref[:, col - qkv_cols:col - qkv_cols + PROJ_CN] = y

    def project_dt(_):
        dt_ref[...] = lax.dot_general(h, wdt_ref[...], nt, preferred_element_type=F32)

    def scan_steps():
        for c in range(PROJ_TM // SSD_Q):
            rows = pl.ds(c * SSD_Q, SSD_Q)
            yield from _ssd_steps(
                zxbc_ref.at[rows, pl.ds(0, SSM_WIDTH)],
                zxbc_ref.at[rows, pl.ds(SSM_WIDTH, SSM_WIDTH)],
                zxbc_ref.at[rows, pl.ds(2 * SSM_WIDTH, BC_COLS)],
                dt_ref.at[rows, :],
                *consts, ssm_ref.at[rows, :], halox_ref, halobc_ref, state_ref)

    z0, x0, bc0 = qkv_cols, qkv_cols + SSM_WIDTH, qkv_cols + 2 * SSM_WIDTH
    plan = [(project, x0, 0), (project, x0 + PROJ_CN, 1), (project, bc0, 0), (project_dt, None, 1),
            (project, z0, 1), (project, z0 + PROJ_CN, 1)]
    plan += [(project, col, pieces) for col, pieces in
             zip(range(0, qkv_cols, PROJ_CN), SCAN_PIECES_AFTER_QKV_DOT)]
    scan = scan_steps()
    for emit, col, pieces in plan:
        emit(col)
        for _ in range(pieces):
            next(scan, None)
    for _ in scan:
        pass


def _proj_ssd(x2, g, w_pad, ssd_consts, seq):
    t = x2.shape[0]
    c_specs, c_ops = ssd_consts
    tok = lambda width: pl.BlockSpec((PROJ_TM, width), lambda i: (i, 0))
    return pl.pallas_call(
        functools.partial(_proj_ssd_body, tiles_per_seq=seq // PROJ_TM),
        grid=(t // PROJ_TM,),
        in_specs=[
            tok(D_MODEL),
            _resident((1, D_MODEL)),
            _resident((MAIN_COLS, D_MODEL)),
            pl.BlockSpec((LANES, D_MODEL), lambda i: (MAIN_COLS // LANES, 0), pipeline_mode=pl.Buffered(1)),
        ] + c_specs,
        out_specs=[tok(3 * ATT_WIDTH), tok(SSM_WIDTH)],
        out_shape=[
            jax.ShapeDtypeStruct((t, 3 * ATT_WIDTH), BF16),
            jax.ShapeDtypeStruct((t, SSM_WIDTH), BF16),
        ],
        scratch_shapes=[
            pltpu.VMEM((PROJ_TM, SSD_COLS), BF16),
            pltpu.VMEM((PROJ_TM, LANES), F32),
            pltpu.VMEM((2 * HALO, SSM_WIDTH), F32),
            pltpu.VMEM((2 * HALO, BC_COLS), F32),
            pltpu.VMEM((SSM_GROUPS, SSM_STATE, SSM_GROUP_WIDTH), F32),
        ],
        compiler_params=pltpu.CompilerParams(
            dimension_semantics=("arbitrary",), vmem_limit_bytes=VMEM_LIMIT),
        name="proj_ssd",
    )(x2, g, w_pad, w_pad, *c_ops)


def _ffn_body(x_ref, att_ref, ssm_ref, wo_ref, g2_ref, wup_ref, cw_ref, cb_ref, wdn_ref, gf_ref,
              o_ref, xc_ref, x1p_ref, h2_ref, oc_ref, carry_ref, *, tiles_per_seq):
    i = pl.program_id(0)
    tm, cw = FFN_TM, FFN_CW
    ph = tm // SUBLANES
    n_lane_blocks = D_MODEL // LANES

    @pl.when(i % tiles_per_seq == 0)
    def _():
        carry_ref[...] = jnp.zeros(carry_ref.shape, F32)

    x1 = (x_ref[...]
          + jnp.dot(att_ref[...], wo_ref[0:ATT_WIDTH, :], preferred_element_type=F32)
          + jnp.dot(ssm_ref[...], wo_ref[ATT_WIDTH:, :], preferred_element_type=F32))
    for c in range(n_lane_blocks):
        xc_ref[c] = x1[:, c * LANES:(c + 1) * LANES]
    x1p = jnp.concatenate(
        [jnp.concatenate([xc_ref[c, pl.ds(k, ph, stride=SUBLANES), :] for k in range(SUBLANES)], axis=0)
         for c in range(n_lane_blocks)], axis=1)
    x1p_ref[...] = x1p
    h2_ref[...] = (x1p * lax.rsqrt(jnp.mean(x1p * x1p, axis=-1, keepdims=True) + NORM_EPS)
                   * g2_ref[...]).astype(BF16)

    first_row = lax.broadcasted_iota(jnp.int32, (ph, cw), 0) == 0

    def prev_token(block, carry_slot):
        tail = carry_ref[carry_slot]
        carry_ref[carry_slot] = block[ph - HALO:ph]
        return jnp.where(first_row, jnp.broadcast_to(tail[HALO - 1:HALO, :], (ph, cw)),
                         pltpu.roll(block, 1, 0))

    def up_conv(slot, col):
        u = jnp.dot(h2_ref[...], wup_ref[:, col:col + cw], preferred_element_type=F32)
        blocks = [u[k * ph:(k + 1) * ph] for k in range(SUBLANES)]
        back1 = prev_token(blocks[SUBLANES - 1], 2 * slot)
        back2 = prev_token(blocks[SUBLANES - 2], 2 * slot + 1)
        hist = [back2, back1] + blocks
        w = [cw_ref[t:t + 1, col:col + cw] for t in range(FFN_CONV)]
        b = cb_ref[:, col:col + cw]
        return jnp.concatenate(
            [b + w[2] * hist[k + 2] + w[1] * hist[k + 1] + w[0] * hist[k] for k in range(SUBLANES)],
            axis=0)

    acts = []
    for j in range(FFN_DIM // cw):
        gate = up_conv(2 * j, j * cw)
        val = up_conv(2 * j + 1, FFN_DIM + j * cw)
        acts.append((gate * _sigmoid(gate) * val).astype(BF16))

    x2 = x1p_ref[...] + jnp.dot(jnp.concatenate(acts, axis=1), wdn_ref[...], preferred_element_type=F32)
    out = x2 * lax.rsqrt(jnp.mean(x2 * x2, axis=-1, keepdims=True) + NORM_EPS) * gf_ref[...]
    for c in range(n_lane_blocks):
        for k in range(SUBLANES):
            oc_ref[c, pl.ds(k, ph, stride=SUBLANES), :] = out[k * ph:(k + 1) * ph, c * LANES:(c + 1) * LANES]
    for c in range(n_lane_blocks):
        o_ref[:, c * LANES:(c + 1) * LANES] = oc_ref[c]


def _mix_ffn(x2, att, ssm, w_out, g2, w_up, conv_w, conv_b, w_down, gf, seq):
    t = x2.shape[0]
    n_carry = 2 * (FFN_CONV - 1) * (FFN_DIM // FFN_CW)
    tok = lambda width: pl.BlockSpec((FFN_TM, width), lambda i: (i, 0))
    return pl.pallas_call(
        functools.partial(_ffn_body, tiles_per_seq=seq // FFN_TM),
        grid=(t // FFN_TM,),
        in_specs=[
            tok(D_MODEL), tok(ATT_WIDTH), tok(SSM_WIDTH),
            _resident((ATT_WIDTH + SSM_WIDTH, D_MODEL)),
            _resident((1, D_MODEL)),
            _resident((D_MODEL, 2 * FFN_DIM)),
            _resident((FFN_CONV, 2 * FFN_DIM)),
            _resident((1, 2 * FFN_DIM)),
            _resident((FFN_DIM, D_MODEL)),
            _resident((1, D_MODEL)),
        ],
        out_specs=tok(D_MODEL),
        out_shape=jax.ShapeDtypeStruct((t, D_MODEL), F32),
        scratch_shapes=[
            pltpu.VMEM((D_MODEL // LANES, FFN_TM, LANES), F32),
            pltpu.VMEM((FFN_TM, D_MODEL), F32),
            pltpu.VMEM((FFN_TM, D_MODEL), BF16),
            pltpu.VMEM((D_MODEL // LANES, FFN_TM, LANES), F32),
            pltpu.VMEM((n_carry, HALO, FFN_CW), F32),
        ],
        compiler_params=pltpu.CompilerParams(
            dimension_semantics=("arbitrary",), vmem_limit_bytes=VMEM_LIMIT),
        name="mix_ffn",
    )(x2, att, ssm, w_out, g2, w_up, conv_w, conv_b, w_down, gf)


def kernel(x, rel_bias_table, attn_norm_g, w_in, lambda_q1, lambda_k1, lambda_q2, lambda_k2,
           attn_subln_g, ssm_conv_w, ssm_conv_b, ssm_dt_bias, ssm_a_log, ssm_d, ssm_norm_g,
           w_out, ffn_norm_g, ffn_w_up, ffn_conv_w, ffn_conv_b, ffn_w_down, final_norm_g):
    batch, seq, _ = x.shape
    depth = w_in.shape[0]
    assert seq % max(ATT_QB, SSD_Q, FFN_TM) == 0 and (batch * seq) % PROJ_TM == 0
    x2 = x.reshape(batch * seq, D_MODEL)
    row = lambda v: v.astype(F32).reshape(1, -1)
    for i in range(depth):
        lam_init = 0.8 - 0.6 * math.exp(-0.3 * i)
        lam = (jnp.exp(jnp.sum(lambda_q1[i].astype(F32) * lambda_k1[i].astype(F32)))
               - jnp.exp(jnp.sum(lambda_q2[i].astype(F32) * lambda_k2[i].astype(F32)))
               + lam_init).reshape(1)
        qkv, ssm = _proj_ssd(
            x2, row(attn_norm_g[i]), _w_prep(w_in[i].astype(F32).T),
            _ssd_consts(ssm_conv_w[i], ssm_conv_b[i], ssm_dt_bias[i], ssm_a_log[i], ssm_d[i], ssm_norm_g[i]),
            seq)
        att, (w_out_b, w_up_b, w_down_b) = _attention(
            qkv, lam, rel_bias_table, row(attn_subln_g[i]), batch, seq, lam_init,
            [w_out[i].astype(F32), ffn_w_up[i].astype(F32), ffn_w_down[i].astype(F32)])
        assert depth == 1
        x2 = _mix_ffn(x2, att, ssm, w_out_b, row(ffn_norm_g[i]), w_up_b, ffn_conv_w[i].astype(F32),
                      row(ffn_conv_b[i]), w_down_b, row(final_norm_g), seq)
    return x2.reshape(batch, seq, D_MODEL)
```

```python
import functools
import math

import jax
import jax.numpy as jnp
from jax import lax
from jax.experimental import pallas as pl
from jax.experimental.pallas import tpu as pltpu

F32 = jnp.float32
BF16 = jnp.bfloat16

D_MODEL = 1024
CHUNK = 64
ATT_HEADS = 8
ATT_HEAD_DIM = 64
ATT_V_DIM = 2 * ATT_HEAD_DIM
ATT_WIDTH = ATT_HEADS * ATT_V_DIM
SSM_HEADS = 16
SSM_HEAD_DIM = 64
SSM_WIDTH = SSM_HEADS * SSM_HEAD_DIM
SSM_GROUPS = 2
SSM_STATE = 128
SSM_CONV = 4
SSM_HEADS_PER_GROUP = SSM_HEADS // SSM_GROUPS
SSM_GROUP_WIDTH = SSM_WIDTH // SSM_GROUPS
FFN_DIM = 2816
FFN_CONV = 3
REL_BUCKETS = 32
REL_MAX_DIST = 128
NORM_EPS = 1e-6
SUBLN_EPS = 1e-5
SSM_NORM_EPS = 1e-5
BC_COLS = 2 * SSM_GROUPS * SSM_STATE
MAIN_COLS = 3 * ATT_WIDTH + SSM_WIDTH + SSM_WIDTH + BC_COLS
DT_COLS = SSM_HEADS
LOG2E = math.log2(math.e)

LANES = 128
SUBLANES = 8
VMEM_LIMIT = 56 * 1024 * 1024

PROJ_TM = 512
PROJ_CN = 512
ATT_QB = 256
ATT_KB = 256
SSD_Q = 256
FFN_TM = 512
FFN_CW = 256
HALO = SUBLANES

def _resident(shape):
    nd = len(shape)
    return pl.BlockSpec(shape, lambda *_: (0,) * nd, pipeline_mode=pl.Buffered(1))


def _sigmoid(x):
    return 1.0 / (1.0 + jnp.exp2(x * -LOG2E))


W_PAD_COLS = MAIN_COLS + LANES


def _wprep_body(w_ref, o_ref):
    q_scale = LOG2E * ATT_HEAD_DIM ** -0.5
    o_ref[0:ATT_WIDTH, :] = (w_ref[0:ATT_WIDTH, :] * q_scale).astype(BF16)
    o_ref[ATT_WIDTH:MAIN_COLS + DT_COLS, :] = w_ref[ATT_WIDTH:MAIN_COLS + DT_COLS, :].astype(BF16)
    o_ref[MAIN_COLS + DT_COLS:W_PAD_COLS, :] = jnp.zeros((LANES - DT_COLS, D_MODEL), BF16)


def _w_prep(w_t):
    whole = lambda shape: pl.BlockSpec(shape, lambda: (0, 0))
    return pl.pallas_call(
        _wprep_body,
        in_specs=[whole((MAIN_COLS + DT_COLS, D_MODEL))],
        out_specs=whole((W_PAD_COLS, D_MODEL)),
        out_shape=jax.ShapeDtypeStruct((W_PAD_COLS, D_MODEL), BF16),
        compiler_params=pltpu.CompilerParams(vmem_limit_bytes=VMEM_LIMIT),
        name="w_prep",
    )(w_t)


FAR_BUCKET = REL_BUCKETS // 2 - 1
BIAS_SPAN = 4 * ATT_KB


def _t5_bucket(rel):
    nb = REL_BUCKETS // 2
    max_exact = nb // 2
    bucket = jnp.where(rel > 0, nb, 0)
    n = jnp.abs(rel)
    nf = jnp.maximum(n, 1).astype(F32)
    large = max_exact + (jnp.log(nf / max_exact) / math.log(REL_MAX_DIST / max_exact)
                         * (nb - max_exact)).astype(jnp.int32)
    large = jnp.minimum(large, nb - 1)
    return bucket + jnp.where(n < max_exact, n, large)


def _bias_bucket_row():
    rel = jnp.arange(BIAS_SPAN, dtype=jnp.int32) - 2 * ATT_KB
    return jnp.broadcast_to(_t5_bucket(rel)[None, :], (SUBLANES, BIAS_SPAN))


def _bias_tiles(tbl_ref, idx_ref, h):
    idx = idx_ref[...]
    base = tbl_ref[FAR_BUCKET, h]
    r = jnp.zeros(idx.shape, F32)
    for b in range(REL_BUCKETS):
        r = jnp.where(idx == b, (tbl_ref[b, h] - base) * LOG2E, r)
    rows = jnp.concatenate([r] * (ATT_QB // SUBLANES), axis=0)
    rolled = pltpu.roll(rows, 0, 1, stride=1, stride_axis=0)
    return rolled[:, 2 * ATT_KB:3 * ATT_KB], rolled[:, ATT_KB:2 * ATT_KB]


ATT_STREAMS = 2


def _attn_body(lam_ref, tbl_ref, q_ref, k_ref, v_ref, idx_ref, subg_ref, *rest, lam_init):
    n_cast = (len(rest) - 2) // 2
    cast_in, (o_ref, *cast_out, v1_ref) = rest[:n_cast], rest[n_cast:]
    for src, dst in zip(cast_in, cast_out):
        dst[...] = src[...].astype(BF16)

    qb, kb = ATT_QB, ATT_KB
    seq = q_ref.shape[1]
    lane = lax.broadcasted_iota(jnp.int32, (qb, LANES), 1)
    lam = lam_ref[0]

    @functools.cache
    def bias_and_mask():
        row = lax.broadcasted_iota(jnp.int32, (qb, kb), 0)
        col = lax.broadcasted_iota(jnp.int32, (qb, kb), 1)
        allowed = (col // CHUNK) <= (row // CHUNK)
        b_diag, b_prev = _bias_tiles(tbl_ref, idx_ref, pl.program_id(1))
        return tuple(jnp.concatenate([t, t], axis=0) for t in (allowed, b_diag, b_prev))

    @functools.cache
    def build_v1(s):
        v1_ref[s, :, 0:LANES] = v_ref[s]
        v1_ref[s, :, LANES:2 * LANES] = jnp.ones((seq, LANES), BF16)

    def scores(s, i):
        kvl = (i + 1) * kb
        q = q_ref[s, i * qb:(i + 1) * qb, :]
        zero = jnp.zeros_like(q)
        qs = jnp.concatenate([jnp.where(lane < ATT_HEAD_DIM, q, zero),
                              jnp.where(lane >= ATT_HEAD_DIM, q, zero)], axis=0)
        sc = lax.dot_general(qs, k_ref[s, 0:kvl, :], (((1,), (1,)), ((), ())),
                             preferred_element_type=F32)
        allowed, b_diag, b_prev = bias_and_mask()
        blocks = [sc[:, j * kb:(j + 1) * kb] for j in range(i + 1)]
        blocks[i] = jnp.where(allowed, blocks[i] + b_diag, -1e30)
        if i >= 1:
            blocks[i - 1] = blocks[i - 1] + b_prev
        m = jnp.max(functools.reduce(jnp.maximum, blocks), axis=-1, keepdims=True)
        return blocks, m

    def probs(blocks, m):
        return jnp.concatenate([jnp.exp2(blk - m).astype(BF16) for blk in blocks], axis=1)

    def finish(s, i, p):
        build_v1(s)
        kvl = (i + 1) * kb
        acc = jnp.dot(p, v1_ref[s, 0:kvl, :], preferred_element_type=F32)
        o = acc[:, 0:LANES] / acc[:, LANES:2 * LANES]
        o = o[:qb] - lam * o[qb:]
        o = o * lax.rsqrt(jnp.mean(o * o, axis=-1, keepdims=True) + SUBLN_EPS) * subg_ref[...]
        o_ref[s, i * qb:(i + 1) * qb, :] = (o * (1.0 - lam_init)).astype(BF16)

    order = [(s, i) for i in reversed(range(seq // qb)) for s in range(ATT_STREAMS)]
    scored, exped = {}, {}
    for n in range(len(order) + 2 * ATT_STREAMS):
        if n < len(order):
            scored[n] = scores(*order[n])
        n1, n2 = n - ATT_STREAMS, n - 2 * ATT_STREAMS
        if 0 <= n1 < len(order):
            exped[n1] = probs(*scored.pop(n1))
        if 0 <= n2 < len(order):
            finish(*order[n2], exped.pop(n2))


BF16_ROWS = 2 * SUBLANES


def _attention(qkv, lam, rel_table, sub_g, batch, seq, lam_init, cast_weights):
    assert batch % ATT_STREAMS == 0
    kcol = ATT_WIDTH // LANES
    vcol = 2 * ATT_WIDTH // LANES
    grid = (batch // ATT_STREAMS, ATT_HEADS)
    n_steps = grid[0] * grid[1]
    rows = lambda col0: pl.BlockSpec((None, ATT_STREAMS, seq, LANES), lambda b, h: (b, 0, 0, col0 + h))

    def cast_spec(w):
        n_blocks = next(n for n in range(n_steps, 0, -1)
                        if n_steps % n == 0 and w.shape[0] % (n * BF16_ROWS) == 0)
        return pl.BlockSpec((w.shape[0] // n_blocks, w.shape[1]),
                            lambda b, h: ((b * ATT_HEADS + h) * n_blocks // n_steps, 0))

    cast_specs = [cast_spec(w) for w in cast_weights]
    att, *cast = pl.pallas_call(
        functools.partial(_attn_body, lam_init=lam_init),
        grid=grid,
        in_specs=[
            pl.BlockSpec(memory_space=pltpu.SMEM),
            pl.BlockSpec(memory_space=pltpu.SMEM),
            rows(0), rows(kcol), rows(vcol),
            _resident((SUBLANES, BIAS_SPAN)),
            pl.BlockSpec((1, ATT_V_DIM), lambda b, h: (0, 0)),
        ] + cast_specs,
        out_specs=[rows(0)] + cast_specs,
        out_shape=[jax.ShapeDtypeStruct((batch // ATT_STREAMS, ATT_STREAMS, seq, ATT_WIDTH), BF16)]
        + [jax.ShapeDtypeStruct(w.shape, BF16) for w in cast_weights],
        scratch_shapes=[pltpu.VMEM((ATT_STREAMS, seq, 2 * LANES), BF16)],
        compiler_params=pltpu.CompilerParams(
            dimension_semantics=("arbitrary", "arbitrary"), vmem_limit_bytes=VMEM_LIMIT),
        name="diff_attn",
    )(lam, rel_table.astype(F32), *[qkv.reshape(batch // ATT_STREAMS, ATT_STREAMS, seq, -1)] * 3,
      _bias_bucket_row(), sub_g, *cast_weights)
    return att.reshape(batch * seq, ATT_WIDTH), cast


def _ssd_steps(z_ref, xs_ref, bc_ref, dt_ref, cw_ref, cb_ref, dtb_ref, alog_ref, dexp_ref, ng_ref,
               o_ref, halox_ref, halobc_ref, state_ref):
    q = SSD_Q
    x_cols, bc_cols = pl.ds(0, SSM_WIDTH), pl.ds(SSM_WIDTH, BC_COLS)
    cwx_ref, cbx_ref = cw_ref.at[:, x_cols], cb_ref.at[:, x_cols]
    cwbc_ref, cbbc_ref = cw_ref.at[:, bc_cols], cb_ref.at[:, bc_cols]

    def conv_silu(raw_ref, halo_ref, w_ref, b_ref):
        x = raw_ref[...].astype(F32)
        first_rows = lax.broadcasted_iota(jnp.int32, (HALO, x.shape[1]), 0)
        acc = b_ref[...] + w_ref[SSM_CONV - 1:SSM_CONV, :] * x
        for j in range(1, SSM_CONV):
            wj = w_ref[SSM_CONV - 1 - j:SSM_CONV - j, :]
            rolled = pltpu.roll(x, j, 0)
            head = jnp.where(first_rows < j, halo_ref[HALO - j:2 * HALO - j, :], rolled[0:HALO])
            acc = acc + wj * jnp.concatenate([head, rolled[HALO:]], axis=0)
        halo_ref[0:HALO, :] = x[q - HALO:q]
        return acc * _sigmoid(acc)

    xs = conv_silu(xs_ref, halox_ref, cwx_ref, cbx_ref)
    yield
    bc = conv_silu(bc_ref, halobc_ref, cwbc_ref, cbbc_ref)
    yield

    dtr = dt_ref[...] + dtb_ref[...]
    dt = jnp.maximum(dtr, 0.0) + jnp.log(1.0 + jnp.exp(-jnp.abs(dtr)))
    a = (-LOG2E * jnp.exp(alog_ref[...])) * dt

    row = lax.broadcasted_iota(jnp.int32, (q, q), 0)
    col = lax.broadcasted_iota(jnp.int32, (q, q), 1)
    causal = col <= row
    acum = a
    row8 = lax.broadcasted_iota(jnp.int32, (q, LANES), 0)
    sh = 1
    while sh < q:
        if sh < SUBLANES:
            shifted = jnp.where(row8 < sh, 0.0, pltpu.roll(acum, sh, 0))
        else:
            shifted = jnp.concatenate([jnp.zeros((sh, LANES), F32), acum[:q - sh]], axis=0)
        acum = acum + shifted
        sh *= 2

    lane = lax.broadcasted_iota(jnp.int32, (q, LANES), 1)

    def expand(x):
        cols = [jnp.broadcast_to(x[:, h:h + 1], (q, LANES)) for h in range(SSM_HEADS)]
        return jnp.concatenate([jnp.where(lane < SSM_HEAD_DIM, cols[h], cols[h + 1])
                                for h in range(0, SSM_HEADS, 2)], axis=1)

    dt_exp = expand(dt)
    acum_exp = expand(acum)
    ea_exp = jnp.exp2(acum_exp)
    last_exp = acum_exp[q - 1:q, :]
    dte_exp = jnp.exp2(last_exp - acum_exp)
    ea_last = ea_exp[q - 1:q, :]

    xdt = xs * dt_exp
    xdt_b = xdt.astype(BF16)
    acum_t = acum.T
    pair_lane = lax.broadcasted_iota(jnp.int32, (q, LANES), 1)
    yield

    for g in range(SSM_GROUPS):
        gsl = slice(g * SSM_GROUP_WIDTH, (g + 1) * SSM_GROUP_WIDTH)
        bg = bc[:, g * SSM_STATE:(g + 1) * SSM_STATE]
        cg = bc[:, (SSM_GROUPS + g) * SSM_STATE:(SSM_GROUPS + g + 1) * SSM_STATE]
        bg_b = bg.astype(BF16)
        cg_b = cg.astype(BF16)
        cb = lax.dot_general(cg_b, bg_b, (((1,), (1,)), ((), ())), preferred_element_type=F32)

        def masked(h):
            seg = acum[:, h:h + 1] - acum_t[h:h + 1, :]
            return (cb * jnp.exp2(jnp.where(causal, seg, -jnp.inf))).astype(BF16)

        pairs = []
        for j in range(SSM_HEADS_PER_GROUP // 2):
            h0 = g * SSM_HEADS_PER_GROUP + 2 * j
            xp = xdt_b[:, h0 * SSM_HEAD_DIM:(h0 + 2) * SSM_HEAD_DIM]
            r0 = jnp.dot(masked(h0), xp, preferred_element_type=F32)
            r1 = jnp.dot(masked(h0 + 1), xp, preferred_element_type=F32)
            pairs.append(jnp.where(pair_lane < SSM_HEAD_DIM, r0, r1))
            if j % 2 == 1:
                yield
        y_diag = jnp.concatenate(pairs, axis=1)

        st = state_ref[g]
        y_off = jnp.dot(cg_b, st.astype(BF16), preferred_element_type=F32) * ea_exp[:, gsl]
        w = (xdt[:, gsl] * dte_exp[:, gsl]).astype(BF16)
        s_new = jnp.dot(bg.T.astype(BF16), w, preferred_element_type=F32)
        state_ref[g] = st * ea_last[:, gsl] + s_new

        y = y_diag + y_off + xs[:, gsl] * dexp_ref[:, gsl]
        zf = z_ref[:, gsl].astype(F32)
        gated = y * (zf * _sigmoid(zf))
        gated = gated * lax.rsqrt(jnp.mean(gated * gated, axis=-1, keepdims=True) + SSM_NORM_EPS)
        o_ref[:, gsl] = (gated * ng_ref[:, gsl]).astype(BF16)
        yield


def _ssd_consts(conv_w, conv_b, dt_bias, a_log, d_skip, norm_g):
    pad = LANES - SSM_HEADS
    dtb = jnp.pad(dt_bias.astype(F32), (0, pad)).reshape(1, LANES)
    alog = jnp.pad(a_log.astype(F32), (0, pad)).reshape(1, LANES)
    dexp = jnp.repeat(d_skip.astype(F32), SSM_HEAD_DIM).reshape(1, SSM_WIDTH)
    in_specs = [
        _resident((SSM_CONV, SSM_WIDTH + BC_COLS)),
        _resident((1, SSM_WIDTH + BC_COLS)),
        _resident((1, LANES)),
        _resident((1, LANES)),
        _resident((1, SSM_WIDTH)),
        _resident((1, SSM_WIDTH)),
    ]
    operands = (conv_w.astype(F32), conv_b.astype(F32).reshape(1, -1), dtb, alog, dexp,
                norm_g.astype(F32).reshape(1, -1))
    return in_specs, operands


SSD_COLS = 2 * SSM_WIDTH + BC_COLS
SCAN_PIECES_AFTER_QKV_DOT = (2, 2, 2, 3, 3, 2)


def _proj_ssd_body(x_ref, g_ref, w_ref, wdt_ref, *rest, tiles_per_seq):
    consts, (qkv_ref, ssm_ref, zxbc_ref, dt_ref, halox_ref, halobc_ref, state_ref) = rest[:-7], rest[-7:]

    @pl.when(pl.program_id(0) % tiles_per_seq == 0)
    def _():
        halox_ref[...] = jnp.zeros(halox_ref.shape, F32)
        halobc_ref[...] = jnp.zeros(halobc_ref.shape, F32)
        state_ref[...] = jnp.zeros(state_ref.shape, F32)

    x = x_ref[...]
    h = x * lax.rsqrt(jnp.mean(x * x, axis=-1, keepdims=True) + NORM_EPS) * g_ref[...]
    h = h.astype(BF16)
    nt = (((1,), (1,)), ((), ()))

    qkv_cols = 3 * ATT_WIDTH

    def project(col):
        y = lax.dot_general(h, w_ref[col:col + PROJ_CN, :], nt, preferred_element_type=F32).astype(BF16)
        if col < qkv_cols:
            qkv_ref[:, col:col + PROJ_CN] = y
        else:
            zxbc_ref[:, col - qkv_cols:col - qkv_cols + PROJ_CN] = y

    def project_dt(_):
        dt_t = lax.dot_general(wdt_ref[0:DT_COLS, :], h, nt, preferred_element_type=F32)
        dt_t = jnp.concatenate([dt_t, jnp.zeros((LANES - DT_COLS, PROJ_TM), F32)], axis=0)
        dt_ref[...] = dt_t.T

    def scan_steps():
        for c in range(PROJ_TM // SSD_Q):
            rows = pl.ds(c * SSD_Q, SSD_Q)
            yield from _ssd_steps(
                zxbc_ref.at[rows, pl.ds(0, SSM_WIDTH)],
                zxbc_ref.at[rows, pl.ds(SSM_WIDTH, SSM_WIDTH)],
                zxbc_ref.at[rows, pl.ds(2 * SSM_WIDTH, BC_COLS)],
                dt_ref.at[rows, :],
                *consts, ssm_ref.at[rows, :], halox_ref, halobc_ref, state_ref)

    z0, x0, bc0 = qkv_cols, qkv_cols + SSM_WIDTH, qkv_cols + 2 * SSM_WIDTH
    plan = [(project, x0, 0), (project, x0 + PROJ_CN, 1), (project, bc0, 0), (project_dt, None, 1),
            (project, z0, 1), (project, z0 + PROJ_CN, 1)]
    plan += [(project, col, pieces) for col, pieces in
             zip(range(0, qkv_cols, PROJ_CN), SCAN_PIECES_AFTER_QKV_DOT)]
    scan = scan_steps()
    for emit, col, pieces in plan:
        emit(col)
        for _ in range(pieces):
            next(scan, None)
    for _ in scan:
        pass


def _proj_ssd(x2, g, w_pad, ssd_consts, seq):
    t = x2.shape[0]
    c_specs, c_ops = ssd_consts
    tok = lambda width: pl.BlockSpec((PROJ_TM, width), lambda i: (i, 0))
    return pl.pallas_call(
        functools.partial(_proj_ssd_body, tiles_per_seq=seq // PROJ_TM),
        grid=(t // PROJ_TM,),
        in_specs=[
            tok(D_MODEL),
            _resident((1, D_MODEL)),
            _resident((MAIN_COLS, D_MODEL)),
            pl.BlockSpec((LANES, D_MODEL), lambda i: (MAIN_COLS // LANES, 0), pipeline_mode=pl.Buffered(1)),
        ] + c_specs,
        out_specs=[tok(3 * ATT_WIDTH), tok(SSM_WIDTH)],
        out_shape=[
            jax.ShapeDtypeStruct((t, 3 * ATT_WIDTH), BF16),
            jax.ShapeDtypeStruct((t, SSM_WIDTH), BF16),
        ],
        scratch_shapes=[
            pltpu.VMEM((PROJ_TM, SSD_COLS), BF16),
            pltpu.VMEM((PROJ_TM, LANES), F32),
            pltpu.VMEM((2 * HALO, SSM_WIDTH), F32),
            pltpu.VMEM((2 * HALO, BC_COLS), F32),
            pltpu.VMEM((SSM_GROUPS, SSM_STATE, SSM_GROUP_WIDTH), F32),
        ],
        compiler_params=pltpu.CompilerParams(
            dimension_semantics=("arbitrary",), vmem_limit_bytes=VMEM_LIMIT),
        name="proj_ssd",
    )(x2, g, w_pad, w_pad, *c_ops)


def _ffn_body(x_ref, att_ref, ssm_ref, wo_ref, g2_ref, wup_ref, cw_ref, cb_ref, wdn_ref, gf_ref,
              o_ref, xc_ref, x1p_ref, h2_ref, oc_ref, carry_ref, *, tiles_per_seq):
    i = pl.program_id(0)
    tm, cw = FFN_TM, FFN_CW
    ph = tm // SUBLANES
    n_lane_blocks = D_MODEL // LANES

    @pl.when(i % tiles_per_seq == 0)
    def _():
        carry_ref[...] = jnp.zeros(carry_ref.shape, F32)

    x1 = (x_ref[...]
          + jnp.dot(att_ref[...], wo_ref[0:ATT_WIDTH, :], preferred_element_type=F32)
          + jnp.dot(ssm_ref[...], wo_ref[ATT_WIDTH:, :], preferred_element_type=F32))
    for c in range(n_lane_blocks):
        xc_ref[c] = x1[:, c * LANES:(c + 1) * LANES]
    x1p = jnp.concatenate(
        [jnp.concatenate([xc_ref[c, pl.ds(k, ph, stride=SUBLANES), :] for k in range(SUBLANES)], axis=0)
         for c in range(n_lane_blocks)], axis=1)
    x1p_ref[...] = x1p
    h2_ref[...] = (x1p * lax.rsqrt(jnp.mean(x1p * x1p, axis=-1, keepdims=True) + NORM_EPS)
                   * g2_ref[...]).astype(BF16)

    first_row = lax.broadcasted_iota(jnp.int32, (ph, cw), 0) == 0

    def prev_token(block, carry_slot):
        tail = carry_ref[carry_slot]
        carry_ref[carry_slot] = block[ph - HALO:ph]
        return jnp.where(first_row, jnp.broadcast_to(tail[HALO - 1:HALO, :], (ph, cw)),
                         pltpu.roll(block, 1, 0))

    def up_conv(slot, col):
        u = jnp.dot(h2_ref[...], wup_ref[:, col:col + cw], preferred_element_type=F32)
        blocks = [u[k * ph:(k + 1) * ph] for k in range(SUBLANES)]
        back1 = prev_token(blocks[SUBLANES - 1], 2 * slot)
        back2 = prev_token(blocks[SUBLANES - 2], 2 * slot + 1)
        hist = [back2, back1] + blocks
        w = [cw_ref[t:t + 1, col:col + cw] for t in range(FFN_CONV)]
        b = cb_ref[:, col:col + cw]
        return jnp.concatenate(
            [b + w[2] * hist[k + 2] + w[1] * hist[k + 1] + w[0] * hist[k] for k in range(SUBLANES)],
            axis=0)

    acts = []
    for j in range(FFN_DIM // cw):
        gate = up_conv(2 * j, j * cw)
        val = up_conv(2 * j + 1, FFN_DIM + j * cw)
        acts.append((gate * _sigmoid(gate) * val).astype(BF16))

    x2 = x1p_ref[...] + jnp.dot(jnp.concatenate(acts, axis=1), wdn_ref[...], preferred_element_type=F32)
    out = x2 * lax.rsqrt(jnp.mean(x2 * x2, axis=-1, keepdims=True) + NORM_EPS) * gf_ref[...]
    for c in range(n_lane_blocks):
        for k in range(SUBLANES):
            oc_ref[c, pl.ds(k, ph, stride=SUBLANES), :] = out[k * ph:(k + 1) * ph, c * LANES:(c + 1) * LANES]
    for c in range(n_lane_blocks):
        o_ref[:, c * LANES:(c + 1) * LANES] = oc_ref[c]


def _mix_ffn(x2, att, ssm, w_out, g2, w_up, conv_w, conv_b, w_down, gf, seq):
    t = x2.shape[0]
    n_carry = 2 * (FFN_CONV - 1) * (FFN_DIM // FFN_CW)
    tok = lambda width: pl.BlockSpec((FFN_TM, width), lambda i: (i, 0))
    return pl.pallas_call(
        functools.partial(_ffn_body, tiles_per_seq=seq // FFN_TM),
        grid=(t // FFN_TM,),
        in_specs=[
            tok(D_MODEL), tok(ATT_WIDTH), tok(SSM_WIDTH),
            _resident((ATT_WIDTH + SSM_WIDTH, D_MODEL)),
            _resident((1, D_MODEL)),
            _resident((D_MODEL, 2 * FFN_DIM)),
            _resident((FFN_CONV, 2 * FFN_DIM)),
            _resident((1, 2 * FFN_DIM)),
            _resident((FFN_DIM, D_MODEL)),
            _resident((1, D_MODEL)),
        ],
        out_specs=tok(D_MODEL),
        out_shape=jax.ShapeDtypeStruct((t, D_MODEL), F32),
        scratch_shapes=[
            pltpu.VMEM((D_MODEL // LANES, FFN_TM, LANES), F32),
            pltpu.VMEM((FFN_TM, D_MODEL), F32),
            pltpu.VMEM((FFN_TM, D_MODEL), BF16),
            pltpu.VMEM((D_MODEL // LANES, FFN_TM, LANES), F32),
            pltpu.VMEM((n_carry, HALO, FFN_CW), F32),
        ],
        compiler_params=pltpu.CompilerParams(
            dimension_semantics=("arbitrary",), vmem_limit_bytes=VMEM_LIMIT),
        name="mix_ffn",
    )(x2, att, ssm, w_out, g2, w_up, conv_w, conv_b, w_down, gf)


def kernel(x, rel_bias_table, attn_norm_g, w_in, lambda_q1, lambda_k1, lambda_q2, lambda_k2,
           attn_subln_g, ssm_conv_w, ssm_conv_b, ssm_dt_bias, ssm_a_log, ssm_d, ssm_norm_g,
           w_out, ffn_norm_g, ffn_w_up, ffn_conv_w, ffn_conv_b, ffn_w_down, final_norm_g):
    batch, seq, _ = x.shape
    depth = w_in.shape[0]
    assert seq % max(ATT_QB, SSD_Q, FFN_TM) == 0 and (batch * seq) % PROJ_TM == 0
    x2 = x.reshape(batch * seq, D_MODEL)
    row = lambda v: v.astype(F32).reshape(1, -1)
    for i in range(depth):
        lam_init = 0.8 - 0.6 * math.exp(-0.3 * i)
        lam = (jnp.exp(jnp.sum(lambda_q1[i].astype(F32) * lambda_k1[i].astype(F32)))
               - jnp.exp(jnp.sum(lambda_q2[i].astype(F32) * lambda_k2[i].astype(F32)))
               + lam_init).reshape(1)
        qkv, ssm = _proj_ssd(
            x2, row(attn_norm_g[i]), _w_prep(w_in[i].astype(F32).T),
            _ssd_consts(ssm_conv_w[i], ssm_conv_b[i], ssm_dt_bias[i], ssm_a_log[i], ssm_d[i], ssm_norm_g[i]),
            seq)
        att, (w_out_b, w_up_b, w_down_b) = _attention(
            qkv, lam, rel_bias_table, row(attn_subln_g[i]), batch, seq, lam_init,
            [w_out[i].astype(F32), ffn_w_up[i].astype(F32), ffn_w_down[i].astype(F32)])
        assert depth == 1
        x2 = _mix_ffn(x2, att, ssm, w_out_b, row(ffn_norm_g[i]), w_up_b, ffn_conv_w[i].astype(F32),
                      row(ffn_conv_b[i]), w_down_b, row(final_norm_g), seq)
    return x2.reshape(batch, seq, D_MODEL)
```

```python
import functools
import math

import jax
import jax.numpy as jnp
from jax import lax
from jax.experimental import pallas as pl
from jax.experimental.pallas import tpu as pltpu

F32 = jnp.float32
BF16 = jnp.bfloat16

D_MODEL = 1024
CHUNK = 64
ATT_HEADS = 8
ATT_HEAD_DIM = 64
ATT_V_DIM = 2 * ATT_HEAD_DIM
ATT_WIDTH = ATT_HEADS * ATT_V_DIM
SSM_HEADS = 16
SSM_HEAD_DIM = 64
SSM_WIDTH = SSM_HEADS * SSM_HEAD_DIM
SSM_GROUPS = 2
SSM_STATE = 128
SSM_CONV = 4
SSM_HEADS_PER_GROUP = SSM_HEADS // SSM_GROUPS
SSM_GROUP_WIDTH = SSM_WIDTH // SSM_GROUPS
FFN_DIM = 2816
FFN_CONV = 3
REL_BUCKETS = 32
REL_MAX_DIST = 128
NORM_EPS = 1e-6
SUBLN_EPS = 1e-5
SSM_NORM_EPS = 1e-5
BC_COLS = 2 * SSM_GROUPS * SSM_STATE
MAIN_COLS = 3 * ATT_WIDTH + SSM_WIDTH + SSM_WIDTH + BC_COLS
DT_COLS = SSM_HEADS
LOG2E = math.log2(math.e)

LANES = 128
SUBLANES = 8
VMEM_LIMIT = 56 * 1024 * 1024

PROJ_TM = 512
PROJ_CN = 512
ATT_QB = 256
ATT_KB = 256
SSD_Q = 256
FFN_TM = 512
FFN_CW = 256
HALO = SUBLANES

def _resident(shape):
    nd = len(shape)
    return pl.BlockSpec(shape, lambda *_: (0,) * nd, pipeline_mode=pl.Buffered(1))


def _sigmoid(x):
    return 1.0 / (1.0 + jnp.exp2(x * -LOG2E))


W_PAD_COLS = MAIN_COLS + LANES
WPREP_LANES = 256


def _wprep_body(w_ref, o_ref):
    q_scale = LOG2E * ATT_HEAD_DIM ** -0.5
    o_ref[0:ATT_WIDTH, :] = (w_ref[0:ATT_WIDTH, :] * q_scale).astype(BF16)
    o_ref[ATT_WIDTH:MAIN_COLS + DT_COLS, :] = w_ref[ATT_WIDTH:MAIN_COLS + DT_COLS, :].astype(BF16)
    o_ref[MAIN_COLS + DT_COLS:W_PAD_COLS, :] = jnp.zeros((LANES - DT_COLS, WPREP_LANES), BF16)


def _w_prep(w_t):
    cols = lambda rows: pl.BlockSpec((rows, WPREP_LANES), lambda j: (0, j))
    return pl.pallas_call(
        _wprep_body,
        grid=(D_MODEL // WPREP_LANES,),
        in_specs=[cols(MAIN_COLS + DT_COLS)],
        out_specs=cols(W_PAD_COLS),
        out_shape=jax.ShapeDtypeStruct((W_PAD_COLS, D_MODEL), BF16),
        compiler_params=pltpu.CompilerParams(
            dimension_semantics=("arbitrary",), vmem_limit_bytes=VMEM_LIMIT),
        name="w_prep",
    )(w_t)


FAR_BUCKET = REL_BUCKETS // 2 - 1
BIAS_SPAN = 4 * ATT_KB


def _t5_bucket(rel):
    nb = REL_BUCKETS // 2
    max_exact = nb // 2
    bucket = jnp.where(rel > 0, nb, 0)
    n = jnp.abs(rel)
    nf = jnp.maximum(n, 1).astype(F32)
    large = max_exact + (jnp.log(nf / max_exact) / math.log(REL_MAX_DIST / max_exact)
                         * (nb - max_exact)).astype(jnp.int32)
    large = jnp.minimum(large, nb - 1)
    return bucket + jnp.where(n < max_exact, n, large)


def _bias_bucket_row():
    rel = jnp.arange(BIAS_SPAN, dtype=jnp.int32) - 2 * ATT_KB
    return jnp.broadcast_to(_t5_bucket(rel)[None, :], (SUBLANES, BIAS_SPAN))


def _bias_tiles(tbl_ref, idx_ref, h):
    idx = idx_ref[...]
    base = tbl_ref[FAR_BUCKET, h]
    r = jnp.zeros(idx.shape, F32)
    for b in range(REL_BUCKETS):
        r = jnp.where(idx == b, (tbl_ref[b, h] - base) * LOG2E, r)
    rows = jnp.concatenate([r] * (ATT_QB // SUBLANES), axis=0)
    rolled = pltpu.roll(rows, 0, 1, stride=1, stride_axis=0)
    return rolled[:, 2 * ATT_KB:3 * ATT_KB], rolled[:, ATT_KB:2 * ATT_KB]


ATT_STREAMS = 2


def _attn_body(lam_ref, tbl_ref, q_ref, k_ref, v_ref, idx_ref, subg_ref, *rest, lam_init):
    n_cast = (len(rest) - 2) // 2
    cast_in, (o_ref, *cast_out, v1_ref) = rest[:n_cast], rest[n_cast:]
    for src, dst in zip(cast_in, cast_out):
        dst[...] = src[...].astype(BF16)

    qb, kb = ATT_QB, ATT_KB
    seq = q_ref.shape[1]
    lane = lax.broadcasted_iota(jnp.int32, (qb, LANES), 1)
    lam = lam_ref[0]

    @functools.cache
    def bias_and_mask():
        row = lax.broadcasted_iota(jnp.int32, (qb, kb), 0)
        col = lax.broadcasted_iota(jnp.int32, (qb, kb), 1)
        allowed = (col // CHUNK) <= (row // CHUNK)
        b_diag, b_prev = _bias_tiles(tbl_ref, idx_ref, pl.program_id(1))
        return tuple(jnp.concatenate([t, t], axis=0) for t in (allowed, b_diag, b_prev))

    @functools.cache
    def build_v1(s):
        v1_ref[s, :, 0:LANES] = v_ref[s]
        v1_ref[s, :, LANES:2 * LANES] = jnp.ones((seq, LANES), BF16)

    def scores(s, i):
        kvl = (i + 1) * kb
        q = q_ref[s, i * qb:(i + 1) * qb, :]
        zero = jnp.zeros_like(q)
        qs = jnp.concatenate([jnp.where(lane < ATT_HEAD_DIM, q, zero),
                              jnp.where(lane >= ATT_HEAD_DIM, q, zero)], axis=0)
        sc = lax.dot_general(qs, k_ref[s, 0:kvl, :], (((1,), (1,)), ((), ())),
                             preferred_element_type=F32)
        allowed, b_diag, b_prev = bias_and_mask()
        blocks = [sc[:, j * kb:(j + 1) * kb] for j in range(i + 1)]
        blocks[i] = jnp.where(allowed, blocks[i] + b_diag, -1e30)
        if i >= 1:
            blocks[i - 1] = blocks[i - 1] + b_prev
        m = jnp.max(functools.reduce(jnp.maximum, blocks), axis=-1, keepdims=True)
        return blocks, m

    def probs(blocks, m):
        return jnp.concatenate([jnp.exp2(blk - m).astype(BF16) for blk in blocks], axis=1)

    def finish(s, i, p):
        build_v1(s)
        kvl = (i + 1) * kb
        acc = jnp.dot(p, v1_ref[s, 0:kvl, :], preferred_element_type=F32)
        o = acc[:, 0:LANES] / acc[:, LANES:2 * LANES]
        o = o[:qb] - lam * o[qb:]
        o = o * lax.rsqrt(jnp.mean(o * o, axis=-1, keepdims=True) + SUBLN_EPS) * subg_ref[...]
        o_ref[s, i * qb:(i + 1) * qb, :] = (o * (1.0 - lam_init)).astype(BF16)

    order = [(s, i) for i in reversed(range(seq // qb)) for s in range(ATT_STREAMS)]
    scored, exped = {}, {}
    for n in range(len(order) + 2 * ATT_STREAMS):
        if n < len(order):
            scored[n] = scores(*order[n])
        n1, n2 = n - ATT_STREAMS, n - 2 * ATT_STREAMS
        if 0 <= n1 < len(order):
            exped[n1] = probs(*scored.pop(n1))
        if 0 <= n2 < len(order):
            finish(*order[n2], exped.pop(n2))


BF16_ROWS = 2 * SUBLANES


def _attention(qkv, lam, rel_table, sub_g, batch, seq, lam_init, cast_weights):
    assert batch % ATT_STREAMS == 0
    kcol = ATT_WIDTH // LANES
    vcol = 2 * ATT_WIDTH // LANES
    grid = (batch // ATT_STREAMS, ATT_HEADS)
    n_steps = grid[0] * grid[1]
    rows = lambda col0: pl.BlockSpec((None, ATT_STREAMS, seq, LANES), lambda b, h: (b, 0, 0, col0 + h))

    def cast_spec(w):
        n_blocks = next(n for n in range(n_steps, 0, -1)
                        if n_steps % n == 0 and w.shape[0] % (n * BF16_ROWS) == 0)
        return pl.BlockSpec((w.shape[0] // n_blocks, w.shape[1]),
                            lambda b, h: ((b * ATT_HEADS + h) * n_blocks // n_steps, 0))

    cast_specs = [cast_spec(w) for w in cast_weights]
    att, *cast = pl.pallas_call(
        functools.partial(_attn_body, lam_init=lam_init),
        grid=grid,
        in_specs=[
            pl.BlockSpec(memory_space=pltpu.SMEM),
            pl.BlockSpec(memory_space=pltpu.SMEM),
            rows(0), rows(kcol), rows(vcol),
            _resident((SUBLANES, BIAS_SPAN)),
            pl.BlockSpec((1, ATT_V_DIM), lambda b, h: (0, 0)),
        ] + cast_specs,
        out_specs=[rows(0)] + cast_specs,
        out_shape=[jax.ShapeDtypeStruct((batch // ATT_STREAMS, ATT_STREAMS, seq, ATT_WIDTH), BF16)]
        + [jax.ShapeDtypeStruct(w.shape, BF16) for w in cast_weights],
        scratch_shapes=[pltpu.VMEM((ATT_STREAMS, seq, 2 * LANES), BF16)],
        compiler_params=pltpu.CompilerParams(
            dimension_semantics=("arbitrary", "arbitrary"), vmem_limit_bytes=VMEM_LIMIT),
        name="diff_attn",
    )(lam, rel_table.astype(F32), *[qkv.reshape(batch // ATT_STREAMS, ATT_STREAMS, seq, -1)] * 3,
      _bias_bucket_row(), sub_g, *cast_weights)
    return att.reshape(batch * seq, ATT_WIDTH), cast


def _ssd_steps(z_ref, xs_ref, bc_ref, dt_ref, cw_ref, cb_ref, dtb_ref, alog_ref, dexp_ref, ng_ref,
               o_ref, halox_ref, halobc_ref, state_ref):
    q = SSD_Q
    x_cols, bc_cols = pl.ds(0, SSM_WIDTH), pl.ds(SSM_WIDTH, BC_COLS)
    cwx_ref, cbx_ref = cw_ref.at[:, x_cols], cb_ref.at[:, x_cols]
    cwbc_ref, cbbc_ref = cw_ref.at[:, bc_cols], cb_ref.at[:, bc_cols]

    def conv_silu(raw_ref, halo_ref, w_ref, b_ref):
        x = raw_ref[...].astype(F32)
        first_rows = lax.broadcasted_iota(jnp.int32, (HALO, x.shape[1]), 0)
        acc = b_ref[...] + w_ref[SSM_CONV - 1:SSM_CONV, :] * x
        for j in range(1, SSM_CONV):
            wj = w_ref[SSM_CONV - 1 - j:SSM_CONV - j, :]
            rolled = pltpu.roll(x, j, 0)
            head = jnp.where(first_rows < j, halo_ref[HALO - j:2 * HALO - j, :], rolled[0:HALO])
            acc = acc + wj * jnp.concatenate([head, rolled[HALO:]], axis=0)
        halo_ref[0:HALO, :] = x[q - HALO:q]
        return acc * _sigmoid(acc)

    xs = conv_silu(xs_ref, halox_ref, cwx_ref, cbx_ref)
    yield
    bc = conv_silu(bc_ref, halobc_ref, cwbc_ref, cbbc_ref)
    yield

    dtr = dt_ref[...] + dtb_ref[...]
    dt = jnp.maximum(dtr, 0.0) + jnp.log(1.0 + jnp.exp(-jnp.abs(dtr)))
    a = (-LOG2E * jnp.exp(alog_ref[...])) * dt

    row = lax.broadcasted_iota(jnp.int32, (q, q), 0)
    col = lax.broadcasted_iota(jnp.int32, (q, q), 1)
    causal = col <= row
    acum = a
    row8 = lax.broadcasted_iota(jnp.int32, (q, LANES), 0)
    sh = 1
    while sh < q:
        if sh < SUBLANES:
            shifted = jnp.where(row8 < sh, 0.0, pltpu.roll(acum, sh, 0))
        else:
            shifted = jnp.concatenate([jnp.zeros((sh, LANES), F32), acum[:q - sh]], axis=0)
        acum = acum + shifted
        sh *= 2

    lane = lax.broadcasted_iota(jnp.int32, (q, LANES), 1)

    def expand(x):
        cols = [jnp.broadcast_to(x[:, h:h + 1], (q, LANES)) for h in range(SSM_HEADS)]
        return jnp.concatenate([jnp.where(lane < SSM_HEAD_DIM, cols[h], cols[h + 1])
                                for h in range(0, SSM_HEADS, 2)], axis=1)

    dt_exp = expand(dt)
    acum_exp = expand(acum)
    ea_exp = jnp.exp2(acum_exp)
    last_exp = acum_exp[q - 1:q, :]
    dte_exp = jnp.exp2(last_exp - acum_exp)
    ea_last = ea_exp[q - 1:q, :]

    xdt = xs * dt_exp
    xdt_b = xdt.astype(BF16)
    acum_t = acum.T
    pair_lane = lax.broadcasted_iota(jnp.int32, (q, LANES), 1)
    yield

    for g in range(SSM_GROUPS):
        gsl = slice(g * SSM_GROUP_WIDTH, (g + 1) * SSM_GROUP_WIDTH)
        bg = bc[:, g * SSM_STATE:(g + 1) * SSM_STATE]
        cg = bc[:, (SSM_GROUPS + g) * SSM_STATE:(SSM_GROUPS + g + 1) * SSM_STATE]
        bg_b = bg.astype(BF16)
        cg_b = cg.astype(BF16)
        cb = lax.dot_general(cg_b, bg_b, (((1,), (1,)), ((), ())), preferred_element_type=F32)

        def masked(h):
            seg = acum[:, h:h + 1] - acum_t[h:h + 1, :]
            return (cb * jnp.exp2(jnp.where(causal, seg, -jnp.inf))).astype(BF16)

        pairs = []
        for j in range(SSM_HEADS_PER_GROUP // 2):
            h0 = g * SSM_HEADS_PER_GROUP + 2 * j
            xp = xdt_b[:, h0 * SSM_HEAD_DIM:(h0 + 2) * SSM_HEAD_DIM]
            r0 = jnp.dot(masked(h0), xp, preferred_element_type=F32)
            r1 = jnp.dot(masked(h0 + 1), xp, preferred_element_type=F32)
            pairs.append(jnp.where(pair_lane < SSM_HEAD_DIM, r0, r1))
            if j % 2 == 1:
                yield
        y_diag = jnp.concatenate(pairs, axis=1)

        st = state_ref[g]
        y_off = jnp.dot(cg_b, st.astype(BF16), preferred_element_type=F32) * ea_exp[:, gsl]
        w = (xdt[:, gsl] * dte_exp[:, gsl]).astype(BF16)
        s_new = jnp.dot(bg.T.astype(BF16), w, preferred_element_type=F32)
        state_ref[g] = st * ea_last[:, gsl] + s_new

        y = y_diag + y_off + xs[:, gsl] * dexp_ref[:, gsl]
        zf = z_ref[:, gsl].astype(F32)
        gated = y * (zf * _sigmoid(zf))
        gated = gated * lax.rsqrt(jnp.mean(gated * gated, axis=-1, keepdims=True) + SSM_NORM_EPS)
        o_ref[:, gsl] = (gated * ng_ref[:, gsl]).astype(BF16)
        yield


def _ssd_consts(conv_w, conv_b, dt_bias, a_log, d_skip, norm_g):
    pad = LANES - SSM_HEADS
    dtb = jnp.pad(dt_bias.astype(F32), (0, pad)).reshape(1, LANES)
    alog = jnp.pad(a_log.astype(F32), (0, pad)).reshape(1, LANES)
    dexp = jnp.repeat(d_skip.astype(F32), SSM_HEAD_DIM).reshape(1, SSM_WIDTH)
    in_specs = [
        _resident((SSM_CONV, SSM_WIDTH + BC_COLS)),
        _resident((1, SSM_WIDTH + BC_COLS)),
        _resident((1, LANES)),
        _resident((1, LANES)),
        _resident((1, SSM_WIDTH)),
        _resident((1, SSM_WIDTH)),
    ]
    operands = (conv_w.astype(F32), conv_b.astype(F32).reshape(1, -1), dtb, alog, dexp,
                norm_g.astype(F32).reshape(1, -1))
    return in_specs, operands


SSD_COLS = 2 * SSM_WIDTH + BC_COLS
SCAN_PIECES_AFTER_QKV_DOT = (2, 2, 2, 3, 3, 2)


def _proj_ssd_body(x_ref, g_ref, w_ref, wdt_ref, *rest, tiles_per_seq):
    consts, (qkv_ref, ssm_ref, zxbc_ref, dt_ref, halox_ref, halobc_ref, state_ref) = rest[:-7], rest[-7:]

    @pl.when(pl.program_id(0) % tiles_per_seq == 0)
    def _():
        halox_ref[...] = jnp.zeros(halox_ref.shape, F32)
        halobc_ref[...] = jnp.zeros(halobc_ref.shape, F32)
        state_ref[...] = jnp.zeros(state_ref.shape, F32)

    x = x_ref[...]
    h = x * lax.rsqrt(jnp.mean(x * x, axis=-1, keepdims=True) + NORM_EPS) * g_ref[...]
    h = h.astype(BF16)
    nt = (((1,), (1,)), ((), ()))

    qkv_cols = 3 * ATT_WIDTH

    def project(col):
        y = lax.dot_general(h, w_ref[col:col + PROJ_CN, :], nt, preferred_element_type=F32).astype(BF16)
        if col < qkv_cols:
            qkv_ref[:, col:col + PROJ_CN] = y
        else:
            zxbc_ref[:, col - qkv_cols:col - qkv_cols + PROJ_CN] = y

    def project_dt(_):
        dt_t = lax.dot_general(wdt_ref[0:DT_COLS, :], h, nt, preferred_element_type=F32)
        dt_t = jnp.concatenate([dt_t, jnp.zeros((LANES - DT_COLS, PROJ_TM), F32)], axis=0)
        dt_ref[...] = dt_t.T

    def scan_steps():
        for c in range(PROJ_TM // SSD_Q):
            rows = pl.ds(c * SSD_Q, SSD_Q)
            yield from _ssd_steps(
                zxbc_ref.at[rows, pl.ds(0, SSM_WIDTH)],
                zxbc_ref.at[rows, pl.ds(SSM_WIDTH, SSM_WIDTH)],
                zxbc_ref.at[rows, pl.ds(2 * SSM_WIDTH, BC_COLS)],
                dt_ref.at[rows, :],
                *consts, ssm_ref.at[rows, :], halox_ref, halobc_ref, state_ref)

    z0, x0, bc0 = qkv_cols, qkv_cols + SSM_WIDTH, qkv_cols + 2 * SSM_WIDTH
    plan = [(project, x0, 0), (project, x0 + PROJ_CN, 1), (project, bc0, 0), (project_dt, None, 1),
            (project, z0, 1), (project, z0 + PROJ_CN, 1)]
    plan += [(project, col, pieces) for col, pieces in
             zip(range(0, qkv_cols, PROJ_CN), SCAN_PIECES_AFTER_QKV_DOT)]
    scan = scan_steps()
    for emit, col, pieces in plan:
        emit(col)
        for _ in range(pieces):
            next(scan, None)
    for _ in scan:
        pass


def _proj_ssd(x2, g, w_pad, ssd_consts, seq):
    t = x2.shape[0]
    c_specs, c_ops = ssd_consts
    tok = lambda width: pl.BlockSpec((PROJ_TM, width), lambda i: (i, 0))
    return pl.pallas_call(
        functools.partial(_proj_ssd_body, tiles_per_seq=seq // PROJ_TM),
        grid=(t // PROJ_TM,),
        in_specs=[
            tok(D_MODEL),
            _resident((1, D_MODEL)),
            _resident((MAIN_COLS, D_MODEL)),
            pl.BlockSpec((LANES, D_MODEL), lambda i: (MAIN_COLS // LANES, 0), pipeline_mode=pl.Buffered(1)),
        ] + c_specs,
        out_specs=[tok(3 * ATT_WIDTH), tok(SSM_WIDTH)],
        out_shape=[
            jax.ShapeDtypeStruct((t, 3 * ATT_WIDTH), BF16),
            jax.ShapeDtypeStruct((t, SSM_WIDTH), BF16),
        ],
        scratch_shapes=[
            pltpu.VMEM((PROJ_TM, SSD_COLS), BF16),
            pltpu.VMEM((PROJ_TM, LANES), F32),
            pltpu.VMEM((2 * HALO, SSM_WIDTH), F32),
            pltpu.VMEM((2 * HALO, BC_COLS), F32),
            pltpu.VMEM((SSM_GROUPS, SSM_STATE, SSM_GROUP_WIDTH), F32),
        ],
        compiler_params=pltpu.CompilerParams(
            dimension_semantics=("arbitrary",), vmem_limit_bytes=VMEM_LIMIT),
        name="proj_ssd",
    )(x2, g, w_pad, w_pad, *c_ops)


def _ffn_body(x_ref, att_ref, ssm_ref, wo_ref, g2_ref, wup_ref, cw_ref, cb_ref, wdn_ref, gf_ref,
              o_ref, xc_ref, x1p_ref, h2_ref, oc_ref, carry_ref, *, tiles_per_seq):
    i = pl.program_id(0)
    tm, cw = FFN_TM, FFN_CW
    ph = tm // SUBLANES
    n_lane_blocks = D_MODEL // LANES

    @pl.when(i % tiles_per_seq == 0)
    def _():
        carry_ref[...] = jnp.zeros(carry_ref.shape, F32)

    x1 = (x_ref[...]
          + jnp.dot(att_ref[...], wo_ref[0:ATT_WIDTH, :], preferred_element_type=F32)
          + jnp.dot(ssm_ref[...], wo_ref[ATT_WIDTH:, :], preferred_element_type=F32))
    for c in range(n_lane_blocks):
        xc_ref[c] = x1[:, c * LANES:(c + 1) * LANES]
    x1p = jnp.concatenate(
        [jnp.concatenate([xc_ref[c, pl.ds(k, ph, stride=SUBLANES), :] for k in range(SUBLANES)], axis=0)
         for c in range(n_lane_blocks)], axis=1)
    x1p_ref[...] = x1p
    h2_ref[...] = (x1p * lax.rsqrt(jnp.mean(x1p * x1p, axis=-1, keepdims=True) + NORM_EPS)
                   * g2_ref[...]).astype(BF16)

    first_row = lax.broadcasted_iota(jnp.int32, (ph, cw), 0) == 0

    def prev_token(block, carry_slot):
        tail = carry_ref[carry_slot]
        carry_ref[carry_slot] = block[ph - HALO:ph]
        return jnp.where(first_row, jnp.broadcast_to(tail[HALO - 1:HALO, :], (ph, cw)),
                         pltpu.roll(block, 1, 0))

    def up_conv(slot, col):
        u = jnp.dot(h2_ref[...], wup_ref[:, col:col + cw], preferred_element_type=F32)
        blocks = [u[k * ph:(k + 1) * ph] for k in range(SUBLANES)]
        back1 = prev_token(blocks[SUBLANES - 1], 2 * slot)
        back2 = prev_token(blocks[SUBLANES - 2], 2 * slot + 1)
        hist = [back2, back1] + blocks
        w = [cw_ref[t:t + 1, col:col + cw] for t in range(FFN_CONV)]
        b = cb_ref[:, col:col + cw]
        return jnp.concatenate(
            [b + w[2] * hist[k + 2] + w[1] * hist[k + 1] + w[0] * hist[k] for k in range(SUBLANES)],
            axis=0)

    acts = []
    for j in range(FFN_DIM // cw):
        gate = up_conv(2 * j, j * cw)
        val = up_conv(2 * j + 1, FFN_DIM + j * cw)
        acts.append((gate * _sigmoid(gate) * val).astype(BF16))

    x2 = x1p_ref[...] + jnp.dot(jnp.concatenate(acts, axis=1), wdn_ref[...], preferred_element_type=F32)
    out = x2 * lax.rsqrt(jnp.mean(x2 * x2, axis=-1, keepdims=True) + NORM_EPS) * gf_ref[...]
    for c in range(n_lane_blocks):
        for k in range(SUBLANES):
            oc_ref[c, pl.ds(k, ph, stride=SUBLANES), :] = out[k * ph:(k + 1) * ph, c * LANES:(c + 1) * LANES]
    for c in range(n_lane_blocks):
        o_ref[:, c * LANES:(c + 1) * LANES] = oc_ref[c]


def _mix_ffn(x2, att, ssm, w_out, g2, w_up, conv_w, conv_b, w_down, gf, seq):
    t = x2.shape[0]
    n_carry = 2 * (FFN_CONV - 1) * (FFN_DIM // FFN_CW)
    tok = lambda width: pl.BlockSpec((FFN_TM, width), lambda i: (i, 0))
    return pl.pallas_call(
        functools.partial(_ffn_body, tiles_per_seq=seq // FFN_TM),
        grid=(t // FFN_TM,),
        in_specs=[
            tok(D_MODEL), tok(ATT_WIDTH), tok(SSM_WIDTH),
            _resident((ATT_WIDTH + SSM_WIDTH, D_MODEL)),
            _resident((1, D_MODEL)),
            _resident((D_MODEL, 2 * FFN_DIM)),
            _resident((FFN_CONV, 2 * FFN_DIM)),
            _resident((1, 2 * FFN_DIM)),
            _resident((FFN_DIM, D_MODEL)),
            _resident((1, D_MODEL)),
        ],
        out_specs=tok(D_MODEL),
        out_shape=jax.ShapeDtypeStruct((t, D_MODEL), F32),
        scratch_shapes=[
            pltpu.VMEM((D_MODEL // LANES, FFN_TM, LANES), F32),
            pltpu.VMEM((FFN_TM, D_MODEL), F32),
            pltpu.VMEM((FFN_TM, D_MODEL), BF16),
            pltpu.VMEM((D_MODEL // LANES, FFN_TM, LANES), F32),
            pltpu.VMEM((n_carry, HALO, FFN_CW), F32),
        ],
        compiler_params=pltpu.CompilerParams(
            dimension_semantics=("arbitrary",), vmem_limit_bytes=VMEM_LIMIT),
        name="mix_ffn",
    )(x2, att, ssm, w_out, g2, w_up, conv_w, conv_b, w_down, gf)


def kernel(x, rel_bias_table, attn_norm_g, w_in, lambda_q1, lambda_k1, lambda_q2, lambda_k2,
           attn_subln_g, ssm_conv_w, ssm_conv_b, ssm_dt_bias, ssm_a_log, ssm_d, ssm_norm_g,
           w_out, ffn_norm_g, ffn_w_up, ffn_conv_w, ffn_conv_b, ffn_w_down, final_norm_g):
    batch, seq, _ = x.shape
    depth = w_in.shape[0]
    assert seq % max(ATT_QB, SSD_Q, FFN_TM) == 0 and (batch * seq) % PROJ_TM == 0
    x2 = x.reshape(batch * seq, D_MODEL)
    row = lambda v: v.astype(F32).reshape(1, -1)
    for i in range(depth):
        lam_init = 0.8 - 0.6 * math.exp(-0.3 * i)
        lam = (jnp.exp(jnp.sum(lambda_q1[i].astype(F32) * lambda_k1[i].astype(F32)))
               - jnp.exp(jnp.sum(lambda_q2[i].astype(F32) * lambda_k2[i].astype(F32)))
               + lam_init).reshape(1)
        qkv, ssm = _proj_ssd(
            x2, row(attn_norm_g[i]), _w_prep(w_in[i].astype(F32).T),
            _ssd_consts(ssm_conv_w[i], ssm_conv_b[i], ssm_dt_bias[i], ssm_a_log[i], ssm_d[i], ssm_norm_g[i]),
            seq)
        att, (w_out_b, w_up_b, w_down_b) = _attention(
            qkv, lam, rel_bias_table, row(attn_subln_g[i]), batch, seq, lam_init,
            [w_out[i].astype(F32), ffn_w_up[i].astype(F32), ffn_w_down[i].astype(F32)])
        assert depth == 1
        x2 = _mix_ffn(x2, att, ssm, w_out_b, row(ffn_norm_g[i]), w_up_b, ffn_conv_w[i].astype(F32),
                      row(ffn_conv_b[i]), w_down_b, row(final_norm_g), seq)
    return x2.reshape(batch, seq, D_MODEL)
```

```python
import functools
import math

import jax
import jax.numpy as jnp
from jax import lax
from jax.experimental import pallas as pl
from jax.experimental.pallas import tpu as pltpu

F32 = jnp.float32
BF16 = jnp.bfloat16

D_MODEL = 1024
CHUNK = 64
ATT_HEADS = 8
ATT_HEAD_DIM = 64
ATT_V_DIM = 2 * ATT_HEAD_DIM
ATT_WIDTH = ATT_HEADS * ATT_V_DIM
SSM_HEADS = 16
SSM_HEAD_DIM = 64
SSM_WIDTH = SSM_HEADS * SSM_HEAD_DIM
SSM_GROUPS = 2
SSM_STATE = 128
SSM_CONV = 4
SSM_HEADS_PER_GROUP = SSM_HEADS // SSM_GROUPS
SSM_GROUP_WIDTH = SSM_WIDTH // SSM_GROUPS
FFN_DIM = 2816
FFN_CONV = 3
REL_BUCKETS = 32
REL_MAX_DIST = 128
NORM_EPS = 1e-6
SUBLN_EPS = 1e-5
SSM_NORM_EPS = 1e-5
BC_COLS = 2 * SSM_GROUPS * SSM_STATE
MAIN_COLS = 3 * ATT_WIDTH + SSM_WIDTH + SSM_WIDTH + BC_COLS
DT_COLS = SSM_HEADS
LOG2E = math.log2(math.e)

LANES = 128
SUBLANES = 8
VMEM_LIMIT = 56 * 1024 * 1024

PROJ_TM = 512
PROJ_CN = 512
ATT_QB = 256
ATT_KB = 256
SSD_Q = 256
FFN_TM = 512
FFN_CW = 256
HALO = SUBLANES

def _resident(shape):
    nd = len(shape)
    return pl.BlockSpec(shape, lambda *_: (0,) * nd, pipeline_mode=pl.Buffered(1))


def _sigmoid(x):
    return 1.0 / (1.0 + jnp.exp2(x * -LOG2E))


W_PAD_COLS = MAIN_COLS + LANES
WPREP_LANES = 256


def _wprep_body(w_ref, o_ref):
    q_scale = LOG2E * ATT_HEAD_DIM ** -0.5
    o_ref[0:ATT_WIDTH, :] = (w_ref[0:ATT_WIDTH, :] * q_scale).astype(BF16)
    o_ref[ATT_WIDTH:MAIN_COLS + DT_COLS, :] = w_ref[ATT_WIDTH:MAIN_COLS + DT_COLS, :].astype(BF16)
    o_ref[MAIN_COLS + DT_COLS:W_PAD_COLS, :] = jnp.zeros((LANES - DT_COLS, WPREP_LANES), BF16)


def _w_prep(w_t):
    cols = lambda rows: pl.BlockSpec((rows, WPREP_LANES), lambda j: (0, j))
    return pl.pallas_call(
        _wprep_body,
        grid=(D_MODEL // WPREP_LANES,),
        in_specs=[cols(MAIN_COLS + DT_COLS)],
        out_specs=cols(W_PAD_COLS),
        out_shape=jax.ShapeDtypeStruct((W_PAD_COLS, D_MODEL), BF16),
        compiler_params=pltpu.CompilerParams(
            dimension_semantics=("arbitrary",), vmem_limit_bytes=VMEM_LIMIT),
        name="w_prep",
    )(w_t)


FAR_BUCKET = REL_BUCKETS // 2 - 1
BIAS_SPAN = 4 * ATT_KB


def _t5_bucket(rel):
    nb = REL_BUCKETS // 2
    max_exact = nb // 2
    bucket = jnp.where(rel > 0, nb, 0)
    n = jnp.abs(rel)
    nf = jnp.maximum(n, 1).astype(F32)
    large = max_exact + (jnp.log(nf / max_exact) / math.log(REL_MAX_DIST / max_exact)
                         * (nb - max_exact)).astype(jnp.int32)
    large = jnp.minimum(large, nb - 1)
    return bucket + jnp.where(n < max_exact, n, large)


def _bias_bucket_row():
    rel = jnp.arange(BIAS_SPAN, dtype=jnp.int32) - 2 * ATT_KB
    return jnp.broadcast_to(_t5_bucket(rel)[None, :], (SUBLANES, BIAS_SPAN))


def _bias_tiles(tbl_ref, idx_ref, h):
    idx = idx_ref[...]
    base = tbl_ref[FAR_BUCKET, h]
    r = jnp.zeros(idx.shape, F32)
    for b in range(REL_BUCKETS):
        r = jnp.where(idx == b, (tbl_ref[b, h] - base) * LOG2E, r)
    rows = jnp.concatenate([r] * (ATT_QB // SUBLANES), axis=0)
    rolled = pltpu.roll(rows, 0, 1, stride=1, stride_axis=0)
    return rolled[:, 2 * ATT_KB:3 * ATT_KB], rolled[:, ATT_KB:2 * ATT_KB]


ATT_STREAMS = 2


def _attn_body(lam_ref, tbl_ref, q_ref, k_ref, v_ref, idx_ref, subg_ref, *rest, lam_init):
    n_cast = (len(rest) - 2) // 2
    cast_in, (o_ref, *cast_out, v1_ref) = rest[:n_cast], rest[n_cast:]
    for src, dst in zip(cast_in, cast_out):
        dst[...] = src[...].astype(BF16)

    qb, kb = ATT_QB, ATT_KB
    seq = q_ref.shape[1]
    lane = lax.broadcasted_iota(jnp.int32, (qb, LANES), 1)
    lam = lam_ref[0]

    @functools.cache
    def bias_and_mask():
        row = lax.broadcasted_iota(jnp.int32, (qb, kb), 0)
        col = lax.broadcasted_iota(jnp.int32, (qb, kb), 1)
        allowed = (col // CHUNK) <= (row // CHUNK)
        b_diag, b_prev = _bias_tiles(tbl_ref, idx_ref, pl.program_id(1))
        return tuple(jnp.concatenate([t, t], axis=0) for t in (allowed, b_diag, b_prev))

    @functools.cache
    def build_v1(s):
        v1_ref[s, :, 0:LANES] = v_ref[s]
        v1_ref[s, :, LANES:2 * LANES] = jnp.ones((seq, LANES), BF16)

    def scores(s, i):
        kvl = (i + 1) * kb
        q = q_ref[s, i * qb:(i + 1) * qb, :]
        zero = jnp.zeros_like(q)
        qs = jnp.concatenate([jnp.where(lane < ATT_HEAD_DIM, q, zero),
                              jnp.where(lane >= ATT_HEAD_DIM, q, zero)], axis=0)
        sc = lax.dot_general(qs, k_ref[s, 0:kvl, :], (((1,), (1,)), ((), ())),
                             preferred_element_type=F32)
        allowed, b_diag, b_prev = bias_and_mask()
        blocks = [sc[:, j * kb:(j + 1) * kb] for j in range(i + 1)]
        blocks[i] = jnp.where(allowed, blocks[i] + b_diag, -1e30)
        if i >= 1:
            blocks[i - 1] = blocks[i - 1] + b_prev
        m = jnp.max(functools.reduce(jnp.maximum, blocks), axis=-1, keepdims=True)
        return blocks, m

    def probs(blocks, m):
        return jnp.concatenate([jnp.exp2(blk - m).astype(BF16) for blk in blocks], axis=1)

    def finish(s, i, p):
        build_v1(s)
        kvl = (i + 1) * kb
        acc = jnp.dot(p, v1_ref[s, 0:kvl, :], preferred_element_type=F32)
        o = acc[:, 0:LANES] / acc[:, LANES:2 * LANES]
        o = o[:qb] - lam * o[qb:]
        o = o * lax.rsqrt(jnp.mean(o * o, axis=-1, keepdims=True) + SUBLN_EPS) * subg_ref[...]
        o_ref[s, i * qb:(i + 1) * qb, :] = (o * (1.0 - lam_init)).astype(BF16)

    order = [(s, i) for i in reversed(range(seq // qb)) for s in range(ATT_STREAMS)]
    scored, exped = {}, {}
    for n in range(len(order) + 2 * ATT_STREAMS):
        if n < len(order):
            scored[n] = scores(*order[n])
        n1, n2 = n - ATT_STREAMS, n - 2 * ATT_STREAMS
        if 0 <= n1 < len(order):
            exped[n1] = probs(*scored.pop(n1))
        if 0 <= n2 < len(order):
            finish(*order[n2], exped.pop(n2))


BF16_ROWS = 2 * SUBLANES


def _attention(qkv, lam, rel_table, sub_g, batch, seq, lam_init, cast_weights):
    assert batch % ATT_STREAMS == 0
    kcol = ATT_WIDTH // LANES
    vcol = 2 * ATT_WIDTH // LANES
    grid = (batch // ATT_STREAMS, ATT_HEADS)
    n_steps = grid[0] * grid[1]
    rows = lambda col0: pl.BlockSpec((None, ATT_STREAMS, seq, LANES), lambda b, h: (b, 0, 0, col0 + h))

    def cast_spec(w):
        n_blocks = next(n for n in range(n_steps, 0, -1)
                        if n_steps % n == 0 and w.shape[0] % (n * BF16_ROWS) == 0)
        return pl.BlockSpec((w.shape[0] // n_blocks, w.shape[1]),
                            lambda b, h: ((b * ATT_HEADS + h) * n_blocks // n_steps, 0))

    cast_specs = [cast_spec(w) for w in cast_weights]
    att, *cast = pl.pallas_call(
        functools.partial(_attn_body, lam_init=lam_init),
        grid=grid,
        in_specs=[
            pl.BlockSpec(memory_space=pltpu.SMEM),
            pl.BlockSpec(memory_space=pltpu.SMEM),
            rows(0), rows(kcol), rows(vcol),
            _resident((SUBLANES, BIAS_SPAN)),
            pl.BlockSpec((1, ATT_V_DIM), lambda b, h: (0, 0)),
        ] + cast_specs,
        out_specs=[rows(0)] + cast_specs,
        out_shape=[jax.ShapeDtypeStruct((batch // ATT_STREAMS, ATT_STREAMS, seq, ATT_WIDTH), BF16)]
        + [jax.ShapeDtypeStruct(w.shape, BF16) for w in cast_weights],
        scratch_shapes=[pltpu.VMEM((ATT_STREAMS, seq, 2 * LANES), BF16)],
        compiler_params=pltpu.CompilerParams(
            dimension_semantics=("arbitrary", "arbitrary"), vmem_limit_bytes=VMEM_LIMIT),
        name="diff_attn",
    )(lam, rel_table.astype(F32), *[qkv.reshape(batch // ATT_STREAMS, ATT_STREAMS, seq, -1)] * 3,
      _bias_bucket_row(), sub_g, *cast_weights)
    return att.reshape(batch * seq, ATT_WIDTH), cast


def _ssd_steps(z_ref, xs_ref, bc_ref, dt_ref, cw_ref, cb_ref, dtb_ref, alog_ref, dexp_ref, ng_ref,
               o_ref, halox_ref, halobc_ref, state_ref):
    q = SSD_Q
    x_cols, bc_cols = pl.ds(0, SSM_WIDTH), pl.ds(SSM_WIDTH, BC_COLS)
    cwx_ref, cbx_ref = cw_ref.at[:, x_cols], cb_ref.at[:, x_cols]
    cwbc_ref, cbbc_ref = cw_ref.at[:, bc_cols], cb_ref.at[:, bc_cols]

    def conv_silu(raw_ref, halo_ref, w_ref, b_ref):
        x = raw_ref[...].astype(F32)
        first_rows = lax.broadcasted_iota(jnp.int32, (HALO, x.shape[1]), 0)
        acc = b_ref[...] + w_ref[SSM_CONV - 1:SSM_CONV, :] * x
        for j in range(1, SSM_CONV):
            wj = w_ref[SSM_CONV - 1 - j:SSM_CONV - j, :]
            rolled = pltpu.roll(x, j, 0)
            head = jnp.where(first_rows < j, halo_ref[HALO - j:2 * HALO - j, :], rolled[0:HALO])
            acc = acc + wj * jnp.concatenate([head, rolled[HALO:]], axis=0)
        halo_ref[0:HALO, :] = x[q - HALO:q]
        return acc * _sigmoid(acc)

    xs = conv_silu(xs_ref, halox_ref, cwx_ref, cbx_ref)
    yield
    bc = conv_silu(bc_ref, halobc_ref, cwbc_ref, cbbc_ref)
    yield

    dtr = dt_ref[...] + dtb_ref[...]
    dt = jnp.maximum(dtr, 0.0) + jnp.log(1.0 + jnp.exp(-jnp.abs(dtr)))
    a = (-LOG2E * jnp.exp(alog_ref[...])) * dt

    row = lax.broadcasted_iota(jnp.int32, (q, q), 0)
    col = lax.broadcasted_iota(jnp.int32, (q, q), 1)
    causal = col <= row
    acum = a
    row8 = lax.broadcasted_iota(jnp.int32, (q, LANES), 0)
    sh = 1
    while sh < q:
        if sh < SUBLANES:
            shifted = jnp.where(row8 < sh, 0.0, pltpu.roll(acum, sh, 0))
        else:
            shifted = jnp.concatenate([jnp.zeros((sh, LANES), F32), acum[:q - sh]], axis=0)
        acum = acum + shifted
        sh *= 2

    lane = lax.broadcasted_iota(jnp.int32, (q, LANES), 1)

    def expand(x):
        cols = [jnp.broadcast_to(x[:, h:h + 1], (q, LANES)) for h in range(SSM_HEADS)]
        return jnp.concatenate([jnp.where(lane < SSM_HEAD_DIM, cols[h], cols[h + 1])
                                for h in range(0, SSM_HEADS, 2)], axis=1)

    dt_exp = expand(dt)
    acum_exp = expand(acum)
    ea_exp = jnp.exp2(acum_exp)
    last_exp = acum_exp[q - 1:q, :]
    dte_exp = jnp.exp2(last_exp - acum_exp)
    ea_last = ea_exp[q - 1:q, :]

    xdt = xs * dt_exp
    xdt_b = xdt.astype(BF16)
    acum_t = acum.T
    pair_lane = lax.broadcasted_iota(jnp.int32, (q, LANES), 1)
    yield

    for g in range(SSM_GROUPS):
        gsl = slice(g * SSM_GROUP_WIDTH, (g + 1) * SSM_GROUP_WIDTH)
        bg = bc[:, g * SSM_STATE:(g + 1) * SSM_STATE]
        cg = bc[:, (SSM_GROUPS + g) * SSM_STATE:(SSM_GROUPS + g + 1) * SSM_STATE]
        bg_b = bg.astype(BF16)
        cg_b = cg.astype(BF16)
        cb = lax.dot_general(cg_b, bg_b, (((1,), (1,)), ((), ())), preferred_element_type=F32)

        def masked(h):
            seg = acum[:, h:h + 1] - acum_t[h:h + 1, :]
            return (cb * jnp.exp2(jnp.where(causal, seg, -jnp.inf))).astype(BF16)

        pairs = []
        for j in range(SSM_HEADS_PER_GROUP // 2):
            h0 = g * SSM_HEADS_PER_GROUP + 2 * j
            xp = xdt_b[:, h0 * SSM_HEAD_DIM:(h0 + 2) * SSM_HEAD_DIM]
            r0 = jnp.dot(masked(h0), xp, preferred_element_type=F32)
            r1 = jnp.dot(masked(h0 + 1), xp, preferred_element_type=F32)
            pairs.append(jnp.where(pair_lane < SSM_HEAD_DIM, r0, r1))
            if j % 2 == 1:
                yield
        y_diag = jnp.concatenate(pairs, axis=1)

        st = state_ref[g]
        y_off = jnp.dot(cg_b, st.astype(BF16), preferred_element_type=F32) * ea_exp[:, gsl]
        w = (xdt[:, gsl] * dte_exp[:, gsl]).astype(BF16)
        s_new = jnp.dot(bg.T.astype(BF16), w, preferred_element_type=F32)
        state_ref[g] = st * ea_last[:, gsl] + s_new

        y = y_diag + y_off + xs[:, gsl] * dexp_ref[:, gsl]
        zf = z_ref[:, gsl].astype(F32)
        gated = y * (zf * _sigmoid(zf))
        gated = gated * lax.rsqrt(jnp.mean(gated * gated, axis=-1, keepdims=True) + SSM_NORM_EPS)
        o_ref[:, gsl] = (gated * ng_ref[:, gsl]).astype(BF16)
        yield


def _ssd_consts(conv_w, conv_b, dt_bias, a_log, d_skip, norm_g):
    pad = LANES - SSM_HEADS
    dtb = jnp.pad(dt_bias.astype(F32), (0, pad)).reshape(1, LANES)
    alog = jnp.pad(a_log.astype(F32), (0, pad)).reshape(1, LANES)
    dexp = jnp.repeat(d_skip.astype(F32), SSM_HEAD_DIM).reshape(1, SSM_WIDTH)
    in_specs = [
        _resident((SSM_CONV, SSM_WIDTH + BC_COLS)),
        _resident((1, SSM_WIDTH + BC_COLS)),
        _resident((1, LANES)),
        _resident((1, LANES)),
        _resident((1, SSM_WIDTH)),
        _resident((1, SSM_WIDTH)),
    ]
    operands = (conv_w.astype(F32), conv_b.astype(F32).reshape(1, -1), dtb, alog, dexp,
                norm_g.astype(F32).reshape(1, -1))
    return in_specs, operands


SSD_COLS = 2 * SSM_WIDTH + BC_COLS
SCAN_PIECES_AFTER_QKV_DOT = (2, 3, 3, 3, 3, 2)


def _proj_ssd_body(x_ref, g_ref, w_ref, wdt_ref, wo_ref, *rest, tiles_per_seq):
    consts, (qkv_ref, x1s_ref, zxbc_ref, dt_ref, halox_ref, halobc_ref, state_ref, ssm_ref) = (
        rest[:-8], rest[-8:])

    @pl.when(pl.program_id(0) % tiles_per_seq == 0)
    def _():
        halox_ref[...] = jnp.zeros(halox_ref.shape, F32)
        halobc_ref[...] = jnp.zeros(halobc_ref.shape, F32)
        state_ref[...] = jnp.zeros(state_ref.shape, F32)

    x = x_ref[...]
    h = x * lax.rsqrt(jnp.mean(x * x, axis=-1, keepdims=True) + NORM_EPS) * g_ref[...]
    h = h.astype(BF16)
    nt = (((1,), (1,)), ((), ()))

    qkv_cols = 3 * ATT_WIDTH

    def project(col):
        y = lax.dot_general(h, w_ref[col:col + PROJ_CN, :], nt, preferred_element_type=F32).astype(BF16)
        if col < qkv_cols:
            qkv_ref[:, col:col + PROJ_CN] = y
        else:
            zxbc_ref[:, col - qkv_cols:col - qkv_cols + PROJ_CN] = y

    def project_dt(_):
        dt_t = lax.dot_general(wdt_ref[0:DT_COLS, :], h, nt, preferred_element_type=F32)
        dt_t = jnp.concatenate([dt_t, jnp.zeros((LANES - DT_COLS, PROJ_TM), F32)], axis=0)
        dt_ref[...] = dt_t.T

    def scan_steps():
        for c in range(PROJ_TM // SSD_Q):
            rows = pl.ds(c * SSD_Q, SSD_Q)
            yield from _ssd_steps(
                zxbc_ref.at[rows, pl.ds(0, SSM_WIDTH)],
                zxbc_ref.at[rows, pl.ds(SSM_WIDTH, SSM_WIDTH)],
                zxbc_ref.at[rows, pl.ds(2 * SSM_WIDTH, BC_COLS)],
                dt_ref.at[rows, :],
                *consts, ssm_ref.at[rows, :], halox_ref, halobc_ref, state_ref)
            x1s_ref[rows, :] = x_ref[rows, :] + jnp.dot(ssm_ref[rows, :], wo_ref[...],
                                                        preferred_element_type=F32)
            yield

    z0, x0, bc0 = qkv_cols, qkv_cols + SSM_WIDTH, qkv_cols + 2 * SSM_WIDTH
    plan = [(project, x0, 0), (project, x0 + PROJ_CN, 1), (project, bc0, 0), (project_dt, None, 1),
            (project, z0, 1), (project, z0 + PROJ_CN, 1)]
    plan += [(project, col, pieces) for col, pieces in
             zip(range(0, qkv_cols, PROJ_CN), SCAN_PIECES_AFTER_QKV_DOT)]
    scan = scan_steps()
    for emit, col, pieces in plan:
        emit(col)
        for _ in range(pieces):
            next(scan, None)
    for _ in scan:
        pass


def _proj_ssd(x2, g, w_pad, wo_ssm, ssd_consts, seq):
    t = x2.shape[0]
    c_specs, c_ops = ssd_consts
    tok = lambda width: pl.BlockSpec((PROJ_TM, width), lambda i: (i, 0))
    return pl.pallas_call(
        functools.partial(_proj_ssd_body, tiles_per_seq=seq // PROJ_TM),
        grid=(t // PROJ_TM,),
        in_specs=[
            tok(D_MODEL),
            _resident((1, D_MODEL)),
            _resident((MAIN_COLS, D_MODEL)),
            pl.BlockSpec((LANES, D_MODEL), lambda i: (MAIN_COLS // LANES, 0), pipeline_mode=pl.Buffered(1)),
            _resident((SSM_WIDTH, D_MODEL)),
        ] + c_specs,
        out_specs=[tok(3 * ATT_WIDTH), tok(D_MODEL)],
        out_shape=[
            jax.ShapeDtypeStruct((t, 3 * ATT_WIDTH), BF16),
            jax.ShapeDtypeStruct((t, D_MODEL), F32),
        ],
        scratch_shapes=[
            pltpu.VMEM((PROJ_TM, SSD_COLS), BF16),
            pltpu.VMEM((PROJ_TM, LANES), F32),
            pltpu.VMEM((2 * HALO, SSM_WIDTH), F32),
            pltpu.VMEM((2 * HALO, BC_COLS), F32),
            pltpu.VMEM((SSM_GROUPS, SSM_STATE, SSM_GROUP_WIDTH), F32),
            pltpu.VMEM((PROJ_TM, SSM_WIDTH), BF16),
        ],
        compiler_params=pltpu.CompilerParams(
            dimension_semantics=("arbitrary",), vmem_limit_bytes=VMEM_LIMIT),
        name="proj_ssd",
    )(x2, g, w_pad, w_pad, wo_ssm, *c_ops)


def _ffn_body(x1s_ref, att_ref, wo_ref, g2_ref, wup_ref, cw_ref, cb_ref, wdn_ref, gf_ref,
              o_ref, xc_ref, x1p_ref, h2_ref, oc_ref, carry_ref, *, tiles_per_seq):
    i = pl.program_id(0)
    tm, cw = FFN_TM, FFN_CW
    ph = tm // SUBLANES
    n_lane_blocks = D_MODEL // LANES

    @pl.when(i % tiles_per_seq == 0)
    def _():
        carry_ref[...] = jnp.zeros(carry_ref.shape, F32)

    x1 = x1s_ref[...] + jnp.dot(att_ref[...], wo_ref[...], preferred_element_type=F32)
    for c in range(n_lane_blocks):
        xc_ref[c] = x1[:, c * LANES:(c + 1) * LANES]
    x1p = jnp.concatenate(
        [jnp.concatenate([xc_ref[c, pl.ds(k, ph, stride=SUBLANES), :] for k in range(SUBLANES)], axis=0)
         for c in range(n_lane_blocks)], axis=1)
    x1p_ref[...] = x1p
    h2_ref[...] = (x1p * lax.rsqrt(jnp.mean(x1p * x1p, axis=-1, keepdims=True) + NORM_EPS)
                   * g2_ref[...]).astype(BF16)

    first_row = lax.broadcasted_iota(jnp.int32, (ph, cw), 0) == 0

    def prev_token(block, carry_slot):
        tail = carry_ref[carry_slot]
        carry_ref[carry_slot] = block[ph - HALO:ph]
        return jnp.where(first_row, jnp.broadcast_to(tail[HALO - 1:HALO, :], (ph, cw)),
                         pltpu.roll(block, 1, 0))

    def up_conv(slot, col):
        u = jnp.dot(h2_ref[...], wup_ref[:, col:col + cw], preferred_element_type=F32)
        blocks = [u[k * ph:(k + 1) * ph] for k in range(SUBLANES)]
        back1 = prev_token(blocks[SUBLANES - 1], 2 * slot)
        back2 = prev_token(blocks[SUBLANES - 2], 2 * slot + 1)
        hist = [back2, back1] + blocks
        w = [cw_ref[t:t + 1, col:col + cw] for t in range(FFN_CONV)]
        b = cb_ref[:, col:col + cw]
        return jnp.concatenate(
            [b + w[2] * hist[k + 2] + w[1] * hist[k + 1] + w[0] * hist[k] for k in range(SUBLANES)],
            axis=0)

    acts = []
    for j in range(FFN_DIM // cw):
        gate = up_conv(2 * j, j * cw)
        val = up_conv(2 * j + 1, FFN_DIM + j * cw)
        acts.append((gate * _sigmoid(gate) * val).astype(BF16))

    x2 = x1p_ref[...] + jnp.dot(jnp.concatenate(acts, axis=1), wdn_ref[...], preferred_element_type=F32)
    out = x2 * lax.rsqrt(jnp.mean(x2 * x2, axis=-1, keepdims=True) + NORM_EPS) * gf_ref[...]
    for c in range(n_lane_blocks):
        for k in range(SUBLANES):
            oc_ref[c, pl.ds(k, ph, stride=SUBLANES), :] = out[k * ph:(k + 1) * ph, c * LANES:(c + 1) * LANES]
    for c in range(n_lane_blocks):
        o_ref[:, c * LANES:(c + 1) * LANES] = oc_ref[c]


def _mix_ffn(x1s, att, wo_att, g2, w_up, conv_w, conv_b, w_down, gf, seq):
    t = x1s.shape[0]
    n_carry = 2 * (FFN_CONV - 1) * (FFN_DIM // FFN_CW)
    tok = lambda width: pl.BlockSpec((FFN_TM, width), lambda i: (i, 0))
    return pl.pallas_call(
        functools.partial(_ffn_body, tiles_per_seq=seq // FFN_TM),
        grid=(t // FFN_TM,),
        in_specs=[
            tok(D_MODEL), tok(ATT_WIDTH),
            _resident((ATT_WIDTH, D_MODEL)),
            _resident((1, D_MODEL)),
            _resident((D_MODEL, 2 * FFN_DIM)),
            _resident((FFN_CONV, 2 * FFN_DIM)),
            _resident((1, 2 * FFN_DIM)),
            _resident((FFN_DIM, D_MODEL)),
            _resident((1, D_MODEL)),
        ],
        out_specs=tok(D_MODEL),
        out_shape=jax.ShapeDtypeStruct((t, D_MODEL), F32),
        scratch_shapes=[
            pltpu.VMEM((D_MODEL // LANES, FFN_TM, LANES), F32),
            pltpu.VMEM((FFN_TM, D_MODEL), F32),
            pltpu.VMEM((FFN_TM, D_MODEL), BF16),
            pltpu.VMEM((D_MODEL // LANES, FFN_TM, LANES), F32),
            pltpu.VMEM((n_carry, HALO, FFN_CW), F32),
        ],
        compiler_params=pltpu.CompilerParams(
            dimension_semantics=("arbitrary",), vmem_limit_bytes=VMEM_LIMIT),
        name="mix_ffn",
    )(x1s, att, wo_att, g2, w_up, conv_w, conv_b, w_down, gf)


def kernel(x, rel_bias_table, attn_norm_g, w_in, lambda_q1, lambda_k1, lambda_q2, lambda_k2,
           attn_subln_g, ssm_conv_w, ssm_conv_b, ssm_dt_bias, ssm_a_log, ssm_d, ssm_norm_g,
           w_out, ffn_norm_g, ffn_w_up, ffn_conv_w, ffn_conv_b, ffn_w_down, final_norm_g):
    batch, seq, _ = x.shape
    depth = w_in.shape[0]
    assert seq % max(ATT_QB, SSD_Q, FFN_TM) == 0 and (batch * seq) % PROJ_TM == 0
    x2 = x.reshape(batch * seq, D_MODEL)
    row = lambda v: v.astype(F32).reshape(1, -1)
    for i in range(depth):
        lam_init = 0.8 - 0.6 * math.exp(-0.3 * i)
        lam = (jnp.exp(jnp.sum(lambda_q1[i].astype(F32) * lambda_k1[i].astype(F32)))
               - jnp.exp(jnp.sum(lambda_q2[i].astype(F32) * lambda_k2[i].astype(F32)))
               + lam_init).reshape(1)
        qkv, x1s = _proj_ssd(
            x2, row(attn_norm_g[i]), _w_prep(w_in[i].astype(F32).T), w_out[i, ATT_WIDTH:].astype(BF16),
            _ssd_consts(ssm_conv_w[i], ssm_conv_b[i], ssm_dt_bias[i], ssm_a_log[i], ssm_d[i], ssm_norm_g[i]),
            seq)
        att, (wo_att_b, w_up_b, w_down_b) = _attention(
            qkv, lam, rel_bias_table, row(attn_subln_g[i]), batch, seq, lam_init,
            [w_out[i, :ATT_WIDTH].astype(F32), ffn_w_up[i].astype(F32), ffn_w_down[i].astype(F32)])
        assert depth == 1
        x2 = _mix_ffn(x1s, att, wo_att_b, row(ffn_norm_g[i]), w_up_b, ffn_conv_w[i].astype(F32),
                      row(ffn_conv_b[i]), w_down_b, row(final_norm_g), seq)
    return x2.reshape(batch, seq, D_MODEL)
```

```python
import functools
import math

import jax
import jax.numpy as jnp
from jax import lax
from jax.experimental import pallas as pl
from jax.experimental.pallas import tpu as pltpu

F32 = jnp.float32
BF16 = jnp.bfloat16

D_MODEL = 1024
CHUNK = 64
ATT_HEADS = 8
ATT_HEAD_DIM = 64
ATT_V_DIM = 2 * ATT_HEAD_DIM
ATT_WIDTH = ATT_HEADS * ATT_V_DIM
SSM_HEADS = 16
SSM_HEAD_DIM = 64
SSM_WIDTH = SSM_HEADS * SSM_HEAD_DIM
SSM_GROUPS = 2
SSM_STATE = 128
SSM_CONV = 4
SSM_HEADS_PER_GROUP = SSM_HEADS // SSM_GROUPS
SSM_GROUP_WIDTH = SSM_WIDTH // SSM_GROUPS
FFN_DIM = 2816
FFN_CONV = 3
REL_BUCKETS = 32
REL_MAX_DIST = 128
NORM_EPS = 1e-6
SUBLN_EPS = 1e-5
SSM_NORM_EPS = 1e-5
BC_COLS = 2 * SSM_GROUPS * SSM_STATE
MAIN_COLS = 3 * ATT_WIDTH + SSM_WIDTH + SSM_WIDTH + BC_COLS
DT_COLS = SSM_HEADS
LOG2E = math.log2(math.e)

LANES = 128
SUBLANES = 8
VMEM_LIMIT = 56 * 1024 * 1024

PROJ_TM = 512
PROJ_CN = 512
ATT_QB = 256
ATT_KB = 256
SSD_Q = 256
FFN_TM = 512
FFN_CW = 256
HALO = SUBLANES

def _resident(shape):
    nd = len(shape)
    return pl.BlockSpec(shape, lambda *_: (0,) * nd, pipeline_mode=pl.Buffered(1))


def _sigmoid(x):
    return 1.0 / (1.0 + jnp.exp2(x * -LOG2E))


W_PAD_COLS = MAIN_COLS + LANES
WPREP_LANES = 256


def _wprep_body(w_ref, o_ref):
    q_scale = LOG2E * ATT_HEAD_DIM ** -0.5
    o_ref[0:ATT_WIDTH, :] = (w_ref[0:ATT_WIDTH, :] * q_scale).astype(BF16)
    o_ref[ATT_WIDTH:MAIN_COLS + DT_COLS, :] = w_ref[ATT_WIDTH:MAIN_COLS + DT_COLS, :].astype(BF16)
    o_ref[MAIN_COLS + DT_COLS:W_PAD_COLS, :] = jnp.zeros((LANES - DT_COLS, WPREP_LANES), BF16)


def _w_prep(w_t):
    cols = lambda rows: pl.BlockSpec((rows, WPREP_LANES), lambda j: (0, j))
    return pl.pallas_call(
        _wprep_body,
        grid=(D_MODEL // WPREP_LANES,),
        in_specs=[cols(MAIN_COLS + DT_COLS)],
        out_specs=cols(W_PAD_COLS),
        out_shape=jax.ShapeDtypeStruct((W_PAD_COLS, D_MODEL), BF16),
        compiler_params=pltpu.CompilerParams(
            dimension_semantics=("arbitrary",), vmem_limit_bytes=VMEM_LIMIT),
        name="w_prep",
    )(w_t)


FAR_BUCKET = REL_BUCKETS // 2 - 1
BIAS_SPAN = 4 * ATT_KB


def _t5_bucket(rel):
    nb = REL_BUCKETS // 2
    max_exact = nb // 2
    bucket = jnp.where(rel > 0, nb, 0)
    n = jnp.abs(rel)
    nf = jnp.maximum(n, 1).astype(F32)
    large = max_exact + (jnp.log(nf / max_exact) / math.log(REL_MAX_DIST / max_exact)
                         * (nb - max_exact)).astype(jnp.int32)
    large = jnp.minimum(large, nb - 1)
    return bucket + jnp.where(n < max_exact, n, large)


def _bias_bucket_row():
    rel = jnp.arange(BIAS_SPAN, dtype=jnp.int32) - 2 * ATT_KB
    return jnp.broadcast_to(_t5_bucket(rel)[None, :], (SUBLANES, BIAS_SPAN))


def _bias_tiles(tbl_ref, idx_ref, h):
    idx = idx_ref[...]
    base = tbl_ref[FAR_BUCKET, h]
    r = jnp.zeros(idx.shape, F32)
    for b in range(REL_BUCKETS):
        r = jnp.where(idx == b, (tbl_ref[b, h] - base) * LOG2E, r)
    rows = jnp.concatenate([r] * (ATT_QB // SUBLANES), axis=0)
    rolled = pltpu.roll(rows, 0, 1, stride=1, stride_axis=0)
    return rolled[:, 2 * ATT_KB:3 * ATT_KB], rolled[:, ATT_KB:2 * ATT_KB]


ATT_STREAMS = 2


def _attn_body(lam_ref, tbl_ref, q_ref, k_ref, v_ref, idx_ref, subg_ref, *rest, lam_init):
    n_cast = (len(rest) - 2) // 2
    cast_in, (o_ref, *cast_out, v1_ref) = rest[:n_cast], rest[n_cast:]
    for src, dst in zip(cast_in, cast_out):
        dst[...] = src[...].astype(BF16)

    qb, kb = ATT_QB, ATT_KB
    seq = q_ref.shape[1]
    lane = lax.broadcasted_iota(jnp.int32, (qb, LANES), 1)
    lam = lam_ref[0]

    @functools.cache
    def bias_and_mask():
        row = lax.broadcasted_iota(jnp.int32, (qb, kb), 0)
        col = lax.broadcasted_iota(jnp.int32, (qb, kb), 1)
        allowed = (col // CHUNK) <= (row // CHUNK)
        b_diag, b_prev = _bias_tiles(tbl_ref, idx_ref, pl.program_id(1))
        return tuple(jnp.concatenate([t, t], axis=0) for t in (allowed, b_diag, b_prev))

    @functools.cache
    def build_v1(s):
        v1_ref[s, :, 0:LANES] = v_ref[s]
        v1_ref[s, :, LANES:2 * LANES] = jnp.ones((seq, LANES), BF16)

    def scores(s, i):
        kvl = (i + 1) * kb
        q = q_ref[s, i * qb:(i + 1) * qb, :]
        zero = jnp.zeros_like(q)
        qs = jnp.concatenate([jnp.where(lane < ATT_HEAD_DIM, q, zero),
                              jnp.where(lane >= ATT_HEAD_DIM, q, zero)], axis=0)
        sc = lax.dot_general(qs, k_ref[s, 0:kvl, :], (((1,), (1,)), ((), ())),
                             preferred_element_type=F32)
        allowed, b_diag, b_prev = bias_and_mask()
        blocks = [sc[:, j * kb:(j + 1) * kb] for j in range(i + 1)]
        blocks[i] = jnp.where(allowed, blocks[i] + b_diag, -1e30)
        if i >= 1:
            blocks[i - 1] = blocks[i - 1] + b_prev
        m = jnp.max(functools.reduce(jnp.maximum, blocks), axis=-1, keepdims=True)
        return blocks, m

    def probs(blocks, m):
        return jnp.concatenate([jnp.exp2(blk - m).astype(BF16) for blk in blocks], axis=1)

    def finish(s, i, p):
        build_v1(s)
        kvl = (i + 1) * kb
        acc = jnp.dot(p, v1_ref[s, 0:kvl, :], preferred_element_type=F32)
        o = acc[:, 0:LANES] / acc[:, LANES:2 * LANES]
        o = o[:qb] - lam * o[qb:]
        o = o * lax.rsqrt(jnp.mean(o * o, axis=-1, keepdims=True) + SUBLN_EPS) * subg_ref[...]
        o_ref[s, i * qb:(i + 1) * qb, :] = (o * (1.0 - lam_init)).astype(BF16)

    order = [(s, i) for i in reversed(range(seq // qb)) for s in range(ATT_STREAMS)]
    scored, exped = {}, {}
    for n in range(len(order) + 2 * ATT_STREAMS):
        if n < len(order):
            scored[n] = scores(*order[n])
        n1, n2 = n - ATT_STREAMS, n - 2 * ATT_STREAMS
        if 0 <= n1 < len(order):
            exped[n1] = probs(*scored.pop(n1))
        if 0 <= n2 < len(order):
            finish(*order[n2], exped.pop(n2))


BF16_ROWS = 2 * SUBLANES


def _attention(qkv, lam, rel_table, sub_g, batch, seq, lam_init, cast_weights):
    assert batch % ATT_STREAMS == 0
    kcol = ATT_WIDTH // LANES
    vcol = 2 * ATT_WIDTH // LANES
    grid = (batch // ATT_STREAMS, ATT_HEADS)
    n_steps = grid[0] * grid[1]
    rows = lambda col0: pl.BlockSpec((None, ATT_STREAMS, seq, LANES), lambda b, h: (b, 0, 0, col0 + h))

    def cast_spec(w):
        n_blocks = next(n for n in range(n_steps, 0, -1)
                        if n_steps % n == 0 and w.shape[0] % (n * BF16_ROWS) == 0)
        return pl.BlockSpec((w.shape[0] // n_blocks, w.shape[1]),
                            lambda b, h: ((b * ATT_HEADS + h) * n_blocks // n_steps, 0))

    cast_specs = [cast_spec(w) for w in cast_weights]
    att, *cast = pl.pallas_call(
        functools.partial(_attn_body, lam_init=lam_init),
        grid=grid,
        in_specs=[
            pl.BlockSpec(memory_space=pltpu.SMEM),
            pl.BlockSpec(memory_space=pltpu.SMEM),
            rows(0), rows(kcol), rows(vcol),
            _resident((SUBLANES, BIAS_SPAN)),
            pl.BlockSpec((1, ATT_V_DIM), lambda b, h: (0, 0)),
        ] + cast_specs,
        out_specs=[rows(0)] + cast_specs,
        out_shape=[jax.ShapeDtypeStruct((batch // ATT_STREAMS, ATT_STREAMS, seq, ATT_WIDTH), BF16)]
        + [jax.ShapeDtypeStruct(w.shape, BF16) for w in cast_weights],
        scratch_shapes=[pltpu.VMEM((ATT_STREAMS, seq, 2 * LANES), BF16)],
        compiler_params=pltpu.CompilerParams(
            dimension_semantics=("arbitrary", "arbitrary"), vmem_limit_bytes=VMEM_LIMIT),
        name="diff_attn",
    )(lam, rel_table.astype(F32), *[qkv.reshape(batch // ATT_STREAMS, ATT_STREAMS, seq, -1)] * 3,
      _bias_bucket_row(), sub_g, *cast_weights)
    return att.reshape(batch * seq, ATT_WIDTH), cast


def _ssd_steps(z_ref, xs_ref, bc_ref, dt_ref, cw_ref, cb_ref, dtb_ref, alog_ref, dexp_ref, ng_ref,
               o_ref, halox_ref, halobc_ref, state_ref):
    q = SSD_Q
    x_cols, bc_cols = pl.ds(0, SSM_WIDTH), pl.ds(SSM_WIDTH, BC_COLS)
    cwx_ref, cbx_ref = cw_ref.at[:, x_cols], cb_ref.at[:, x_cols]
    cwbc_ref, cbbc_ref = cw_ref.at[:, bc_cols], cb_ref.at[:, bc_cols]

    def conv_silu(raw_ref, halo_ref, w_ref, b_ref):
        x = raw_ref[...].astype(F32)
        first_rows = lax.broadcasted_iota(jnp.int32, (HALO, x.shape[1]), 0)
        acc = b_ref[...] + w_ref[SSM_CONV - 1:SSM_CONV, :] * x
        for j in range(1, SSM_CONV):
            wj = w_ref[SSM_CONV - 1 - j:SSM_CONV - j, :]
            rolled = pltpu.roll(x, j, 0)
            head = jnp.where(first_rows < j, halo_ref[HALO - j:2 * HALO - j, :], rolled[0:HALO])
            acc = acc + wj * jnp.concatenate([head, rolled[HALO:]], axis=0)
        halo_ref[0:HALO, :] = x[q - HALO:q]
        return acc * _sigmoid(acc)

    xs = conv_silu(xs_ref, halox_ref, cwx_ref, cbx_ref)
    yield
    bc = conv_silu(bc_ref, halobc_ref, cwbc_ref, cbbc_ref)
    yield

    dtr = dt_ref[...] + dtb_ref[...]
    dt = jnp.maximum(dtr, 0.0) + jnp.log(1.0 + jnp.exp(-jnp.abs(dtr)))
    a = (-LOG2E * jnp.exp(alog_ref[...])) * dt

    row = lax.broadcasted_iota(jnp.int32, (q, q), 0)
    col = lax.broadcasted_iota(jnp.int32, (q, q), 1)
    causal = col <= row
    acum = a
    row8 = lax.broadcasted_iota(jnp.int32, (q, LANES), 0)
    sh = 1
    while sh < q:
        if sh < SUBLANES:
            shifted = jnp.where(row8 < sh, 0.0, pltpu.roll(acum, sh, 0))
        else:
            shifted = jnp.concatenate([jnp.zeros((sh, LANES), F32), acum[:q - sh]], axis=0)
        acum = acum + shifted
        sh *= 2

    lane = lax.broadcasted_iota(jnp.int32, (q, LANES), 1)

    def expand(x):
        cols = [jnp.broadcast_to(x[:, h:h + 1], (q, LANES)) for h in range(SSM_HEADS)]
        return jnp.concatenate([jnp.where(lane < SSM_HEAD_DIM, cols[h], cols[h + 1])
                                for h in range(0, SSM_HEADS, 2)], axis=1)

    dt_exp = expand(dt)
    acum_exp = expand(acum)
    ea_exp = jnp.exp2(acum_exp)
    last_exp = acum_exp[q - 1:q, :]
    dte_exp = jnp.exp2(last_exp - acum_exp)
    ea_last = ea_exp[q - 1:q, :]

    xdt = xs * dt_exp
    xdt_b = xdt.astype(BF16)
    acum_t = acum.T
    pair_lane = lax.broadcasted_iota(jnp.int32, (q, LANES), 1)
    yield

    for g in range(SSM_GROUPS):
        gsl = slice(g * SSM_GROUP_WIDTH, (g + 1) * SSM_GROUP_WIDTH)
        bg = bc[:, g * SSM_STATE:(g + 1) * SSM_STATE]
        cg = bc[:, (SSM_GROUPS + g) * SSM_STATE:(SSM_GROUPS + g + 1) * SSM_STATE]
        bg_b = bg.astype(BF16)
        cg_b = cg.astype(BF16)
        cb = lax.dot_general(cg_b, bg_b, (((1,), (1,)), ((), ())), preferred_element_type=F32)

        hq = q // 2
        tri = causal[0:hq, 0:hq]

        def masked(h, r, c):
            rows, cols = slice(r * hq, (r + 1) * hq), slice(c * hq, (c + 1) * hq)
            seg = acum[rows, h:h + 1] - acum_t[h:h + 1, cols]
            if r == c:
                seg = jnp.where(tri, seg, -jnp.inf)
            return (cb[rows, cols] * jnp.exp2(seg)).astype(BF16)

        pairs = []
        for j in range(SSM_HEADS_PER_GROUP // 2):
            h0 = g * SSM_HEADS_PER_GROUP + 2 * j
            xp = xdt_b[:, h0 * SSM_HEAD_DIM:(h0 + 2) * SSM_HEAD_DIM]
            zero = jnp.zeros_like(xp)
            xp_lo = jnp.where(pair_lane < SSM_HEAD_DIM, xp, zero)
            xp_hi = jnp.where(pair_lane >= SSM_HEAD_DIM, xp, zero)
            halves = []
            for r in range(2):
                lhs = jnp.concatenate([masked(h, r, c) for c in range(r + 1) for h in (h0, h0 + 1)], axis=1)
                rhs = jnp.concatenate([part[c * hq:(c + 1) * hq] for c in range(r + 1)
                                       for part in (xp_lo, xp_hi)], axis=0)
                halves.append(jnp.dot(lhs, rhs, preferred_element_type=F32))
            pairs.append(jnp.concatenate(halves, axis=0))
            if j % 2 == 1:
                yield
        y_diag = jnp.concatenate(pairs, axis=1)

        st = state_ref[g]
        y_off = jnp.dot(cg_b, st.astype(BF16), preferred_element_type=F32) * ea_exp[:, gsl]
        w = (xdt[:, gsl] * dte_exp[:, gsl]).astype(BF16)
        s_new = jnp.dot(bg.T.astype(BF16), w, preferred_element_type=F32)
        state_ref[g] = st * ea_last[:, gsl] + s_new

        y = y_diag + y_off + xs[:, gsl] * dexp_ref[:, gsl]
        zf = z_ref[:, gsl].astype(F32)
        gated = y * (zf * _sigmoid(zf))
        gated = gated * lax.rsqrt(jnp.mean(gated * gated, axis=-1, keepdims=True) + SSM_NORM_EPS)
        o_ref[:, gsl] = (gated * ng_ref[:, gsl]).astype(BF16)
        yield


def _ssd_consts(conv_w, conv_b, dt_bias, a_log, d_skip, norm_g):
    pad = LANES - SSM_HEADS
    dtb = jnp.pad(dt_bias.astype(F32), (0, pad)).reshape(1, LANES)
    alog = jnp.pad(a_log.astype(F32), (0, pad)).reshape(1, LANES)
    dexp = jnp.repeat(d_skip.astype(F32), SSM_HEAD_DIM).reshape(1, SSM_WIDTH)
    in_specs = [
        _resident((SSM_CONV, SSM_WIDTH + BC_COLS)),
        _resident((1, SSM_WIDTH + BC_COLS)),
        _resident((1, LANES)),
        _resident((1, LANES)),
        _resident((1, SSM_WIDTH)),
        _resident((1, SSM_WIDTH)),
    ]
    operands = (conv_w.astype(F32), conv_b.astype(F32).reshape(1, -1), dtb, alog, dexp,
                norm_g.astype(F32).reshape(1, -1))
    return in_specs, operands


SSD_COLS = 2 * SSM_WIDTH + BC_COLS
SCAN_PIECES_AFTER_QKV_DOT = (2, 2, 2, 3, 3, 2)


def _proj_ssd_body(x_ref, g_ref, w_ref, wdt_ref, *rest, tiles_per_seq):
    consts, (qkv_ref, ssm_ref, zxbc_ref, dt_ref, halox_ref, halobc_ref, state_ref) = rest[:-7], rest[-7:]

    @pl.when(pl.program_id(0) % tiles_per_seq == 0)
    def _():
        halox_ref[...] = jnp.zeros(halox_ref.shape, F32)
        halobc_ref[...] = jnp.zeros(halobc_ref.shape, F32)
        state_ref[...] = jnp.zeros(state_ref.shape, F32)

    x = x_ref[...]
    h = x * lax.rsqrt(jnp.mean(x * x, axis=-1, keepdims=True) + NORM_EPS) * g_ref[...]
    h = h.astype(BF16)
    nt = (((1,), (1,)), ((), ()))

    qkv_cols = 3 * ATT_WIDTH

    def project(col):
        y = lax.dot_general(h, w_ref[col:col + PROJ_CN, :], nt, preferred_element_type=F32).astype(BF16)
        if col < qkv_cols:
            qkv_ref[:, col:col + PROJ_CN] = y
        else:
            zxbc_ref[:, col - qkv_cols:col - qkv_cols + PROJ_CN] = y

    def project_dt(_):
        dt_t = lax.dot_general(wdt_ref[0:DT_COLS, :], h, nt, preferred_element_type=F32)
        dt_t = jnp.concatenate([dt_t, jnp.zeros((LANES - DT_COLS, PROJ_TM), F32)], axis=0)
        dt_ref[...] = dt_t.T

    def scan_steps():
        for c in range(PROJ_TM // SSD_Q):
            rows = pl.ds(c * SSD_Q, SSD_Q)
            yield from _ssd_steps(
                zxbc_ref.at[rows, pl.ds(0, SSM_WIDTH)],
                zxbc_ref.at[rows, pl.ds(SSM_WIDTH, SSM_WIDTH)],
                zxbc_ref.at[rows, pl.ds(2 * SSM_WIDTH, BC_COLS)],
                dt_ref.at[rows, :],
                *consts, ssm_ref.at[rows, :], halox_ref, halobc_ref, state_ref)

    z0, x0, bc0 = qkv_cols, qkv_cols + SSM_WIDTH, qkv_cols + 2 * SSM_WIDTH
    plan = [(project, x0, 0), (project, x0 + PROJ_CN, 1), (project, bc0, 0), (project_dt, None, 1),
            (project, z0, 1), (project, z0 + PROJ_CN, 1)]
    plan += [(project, col, pieces) for col, pieces in
             zip(range(0, qkv_cols, PROJ_CN), SCAN_PIECES_AFTER_QKV_DOT)]
    scan = scan_steps()
    for emit, col, pieces in plan:
        emit(col)
        for _ in range(pieces):
            next(scan, None)
    for _ in scan:
        pass


def _proj_ssd(x2, g, w_pad, ssd_consts, seq):
    t = x2.shape[0]
    c_specs, c_ops = ssd_consts
    tok = lambda width: pl.BlockSpec((PROJ_TM, width), lambda i: (i, 0))
    return pl.pallas_call(
        functools.partial(_proj_ssd_body, tiles_per_seq=seq // PROJ_TM),
        grid=(t // PROJ_TM,),
        in_specs=[
            tok(D_MODEL),
            _resident((1, D_MODEL)),
            _resident((MAIN_COLS, D_MODEL)),
            pl.BlockSpec((LANES, D_MODEL), lambda i: (MAIN_COLS // LANES, 0), pipeline_mode=pl.Buffered(1)),
        ] + c_specs,
        out_specs=[tok(3 * ATT_WIDTH), tok(SSM_WIDTH)],
        out_shape=[
            jax.ShapeDtypeStruct((t, 3 * ATT_WIDTH), BF16),
            jax.ShapeDtypeStruct((t, SSM_WIDTH), BF16),
        ],
        scratch_shapes=[
            pltpu.VMEM((PROJ_TM, SSD_COLS), BF16),
            pltpu.VMEM((PROJ_TM, LANES), F32),
            pltpu.VMEM((2 * HALO, SSM_WIDTH), F32),
            pltpu.VMEM((2 * HALO, BC_COLS), F32),
            pltpu.VMEM((SSM_GROUPS, SSM_STATE, SSM_GROUP_WIDTH), F32),
        ],
        compiler_params=pltpu.CompilerParams(
            dimension_semantics=("arbitrary",), vmem_limit_bytes=VMEM_LIMIT),
        name="proj_ssd",
    )(x2, g, w_pad, w_pad, *c_ops)


def _ffn_body(x_ref, att_ref, ssm_ref, wo_ref, g2_ref, wup_ref, cw_ref, cb_ref, wdn_ref, gf_ref,
              o_ref, xc_ref, x1p_ref, h2_ref, oc_ref, carry_ref, *, tiles_per_seq):
    i = pl.program_id(0)
    tm, cw = FFN_TM, FFN_CW
    ph = tm // SUBLANES
    n_lane_blocks = D_MODEL // LANES

    @pl.when(i % tiles_per_seq == 0)
    def _():
        carry_ref[...] = jnp.zeros(carry_ref.shape, F32)

    x1 = (x_ref[...]
          + jnp.dot(att_ref[...], wo_ref[0:ATT_WIDTH, :], preferred_element_type=F32)
          + jnp.dot(ssm_ref[...], wo_ref[ATT_WIDTH:, :], preferred_element_type=F32))
    for c in range(n_lane_blocks):
        xc_ref[c] = x1[:, c * LANES:(c + 1) * LANES]
    x1p = jnp.concatenate(
        [jnp.concatenate([xc_ref[c, pl.ds(k, ph, stride=SUBLANES), :] for k in range(SUBLANES)], axis=0)
         for c in range(n_lane_blocks)], axis=1)
    x1p_ref[...] = x1p
    h2_ref[...] = (x1p * lax.rsqrt(jnp.mean(x1p * x1p, axis=-1, keepdims=True) + NORM_EPS)
                   * g2_ref[...]).astype(BF16)

    first_row = lax.broadcasted_iota(jnp.int32, (ph, cw), 0) == 0

    def prev_token(block, carry_slot):
        tail = carry_ref[carry_slot]
        carry_ref[carry_slot] = block[ph - HALO:ph]
        return jnp.where(first_row, jnp.broadcast_to(tail[HALO - 1:HALO, :], (ph, cw)),
                         pltpu.roll(block, 1, 0))

    def up_conv(slot, col):
        u = jnp.dot(h2_ref[...], wup_ref[:, col:col + cw], preferred_element_type=F32)
        blocks = [u[k * ph:(k + 1) * ph] for k in range(SUBLANES)]
        back1 = prev_token(blocks[SUBLANES - 1], 2 * slot)
        back2 = prev_token(blocks[SUBLANES - 2], 2 * slot + 1)
        hist = [back2, back1] + blocks
        w = [cw_ref[t:t + 1, col:col + cw] for t in range(FFN_CONV)]
        b = cb_ref[:, col:col + cw]
        return jnp.concatenate(
            [b + w[2] * hist[k + 2] + w[1] * hist[k + 1] + w[0] * hist[k] for k in range(SUBLANES)],
            axis=0)

    acts = []
    for j in range(FFN_DIM // cw):
        gate = up_conv(2 * j, j * cw)
        val = up_conv(2 * j + 1, FFN_DIM + j * cw)
        acts.append((gate * _sigmoid(gate) * val).astype(BF16))

    x2 = x1p_ref[...] + jnp.dot(jnp.concatenate(acts, axis=1), wdn_ref[...], preferred_element_type=F32)
    out = x2 * lax.rsqrt(jnp.mean(x2 * x2, axis=-1, keepdims=True) + NORM_EPS) * gf_ref[...]
    for c in range(n_lane_blocks):
        for k in range(SUBLANES):
            oc_ref[c, pl.ds(k, ph, stride=SUBLANES), :] = out[k * ph:(k + 1) * ph, c * LANES:(c + 1) * LANES]
    for c in range(n_lane_blocks):
        o_ref[:, c * LANES:(c + 1) * LANES] = oc_ref[c]


def _mix_ffn(x2, att, ssm, w_out, g2, w_up, conv_w, conv_b, w_down, gf, seq):
    t = x2.shape[0]
    n_carry = 2 * (FFN_CONV - 1) * (FFN_DIM // FFN_CW)
    tok = lambda width: pl.BlockSpec((FFN_TM, width), lambda i: (i, 0))
    return pl.pallas_call(
        functools.partial(_ffn_body, tiles_per_seq=seq // FFN_TM),
        grid=(t // FFN_TM,),
        in_specs=[
            tok(D_MODEL), tok(ATT_WIDTH), tok(SSM_WIDTH),
            _resident((ATT_WIDTH + SSM_WIDTH, D_MODEL)),
            _resident((1, D_MODEL)),
            _resident((D_MODEL, 2 * FFN_DIM)),
            _resident((FFN_CONV, 2 * FFN_DIM)),
            _resident((1, 2 * FFN_DIM)),
            _resident((FFN_DIM, D_MODEL)),
            _resident((1, D_MODEL)),
        ],
        out_specs=tok(D_MODEL),
        out_shape=jax.ShapeDtypeStruct((t, D_MODEL), F32),
        scratch_shapes=[
            pltpu.VMEM((D_MODEL // LANES, FFN_TM, LANES), F32),
            pltpu.VMEM((FFN_TM, D_MODEL), F32),
            pltpu.VMEM((FFN_TM, D_MODEL), BF16),
            pltpu.VMEM((D_MODEL // LANES, FFN_TM, LANES), F32),
            pltpu.VMEM((n_carry, HALO, FFN_CW), F32),
        ],
        compiler_params=pltpu.CompilerParams(
            dimension_semantics=("arbitrary",), vmem_limit_bytes=VMEM_LIMIT),
        name="mix_ffn",
    )(x2, att, ssm, w_out, g2, w_up, conv_w, conv_b, w_down, gf)


def kernel(x, rel_bias_table, attn_norm_g, w_in, lambda_q1, lambda_k1, lambda_q2, lambda_k2,
           attn_subln_g, ssm_conv_w, ssm_conv_b, ssm_dt_bias, ssm_a_log, ssm_d, ssm_norm_g,
           w_out, ffn_norm_g, ffn_w_up, ffn_conv_w, ffn_conv_b, ffn_w_down, final_norm_g):
    batch, seq, _ = x.shape
    depth = w_in.shape[0]
    assert seq % max(ATT_QB, SSD_Q, FFN_TM) == 0 and (batch * seq) % PROJ_TM == 0
    x2 = x.reshape(batch * seq, D_MODEL)
    row = lambda v: v.astype(F32).reshape(1, -1)
    for i in range(depth):
        lam_init = 0.8 - 0.6 * math.exp(-0.3 * i)
        lam = (jnp.exp(jnp.sum(lambda_q1[i].astype(F32) * lambda_k1[i].astype(F32)))
               - jnp.exp(jnp.sum(lambda_q2[i].astype(F32) * lambda_k2[i].astype(F32)))
               + lam_init).reshape(1)
        qkv, ssm = _proj_ssd(
            x2, row(attn_norm_g[i]), _w_prep(w_in[i].astype(F32).T),
            _ssd_consts(ssm_conv_w[i], ssm_conv_b[i], ssm_dt_bias[i], ssm_a_log[i], ssm_d[i], ssm_norm_g[i]),
            seq)
        att, (w_out_b, w_up_b, w_down_b) = _attention(
            qkv, lam, rel_bias_table, row(attn_subln_g[i]), batch, seq, lam_init,
            [w_out[i].astype(F32), ffn_w_up[i].astype(F32), ffn_w_down[i].astype(F32)])
        assert depth == 1
        x2 = _mix_ffn(x2, att, ssm, w_out_b, row(ffn_norm_g[i]), w_up_b, ffn_conv_w[i].astype(F32),
                      row(ffn_conv_b[i]), w_down_b, row(final_norm_g), seq)
    return x2.reshape(batch, seq, D_MODEL)
```

```python
import functools
import math

import jax
import jax.numpy as jnp
from jax import lax
from jax.experimental import pallas as pl
from jax.experimental.pallas import tpu as pltpu

F32 = jnp.float32
BF16 = jnp.bfloat16

D_MODEL = 1024
CHUNK = 64
ATT_HEADS = 8
ATT_HEAD_DIM = 64
ATT_V_DIM = 2 * ATT_HEAD_DIM
ATT_WIDTH = ATT_HEADS * ATT_V_DIM
SSM_HEADS = 16
SSM_HEAD_DIM = 64
SSM_WIDTH = SSM_HEADS * SSM_HEAD_DIM
SSM_GROUPS = 2
SSM_STATE = 128
SSM_CONV = 4
SSM_HEADS_PER_GROUP = SSM_HEADS // SSM_GROUPS
SSM_GROUP_WIDTH = SSM_WIDTH // SSM_GROUPS
FFN_DIM = 2816
FFN_CONV = 3
REL_BUCKETS = 32
REL_MAX_DIST = 128
NORM_EPS = 1e-6
SUBLN_EPS = 1e-5
SSM_NORM_EPS = 1e-5
BC_COLS = 2 * SSM_GROUPS * SSM_STATE
MAIN_COLS = 3 * ATT_WIDTH + SSM_WIDTH + SSM_WIDTH + BC_COLS
DT_COLS = SSM_HEADS
LOG2E = math.log2(math.e)

LANES = 128
SUBLANES = 8
VMEM_LIMIT = 56 * 1024 * 1024

PROJ_TM = 512
PROJ_CN = 512
ATT_QB = 256
ATT_KB = 256
SSD_Q = 256
FFN_TM = 512
FFN_CW = 256
HALO = SUBLANES

def _resident(shape):
    nd = len(shape)
    return pl.BlockSpec(shape, lambda *_: (0,) * nd, pipeline_mode=pl.Buffered(1))


def _sigmoid(x):
    return 0.5 + 0.5 * jnp.tanh(0.5 * x)


W_PAD_COLS = MAIN_COLS + LANES
WPREP_LANES = 256


def _wprep_body(w_ref, o_ref):
    q_scale = LOG2E * ATT_HEAD_DIM ** -0.5
    o_ref[0:ATT_WIDTH, :] = (w_ref[0:ATT_WIDTH, :] * q_scale).astype(BF16)
    o_ref[ATT_WIDTH:MAIN_COLS + DT_COLS, :] = w_ref[ATT_WIDTH:MAIN_COLS + DT_COLS, :].astype(BF16)
    o_ref[MAIN_COLS + DT_COLS:W_PAD_COLS, :] = jnp.zeros((LANES - DT_COLS, WPREP_LANES), BF16)


def _w_prep(w_t):
    cols = lambda rows: pl.BlockSpec((rows, WPREP_LANES), lambda j: (0, j))
    return pl.pallas_call(
        _wprep_body,
        grid=(D_MODEL // WPREP_LANES,),
        in_specs=[cols(MAIN_COLS + DT_COLS)],
        out_specs=cols(W_PAD_COLS),
        out_shape=jax.ShapeDtypeStruct((W_PAD_COLS, D_MODEL), BF16),
        compiler_params=pltpu.CompilerParams(
            dimension_semantics=("arbitrary",), vmem_limit_bytes=VMEM_LIMIT),
        name="w_prep",
    )(w_t)


FAR_BUCKET = REL_BUCKETS // 2 - 1
BIAS_SPAN = 4 * ATT_KB


def _t5_bucket(rel):
    nb = REL_BUCKETS // 2
    max_exact = nb // 2
    bucket = jnp.where(rel > 0, nb, 0)
    n = jnp.abs(rel)
    nf = jnp.maximum(n, 1).astype(F32)
    large = max_exact + (jnp.log(nf / max_exact) / math.log(REL_MAX_DIST / max_exact)
                         * (nb - max_exact)).astype(jnp.int32)
    large = jnp.minimum(large, nb - 1)
    return bucket + jnp.where(n < max_exact, n, large)


def _bias_bucket_row():
    rel = jnp.arange(BIAS_SPAN, dtype=jnp.int32) - 2 * ATT_KB
    return jnp.broadcast_to(_t5_bucket(rel)[None, :], (SUBLANES, BIAS_SPAN))


def _bias_tiles(tbl_ref, idx_ref, h):
    idx = idx_ref[...]
    base = tbl_ref[FAR_BUCKET, h]
    r = jnp.zeros(idx.shape, F32)
    for b in range(REL_BUCKETS):
        r = jnp.where(idx == b, (tbl_ref[b, h] - base) * LOG2E, r)
    rows = jnp.concatenate([r] * (ATT_QB // SUBLANES), axis=0)
    rolled = pltpu.roll(rows, 0, 1, stride=1, stride_axis=0)
    return rolled[:, 2 * ATT_KB:3 * ATT_KB], rolled[:, ATT_KB:2 * ATT_KB]


ATT_STREAMS = 2


def _attn_body(lam_ref, tbl_ref, q_ref, k_ref, v_ref, idx_ref, subg_ref, *rest, lam_init):
    n_cast = (len(rest) - 2) // 2
    cast_in, (o_ref, *cast_out, v1_ref) = rest[:n_cast], rest[n_cast:]
    for src, dst in zip(cast_in, cast_out):
        dst[...] = src[...].astype(BF16)

    qb, kb = ATT_QB, ATT_KB
    seq = q_ref.shape[1]
    lane = lax.broadcasted_iota(jnp.int32, (qb, LANES), 1)
    lam = lam_ref[0]

    @functools.cache
    def bias_and_mask():
        row = lax.broadcasted_iota(jnp.int32, (qb, kb), 0)
        col = lax.broadcasted_iota(jnp.int32, (qb, kb), 1)
        allowed = (col // CHUNK) <= (row // CHUNK)
        b_diag, b_prev = _bias_tiles(tbl_ref, idx_ref, pl.program_id(1))
        return tuple(jnp.concatenate([t, t], axis=0) for t in (allowed, b_diag, b_prev))

    @functools.cache
    def build_v1(s):
        v1_ref[s, :, 0:LANES] = v_ref[s]
        v1_ref[s, :, LANES:2 * LANES] = jnp.ones((seq, LANES), BF16)

    def scores(s, i):
        kvl = (i + 1) * kb
        q = q_ref[s, i * qb:(i + 1) * qb, :]
        zero = jnp.zeros_like(q)
        qs = jnp.concatenate([jnp.where(lane < ATT_HEAD_DIM, q, zero),
                              jnp.where(lane >= ATT_HEAD_DIM, q, zero)], axis=0)
        sc = lax.dot_general(qs, k_ref[s, 0:kvl, :], (((1,), (1,)), ((), ())),
                             preferred_element_type=F32)
        allowed, b_diag, b_prev = bias_and_mask()
        blocks = [sc[:, j * kb:(j + 1) * kb] for j in range(i + 1)]
        blocks[i] = jnp.where(allowed, blocks[i] + b_diag, -1e30)
        if i >= 1:
            blocks[i - 1] = blocks[i - 1] + b_prev
        m = jnp.max(functools.reduce(jnp.maximum, blocks), axis=-1, keepdims=True)
        return blocks, m

    def probs(blocks, m):
        return jnp.concatenate([jnp.exp2(blk - m).astype(BF16) for blk in blocks], axis=1)

    def finish(s, i, p):
        build_v1(s)
        kvl = (i + 1) * kb
        acc = jnp.dot(p, v1_ref[s, 0:kvl, :], preferred_element_type=F32)
        o = acc[:, 0:LANES] / acc[:, LANES:2 * LANES]
        o = o[:qb] - lam * o[qb:]
        o = o * lax.rsqrt(jnp.mean(o * o, axis=-1, keepdims=True) + SUBLN_EPS) * subg_ref[...]
        o_ref[s, i * qb:(i + 1) * qb, :] = (o * (1.0 - lam_init)).astype(BF16)

    order = [(s, i) for i in reversed(range(seq // qb)) for s in range(ATT_STREAMS)]
    scored, exped = {}, {}
    for n in range(len(order) + 2 * ATT_STREAMS):
        if n < len(order):
            scored[n] = scores(*order[n])
        n1, n2 = n - ATT_STREAMS, n - 2 * ATT_STREAMS
        if 0 <= n1 < len(order):
            exped[n1] = probs(*scored.pop(n1))
        if 0 <= n2 < len(order):
            finish(*order[n2], exped.pop(n2))


BF16_ROWS = 2 * SUBLANES


def _attention(qkv, lam, rel_table, sub_g, batch, seq, lam_init, cast_weights):
    assert batch % ATT_STREAMS == 0
    kcol = ATT_WIDTH // LANES
    vcol = 2 * ATT_WIDTH // LANES
    grid = (batch // ATT_STREAMS, ATT_HEADS)
    n_steps = grid[0] * grid[1]
    rows = lambda col0: pl.BlockSpec((None, ATT_STREAMS, seq, LANES), lambda b, h: (b, 0, 0, col0 + h))

    def cast_spec(w):
        n_blocks = next(n for n in range(n_steps, 0, -1)
                        if n_steps % n == 0 and w.shape[0] % (n * BF16_ROWS) == 0)
        return pl.BlockSpec((w.shape[0] // n_blocks, w.shape[1]),
                            lambda b, h: ((b * ATT_HEADS + h) * n_blocks // n_steps, 0))

    cast_specs = [cast_spec(w) for w in cast_weights]
    att, *cast = pl.pallas_call(
        functools.partial(_attn_body, lam_init=lam_init),
        grid=grid,
        in_specs=[
            pl.BlockSpec(memory_space=pltpu.SMEM),
            pl.BlockSpec(memory_space=pltpu.SMEM),
            rows(0), rows(kcol), rows(vcol),
            _resident((SUBLANES, BIAS_SPAN)),
            pl.BlockSpec((1, ATT_V_DIM), lambda b, h: (0, 0)),
        ] + cast_specs,
        out_specs=[rows(0)] + cast_specs,
        out_shape=[jax.ShapeDtypeStruct((batch // ATT_STREAMS, ATT_STREAMS, seq, ATT_WIDTH), BF16)]
        + [jax.ShapeDtypeStruct(w.shape, BF16) for w in cast_weights],
        scratch_shapes=[pltpu.VMEM((ATT_STREAMS, seq, 2 * LANES), BF16)],
        compiler_params=pltpu.CompilerParams(
            dimension_semantics=("arbitrary", "arbitrary"), vmem_limit_bytes=VMEM_LIMIT),
        name="diff_attn",
    )(lam, rel_table.astype(F32), *[qkv.reshape(batch // ATT_STREAMS, ATT_STREAMS, seq, -1)] * 3,
      _bias_bucket_row(), sub_g, *cast_weights)
    return att.reshape(batch * seq, ATT_WIDTH), cast


def _ssd_steps(z_ref, xs_ref, bc_ref, dt_ref, cw_ref, cb_ref, dtb_ref, alog_ref, dexp_ref, ng_ref,
               o_ref, halox_ref, halobc_ref, state_ref):
    q = SSD_Q
    x_cols, bc_cols = pl.ds(0, SSM_WIDTH), pl.ds(SSM_WIDTH, BC_COLS)
    cwx_ref, cbx_ref = cw_ref.at[:, x_cols], cb_ref.at[:, x_cols]
    cwbc_ref, cbbc_ref = cw_ref.at[:, bc_cols], cb_ref.at[:, bc_cols]

    def conv_silu(raw_ref, halo_ref, w_ref, b_ref):
        x = raw_ref[...].astype(F32)
        first_rows = lax.broadcasted_iota(jnp.int32, (HALO, x.shape[1]), 0)
        acc = b_ref[...] + w_ref[SSM_CONV - 1:SSM_CONV, :] * x
        for j in range(1, SSM_CONV):
            wj = w_ref[SSM_CONV - 1 - j:SSM_CONV - j, :]
            rolled = pltpu.roll(x, j, 0)
            head = jnp.where(first_rows < j, halo_ref[HALO - j:2 * HALO - j, :], rolled[0:HALO])
            acc = acc + wj * jnp.concatenate([head, rolled[HALO:]], axis=0)
        halo_ref[0:HALO, :] = x[q - HALO:q]
        return acc * _sigmoid(acc)

    xs = conv_silu(xs_ref, halox_ref, cwx_ref, cbx_ref)
    yield
    bc = conv_silu(bc_ref, halobc_ref, cwbc_ref, cbbc_ref)
    yield

    dtr = dt_ref[...] + dtb_ref[...]
    dt = jnp.maximum(dtr, 0.0) + jnp.log(1.0 + jnp.exp(-jnp.abs(dtr)))
    a = (-LOG2E * jnp.exp(alog_ref[...])) * dt

    row = lax.broadcasted_iota(jnp.int32, (q, q), 0)
    col = lax.broadcasted_iota(jnp.int32, (q, q), 1)
    causal = col <= row
    acum = a
    row8 = lax.broadcasted_iota(jnp.int32, (q, LANES), 0)
    sh = 1
    while sh < q:
        if sh < SUBLANES:
            shifted = jnp.where(row8 < sh, 0.0, pltpu.roll(acum, sh, 0))
        else:
            shifted = jnp.concatenate([jnp.zeros((sh, LANES), F32), acum[:q - sh]], axis=0)
        acum = acum + shifted
        sh *= 2

    lane = lax.broadcasted_iota(jnp.int32, (q, LANES), 1)

    def expand(x):
        cols = [jnp.broadcast_to(x[:, h:h + 1], (q, LANES)) for h in range(SSM_HEADS)]
        return jnp.concatenate([jnp.where(lane < SSM_HEAD_DIM, cols[h], cols[h + 1])
                                for h in range(0, SSM_HEADS, 2)], axis=1)

    dt_exp = expand(dt)
    acum_exp = expand(acum)
    ea_exp = jnp.exp2(acum_exp)
    last_exp = acum_exp[q - 1:q, :]
    dte_exp = jnp.exp2(last_exp - acum_exp)
    ea_last = ea_exp[q - 1:q, :]

    xdt = xs * dt_exp
    xdt_b = xdt.astype(BF16)
    acum_t = acum.T
    pair_lane = lax.broadcasted_iota(jnp.int32, (q, LANES), 1)
    yield

    for g in range(SSM_GROUPS):
        gsl = slice(g * SSM_GROUP_WIDTH, (g + 1) * SSM_GROUP_WIDTH)
        bg = bc[:, g * SSM_STATE:(g + 1) * SSM_STATE]
        cg = bc[:, (SSM_GROUPS + g) * SSM_STATE:(SSM_GROUPS + g + 1) * SSM_STATE]
        bg_b = bg.astype(BF16)
        cg_b = cg.astype(BF16)
        cb = lax.dot_general(cg_b, bg_b, (((1,), (1,)), ((), ())), preferred_element_type=F32)

        def masked(h):
            seg = acum[:, h:h + 1] - acum_t[h:h + 1, :]
            return (cb * jnp.exp2(jnp.where(causal, seg, -jnp.inf))).astype(BF16)

        pairs = []
        for j in range(SSM_HEADS_PER_GROUP // 2):
            h0 = g * SSM_HEADS_PER_GROUP + 2 * j
            xp = xdt_b[:, h0 * SSM_HEAD_DIM:(h0 + 2) * SSM_HEAD_DIM]
            r0 = jnp.dot(masked(h0), xp, preferred_element_type=F32)
            r1 = jnp.dot(masked(h0 + 1), xp, preferred_element_type=F32)
            pairs.append(jnp.where(pair_lane < SSM_HEAD_DIM, r0, r1))
            if j % 2 == 1:
                yield
        y_diag = jnp.concatenate(pairs, axis=1)

        st = state_ref[g]
        y_off = jnp.dot(cg_b, st.astype(BF16), preferred_element_type=F32) * ea_exp[:, gsl]
        w = (xdt[:, gsl] * dte_exp[:, gsl]).astype(BF16)
        s_new = jnp.dot(bg.T.astype(BF16), w, preferred_element_type=F32)
        state_ref[g] = st * ea_last[:, gsl] + s_new

        y = y_diag + y_off + xs[:, gsl] * dexp_ref[:, gsl]
        zf = z_ref[:, gsl].astype(F32)
        gated = y * (zf * _sigmoid(zf))
        gated = gated * lax.rsqrt(jnp.mean(gated * gated, axis=-1, keepdims=True) + SSM_NORM_EPS)
        o_ref[:, gsl] = (gated * ng_ref[:, gsl]).astype(BF16)
        yield


def _ssd_consts(conv_w, conv_b, dt_bias, a_log, d_skip, norm_g):
    pad = LANES - SSM_HEADS
    dtb = jnp.pad(dt_bias.astype(F32), (0, pad)).reshape(1, LANES)
    alog = jnp.pad(a_log.astype(F32), (0, pad)).reshape(1, LANES)
    dexp = jnp.repeat(d_skip.astype(F32), SSM_HEAD_DIM).reshape(1, SSM_WIDTH)
    in_specs = [
        _resident((SSM_CONV, SSM_WIDTH + BC_COLS)),
        _resident((1, SSM_WIDTH + BC_COLS)),
        _resident((1, LANES)),
        _resident((1, LANES)),
        _resident((1, SSM_WIDTH)),
        _resident((1, SSM_WIDTH)),
    ]
    operands = (conv_w.astype(F32), conv_b.astype(F32).reshape(1, -1), dtb, alog, dexp,
                norm_g.astype(F32).reshape(1, -1))
    return in_specs, operands


SSD_COLS = 2 * SSM_WIDTH + BC_COLS
SCAN_PIECES_AFTER_QKV_DOT = (2, 2, 2, 3, 3, 2)


def _proj_ssd_body(x_ref, g_ref, w_ref, wdt_ref, *rest, tiles_per_seq):
    consts, (qkv_ref, ssm_ref, zxbc_ref, dt_ref, halox_ref, halobc_ref, state_ref) = rest[:-7], rest[-7:]

    @pl.when(pl.program_id(0) % tiles_per_seq == 0)
    def _():
        halox_ref[...] = jnp.zeros(halox_ref.shape, F32)
        halobc_ref[...] = jnp.zeros(halobc_ref.shape, F32)
        state_ref[...] = jnp.zeros(state_ref.shape, F32)

    x = x_ref[...]
    h = x * lax.rsqrt(jnp.mean(x * x, axis=-1, keepdims=True) + NORM_EPS) * g_ref[...]
    h = h.astype(BF16)
    nt = (((1,), (1,)), ((), ()))

    qkv_cols = 3 * ATT_WIDTH

    def project(col):
        y = lax.dot_general(h, w_ref[col:col + PROJ_CN, :], nt, preferred_element_type=F32).astype(BF16)
        if col < qkv_cols:
            qkv_ref[:, col:col + PROJ_CN] = y
        else:
            zxbc_ref[:, col - qkv_cols:col - qkv_cols + PROJ_CN] = y

    def project_dt(_):
        dt_t = lax.dot_general(wdt_ref[0:DT_COLS, :], h, nt, preferred_element_type=F32)
        dt_t = jnp.concatenate([dt_t, jnp.zeros((LANES - DT_COLS, PROJ_TM), F32)], axis=0)
        dt_ref[...] = dt_t.T

    def scan_steps():
        for c in range(PROJ_TM // SSD_Q):
            rows = pl.ds(c * SSD_Q, SSD_Q)
            yield from _ssd_steps(
                zxbc_ref.at[rows, pl.ds(0, SSM_WIDTH)],
                zxbc_ref.at[rows, pl.ds(SSM_WIDTH, SSM_WIDTH)],
                zxbc_ref.at[rows, pl.ds(2 * SSM_WIDTH, BC_COLS)],
                dt_ref.at[rows, :],
                *consts, ssm_ref.at[rows, :], halox_ref, halobc_ref, state_ref)

    z0, x0, bc0 = qkv_cols, qkv_cols + SSM_WIDTH, qkv_cols + 2 * SSM_WIDTH
    plan = [(project, x0, 0), (project, x0 + PROJ_CN, 1), (project, bc0, 0), (project_dt, None, 1),
            (project, z0, 1), (project, z0 + PROJ_CN, 1)]
    plan += [(project, col, pieces) for col, pieces in
             zip(range(0, qkv_cols, PROJ_CN), SCAN_PIECES_AFTER_QKV_DOT)]
    scan = scan_steps()
    for emit, col, pieces in plan:
        emit(col)
        for _ in range(pieces):
            next(scan, None)
    for _ in scan:
        pass


def _proj_ssd(x2, g, w_pad, ssd_consts, seq):
    t = x2.shape[0]
    c_specs, c_ops = ssd_consts
    tok = lambda width: pl.BlockSpec((PROJ_TM, width), lambda i: (i, 0))
    return pl.pallas_call(
        functools.partial(_proj_ssd_body, tiles_per_seq=seq // PROJ_TM),
        grid=(t // PROJ_TM,),
        in_specs=[
            tok(D_MODEL),
            _resident((1, D_MODEL)),
            _resident((MAIN_COLS, D_MODEL)),
            pl.BlockSpec((LANES, D_MODEL), lambda i: (MAIN_COLS // LANES, 0), pipeline_mode=pl.Buffered(1)),
        ] + c_specs,
        out_specs=[tok(3 * ATT_WIDTH), tok(SSM_WIDTH)],
        out_shape=[
            jax.ShapeDtypeStruct((t, 3 * ATT_WIDTH), BF16),
            jax.ShapeDtypeStruct((t, SSM_WIDTH), BF16),
        ],
        scratch_shapes=[
            pltpu.VMEM((PROJ_TM, SSD_COLS), BF16),
            pltpu.VMEM((PROJ_TM, LANES), F32),
            pltpu.VMEM((2 * HALO, SSM_WIDTH), F32),
            pltpu.VMEM((2 * HALO, BC_COLS), F32),
            pltpu.VMEM((SSM_GROUPS, SSM_STATE, SSM_GROUP_WIDTH), F32),
        ],
        compiler_params=pltpu.CompilerParams(
            dimension_semantics=("arbitrary",), vmem_limit_bytes=VMEM_LIMIT),
        name="proj_ssd",
    )(x2, g, w_pad, w_pad, *c_ops)


def _ffn_body(x_ref, att_ref, ssm_ref, wo_ref, g2_ref, wup_ref, cw_ref, cb_ref, wdn_ref, gf_ref,
              o_ref, xc_ref, x1p_ref, h2_ref, oc_ref, carry_ref, *, tiles_per_seq):
    i = pl.program_id(0)
    tm, cw = FFN_TM, FFN_CW
    ph = tm // SUBLANES
    n_lane_blocks = D_MODEL // LANES

    @pl.when(i % tiles_per_seq == 0)
    def _():
        carry_ref[...] = jnp.zeros(carry_ref.shape, F32)

    x1 = (x_ref[...]
          + jnp.dot(att_ref[...], wo_ref[0:ATT_WIDTH, :], preferred_element_type=F32)
          + jnp.dot(ssm_ref[...], wo_ref[ATT_WIDTH:, :], preferred_element_type=F32))
    for c in range(n_lane_blocks):
        xc_ref[c] = x1[:, c * LANES:(c + 1) * LANES]
    x1p = jnp.concatenate(
        [jnp.concatenate([xc_ref[c, pl.ds(k, ph, stride=SUBLANES), :] for k in range(SUBLANES)], axis=0)
         for c in range(n_lane_blocks)], axis=1)
    x1p_ref[...] = x1p
    h2_ref[...] = (x1p * lax.rsqrt(jnp.mean(x1p * x1p, axis=-1, keepdims=True) + NORM_EPS)
                   * g2_ref[...]).astype(BF16)

    first_row = lax.broadcasted_iota(jnp.int32, (ph, cw), 0) == 0

    def prev_token(block, carry_slot):
        tail = carry_ref[carry_slot]
        carry_ref[carry_slot] = block[ph - HALO:ph]
        return jnp.where(first_row, jnp.broadcast_to(tail[HALO - 1:HALO, :], (ph, cw)),
                         pltpu.roll(block, 1, 0))

    def up_conv(slot, col):
        u = jnp.dot(h2_ref[...], wup_ref[:, col:col + cw], preferred_element_type=F32)
        blocks = [u[k * ph:(k + 1) * ph] for k in range(SUBLANES)]
        back1 = prev_token(blocks[SUBLANES - 1], 2 * slot)
        back2 = prev_token(blocks[SUBLANES - 2], 2 * slot + 1)
        hist = [back2, back1] + blocks
        w = [cw_ref[t:t + 1, col:col + cw] for t in range(FFN_CONV)]
        b = cb_ref[:, col:col + cw]
        return jnp.concatenate(
            [b + w[2] * hist[k + 2] + w[1] * hist[k + 1] + w[0] * hist[k] for k in range(SUBLANES)],
            axis=0)

    acts = []
    for j in range(FFN_DIM // cw):
        gate = up_conv(2 * j, j * cw)
        val = up_conv(2 * j + 1, FFN_DIM + j * cw)
        acts.append((gate * _sigmoid(gate) * val).astype(BF16))

    x2 = x1p_ref[...] + jnp.dot(jnp.concatenate(acts, axis=1), wdn_ref[...], preferred_element_type=F32)
    out = x2 * lax.rsqrt(jnp.mean(x2 * x2, axis=-1, keepdims=True) + NORM_EPS) * gf_ref[...]
    for c in range(n_lane_blocks):
        for k in range(SUBLANES):
            oc_ref[c, pl.ds(k, ph, stride=SUBLANES), :] = out[k * ph:(k + 1) * ph, c * LANES:(c + 1) * LANES]
    for c in range(n_lane_blocks):
        o_ref[:, c * LANES:(c + 1) * LANES] = oc_ref[c]


def _mix_ffn(x2, att, ssm, w_out, g2, w_up, conv_w, conv_b, w_down, gf, seq):
    t = x2.shape[0]
    n_carry = 2 * (FFN_CONV - 1) * (FFN_DIM // FFN_CW)
    tok = lambda width: pl.BlockSpec((FFN_TM, width), lambda i: (i, 0))
    return pl.pallas_call(
        functools.partial(_ffn_body, tiles_per_seq=seq // FFN_TM),
        grid=(t // FFN_TM,),
        in_specs=[
            tok(D_MODEL), tok(ATT_WIDTH), tok(SSM_WIDTH),
            _resident((ATT_WIDTH + SSM_WIDTH, D_MODEL)),
            _resident((1, D_MODEL)),
            _resident((D_MODEL, 2 * FFN_DIM)),
            _resident((FFN_CONV, 2 * FFN_DIM)),
            _resident((1, 2 * FFN_DIM)),
            _resident((FFN_DIM, D_MODEL)),
            _resident((1, D_MODEL)),
        ],
        out_specs=tok(D_MODEL),
        out_shape=jax.ShapeDtypeStruct((t, D_MODEL), F32),
        scratch_shapes=[
            pltpu.VMEM((D_MODEL // LANES, FFN_TM, LANES), F32),
            pltpu.VMEM((FFN_TM, D_MODEL), F32),
            pltpu.VMEM((FFN_TM, D_MODEL), BF16),
            pltpu.VMEM((D_MODEL // LANES, FFN_TM, LANES), F32),
            pltpu.VMEM((n_carry, HALO, FFN_CW), F32),
        ],
        compiler_params=pltpu.CompilerParams(
            dimension_semantics=("arbitrary",), vmem_limit_bytes=VMEM_LIMIT),
        name="mix_ffn",
    )(x2, att, ssm, w_out, g2, w_up, conv_w, conv_b, w_down, gf)


def kernel(x, rel_bias_table, attn_norm_g, w_in, lambda_q1, lambda_k1, lambda_q2, lambda_k2,
           attn_subln_g, ssm_conv_w, ssm_conv_b, ssm_dt_bias, ssm_a_log, ssm_d, ssm_norm_g,
           w_out, ffn_norm_g, ffn_w_up, ffn_conv_w, ffn_conv_b, ffn_w_down, final_norm_g):
    batch, seq, _ = x.shape
    depth = w_in.shape[0]
    assert seq % max(ATT_QB, SSD_Q, FFN_TM) == 0 and (batch * seq) % PROJ_TM == 0
    x2 = x.reshape(batch * seq, D_MODEL)
    row = lambda v: v.astype(F32).reshape(1, -1)
    for i in range(depth):
        lam_init = 0.8 - 0.6 * math.exp(-0.3 * i)
        lam = (jnp.exp(jnp.sum(lambda_q1[i].astype(F32) * lambda_k1[i].astype(F32)))
               - jnp.exp(jnp.sum(lambda_q2[i].astype(F32) * lambda_k2[i].astype(F32)))
               + lam_init).reshape(1)
        qkv, ssm = _proj_ssd(
            x2, row(attn_norm_g[i]), _w_prep(w_in[i].astype(F32).T),
            _ssd_consts(ssm_conv_w[i], ssm_conv_b[i], ssm_dt_bias[i], ssm_a_log[i], ssm_d[i], ssm_norm_g[i]),
            seq)
        att, (w_out_b, w_up_b, w_down_b) = _attention(
            qkv, lam, rel_bias_table, row(attn_subln_g[i]), batch, seq, lam_init,
            [w_out[i].astype(F32), ffn_w_up[i].astype(F32), ffn_w_down[i].astype(F32)])
        assert depth == 1
        x2 = _mix_ffn(x2, att, ssm, w_out_b, row(ffn_norm_g[i]), w_up_b, ffn_conv_w[i].astype(F32),
                      row(ffn_conv_b[i]), w_down_b, row(final_norm_g), seq)
    return x2.reshape(batch, seq, D_MODEL)
```
